```python
import math
import jax, jax.numpy as jnp
from jax import lax
import numpy as np

D_MODEL = 1024
BATCH = 2
SEQ = 8192
DEPTH = 2

A_GROUPS = 4
A_GROUP_DIM = 128
A_WIDTH = A_GROUPS * A_GROUP_DIM
A_CHUNK = 128
B_HEADS = 4
B_KDIM = 128
B_VDIM = 128
B_QK_WIDTH = B_HEADS * B_KDIM
B_WIDTH = B_HEADS * B_VDIM
B_CHUNK = 64
C_HEADS = 16
C_HEAD_DIM = D_MODEL // C_HEADS
C_WIDTH = C_HEADS * C_HEAD_DIM
DILATED_PATTERNS = ((128, 1), (512, 4), (2048, 16))
REL_BUCKETS = 32
REL_MAX_DISTANCE = 1024
N_EXPERTS = 16
EXPERT_D_FF = 1024
CAPACITY_FACTOR = 2
EPS = 1e-6
NEG_INF = -1e30

kernel_name = 'hybrid_gmlp_hgrn2_dilated_ec_moe'


def rmsnorm(x, g):
    xf = x.astype(jnp.float32)
    y = xf * lax.rsqrt(jnp.mean(xf * xf, axis=-1, keepdims=True) + EPS)
    return (y * g.astype(jnp.float32)).astype(x.dtype)


def layernorm(x, g, b):
    xf = x.astype(jnp.float32)
    mu = jnp.mean(xf, axis=-1, keepdims=True)
    xc = xf - mu
    y = xc * lax.rsqrt(jnp.mean(xc * xc, axis=-1, keepdims=True) + EPS)
    return (y * g.astype(jnp.float32) + b.astype(jnp.float32)).astype(x.dtype)


def chunked_spatial_gating(u, v, w_s, b_s, ln_g, ln_b):
    B, S, _ = v.shape
    v = layernorm(v, ln_g, ln_b)
    vc = v.reshape(B, S // A_CHUNK, A_CHUNK, A_GROUPS, A_GROUP_DIM)
    mixed = jnp.einsum('gts,bnsgc->bntgc', w_s, vc) + b_s.T[:, :, None]
    return u * mixed.reshape(B, S, A_WIDTH)


def gla_chunk_scan(q, k, log_f, v):
    B, S, H, K = q.shape
    V = v.shape[-1]
    n = S // B_CHUNK

    def chunks(t):
        return t.reshape(B, n, B_CHUNK, H, t.shape[-1]).transpose(1, 0, 3, 2, 4)

    lower = jnp.tril(jnp.ones((B_CHUNK, B_CHUNK), dtype=bool))[:, :, None]

    def step(state, inp):
        qc, kc, fc, vc = inp
        b = jnp.cumsum(fc, axis=2)
        diff = b[:, :, :, None, :] - b[:, :, None, :, :]
        decay = jnp.exp(jnp.where(lower, diff, -jnp.inf))
        scores = jnp.einsum('bhtsk,bhsk->bhts', qc[:, :, :, None, :] * decay, kc)
        o = (jnp.einsum('bhts,bhsv->bhtv', scores, vc)
             + jnp.einsum('bhtk,bhkv->bhtv', qc * jnp.exp(b), state))
        b_end = b[:, :, -1]
        state = (state * jnp.exp(b_end)[..., None]
                 + jnp.einsum('bhsk,bhsv->bhkv', kc * jnp.exp(b_end[:, :, None] - b), vc))
        return state, o

    state0 = jnp.zeros((B, H, K, V), jnp.float32)
    _, o = lax.scan(step, state0, (chunks(q), chunks(k), chunks(log_f), chunks(v)))
    return o.transpose(1, 0, 3, 2, 4).reshape(B, S, H, V)


def hgrn2_bidirectional(q_raw, f_fwd_raw, f_bwd_raw, i_raw, gate, lb_fwd, lb_bwd, norm_g):
    B, S, _ = q_raw.shape

    def heads(t, d):
        return t.reshape(B, S, B_HEADS, d).astype(jnp.float32)

    q = heads(jax.nn.silu(q_raw), B_KDIM)
    v = heads(i_raw, B_VDIM)

    def direction(f_raw, lb, rev):
        f = lb + (1.0 - lb) * jax.nn.sigmoid(f_raw.astype(jnp.float32))
        args = (q, heads(1.0 - f, B_KDIM), heads(jnp.log(f), B_KDIM), v)
        if rev:
            return gla_chunk_scan(*[a[:, ::-1] for a in args])[:, ::-1]
        return gla_chunk_scan(*args)

    o = direction(f_fwd_raw, lb_fwd, False) + direction(f_bwd_raw, lb_bwd, True)
    o = o * lax.rsqrt(jnp.mean(o * o, axis=-1, keepdims=True) + EPS)
    o = o.reshape(B, S, B_WIDTH) * norm_g.astype(jnp.float32)
    o = o * jax.nn.sigmoid(gate.astype(jnp.float32))
    return o.astype(q_raw.dtype)


def t5_bucket(rel):
    half_buckets = REL_BUCKETS // 2
    max_exact = half_buckets // 2
    n = jnp.abs(rel)
    scaled = (jnp.log(jnp.maximum(n, 1).astype(jnp.float32) / max_exact)
              / math.log(REL_MAX_DISTANCE / max_exact))
    large = jnp.minimum(max_exact + (scaled * (half_buckets - max_exact)).astype(jnp.int32),
                        half_buckets - 1)
    return jnp.where(rel > 0, half_buckets, 0) + jnp.where(n < max_exact, n, large)


def dilated_branch(q, k, v, rel_table, dilation, half):
    B, S, H, Dh = q.shape
    n = S // dilation
    nb = -(-n // half)
    pad = nb * half - n

    def strided(t, front, back):
        t = t.reshape(B, n, dilation, H, Dh).transpose(0, 2, 1, 3, 4)
        return jnp.pad(t, ((0, 0), (0, 0), (front, back), (0, 0), (0, 0)))

    qb = strided(q, 0, pad).reshape(B, dilation, nb, half, H, Dh)

    def band(t):
        tb = strided(t, half, half + pad).reshape(B, dilation, nb + 2, half, H, Dh)
        return jnp.concatenate([tb[:, :, :-2], tb[:, :, 1:-1], tb[:, :, 2:]], axis=3)

    kb, vb = band(k), band(v)
    qi = jnp.arange(half)
    kc = jnp.arange(3 * half)
    rel = kc[None, :] - half - qi[:, None]
    bias = rel_table[t5_bucket(rel * dilation)].transpose(2, 0, 1).astype(jnp.float32)
    k_sub = (jnp.arange(nb)[:, None] - 1) * half + kc[None, :]
    valid = (jnp.abs(rel) <= half)[None] & ((k_sub >= 0) & (k_sub < n))[:, None, :]
    logits = jnp.einsum('brnqhc,brnkhc->brnhqk', qb, kb).astype(jnp.float32) + bias
    logits = jnp.where(valid[None, None, :, None], logits, NEG_INF)
    m = jnp.max(logits, axis=-1, keepdims=True)
    p = jnp.exp(logits - m)
    den = jnp.sum(p, axis=-1, keepdims=True)
    o = jnp.einsum('brnhqk,brnkhc->brnqhc', (p / den).astype(v.dtype), vb).astype(jnp.float32)
    lse = (m + jnp.log(den))[..., 0]

    def unstride(t):
        t = t[:, :, :n]
        return jnp.swapaxes(t, 1, 2).reshape((B, S) + t.shape[3:])

    o = unstride(o.reshape(B, dilation, nb * half, H, Dh))
    lse = unstride(lse.transpose(0, 1, 2, 4, 3).reshape(B, dilation, nb * half, H))
    return o, lse


def dilated_mixture(q, k, v, rel_table):
    outs, lses = [], []
    for window, dilation in DILATED_PATTERNS:
        o, lse = dilated_branch(q, k, v, rel_table, dilation, window // (2 * dilation))
        outs.append(o)
        lses.append(lse)
    w = jax.nn.softmax(jnp.stack(lses), axis=0)
    return jnp.einsum('pbsh,pbshc->bshc', w, jnp.stack(outs)).astype(q.dtype)


def expert_choice_moe(x, w_router, w_gate, w_up, w_down):
    B, S, D = x.shape
    cap = max(1, CAPACITY_FACTOR * S // N_EXPERTS)
    logits = jnp.einsum('bsd,de->bse', x, w_router).astype(jnp.float32)
    affinity = jax.nn.softmax(logits, axis=-1)
    g, idx = lax.top_k(jnp.swapaxes(affinity, 1, 2), cap)
    bidx = jnp.arange(B)[:, None, None]
    xe = x[bidx, idx]
    h = (jax.nn.silu(jnp.einsum('becd,edf->becf', xe, w_gate))
         * jnp.einsum('becd,edf->becf', xe, w_up))
    y = jnp.einsum('becf,efd->becd', h, w_down) * g[..., None].astype(x.dtype)
    return jnp.zeros_like(x).at[bidx, idx].add(y)


def setup_inputs(seed: int = 0) -> dict:
    key = jax.random.key(seed)
    ks = jax.random.split(key, 19)
    n_even, n_odd = (DEPTH + 1) // 2, DEPTH // 2
    even_in = 2 * A_WIDTH + 3 * B_QK_WIDTH + 2 * B_WIDTH
    even_out = A_WIDTH + B_WIDTH

    def nrm(k, shape, scale):
        return jax.random.normal(k, shape, jnp.float32) * scale

    return {
        'x': nrm(ks[0], (BATCH, SEQ, D_MODEL), 1.0),
        'mix_norm': 1.0 + nrm(ks[1], (DEPTH, D_MODEL), 0.05),
        'ffn_norm': 1.0 + nrm(ks[2], (DEPTH, D_MODEL), 0.05),
        'final_norm': 1.0 + nrm(ks[3], (D_MODEL,), 0.05),
        'w_in_even': nrm(ks[4], (n_even, D_MODEL, even_in), D_MODEL ** -0.5),
        'w_out_even': nrm(ks[5], (n_even, even_out, D_MODEL), even_out ** -0.5),
        'a_ln_g': 1.0 + nrm(ks[6], (n_even, A_WIDTH), 0.05),
        'a_ln_b': nrm(ks[7], (n_even, A_WIDTH), 0.02),
        'a_w_s': nrm(ks[8], (n_even, A_GROUPS, A_CHUNK, A_CHUNK), A_CHUNK ** -0.5),
        'a_b_s': 1.0 + nrm(ks[9], (n_even, A_GROUPS, A_CHUNK), 0.1),
        'b_lb_table': nrm(ks[10], (2, DEPTH + 1, B_QK_WIDTH), 0.1),
        'b_norm_g': 1.0 + nrm(ks[11], (n_even, B_WIDTH), 0.05),
        'w_qkv_odd': nrm(ks[12], (n_odd, D_MODEL, 3 * C_WIDTH), D_MODEL ** -0.5),
        'w_o_odd': nrm(ks[13], (n_odd, C_WIDTH, D_MODEL), C_WIDTH ** -0.5),
        'rel_bias': nrm(ks[14], (REL_BUCKETS, C_HEADS), 0.5),
        'w_router': nrm(ks[15], (DEPTH, D_MODEL, N_EXPERTS), D_MODEL ** -0.5),
        'w_gate': nrm(ks[16], (DEPTH, N_EXPERTS, D_MODEL, EXPERT_D_FF), D_MODEL ** -0.5),
        'w_up': nrm(ks[17], (DEPTH, N_EXPERTS, D_MODEL, EXPERT_D_FF), D_MODEL ** -0.5),
        'w_down': nrm(ks[18], (DEPTH, N_EXPERTS, EXPERT_D_FF, D_MODEL), EXPERT_D_FF ** -0.5),
    }


def reference(x, mix_norm, ffn_norm, final_norm, w_in_even, w_out_even, a_ln_g, a_ln_b,
              a_w_s, a_b_s, b_lb_table, b_norm_g, w_qkv_odd, w_o_odd, rel_bias,
              w_router, w_gate, w_up, w_down):
    B, S, _ = x.shape
    sizes = (A_WIDTH, A_WIDTH, B_QK_WIDTH, B_QK_WIDTH, B_QK_WIDTH, B_WIDTH, B_WIDTH)
    cuts = [sum(sizes[:j]) for j in range(1, len(sizes))]
    lower_bounds = jnp.cumsum(jax.nn.softmax(b_lb_table.astype(jnp.float32), axis=1), axis=1)
    for layer in range(DEPTH):
        h = rmsnorm(x, mix_norm[layer])
        if layer % 2 == 0:
            e = layer // 2
            proj = jnp.einsum('bsd,dn->bsn', h, w_in_even[e])
            u, v, q_b, f_fwd, f_bwd, i_b, g_b = jnp.split(proj, cuts, axis=-1)
            a_out = chunked_spatial_gating(jax.nn.gelu(u), jax.nn.gelu(v), a_w_s[e], a_b_s[e],
                                           a_ln_g[e], a_ln_b[e])
            b_out = hgrn2_bidirectional(q_b, f_fwd, f_bwd, i_b, g_b, lower_bounds[0, layer],
                                        lower_bounds[1, layer], b_norm_g[e])
            mixed = jnp.einsum('bsn,nd->bsd', jnp.concatenate([a_out, b_out], axis=-1), w_out_even[e])
        else:
            o = layer // 2
            qkv = jnp.einsum('bsd,dn->bsn', h, w_qkv_odd[o]).reshape(B, S, 3, C_HEADS, C_HEAD_DIM)
            q = qkv[:, :, 0] * (C_HEAD_DIM ** -0.5)
            attn = dilated_mixture(q, qkv[:, :, 1], qkv[:, :, 2], rel_bias)
            mixed = jnp.einsum('bsn,nd->bsd', attn.reshape(B, S, C_WIDTH), w_o_odd[o])
        x = x + mixed
        x = x + expert_choice_moe(rmsnorm(x, ffn_norm[layer]), w_router[layer], w_gate[layer],
                                  w_up[layer], w_down[layer])
    return rmsnorm(x, final_norm)
```

```python
import functools
import math

import numpy as np
import jax
import jax.numpy as jnp
from jax import lax
from jax.experimental import pallas as pl
from jax.experimental.pallas import tpu as pltpu

F32 = jnp.float32
BF16 = jnp.bfloat16
I32 = jnp.int32
EPS = 1e-6
NEG_INF = -1e30
HIGHEST = lax.Precision.HIGHEST

LANES = 128
VMEM_LIMIT = 56 * 1024 * 1024

A_GROUPS = 4
A_CHUNK = 128
B_HEADS = 4
B_DIM = 128
GLA_CHUNK = 64
C_HEADS = 16
C_HEAD_DIM = 64
HALF_WINDOW = 64
DILATIONS = (1, 4, 16)
REL_BUCKETS = 32
REL_MAX_DISTANCE = 1024
N_EXPERTS = 16
CAPACITY_FACTOR = 2
TOKEN_CHUNK = 128
SLOT_WINDOW = 256


def _params(*sem):
    return pltpu.CompilerParams(dimension_semantics=sem, vmem_limit_bytes=VMEM_LIMIT)


def _dot(a, b, **kw):
    return jnp.dot(a, b, preferred_element_type=F32, **kw)


def _dot_nt(a, b):
    return lax.dot_general(a, b, (((1,), (1,)), ((), ())), preferred_element_type=F32)


def _dot_tn(a, b):
    return lax.dot_general(a, b, (((0,), (0,)), ((), ())), preferred_element_type=F32)


def _rms(x, g):
    return x * lax.rsqrt(jnp.mean(x * x, axis=-1, keepdims=True) + EPS) * g


def _norm_matmul_kernel(x_ref, g_ref, w_ref, o_ref, h_ref):
    @pl.when(pl.program_id(1) == 0)
    def _():
        h_ref[...] = _rms(x_ref[...], g_ref[...]).astype(BF16)

    o_ref[...] = _dot(h_ref[...], w_ref[...]).astype(o_ref.dtype)


def norm_matmul(x, g, w, tm=512, tn=512):
    T, D = x.shape
    N = w.shape[1]
    return pl.pallas_call(
        _norm_matmul_kernel,
        grid=(T // tm, N // tn),
        in_specs=[
            pl.BlockSpec((tm, D), lambda i, j: (i, 0)),
            pl.BlockSpec((1, D), lambda i, j: (0, 0)),
            pl.BlockSpec((D, tn), lambda i, j: (0, j)),
        ],
        out_specs=pl.BlockSpec((tm, tn), lambda i, j: (i, j)),
        out_shape=jax.ShapeDtypeStruct((T, N), BF16),
        scratch_shapes=[pltpu.VMEM((tm, D), BF16)],
        compiler_params=_params("parallel", "arbitrary"),
        name="norm_matmul",
    )(x, g.reshape(1, D), w)


def _mixa_kernel(u_ref, v_ref, lg_ref, lb_ref, ws_ref, bs_ref, o_ref):
    tm = u_ref.shape[0]
    u = jax.nn.gelu(u_ref[...].astype(F32))
    v = jax.nn.gelu(v_ref[...].astype(F32))
    mu = jnp.mean(v, axis=-1, keepdims=True)
    vc = v - mu
    vn = vc * lax.rsqrt(jnp.mean(vc * vc, axis=-1, keepdims=True) + EPS)
    vb = (vn * lg_ref[...] + lb_ref[...]).astype(BF16)
    for n in range(tm // A_CHUNK):
        rows = slice(n * A_CHUNK, (n + 1) * A_CHUNK)
        for g in range(A_GROUPS):
            cols = slice(g * LANES, (g + 1) * LANES)
            mixed = _dot(ws_ref[g], vb[rows, cols]) + bs_ref[:, g:g + 1]
            o_ref[rows, cols] = (u[rows, cols] * mixed).astype(BF16)


def mixer_a(proj, ln_g, ln_b, w_s, b_s, tm=512):
    T = proj.shape[0]
    W = A_GROUPS * LANES
    return pl.pallas_call(
        _mixa_kernel,
        grid=(T // tm,),
        in_specs=[
            pl.BlockSpec((tm, W), lambda i: (i, 0)),
            pl.BlockSpec((tm, W), lambda i: (i, 1)),
            pl.BlockSpec((1, W), lambda i: (0, 0)),
            pl.BlockSpec((1, W), lambda i: (0, 0)),
            pl.BlockSpec((A_GROUPS, A_CHUNK, A_CHUNK), lambda i: (0, 0, 0)),
            pl.BlockSpec((A_CHUNK, A_GROUPS), lambda i: (0, 0)),
        ],
        out_specs=pl.BlockSpec((tm, W), lambda i: (i, 0)),
        out_shape=jax.ShapeDtypeStruct((T, W), BF16),
        compiler_params=_params("parallel"),
        name="mixer_a",
    )(proj, proj, ln_g.reshape(1, W), ln_b.reshape(1, W), w_s.astype(BF16), b_s.T)


def _gla_consts(C, reverse):
    t = np.arange(C)
    tri = (t[None, :] >= t[:, None]) if reverse else (t[None, :] <= t[:, None])
    L = int(round(math.log2(C)))
    gsel = np.zeros((L, C, C), np.float32)
    masks = np.zeros((L + 1, C, C), np.float32)
    for l in range(L):
        bs = (t >> (l + 1)) << (l + 1)
        mid = bs + (1 << l)
        ref = mid if reverse else mid - 1
        gsel[l, t, ref] = 1.0
        act_q = (t < mid) if reverse else (t >= mid)
        same = bs[:, None] == bs[None, :]
        masks[l] = same & act_q[:, None] & (~act_q)[None, :]
    masks[L] = np.eye(C)
    return tri.astype(np.float32), gsel.reshape(L * C, C), masks


def _gla_direction(q, fraw, v, lb, tri, gsel, masks_ref, state_ref, d, reverse):
    C = q.shape[0]
    L = masks_ref.shape[1] - 1
    f = lb + (1.0 - lb) * jax.nn.sigmoid(fraw)
    k = 1.0 - f
    cum = _dot(tri, jnp.log(f), precision=HIGHEST)
    refs = _dot(gsel, cum, precision=HIGHEST)
    tot = cum[0:1] if reverse else cum[C - 1:C]
    outs = []
    for h in range(B_HEADS):
        sl = slice(h * B_DIM, (h + 1) * B_DIM)
        qh, kh, ch = q[:, sl], k[:, sl], cum[:, sl]
        vb = v[:, sl].astype(BF16)
        p = masks_ref[d, L] * _dot_nt(qh.astype(BF16), kh.astype(BF16))
        for l in range(L):
            r = refs[l * C:(l + 1) * C, sl]
            qa = qh * jnp.exp(jnp.minimum(ch - r, 0.0))
            ka = kh * jnp.exp(jnp.minimum(r - ch, 0.0))
            p = p + masks_ref[d, l] * _dot_nt(qa.astype(BF16), ka.astype(BF16))
        st = state_ref[d, h]
        o = _dot(p.astype(BF16), vb) + _dot_nt((qh * jnp.exp(ch)).astype(BF16), st.astype(BF16))
        ke = kh * jnp.exp(tot[:, sl] - ch)
        state_ref[d, h] = st * jnp.exp(tot[:, sl]) + _dot_tn(vb, ke.astype(BF16))
        outs.append(o)
    return jnp.concatenate(outs, axis=-1)


def _gla_kernel(qf_ref, ff_ref, if_ref, qb_ref, fb_ref, ib_ref, tbl_ref, tri_ref, gsel_ref, masks_ref,
                of_ref, ob_ref, state_ref, *, layer):
    @pl.when(pl.program_id(1) == 0)
    def _():
        state_ref[...] = jnp.zeros_like(state_ref)

    for d, (q_ref, f_ref, i_ref, o_ref) in enumerate(
            ((qf_ref, ff_ref, if_ref, of_ref), (qb_ref, fb_ref, ib_ref, ob_ref))):
        tb = tbl_ref[d]
        e = jnp.exp(tb - jnp.max(tb, axis=0, keepdims=True))
        lb = jnp.sum(e[0:layer + 1], axis=0, keepdims=True) / jnp.sum(e, axis=0, keepdims=True)
        o_ref[...] = _gla_direction(
            jax.nn.silu(q_ref[...].astype(F32)), f_ref[...].astype(F32), i_ref[...].astype(F32), lb,
            tri_ref[d], gsel_ref[d], masks_ref, state_ref, d, reverse=(d == 1))


def mixer_b(proj, b_lb_table, layer, batch, C=GLA_CHUNK):
    T = proj.shape[0]
    W = B_HEADS * B_DIM
    n = T // batch // C
    consts = [_gla_consts(C, rev) for rev in (False, True)]
    tri = jnp.asarray(np.stack([c[0] for c in consts]))
    gsel = jnp.asarray(np.stack([c[1] for c in consts]))
    masks = jnp.asarray(np.stack([c[2] for c in consts]))
    fwd = lambda col: pl.BlockSpec((C, W), lambda b, c: (b * n + c, col))
    bwd = lambda col: pl.BlockSpec((C, W), lambda b, c: (b * n + n - 1 - c, col))
    full = lambda a: pl.BlockSpec(a.shape, lambda b, c: (0,) * a.ndim)
    return pl.pallas_call(
        functools.partial(_gla_kernel, layer=layer),
        grid=(batch, n),
        in_specs=[fwd(2), fwd(3), fwd(5), bwd(2), bwd(4), bwd(5),
                  full(b_lb_table), full(tri), full(gsel), full(masks)],
        out_specs=[pl.BlockSpec((C, W), lambda b, c: (b * n + c, 0)),
                   pl.BlockSpec((C, W), lambda b, c: (b * n + n - 1 - c, 0))],
        out_shape=[jax.ShapeDtypeStruct((T, W), F32)] * 2,
        scratch_shapes=[pltpu.VMEM((2, B_HEADS, B_DIM, B_DIM), F32)],
        compiler_params=_params("arbitrary", "arbitrary"),
        name="mixer_b",
    )(proj, proj, proj, proj, proj, proj, b_lb_table, tri, gsel, masks)


def _router_epilogue(x_new, g_ref, wr_ref, x_ref, h_ref, aff_ref):
    x_ref[...] = x_new
    h = _rms(x_new, g_ref[...])
    h_ref[...] = h.astype(BF16)
    logits = _dot(h, wr_ref[...], precision=HIGHEST)
    e = jnp.exp(logits - jnp.max(logits, axis=-1, keepdims=True))
    aff_ref[...] = e / jnp.sum(e, axis=-1, keepdims=True)


def _outproj_even_kernel(a_ref, of_ref, ob_ref, gate_ref, bng_ref, w_ref, x_ref, g_ref, wr_ref,
                         xo_ref, h_ref, aff_ref):
    o = of_ref[...] + ob_ref[...]
    parts = []
    for h in range(B_HEADS):
        oh = o[:, h * B_DIM:(h + 1) * B_DIM]
        parts.append(oh * lax.rsqrt(jnp.mean(oh * oh, axis=-1, keepdims=True) + EPS))
    on = jnp.concatenate(parts, axis=-1) * bng_ref[...] * jax.nn.sigmoid(gate_ref[...].astype(F32))
    wa = a_ref.shape[1]
    mixed = _dot(a_ref[...], w_ref[0:wa, :]) + _dot(on.astype(BF16), w_ref[wa:, :])
    _router_epilogue(x_ref[...] + mixed, g_ref, wr_ref, xo_ref, h_ref, aff_ref)


def _router_out(T, D, tm):
    specs = [pl.BlockSpec((tm, D), lambda i: (i, 0)), pl.BlockSpec((tm, D), lambda i: (i, 0)),
             pl.BlockSpec((tm, N_EXPERTS), lambda i: (i, 0))]
    shapes = [jax.ShapeDtypeStruct((T, D), F32), jax.ShapeDtypeStruct((T, D), BF16),
              jax.ShapeDtypeStruct((T, N_EXPERTS), F32)]
    return specs, shapes


def outproj_even(a_out, o_f, o_b, proj, b_norm_g, w_out, x, ffn_g, w_router, tm=256):
    T, D = x.shape
    W = a_out.shape[1]
    row = lambda w, col=0: pl.BlockSpec((tm, w), lambda i: (i, col))
    full = lambda s: pl.BlockSpec(s, lambda i: (0, 0))
    out_specs, out_shapes = _router_out(T, D, tm)
    return pl.pallas_call(
        _outproj_even_kernel,
        grid=(T // tm,),
        in_specs=[row(W), row(W), row(W), row(W, 6), full((1, W)), full(w_out.shape), row(D),
                  full((1, D)), full(w_router.shape)],
        out_specs=out_specs,
        out_shape=out_shapes,
        compiler_params=_params("parallel"),
        name="outproj_even",
    )(a_out, o_f, o_b, proj, b_norm_g.reshape(1, W), w_out.astype(BF16), x, ffn_g.reshape(1, D), w_router)


def _outproj_odd_kernel(o1_ref, o2_ref, o3_ref, l1_ref, l2_ref, l3_ref, ex_ref, w_ref, x_ref, g_ref, wr_ref,
                        xo_ref, h_ref, aff_ref):
    ls = [l1_ref[...], l2_ref[...], l3_ref[...]]
    m = jnp.maximum(jnp.maximum(ls[0], ls[1]), ls[2])
    es = [jnp.exp(l - m) for l in ls]
    den = es[0] + es[1] + es[2]
    attn = None
    for e, o_ref in zip(es, (o1_ref, o2_ref, o3_ref)):
        wfull = _dot((e / den).astype(BF16), ex_ref[...])
        term = wfull * o_ref[...].astype(F32)
        attn = term if attn is None else attn + term
    mixed = _dot(attn.astype(BF16), w_ref[...])
    _router_epilogue(x_ref[...] + mixed, g_ref, wr_ref, xo_ref, h_ref, aff_ref)


def outproj_odd(outs, lses, w_o, x, ffn_g, w_router, tm=256):
    T, D = x.shape
    expand = jnp.asarray(np.kron(np.eye(C_HEADS), np.ones((1, C_HEAD_DIM))), BF16)
    row = lambda w: pl.BlockSpec((tm, w), lambda i: (i, 0))
    full = lambda s: pl.BlockSpec(s, lambda i: (0, 0))
    out_specs, out_shapes = _router_out(T, D, tm)
    return pl.pallas_call(
        _outproj_odd_kernel,
        grid=(T // tm,),
        in_specs=[row(D)] * 3 + [row(C_HEADS)] * 3 + [full(expand.shape), full(w_o.shape), row(D),
                                                      full((1, D)), full(w_router.shape)],
        out_specs=out_specs,
        out_shape=out_shapes,
        compiler_params=_params("parallel"),
        name="outproj_odd",
    )(*outs, *lses, expand, w_o.astype(BF16), x, ffn_g.reshape(1, D), w_router)


def _topk_kernel(aff_ref, upper_ref, strict_ref, pos_ref, start_ref, *, cap):
    a = aff_ref[0]
    E, NC, _ = a.shape
    bits = lax.bitcast_convert_type(a, I32)

    def count(mask):
        return jnp.sum(jnp.sum(mask.astype(F32), axis=2, keepdims=True), axis=1, keepdims=True)

    def search(i, thr):
        cand = thr | jnp.left_shift(jnp.int32(1), 30 - i)
        return jnp.where(count(bits >= cand) >= cap, cand, thr)

    thr = lax.fori_loop(0, 31, search, jnp.zeros((E, 1, 1), I32))
    gt = bits > thr
    eq = bits == thr
    need = cap - count(gt)

    def prefix(mask):
        m2 = mask.astype(BF16).reshape(E * NC, LANES)
        within = _dot(m2, upper_ref[...])
        total = within[:, LANES - 1:LANES].astype(BF16)
        tot_b = jnp.broadcast_to(total, (E * NC, LANES))
        starts = jnp.concatenate(
            [_dot(strict_ref[...], tot_b[e * NC:(e + 1) * NC]) for e in range(E)], axis=0)
        return (within + starts).reshape(E, NC, LANES), starts.reshape(E, NC, LANES)

    eq_incl, _ = prefix(eq)
    sel = gt | (eq & (eq_incl - 1.0 < need))
    sel_incl, starts = prefix(sel)
    pos_ref[0] = jnp.where(sel, sel_incl - 1.0, -1.0).astype(I32)
    start_ref[0] = starts.astype(I32)


def route_topk(aff_t, cap):
    B, E, S = aff_t.shape
    NC = S // TOKEN_CHUNK
    t = np.arange(LANES)
    upper = jnp.asarray(t[:, None] <= t[None, :], BF16)
    c = np.arange(NC)
    strict = jnp.asarray(c[None, :] < c[:, None], BF16)
    blk = pl.BlockSpec((1, E, NC, LANES), lambda b: (b, 0, 0, 0))
    pos, starts = pl.pallas_call(
        functools.partial(_topk_kernel, cap=cap),
        grid=(B,),
        in_specs=[blk, pl.BlockSpec(upper.shape, lambda b: (0, 0)), pl.BlockSpec(strict.shape, lambda b: (0, 0))],
        out_specs=[blk, blk],
        out_shape=[jax.ShapeDtypeStruct((B, E, NC, LANES), I32)] * 2,
        compiler_params=_params("parallel"),
        name="route_topk",
    )(aff_t.reshape(B, E, NC, LANES), upper, strict)
    return pos, starts[..., 0].reshape(-1)


def _window_base(start, cap):
    base = jnp.minimum((start // TOKEN_CHUNK) * TOKEN_CHUNK, cap - SLOT_WINDOW)
    return pl.multiple_of(base, TOKEN_CHUNK)


def _gather_kernel(starts_ref, pos_ref, h_ref, xe_ref, acc_ref):
    b, e = pl.program_id(0), pl.program_id(1)
    n_exp = pl.num_programs(1)
    cap = acc_ref.shape[0]
    NC = pos_ref.shape[2]
    acc_ref[...] = jnp.zeros_like(acc_ref)
    slot = lax.broadcasted_iota(I32, (SLOT_WINDOW, TOKEN_CHUNK), 0)

    def body(c, carry):
        base = _window_base(starts_ref[(b * n_exp + e) * NC + c], cap)
        onehot = jnp.where(slot + base == pos_ref[0, 0, pl.ds(c, 1), :], 1.0, 0.0).astype(BF16)
        tok = h_ref[0, pl.ds(pl.multiple_of(c * TOKEN_CHUNK, TOKEN_CHUNK), TOKEN_CHUNK), :]
        acc_ref[pl.ds(base, SLOT_WINDOW), :] += _dot(onehot, tok)
        return carry

    lax.fori_loop(0, NC, body, 0)
    xe_ref[0, 0] = acc_ref[...].astype(BF16)


def moe_gather(starts, pos, h, cap):
    B, S, D = h.shape
    E, NC = pos.shape[1], pos.shape[2]
    return pl.pallas_call(
        _gather_kernel,
        grid_spec=pltpu.PrefetchScalarGridSpec(
            num_scalar_prefetch=1,
            grid=(B, E),
            in_specs=[pl.BlockSpec((1, 1, NC, LANES), lambda b, e, s: (b, e, 0, 0)),
                      pl.BlockSpec((1, S, D), lambda b, e, s: (b, 0, 0))],
            out_specs=pl.BlockSpec((1, 1, cap, D), lambda b, e, s: (b, e, 0, 0)),
            scratch_shapes=[pltpu.VMEM((cap, D), F32)]),
        out_shape=jax.ShapeDtypeStruct((B, E, cap, D), BF16),
        compiler_params=_params("arbitrary", "arbitrary"),
        name="moe_gather",
    )(starts, pos, h)


def _ffn_kernel(xe_ref, wg_ref, wu_ref, wd_ref, y_ref, wgb_ref, wub_ref, wdb_ref, *, tr):
    @pl.when(pl.program_id(1) == 0)
    def _():
        wgb_ref[...] = wg_ref[0, 0].astype(BF16)
        wub_ref[...] = wu_ref[0, 0].astype(BF16)
        wdb_ref[...] = wd_ref[0, 0].astype(BF16)

    for r in range(xe_ref.shape[2] // tr):
        rows = slice(r * tr, (r + 1) * tr)
        x = xe_ref[0, 0, rows, :]
        g = _dot(x, wgb_ref[...])
        u = _dot(x, wub_ref[...])
        mid = (g * jax.nn.sigmoid(g) * u).astype(BF16)
        y_ref[0, 0, rows, :] = _dot(mid, wdb_ref[...]).astype(BF16)


def moe_ffn(xe, w_gate, w_up, w_down, layer, tr=256):
    B, E, cap, D = xe.shape
    F = w_gate.shape[-1]
    tok = pl.BlockSpec((1, 1, cap, D), lambda e, b: (b, e, 0, 0))
    wspec = lambda r, c: pl.BlockSpec((1, 1, r, c), lambda e, b: (layer, e, 0, 0))
    return pl.pallas_call(
        functools.partial(_ffn_kernel, tr=min(tr, cap)),
        grid=(E, B),
        in_specs=[tok, wspec(D, F), wspec(D, F), wspec(F, D)],
        out_specs=tok,
        out_shape=jax.ShapeDtypeStruct((B, E, cap, D), BF16),
        scratch_shapes=[pltpu.VMEM((D, F), BF16), pltpu.VMEM((D, F), BF16), pltpu.VMEM((F, D), BF16)],
        compiler_params=_params("arbitrary", "arbitrary"),
        name="moe_ffn",
    )(xe, w_gate, w_up, w_down)


def _combine_kernel(starts_ref, post_ref, gate_ref, x_ref, y_ref, o_ref):
    b, c = pl.program_id(0), pl.program_id(2)
    NC = pl.num_programs(2)
    n_exp, cap = y_ref.shape[1], y_ref.shape[2]
    slot = lax.broadcasted_iota(I32, (TOKEN_CHUNK, SLOT_WINDOW), 1)
    acc = x_ref[0]
    for e in range(n_exp):
        base = _window_base(starts_ref[(b * n_exp + e) * NC + c], cap)
        onehot = jnp.where(slot + base == post_ref[0, :, e:e + 1], 1.0, 0.0).astype(BF16)
        acc = acc + _dot(onehot, y_ref[0, e, pl.ds(base, SLOT_WINDOW), :]) * gate_ref[0, :, e:e + 1]
    o_ref[0] = acc


def moe_combine(starts, pos_t, gate, x, y, dsplit=2):
    B, S, D = x.shape
    E, cap = y.shape[1], y.shape[2]
    dh = D // dsplit
    tok = lambda w: pl.BlockSpec((1, TOKEN_CHUNK, w), lambda b, j, c, s: (b, c, 0))
    res = pl.BlockSpec((1, TOKEN_CHUNK, dh), lambda b, j, c, s: (b, c, j))
    return pl.pallas_call(
        _combine_kernel,
        grid_spec=pltpu.PrefetchScalarGridSpec(
            num_scalar_prefetch=1,
            grid=(B, dsplit, S // TOKEN_CHUNK),
            in_specs=[tok(E), tok(E), res,
                      pl.BlockSpec((1, E, cap, dh), lambda b, j, c, s: (b, 0, 0, j))],
            out_specs=res),
        out_shape=jax.ShapeDtypeStruct((B, S, D), F32),
        compiler_params=_params("arbitrary", "arbitrary", "arbitrary"),
        name="moe_combine",
    )(starts, pos_t, gate, x, y)


def expert_choice_moe(x, h, aff, batch, w_gate, w_up, w_down, layer):
    T, D = x.shape
    S = T // batch
    cap = max(1, CAPACITY_FACTOR * S // N_EXPERTS)
    aff = aff.reshape(batch, S, N_EXPERTS)
    pos, starts = route_topk(jnp.swapaxes(aff, 1, 2), cap)
    xe = moe_gather(starts, pos, h.reshape(batch, S, D), cap)
    y = moe_ffn(xe, w_gate, w_up, w_down, layer)
    pos_t = jnp.swapaxes(pos.reshape(batch, N_EXPERTS, S), 1, 2)
    return moe_combine(starts, pos_t, aff, x.reshape(batch, S, D), y).reshape(T, D)


ATTN_TQ = 128
ATTN_TK = ATTN_TQ + 2 * HALF_WINDOW


def _t5_bucket(rel):
    half_buckets = REL_BUCKETS // 2
    max_exact = half_buckets // 2
    n = jnp.abs(rel)
    scaled = (jnp.log(jnp.maximum(n, 1).astype(jnp.float32) / max_exact)
              / math.log(REL_MAX_DISTANCE / max_exact))
    large = jnp.minimum(max_exact + (scaled * (half_buckets - max_exact)).astype(jnp.int32),
                        half_buckets - 1)
    return jnp.where(rel > 0, half_buckets, 0) + jnp.where(n < max_exact, n, large)


def _bias_kernel(table_ref, bucket_ref, o_ref):
    h = pl.program_id(1)
    bucket = bucket_ref[0]
    acc = jnp.zeros(bucket.shape, F32)
    for bk in range(REL_BUCKETS):
        acc = jnp.where(bucket == bk, table_ref[bk * C_HEADS + h], acc)
    q = lax.broadcasted_iota(I32, bucket.shape, 0)
    kc = lax.broadcasted_iota(I32, bucket.shape, 1)
    o_ref[0, 0] = jnp.where(jnp.abs(kc - HALF_WINDOW - q) <= HALF_WINDOW, acc, NEG_INF)


def attention_bias(rel_bias):
    rel = np.arange(ATTN_TK)[None, :] - HALF_WINDOW - np.arange(ATTN_TQ)[:, None]
    buckets = jnp.stack([_t5_bucket(jnp.asarray(rel * d, I32)) for d in DILATIONS]).astype(I32)
    P = len(DILATIONS)
    return pl.pallas_call(
        _bias_kernel,
        grid_spec=pltpu.PrefetchScalarGridSpec(
            num_scalar_prefetch=1,
            grid=(P, C_HEADS),
            in_specs=[pl.BlockSpec((1, ATTN_TQ, ATTN_TK), lambda p, h, t: (p, 0, 0))],
            out_specs=pl.BlockSpec((1, 1, ATTN_TQ, ATTN_TK), lambda p, h, t: (p, h, 0, 0))),
        out_shape=jax.ShapeDtypeStruct((P, C_HEADS, ATTN_TQ, ATTN_TK), F32),
        compiler_params=_params("arbitrary", "arbitrary"),
        name="attention_bias",
    )(rel_bias.reshape(-1), buckets)


def _attn_kernel(q_ref, kp_ref, km_ref, kn_ref, vp_ref, vm_ref, vn_ref, bias_ref, o_ref, lse_ref, k_buf, v_buf):
    i = pl.program_id(1)
    last = pl.num_programs(1) - 1
    hw = HALF_WINDOW
    k_buf[0:hw] = kp_ref[0]
    k_buf[hw:hw + ATTN_TQ] = km_ref[0]
    k_buf[hw + ATTN_TQ:] = kn_ref[0]
    v_buf[0:hw] = vp_ref[0]
    v_buf[hw:hw + ATTN_TQ] = vm_ref[0]
    v_buf[hw + ATTN_TQ:] = vn_ref[0]
    kc = lax.broadcasted_iota(I32, (ATTN_TQ, ATTN_TK), 1)
    in_seq = jnp.logical_and(jnp.logical_or(i > 0, kc >= hw), jnp.logical_or(i < last, kc < hw + ATTN_TQ))
    scale = C_HEAD_DIM ** -0.5
    for h in range(C_HEADS):
        cols = slice(h * C_HEAD_DIM, (h + 1) * C_HEAD_DIM)
        s = _dot_nt(q_ref[0, :, cols] * scale, k_buf[:, cols]) + bias_ref[h]
        s = jnp.where(in_seq, s, NEG_INF)
        m = jnp.max(s, axis=-1, keepdims=True)
        p = jnp.exp(s - m)
        den = jnp.sum(p, axis=-1, keepdims=True)
        o_ref[0, :, cols] = _dot((p / den).astype(BF16), v_buf[:, cols]).astype(BF16)
        lse_ref[0, :, h:h + 1] = m + jnp.log(den)


def dilated_attention(qkv, bias):
    G, n, D3 = qkv.shape
    D = D3 // 3
    nb = n // HALF_WINDOW
    r = ATTN_TQ // HALF_WINDOW
    main = lambda col: pl.BlockSpec((1, ATTN_TQ, D), lambda g, i: (g, i, col))
    prev = lambda col: pl.BlockSpec((1, HALF_WINDOW, D), lambda g, i: (g, jnp.maximum(i * r - 1, 0), col))
    nxt = lambda col: pl.BlockSpec((1, HALF_WINDOW, D), lambda g, i: (g, jnp.minimum(i * r + r, nb - 1), col))
    return pl.pallas_call(
        _attn_kernel,
        grid=(G, n // ATTN_TQ),
        in_specs=[main(0), prev(1), main(1), nxt(1), prev(2), main(2), nxt(2),
                  pl.BlockSpec(bias.shape, lambda g, i: (0, 0, 0))],
        out_specs=[pl.BlockSpec((1, ATTN_TQ, D), lambda g, i: (g, i, 0)),
                   pl.BlockSpec((1, ATTN_TQ, C_HEADS), lambda g, i: (g, i, 0))],
        out_shape=[jax.ShapeDtypeStruct((G, n, D), BF16), jax.ShapeDtypeStruct((G, n, C_HEADS), F32)],
        scratch_shapes=[pltpu.VMEM((ATTN_TK, D), BF16), pltpu.VMEM((ATTN_TK, D), BF16)],
        compiler_params=_params("parallel", "parallel"),
        name="dilated_attention",
    )(qkv, qkv, qkv, qkv, qkv, qkv, qkv, bias)


def dilated_mixture(qkv, rel_bias, batch):
    T, D3 = qkv.shape
    S = T // batch
    bias = attention_bias(rel_bias)
    outs, lses = [], []
    for p, d in enumerate(DILATIONS):
        n = S // d
        strided = qkv.reshape(batch, n, d, D3).swapaxes(1, 2).reshape(batch * d, n, D3)
        o, lse = dilated_attention(strided, bias[p])
        unstride = lambda t: t.reshape(batch, d, n, t.shape[-1]).swapaxes(1, 2).reshape(T, t.shape[-1])
        outs.append(unstride(o))
        lses.append(unstride(lse))
    return outs, lses


def _final_norm_kernel(x_ref, g_ref, o_ref):
    o_ref[...] = _rms(x_ref[...], g_ref[...])


def final_rmsnorm(x, g, tm=512):
    T, D = x.shape
    return pl.pallas_call(
        _final_norm_kernel,
        grid=(T // tm,),
        in_specs=[pl.BlockSpec((tm, D), lambda i: (i, 0)), pl.BlockSpec((1, D), lambda i: (0, 0))],
        out_specs=pl.BlockSpec((tm, D), lambda i: (i, 0)),
        out_shape=jax.ShapeDtypeStruct((T, D), F32),
        compiler_params=_params("parallel"),
        name="final_norm",
    )(x, g.reshape(1, D))


def kernel(x, mix_norm, ffn_norm, final_norm, w_in_even, w_out_even, a_ln_g, a_ln_b, a_w_s, a_b_s, b_lb_table,
           b_norm_g, w_qkv_odd, w_o_odd, rel_bias, w_router, w_gate, w_up, w_down):
    B, S, D = x.shape
    depth = mix_norm.shape[0]
    xt = x.reshape(B * S, D)
    for layer in range(depth):
        j = layer // 2
        if layer % 2 == 0:
            proj = norm_matmul(xt, mix_norm[layer], w_in_even[j].astype(BF16))
            a_out = mixer_a(proj, a_ln_g[j], a_ln_b[j], a_w_s[j], a_b_s[j])
            o_f, o_b = mixer_b(proj, b_lb_table, layer, B)
            xt, h, aff = outproj_even(a_out, o_f, o_b, proj, b_norm_g[j], w_out_even[j], xt,
                                      ffn_norm[layer], w_router[layer])
        else:
            qkv = norm_matmul(xt, mix_norm[layer], w_qkv_odd[j].astype(BF16))
            outs, lses = dilated_mixture(qkv, rel_bias, B)
            xt, h, aff = outproj_odd(outs, lses, w_o_odd[j], xt, ffn_norm[layer], w_router[layer])
        xt = expert_choice_moe(xt, h, aff, B, w_gate, w_up, w_down, layer)
    return final_rmsnorm(xt, final_norm).reshape(B, S, D)
```

```python
import functools
import math

import numpy as np
import jax
import jax.numpy as jnp
from jax import lax
from jax.experimental import pallas as pl
from jax.experimental.pallas import tpu as pltpu

F32 = jnp.float32
BF16 = jnp.bfloat16
I32 = jnp.int32
EPS = 1e-6
NEG_INF = -1e30
HIGHEST = lax.Precision.HIGHEST

LANES = 128
VMEM_LIMIT = 56 * 1024 * 1024

A_GROUPS = 4
A_CHUNK = 128
B_HEADS = 4
B_DIM = 128
GLA_CHUNK = 64
C_HEADS = 16
C_HEAD_DIM = 64
HALF_WINDOW = 64
DILATIONS = (1, 4, 16)
REL_BUCKETS = 32
REL_MAX_DISTANCE = 1024
N_EXPERTS = 16
CAPACITY_FACTOR = 2
TOKEN_CHUNK = 128
SLOT_WINDOW = 256


def _params(*sem):
    return pltpu.CompilerParams(dimension_semantics=sem, vmem_limit_bytes=VMEM_LIMIT)


def _dot(a, b, **kw):
    return jnp.dot(a, b, preferred_element_type=F32, **kw)


def _dot_nt(a, b):
    return lax.dot_general(a, b, (((1,), (1,)), ((), ())), preferred_element_type=F32)


def _dot_tn(a, b):
    return lax.dot_general(a, b, (((0,), (0,)), ((), ())), preferred_element_type=F32)


def _rms(x, g):
    return x * lax.rsqrt(jnp.mean(x * x, axis=-1, keepdims=True) + EPS) * g


def _norm_matmul_kernel(x_ref, g_ref, w_ref, o_ref, h_ref):
    @pl.when(pl.program_id(1) == 0)
    def _():
        h_ref[...] = _rms(x_ref[...], g_ref[...]).astype(BF16)

    o_ref[...] = _dot(h_ref[...], w_ref[...]).astype(o_ref.dtype)


def norm_matmul(x, g, w, tm=512, tn=512):
    T, D = x.shape
    N = w.shape[1]
    return pl.pallas_call(
        _norm_matmul_kernel,
        grid=(T // tm, N // tn),
        in_specs=[
            pl.BlockSpec((tm, D), lambda i, j: (i, 0)),
            pl.BlockSpec((1, D), lambda i, j: (0, 0)),
            pl.BlockSpec((D, tn), lambda i, j: (0, j)),
        ],
        out_specs=pl.BlockSpec((tm, tn), lambda i, j: (i, j)),
        out_shape=jax.ShapeDtypeStruct((T, N), BF16),
        scratch_shapes=[pltpu.VMEM((tm, D), BF16)],
        compiler_params=_params("parallel", "arbitrary"),
        name="norm_matmul",
    )(x, g.reshape(1, D), w)


def _mixa_kernel(u_ref, v_ref, lg_ref, lb_ref, ws_ref, bs_ref, o_ref):
    tm = u_ref.shape[0]
    u = jax.nn.gelu(u_ref[...].astype(F32))
    v = jax.nn.gelu(v_ref[...].astype(F32))
    mu = jnp.mean(v, axis=-1, keepdims=True)
    vc = v - mu
    vn = vc * lax.rsqrt(jnp.mean(vc * vc, axis=-1, keepdims=True) + EPS)
    vb = (vn * lg_ref[...] + lb_ref[...]).astype(BF16)
    for n in range(tm // A_CHUNK):
        rows = slice(n * A_CHUNK, (n + 1) * A_CHUNK)
        for g in range(A_GROUPS):
            cols = slice(g * LANES, (g + 1) * LANES)
            mixed = _dot(ws_ref[g], vb[rows, cols]) + bs_ref[:, g:g + 1]
            o_ref[rows, cols] = (u[rows, cols] * mixed).astype(BF16)


def mixer_a(proj, ln_g, ln_b, w_s, b_s, tm=512):
    T = proj.shape[0]
    W = A_GROUPS * LANES
    return pl.pallas_call(
        _mixa_kernel,
        grid=(T // tm,),
        in_specs=[
            pl.BlockSpec((tm, W), lambda i: (i, 0)),
            pl.BlockSpec((tm, W), lambda i: (i, 1)),
            pl.BlockSpec((1, W), lambda i: (0, 0)),
            pl.BlockSpec((1, W), lambda i: (0, 0)),
            pl.BlockSpec((A_GROUPS, A_CHUNK, A_CHUNK), lambda i: (0, 0, 0)),
            pl.BlockSpec((A_CHUNK, A_GROUPS), lambda i: (0, 0)),
        ],
        out_specs=pl.BlockSpec((tm, W), lambda i: (i, 0)),
        out_shape=jax.ShapeDtypeStruct((T, W), BF16),
        compiler_params=_params("parallel"),
        name="mixer_a",
    )(proj, proj, ln_g.reshape(1, W), ln_b.reshape(1, W), w_s.astype(BF16), b_s.T)


def _gla_consts(C, reverse):
    t = np.arange(C)
    tri = (t[None, :] >= t[:, None]) if reverse else (t[None, :] <= t[:, None])
    L = int(round(math.log2(C)))
    gsel = np.zeros((L, C, C), np.float32)
    masks = np.zeros((L + 1, C, C), np.float32)
    for l in range(L):
        bs = (t >> (l + 1)) << (l + 1)
        mid = bs + (1 << l)
        ref = mid if reverse else mid - 1
        gsel[l, t, ref] = 1.0
        act_q = (t < mid) if reverse else (t >= mid)
        same = bs[:, None] == bs[None, :]
        masks[l] = same & act_q[:, None] & (~act_q)[None, :]
    masks[L] = np.eye(C)
    return tri.astype(np.float32), gsel.reshape(L * C, C), masks


def _gla_direction(q, fraw, v, lb, tri, gsel, masks_ref, state_ref, d, reverse):
    C = q.shape[0]
    L = masks_ref.shape[1] - 1
    f = lb + (1.0 - lb) * jax.nn.sigmoid(fraw)
    k = 1.0 - f
    cum = _dot(tri, jnp.log(f), precision=HIGHEST)
    refs = _dot(gsel, cum, precision=HIGHEST)
    tot = cum[0:1] if reverse else cum[C - 1:C]
    outs = []
    for h in range(B_HEADS):
        sl = slice(h * B_DIM, (h + 1) * B_DIM)
        qh, kh, ch = q[:, sl], k[:, sl], cum[:, sl]
        vb = v[:, sl].astype(BF16)
        p = masks_ref[d, L] * _dot_nt(qh.astype(BF16), kh.astype(BF16))
        for l in range(L):
            r = refs[l * C:(l + 1) * C, sl]
            qa = qh * jnp.exp(jnp.minimum(ch - r, 0.0))
            ka = kh * jnp.exp(jnp.minimum(r - ch, 0.0))
            p = p + masks_ref[d, l] * _dot_nt(qa.astype(BF16), ka.astype(BF16))
        st = state_ref[d, h]
        o = _dot(p.astype(BF16), vb) + _dot_nt((qh * jnp.exp(ch)).astype(BF16), st.astype(BF16))
        ke = kh * jnp.exp(tot[:, sl] - ch)
        state_ref[d, h] = st * jnp.exp(tot[:, sl]) + _dot_tn(vb, ke.astype(BF16))
        outs.append(o)
    return jnp.concatenate(outs, axis=-1)


def _gla_kernel(qf_ref, ff_ref, if_ref, qb_ref, fb_ref, ib_ref, tbl_ref, tri_ref, gsel_ref, masks_ref,
                of_ref, ob_ref, state_ref, *, layer):
    @pl.when(pl.program_id(1) == 0)
    def _():
        state_ref[...] = jnp.zeros_like(state_ref)

    for d, (q_ref, f_ref, i_ref, o_ref) in enumerate(
            ((qf_ref, ff_ref, if_ref, of_ref), (qb_ref, fb_ref, ib_ref, ob_ref))):
        tb = tbl_ref[d]
        e = jnp.exp(tb - jnp.max(tb, axis=0, keepdims=True))
        lb = jnp.sum(e[0:layer + 1], axis=0, keepdims=True) / jnp.sum(e, axis=0, keepdims=True)
        o_ref[...] = _gla_direction(
            jax.nn.silu(q_ref[...].astype(F32)), f_ref[...].astype(F32), i_ref[...].astype(F32), lb,
            tri_ref[d], gsel_ref[d], masks_ref, state_ref, d, reverse=(d == 1))


def mixer_b(proj, b_lb_table, layer, batch, C=GLA_CHUNK):
    T = proj.shape[0]
    W = B_HEADS * B_DIM
    n = T // batch // C
    consts = [_gla_consts(C, rev) for rev in (False, True)]
    tri = jnp.asarray(np.stack([c[0] for c in consts]))
    gsel = jnp.asarray(np.stack([c[1] for c in consts]))
    masks = jnp.asarray(np.stack([c[2] for c in consts]))
    fwd = lambda col: pl.BlockSpec((C, W), lambda b, c: (b * n + c, col))
    bwd = lambda col: pl.BlockSpec((C, W), lambda b, c: (b * n + n - 1 - c, col))
    full = lambda a: pl.BlockSpec(a.shape, lambda b, c: (0,) * a.ndim)
    return pl.pallas_call(
        functools.partial(_gla_kernel, layer=layer),
        grid=(batch, n),
        in_specs=[fwd(2), fwd(3), fwd(5), bwd(2), bwd(4), bwd(5),
                  full(b_lb_table), full(tri), full(gsel), full(masks)],
        out_specs=[pl.BlockSpec((C, W), lambda b, c: (b * n + c, 0)),
                   pl.BlockSpec((C, W), lambda b, c: (b * n + n - 1 - c, 0))],
        out_shape=[jax.ShapeDtypeStruct((T, W), F32)] * 2,
        scratch_shapes=[pltpu.VMEM((2, B_HEADS, B_DIM, B_DIM), F32)],
        compiler_params=_params("arbitrary", "arbitrary"),
        name="mixer_b",
    )(proj, proj, proj, proj, proj, proj, b_lb_table, tri, gsel, masks)


def _router_epilogue(x_new, g_ref, wr_ref, x_ref, h_ref, aff_ref):
    x_ref[...] = x_new
    h = _rms(x_new, g_ref[...])
    h_ref[...] = h
    logits = _dot(h, wr_ref[...], precision=HIGHEST)
    e = jnp.exp(logits - jnp.max(logits, axis=-1, keepdims=True))
    aff_ref[...] = e / jnp.sum(e, axis=-1, keepdims=True)


def _outproj_even_kernel(a_ref, of_ref, ob_ref, gate_ref, bng_ref, w_ref, x_ref, g_ref, wr_ref,
                         xo_ref, h_ref, aff_ref):
    o = of_ref[...] + ob_ref[...]
    parts = []
    for h in range(B_HEADS):
        oh = o[:, h * B_DIM:(h + 1) * B_DIM]
        parts.append(oh * lax.rsqrt(jnp.mean(oh * oh, axis=-1, keepdims=True) + EPS))
    on = jnp.concatenate(parts, axis=-1) * bng_ref[...] * jax.nn.sigmoid(gate_ref[...].astype(F32))
    wa = a_ref.shape[1]
    mixed = _dot(a_ref[...], w_ref[0:wa, :]) + _dot(on.astype(BF16), w_ref[wa:, :])
    _router_epilogue(x_ref[...] + mixed, g_ref, wr_ref, xo_ref, h_ref, aff_ref)


def _router_out(T, D, tm):
    specs = [pl.BlockSpec((tm, D), lambda i: (i, 0)), pl.BlockSpec((tm, D), lambda i: (i, 0)),
             pl.BlockSpec((tm, N_EXPERTS), lambda i: (i, 0))]
    shapes = [jax.ShapeDtypeStruct((T, D), F32), jax.ShapeDtypeStruct((T, D), F32),
              jax.ShapeDtypeStruct((T, N_EXPERTS), F32)]
    return specs, shapes


def outproj_even(a_out, o_f, o_b, proj, b_norm_g, w_out, x, ffn_g, w_router, tm=256):
    T, D = x.shape
    W = a_out.shape[1]
    row = lambda w, col=0: pl.BlockSpec((tm, w), lambda i: (i, col))
    full = lambda s: pl.BlockSpec(s, lambda i: (0, 0))
    out_specs, out_shapes = _router_out(T, D, tm)
    return pl.pallas_call(
        _outproj_even_kernel,
        grid=(T // tm,),
        in_specs=[row(W), row(W), row(W), row(W, 6), full((1, W)), full(w_out.shape), row(D),
                  full((1, D)), full(w_router.shape)],
        out_specs=out_specs,
        out_shape=out_shapes,
        compiler_params=_params("parallel"),
        name="outproj_even",
    )(a_out, o_f, o_b, proj, b_norm_g.reshape(1, W), w_out.astype(BF16), x, ffn_g.reshape(1, D), w_router)


def _outproj_odd_kernel(o1_ref, o2_ref, o3_ref, l1_ref, l2_ref, l3_ref, ex_ref, w_ref, x_ref, g_ref, wr_ref,
                        xo_ref, h_ref, aff_ref):
    ls = [l1_ref[...], l2_ref[...], l3_ref[...]]
    m = jnp.maximum(jnp.maximum(ls[0], ls[1]), ls[2])
    es = [jnp.exp(l - m) for l in ls]
    den = es[0] + es[1] + es[2]
    attn = None
    for e, o_ref in zip(es, (o1_ref, o2_ref, o3_ref)):
        wfull = _dot((e / den).astype(BF16), ex_ref[...])
        term = wfull * o_ref[...].astype(F32)
        attn = term if attn is None else attn + term
    mixed = _dot(attn.astype(BF16), w_ref[...])
    _router_epilogue(x_ref[...] + mixed, g_ref, wr_ref, xo_ref, h_ref, aff_ref)


def outproj_odd(outs, lses, w_o, x, ffn_g, w_router, tm=256):
    T, D = x.shape
    expand = jnp.asarray(np.kron(np.eye(C_HEADS), np.ones((1, C_HEAD_DIM))), BF16)
    row = lambda w: pl.BlockSpec((tm, w), lambda i: (i, 0))
    full = lambda s: pl.BlockSpec(s, lambda i: (0, 0))
    out_specs, out_shapes = _router_out(T, D, tm)
    return pl.pallas_call(
        _outproj_odd_kernel,
        grid=(T // tm,),
        in_specs=[row(D)] * 3 + [row(C_HEADS)] * 3 + [full(expand.shape), full(w_o.shape), row(D),
                                                      full((1, D)), full(w_router.shape)],
        out_specs=out_specs,
        out_shape=out_shapes,
        compiler_params=_params("parallel"),
        name="outproj_odd",
    )(*outs, *lses, expand, w_o.astype(BF16), x, ffn_g.reshape(1, D), w_router)


def _topk_kernel(aff_ref, upper_ref, strict_ref, pos_ref, start_ref, *, cap):
    a = aff_ref[0]
    E, NC, _ = a.shape
    bits = lax.bitcast_convert_type(a, I32)

    def count(mask):
        return jnp.sum(jnp.sum(mask.astype(F32), axis=2, keepdims=True), axis=1, keepdims=True)

    def search(i, thr):
        cand = thr | jnp.left_shift(jnp.int32(1), 30 - i)
        return jnp.where(count(bits >= cand) >= cap, cand, thr)

    thr = lax.fori_loop(0, 31, search, jnp.zeros((E, 1, 1), I32))
    gt = bits > thr
    eq = bits == thr
    need = cap - count(gt)

    def prefix(mask):
        m2 = mask.astype(BF16).reshape(E * NC, LANES)
        within = _dot(m2, upper_ref[...])
        total = within[:, LANES - 1:LANES].astype(BF16)
        tot_b = jnp.broadcast_to(total, (E * NC, LANES))
        starts = jnp.concatenate(
            [_dot(strict_ref[...], tot_b[e * NC:(e + 1) * NC]) for e in range(E)], axis=0)
        return (within + starts).reshape(E, NC, LANES), starts.reshape(E, NC, LANES)

    eq_incl, _ = prefix(eq)
    sel = gt | (eq & (eq_incl - 1.0 < need))
    sel_incl, starts = prefix(sel)
    pos_ref[0] = jnp.where(sel, sel_incl - 1.0, -1.0).astype(I32)
    start_ref[0] = starts.astype(I32)


def route_topk(aff_t, cap):
    B, E, S = aff_t.shape
    NC = S // TOKEN_CHUNK
    t = np.arange(LANES)
    upper = jnp.asarray(t[:, None] <= t[None, :], BF16)
    c = np.arange(NC)
    strict = jnp.asarray(c[None, :] < c[:, None], BF16)
    blk = pl.BlockSpec((1, E, NC, LANES), lambda b: (b, 0, 0, 0))
    pos, starts = pl.pallas_call(
        functools.partial(_topk_kernel, cap=cap),
        grid=(B,),
        in_specs=[blk, pl.BlockSpec(upper.shape, lambda b: (0, 0)), pl.BlockSpec(strict.shape, lambda b: (0, 0))],
        out_specs=[blk, blk],
        out_shape=[jax.ShapeDtypeStruct((B, E, NC, LANES), I32)] * 2,
        compiler_params=_params("parallel"),
        name="route_topk",
    )(aff_t.reshape(B, E, NC, LANES), upper, strict)
    return pos, starts[..., 0].reshape(-1)


def _window_base(start, cap):
    base = jnp.minimum((start // TOKEN_CHUNK) * TOKEN_CHUNK, cap - SLOT_WINDOW)
    return pl.multiple_of(base, TOKEN_CHUNK)


def _index_kernel(starts_ref, pos_ref, sel_ref, idx_ref, acc_ref):
    b, e = pl.program_id(0), pl.program_id(1)
    n_exp = pl.num_programs(1)
    cap = acc_ref.shape[0] * TOKEN_CHUNK
    NC = pos_ref.shape[2]
    acc_ref[...] = jnp.zeros_like(acc_ref)
    slot = lax.broadcasted_iota(I32, (SLOT_WINDOW, TOKEN_CHUNK), 0)

    def body(c, carry):
        base = _window_base(starts_ref[(b * n_exp + e) * NC + c], cap)
        onehot = jnp.where(slot + base == pos_ref[0, 0, pl.ds(c, 1), :], 1.0, 0.0).astype(BF16)
        hit = _dot_nt(sel_ref[...], onehot)
        tok = hit[0:1] + hit[1:2] * lax.convert_element_type(c * TOKEN_CHUNK, F32)
        j = base // TOKEN_CHUNK
        acc_ref[j, 0:1, :] += tok[:, 0:TOKEN_CHUNK]
        acc_ref[j + 1, 0:1, :] += tok[:, TOKEN_CHUNK:]
        return carry

    lax.fori_loop(0, NC, body, 0)
    idx_ref[0, 0] = acc_ref[...].astype(I32)


def moe_slot_tokens(starts, pos, cap):
    B, E, NC, _ = pos.shape
    nt = cap // TOKEN_CHUNK
    sel = np.zeros((8, TOKEN_CHUNK), np.float32)
    sel[0] = np.arange(TOKEN_CHUNK)
    sel[1] = 1.0
    idx = pl.pallas_call(
        _index_kernel,
        grid_spec=pltpu.PrefetchScalarGridSpec(
            num_scalar_prefetch=1,
            grid=(B, E),
            in_specs=[pl.BlockSpec((1, 1, NC, LANES), lambda b, e, s: (b, e, 0, 0)),
                      pl.BlockSpec((8, TOKEN_CHUNK), lambda b, e, s: (0, 0))],
            out_specs=pl.BlockSpec((1, 1, nt, 8, LANES), lambda b, e, s: (b, e, 0, 0, 0)),
            scratch_shapes=[pltpu.VMEM((nt, 8, LANES), F32)]),
        out_shape=jax.ShapeDtypeStruct((B, E, nt, 8, LANES), I32),
        compiler_params=_params("arbitrary", "arbitrary"),
        name="moe_slot_tokens",
    )(starts, pos, jnp.asarray(sel, BF16))
    return idx[:, :, :, 0, :].reshape(-1)


def _ffn_kernel(idx_ref, h_ref, wg_ref, wu_ref, wd_ref, y_ref, x_buf, sem, wgb_ref, wub_ref, wdb_ref, *, tr):
    e, b = pl.program_id(0), pl.program_id(1)
    n_exp, nb = pl.num_programs(0), pl.num_programs(1)
    cap = x_buf.shape[1]
    step = e * nb + b

    def gather(st, buf, start):
        eb = st // nb
        bb = st - eb * nb
        if not start:
            pltpu.make_async_copy(h_ref.at[bb, pl.ds(0, cap), :], x_buf.at[buf], sem.at[buf]).wait()
            return

        def body(i, carry):
            tok = idx_ref[(bb * n_exp + eb) * cap + i]
            pltpu.make_async_copy(h_ref.at[bb, pl.ds(tok, 1), :], x_buf.at[buf, pl.ds(i, 1), :],
                                  sem.at[buf]).start()
            return carry

        lax.fori_loop(0, cap, body, 0, unroll=8)

    cur = lax.rem(step, 2)

    @pl.when(step == 0)
    def _():
        gather(step, 0, True)

    @pl.when(step + 1 < n_exp * nb)
    def _():
        gather(step + 1, 1 - cur, True)

    @pl.when(b == 0)
    def _():
        wgb_ref[...] = wg_ref[0, 0].astype(BF16)
        wub_ref[...] = wu_ref[0, 0].astype(BF16)
        wdb_ref[...] = wd_ref[0, 0].astype(BF16)

    gather(step, cur, False)
    for r in range(cap // tr):
        rows = pl.ds(r * tr, tr)
        x = x_buf[cur, rows, :].astype(BF16)
        g = _dot(x, wgb_ref[...])
        u = _dot(x, wub_ref[...])
        mid = (g * jax.nn.sigmoid(g) * u).astype(BF16)
        y_ref[0, 0, rows, :] = _dot(mid, wdb_ref[...]).astype(BF16)


def moe_ffn(idx, h, w_gate, w_up, w_down, layer, cap, tr=256):
    B, S, D = h.shape
    E, F = w_gate.shape[1], w_gate.shape[-1]
    wspec = lambda r, c: pl.BlockSpec((1, 1, r, c), lambda e, b, s: (layer, e, 0, 0))
    return pl.pallas_call(
        functools.partial(_ffn_kernel, tr=min(tr, cap)),
        grid_spec=pltpu.PrefetchScalarGridSpec(
            num_scalar_prefetch=1,
            grid=(E, B),
            in_specs=[pl.BlockSpec(memory_space=pl.ANY), wspec(D, F), wspec(D, F), wspec(F, D)],
            out_specs=pl.BlockSpec((1, 1, cap, D), lambda e, b, s: (b, e, 0, 0)),
            scratch_shapes=[pltpu.VMEM((2, cap, D), F32), pltpu.SemaphoreType.DMA((2,)),
                            pltpu.VMEM((D, F), BF16), pltpu.VMEM((D, F), BF16), pltpu.VMEM((F, D), BF16)]),
        out_shape=jax.ShapeDtypeStruct((B, E, cap, D), BF16),
        compiler_params=_params("arbitrary", "arbitrary"),
        name="moe_ffn",
    )(idx, h, w_gate, w_up, w_down)


def _combine_kernel(starts_ref, post_ref, gate_ref, x_ref, y_ref, o_ref):
    b, c = pl.program_id(0), pl.program_id(2)
    NC = pl.num_programs(2)
    n_exp, cap = y_ref.shape[1], y_ref.shape[2]
    slot = lax.broadcasted_iota(I32, (TOKEN_CHUNK, SLOT_WINDOW), 1)
    acc = x_ref[0]
    for e in range(n_exp):
        base = _window_base(starts_ref[(b * n_exp + e) * NC + c], cap)
        onehot = jnp.where(slot + base == post_ref[0, :, e:e + 1], 1.0, 0.0).astype(BF16)
        acc = acc + _dot(onehot, y_ref[0, e, pl.ds(base, SLOT_WINDOW), :]) * gate_ref[0, :, e:e + 1]
    o_ref[0] = acc


def moe_combine(starts, pos_t, gate, x, y, dsplit=2):
    B, S, D = x.shape
    E, cap = y.shape[1], y.shape[2]
    dh = D // dsplit
    tok = lambda w: pl.BlockSpec((1, TOKEN_CHUNK, w), lambda b, j, c, s: (b, c, 0))
    res = pl.BlockSpec((1, TOKEN_CHUNK, dh), lambda b, j, c, s: (b, c, j))
    return pl.pallas_call(
        _combine_kernel,
        grid_spec=pltpu.PrefetchScalarGridSpec(
            num_scalar_prefetch=1,
            grid=(B, dsplit, S // TOKEN_CHUNK),
            in_specs=[tok(E), tok(E), res,
                      pl.BlockSpec((1, E, cap, dh), lambda b, j, c, s: (b, 0, 0, j))],
            out_specs=res),
        out_shape=jax.ShapeDtypeStruct((B, S, D), F32),
        compiler_params=_params("arbitrary", "arbitrary", "arbitrary"),
        name="moe_combine",
    )(starts, pos_t, gate, x, y)


def expert_choice_moe(x, h, aff, batch, w_gate, w_up, w_down, layer):
    T, D = x.shape
    S = T // batch
    cap = max(1, CAPACITY_FACTOR * S // N_EXPERTS)
    aff = aff.reshape(batch, S, N_EXPERTS)
    pos, starts = route_topk(jnp.swapaxes(aff, 1, 2), cap)
    idx = moe_slot_tokens(starts, pos, cap)
    y = moe_ffn(idx, h.reshape(batch, S, D), w_gate, w_up, w_down, layer, cap)
    pos_t = jnp.swapaxes(pos.reshape(batch, N_EXPERTS, S), 1, 2)
    return moe_combine(starts, pos_t, aff, x.reshape(batch, S, D), y).reshape(T, D)


ATTN_TQ = 128
ATTN_TK = ATTN_TQ + 2 * HALF_WINDOW


def _t5_bucket(rel):
    half_buckets = REL_BUCKETS // 2
    max_exact = half_buckets // 2
    n = jnp.abs(rel)
    scaled = (jnp.log(jnp.maximum(n, 1).astype(jnp.float32) / max_exact)
              / math.log(REL_MAX_DISTANCE / max_exact))
    large = jnp.minimum(max_exact + (scaled * (half_buckets - max_exact)).astype(jnp.int32),
                        half_buckets - 1)
    return jnp.where(rel > 0, half_buckets, 0) + jnp.where(n < max_exact, n, large)


def _bias_kernel(table_ref, bucket_ref, o_ref):
    h = pl.program_id(1)
    bucket = bucket_ref[0]
    acc = jnp.zeros(bucket.shape, F32)
    for bk in range(REL_BUCKETS):
        acc = jnp.where(bucket == bk, table_ref[bk * C_HEADS + h], acc)
    q = lax.broadcasted_iota(I32, bucket.shape, 0)
    kc = lax.broadcasted_iota(I32, bucket.shape, 1)
    o_ref[0, 0] = jnp.where(jnp.abs(kc - HALF_WINDOW - q) <= HALF_WINDOW, acc, NEG_INF)


def attention_bias(rel_bias):
    rel = np.arange(ATTN_TK)[None, :] - HALF_WINDOW - np.arange(ATTN_TQ)[:, None]
    buckets = jnp.stack([_t5_bucket(jnp.asarray(rel * d, I32)) for d in DILATIONS]).astype(I32)
    P = len(DILATIONS)
    return pl.pallas_call(
        _bias_kernel,
        grid_spec=pltpu.PrefetchScalarGridSpec(
            num_scalar_prefetch=1,
            grid=(P, C_HEADS),
            in_specs=[pl.BlockSpec((1, ATTN_TQ, ATTN_TK), lambda p, h, t: (p, 0, 0))],
            out_specs=pl.BlockSpec((1, 1, ATTN_TQ, ATTN_TK), lambda p, h, t: (p, h, 0, 0))),
        out_shape=jax.ShapeDtypeStruct((P, C_HEADS, ATTN_TQ, ATTN_TK), F32),
        compiler_params=_params("arbitrary", "arbitrary"),
        name="attention_bias",
    )(rel_bias.reshape(-1), buckets)


def _attn_kernel(q_ref, kp_ref, km_ref, kn_ref, vp_ref, vm_ref, vn_ref, bias_ref, o_ref, lse_ref, k_buf, v_buf):
    i = pl.program_id(1)
    last = pl.num_programs(1) - 1
    hw = HALF_WINDOW
    k_buf[0:hw] = kp_ref[0]
    k_buf[hw:hw + ATTN_TQ] = km_ref[0]
    k_buf[hw + ATTN_TQ:] = kn_ref[0]
    v_buf[0:hw] = vp_ref[0]
    v_buf[hw:hw + ATTN_TQ] = vm_ref[0]
    v_buf[hw + ATTN_TQ:] = vn_ref[0]
    kc = lax.broadcasted_iota(I32, (ATTN_TQ, ATTN_TK), 1)
    in_seq = jnp.logical_and(jnp.logical_or(i > 0, kc >= hw), jnp.logical_or(i < last, kc < hw + ATTN_TQ))
    scale = C_HEAD_DIM ** -0.5
    for h in range(C_HEADS):
        cols = slice(h * C_HEAD_DIM, (h + 1) * C_HEAD_DIM)
        s = _dot_nt(q_ref[0, :, cols] * scale, k_buf[:, cols]) + bias_ref[h]
        s = jnp.where(in_seq, s, NEG_INF)
        m = jnp.max(s, axis=-1, keepdims=True)
        p = jnp.exp(s - m)
        den = jnp.sum(p, axis=-1, keepdims=True)
        o_ref[0, :, cols] = _dot((p / den).astype(BF16), v_buf[:, cols]).astype(BF16)
        lse_ref[0, :, h:h + 1] = m + jnp.log(den)


def dilated_attention(qkv, bias):
    G, n, D3 = qkv.shape
    D = D3 // 3
    nb = n // HALF_WINDOW
    r = ATTN_TQ // HALF_WINDOW
    main = lambda col: pl.BlockSpec((1, ATTN_TQ, D), lambda g, i: (g, i, col))
    prev = lambda col: pl.BlockSpec((1, HALF_WINDOW, D), lambda g, i: (g, jnp.maximum(i * r - 1, 0), col))
    nxt = lambda col: pl.BlockSpec((1, HALF_WINDOW, D), lambda g, i: (g, jnp.minimum(i * r + r, nb - 1), col))
    return pl.pallas_call(
        _attn_kernel,
        grid=(G, n // ATTN_TQ),
        in_specs=[main(0), prev(1), main(1), nxt(1), prev(2), main(2), nxt(2),
                  pl.BlockSpec(bias.shape, lambda g, i: (0, 0, 0))],
        out_specs=[pl.BlockSpec((1, ATTN_TQ, D), lambda g, i: (g, i, 0)),
                   pl.BlockSpec((1, ATTN_TQ, C_HEADS), lambda g, i: (g, i, 0))],
        out_shape=[jax.ShapeDtypeStruct((G, n, D), BF16), jax.ShapeDtypeStruct((G, n, C_HEADS), F32)],
        scratch_shapes=[pltpu.VMEM((ATTN_TK, D), BF16), pltpu.VMEM((ATTN_TK, D), BF16)],
        compiler_params=_params("parallel", "parallel"),
        name="dilated_attention",
    )(qkv, qkv, qkv, qkv, qkv, qkv, qkv, bias)


def dilated_mixture(qkv, rel_bias, batch):
    T, D3 = qkv.shape
    S = T // batch
    bias = attention_bias(rel_bias)
    outs, lses = [], []
    for p, d in enumerate(DILATIONS):
        n = S // d
        strided = qkv.reshape(batch, n, d, D3).swapaxes(1, 2).reshape(batch * d, n, D3)
        o, lse = dilated_attention(strided, bias[p])
        unstride = lambda t: t.reshape(batch, d, n, t.shape[-1]).swapaxes(1, 2).reshape(T, t.shape[-1])
        outs.append(unstride(o))
        lses.append(unstride(lse))
    return outs, lses


def _final_norm_kernel(x_ref, g_ref, o_ref):
    o_ref[...] = _rms(x_ref[...], g_ref[...])


def final_rmsnorm(x, g, tm=512):
    T, D = x.shape
    return pl.pallas_call(
        _final_norm_kernel,
        grid=(T // tm,),
        in_specs=[pl.BlockSpec((tm, D), lambda i: (i, 0)), pl.BlockSpec((1, D), lambda i: (0, 0))],
        out_specs=pl.BlockSpec((tm, D), lambda i: (i, 0)),
        out_shape=jax.ShapeDtypeStruct((T, D), F32),
        compiler_params=_params("parallel"),
        name="final_norm",
    )(x, g.reshape(1, D))


def kernel(x, mix_norm, ffn_norm, final_norm, w_in_even, w_out_even, a_ln_g, a_ln_b, a_w_s, a_b_s, b_lb_table,
           b_norm_g, w_qkv_odd, w_o_odd, rel_bias, w_router, w_gate, w_up, w_down):
    B, S, D = x.shape
    depth = mix_norm.shape[0]
    xt = x.reshape(B * S, D)
    for layer in range(depth):
        j = layer // 2
        if layer % 2 == 0:
            proj = norm_matmul(xt, mix_norm[layer], w_in_even[j].astype(BF16))
            a_out = mixer_a(proj, a_ln_g[j], a_ln_b[j], a_w_s[j], a_b_s[j])
            o_f, o_b = mixer_b(proj, b_lb_table, layer, B)
            xt, h, aff = outproj_even(a_out, o_f, o_b, proj, b_norm_g[j], w_out_even[j], xt,
                                      ffn_norm[layer], w_router[layer])
        else:
            qkv = norm_matmul(xt, mix_norm[layer], w_qkv_odd[j].astype(BF16))
            outs, lses = dilated_mixture(qkv, rel_bias, B)
            xt, h, aff = outproj_odd(outs, lses, w_o_odd[j], xt, ffn_norm[layer], w_router[layer])
        xt = expert_choice_moe(xt, h, aff, B, w_gate, w_up, w_down, layer)
    return final_rmsnorm(xt, final_norm).reshape(B, S, D)
```

```python
import functools
import math

import numpy as np
import jax
import jax.numpy as jnp
from jax import lax
from jax.experimental import pallas as pl
from jax.experimental.pallas import tpu as pltpu

F32 = jnp.float32
BF16 = jnp.bfloat16
I32 = jnp.int32
EPS = 1e-6
NEG_INF = -1e30
HIGHEST = lax.Precision.HIGHEST

LANES = 128
VMEM_LIMIT = 56 * 1024 * 1024

A_GROUPS = 4
A_CHUNK = 128
B_HEADS = 4
B_DIM = 128
GLA_CHUNK = 128
C_HEADS = 16
C_HEAD_DIM = 64
HALF_WINDOW = 64
DILATIONS = (1, 4, 16)
REL_BUCKETS = 32
REL_MAX_DISTANCE = 1024
N_EXPERTS = 16
CAPACITY_FACTOR = 2
TOKEN_CHUNK = 128
SLOT_WINDOW = 256


def _params(*sem):
    return pltpu.CompilerParams(dimension_semantics=sem, vmem_limit_bytes=VMEM_LIMIT)


def _dot(a, b, **kw):
    return jnp.dot(a, b, preferred_element_type=F32, **kw)


def _dot_nt(a, b):
    return lax.dot_general(a, b, (((1,), (1,)), ((), ())), preferred_element_type=F32)


def _dot_tn(a, b):
    return lax.dot_general(a, b, (((0,), (0,)), ((), ())), preferred_element_type=F32)


def _dot_split01(a2, x):
    hi = x.astype(BF16)
    lo = (x - hi.astype(F32)).astype(BF16)
    return _dot(a2, jnp.concatenate([hi, lo], axis=0))


def _rms(x, g):
    return x * lax.rsqrt(jnp.mean(x * x, axis=-1, keepdims=True) + EPS) * g


def _norm_matmul_kernel(x_ref, g_ref, w_ref, o_ref, *, tn):
    h = _rms(x_ref[...], g_ref[...]).astype(BF16)
    for j in range(w_ref.shape[1] // tn):
        cols = slice(j * tn, (j + 1) * tn)
        o_ref[:, cols] = _dot(h, w_ref[:, cols]).astype(o_ref.dtype)


def norm_matmul(x, g, w, tm=1024, tn=512):
    T, D = x.shape
    N = w.shape[1]
    return pl.pallas_call(
        functools.partial(_norm_matmul_kernel, tn=tn),
        grid=(T // tm,),
        in_specs=[
            pl.BlockSpec((tm, D), lambda i: (i, 0)),
            pl.BlockSpec((1, D), lambda i: (0, 0)),
            pl.BlockSpec((D, N), lambda i: (0, 0)),
        ],
        out_specs=pl.BlockSpec((tm, N), lambda i: (i, 0)),
        out_shape=jax.ShapeDtypeStruct((T, N), BF16),
        compiler_params=_params("parallel"),
        name="norm_matmul",
    )(x, g.reshape(1, D), w)


def _mixa_kernel(u_ref, v_ref, lg_ref, lb_ref, ws_ref, bs_ref, o_ref):
    tm = u_ref.shape[0]
    u = jax.nn.gelu(u_ref[...].astype(F32))
    v = jax.nn.gelu(v_ref[...].astype(F32))
    mu = jnp.mean(v, axis=-1, keepdims=True)
    vc = v - mu
    vn = vc * lax.rsqrt(jnp.mean(vc * vc, axis=-1, keepdims=True) + EPS)
    vb = (vn * lg_ref[...] + lb_ref[...]).astype(BF16)
    for n in range(tm // A_CHUNK):
        rows = slice(n * A_CHUNK, (n + 1) * A_CHUNK)
        for g in range(A_GROUPS):
            cols = slice(g * LANES, (g + 1) * LANES)
            mixed = _dot(ws_ref[g], vb[rows, cols]) + bs_ref[:, g:g + 1]
            o_ref[rows, cols] = (u[rows, cols] * mixed).astype(BF16)


def mixer_a(proj, ln_g, ln_b, w_s, b_s, tm=512):
    T = proj.shape[0]
    W = A_GROUPS * LANES
    return pl.pallas_call(
        _mixa_kernel,
        grid=(T // tm,),
        in_specs=[
            pl.BlockSpec((tm, W), lambda i: (i, 0)),
            pl.BlockSpec((tm, W), lambda i: (i, 1)),
            pl.BlockSpec((1, W), lambda i: (0, 0)),
            pl.BlockSpec((1, W), lambda i: (0, 0)),
            pl.BlockSpec((A_GROUPS, A_CHUNK, A_CHUNK), lambda i: (0, 0, 0)),
            pl.BlockSpec((A_CHUNK, A_GROUPS), lambda i: (0, 0)),
        ],
        out_specs=pl.BlockSpec((tm, W), lambda i: (i, 0)),
        out_shape=jax.ShapeDtypeStruct((T, W), BF16),
        compiler_params=_params("parallel"),
        name="mixer_a",
    )(proj, proj, ln_g.reshape(1, W), ln_b.reshape(1, W), w_s.astype(BF16), b_s.T)


def _gla_consts(C, reverse):
    t = np.arange(C)[:, None]
    r = np.arange(C)[None, :]
    L = int(round(math.log2(C)))
    spans = [(r >= t) if reverse else (r <= t), (r < t) if reverse else (r > t)]
    level = np.where(np.eye(C, dtype=bool), L, -1).astype(np.int32)
    for l in range(L):
        bs = (t >> (l + 1)) << (l + 1)
        mid = bs + (1 << l)
        if reverse:
            act_q = t < mid
            span = np.where(act_q, (r >= t) & (r < mid), (r >= mid) & (r < t))
        else:
            act_q = t >= mid
            span = np.where(act_q, (r >= mid) & (r <= t), (r > t) & (r < mid))
        spans.append(span)
        level[(bs == bs.T) & act_q & ~act_q.T] = l
    return np.tile(np.concatenate(spans, axis=0).astype(np.float32), (1, 2)), level


def _gla_direction(q, fraw, v, lb, span, level, state_ref, d, reverse):
    C = q.shape[0]
    L = span.shape[0] // C - 2
    f = lb + (1.0 - lb) * jax.nn.sigmoid(fraw)
    e2 = _dot_split01(span, jnp.log2(f))
    tot = jnp.exp2(e2[0:1] if reverse else e2[C - 1:C])
    w = jnp.exp2(e2).astype(BF16)
    qb, kb, vb = q.astype(BF16), (1.0 - f).astype(BF16), v.astype(BF16)
    outs = []
    for h in range(B_HEADS):
        sl = slice(h * B_DIM, (h + 1) * B_DIM)
        qh, kh, vh = qb[:, sl], kb[:, sl], vb[:, sl]
        p = jnp.where(level == L, _dot_nt(qh, kh), 0.0)
        for l in range(L):
            wl = w[(l + 2) * C:(l + 3) * C, sl]
            p = jnp.where(level == l, _dot_nt(qh * wl, kh * wl), p)
        st = state_ref[d, h]
        outs.append(_dot(p.astype(BF16), vh) + _dot_nt(qh * w[0:C, sl], st.astype(BF16)))
        state_ref[d, h] = st * tot[:, sl] + _dot_tn(vh, kh * w[C:2 * C, sl])
    return jnp.concatenate(outs, axis=-1)


def _gla_kernel(qf_ref, ff_ref, if_ref, qb_ref, fb_ref, ib_ref, tbl_ref, span_ref, level_ref,
                of_ref, ob_ref, state_ref, *, layer):
    @pl.when(pl.program_id(1) == 0)
    def _():
        state_ref[...] = jnp.zeros_like(state_ref)

    for d, (q_ref, f_ref, i_ref, o_ref) in enumerate(
            ((qf_ref, ff_ref, if_ref, of_ref), (qb_ref, fb_ref, ib_ref, ob_ref))):
        tb = tbl_ref[d]
        e = jnp.exp(tb - jnp.max(tb, axis=0, keepdims=True))
        lb = jnp.sum(e[0:layer + 1], axis=0, keepdims=True) / jnp.sum(e, axis=0, keepdims=True)
        o_ref[...] = _gla_direction(
            jax.nn.silu(q_ref[...].astype(F32)), f_ref[...].astype(F32), i_ref[...].astype(F32), lb,
            span_ref[d], level_ref[d], state_ref, d, reverse=(d == 1))


def mixer_b(proj, b_lb_table, layer, batch, C=GLA_CHUNK):
    T = proj.shape[0]
    W = B_HEADS * B_DIM
    n = T // batch // C
    consts = [_gla_consts(C, rev) for rev in (False, True)]
    span = jnp.asarray(np.stack([c[0] for c in consts]), BF16)
    level = jnp.asarray(np.stack([c[1] for c in consts]))
    fwd = lambda col: pl.BlockSpec((C, W), lambda b, c: (b * n + c, col))
    bwd = lambda col: pl.BlockSpec((C, W), lambda b, c: (b * n + n - 1 - c, col))
    full = lambda a: pl.BlockSpec(a.shape, lambda b, c: (0,) * a.ndim)
    return pl.pallas_call(
        functools.partial(_gla_kernel, layer=layer),
        grid=(batch, n),
        in_specs=[fwd(2), fwd(3), fwd(5), bwd(2), bwd(4), bwd(5),
                  full(b_lb_table), full(span), full(level)],
        out_specs=[pl.BlockSpec((C, W), lambda b, c: (b * n + c, 0)),
                   pl.BlockSpec((C, W), lambda b, c: (b * n + n - 1 - c, 0))],
        out_shape=[jax.ShapeDtypeStruct((T, W), F32)] * 2,
        scratch_shapes=[pltpu.VMEM((2, B_HEADS, B_DIM, B_DIM), F32)],
        compiler_params=_params("arbitrary", "arbitrary"),
        name="mixer_b",
    )(proj, proj, proj, proj, proj, proj, b_lb_table, span, level)


def _router_epilogue(x_new, g_ref, wr_ref, x_ref, h_ref, aff_ref):
    x_ref[...] = x_new
    h = _rms(x_new, g_ref[...])
    h_ref[...] = h
    w = wr_ref[...]
    h_hi, w_hi = h.astype(BF16), w.astype(BF16)
    h_lo, w_lo = (h - h_hi.astype(F32)).astype(BF16), (w - w_hi.astype(F32)).astype(BF16)
    logits = _dot(h_hi, w_hi) + _dot(h_lo, w_hi) + _dot(h_hi, w_lo)
    e = jnp.exp(logits - jnp.max(logits, axis=-1, keepdims=True))
    aff_ref[...] = e / jnp.sum(e, axis=-1, keepdims=True)


def _outproj_even_kernel(a_ref, of_ref, ob_ref, gate_ref, bng_ref, w_ref, x_ref, g_ref, wr_ref,
                         xo_ref, h_ref, aff_ref):
    o = of_ref[...] + ob_ref[...]
    parts = []
    for h in range(B_HEADS):
        oh = o[:, h * B_DIM:(h + 1) * B_DIM]
        parts.append(oh * lax.rsqrt(jnp.mean(oh * oh, axis=-1, keepdims=True) + EPS))
    on = jnp.concatenate(parts, axis=-1) * bng_ref[...] * jax.nn.sigmoid(gate_ref[...].astype(F32))
    wa = a_ref.shape[1]
    mixed = _dot(a_ref[...], w_ref[0:wa, :]) + _dot(on.astype(BF16), w_ref[wa:, :])
    _router_epilogue(x_ref[...] + mixed, g_ref, wr_ref, xo_ref, h_ref, aff_ref)


def _router_out(T, D, tm):
    specs = [pl.BlockSpec((tm, D), lambda i: (i, 0)), pl.BlockSpec((tm, D), lambda i: (i, 0)),
             pl.BlockSpec((tm, N_EXPERTS), lambda i: (i, 0))]
    shapes = [jax.ShapeDtypeStruct((T, D), F32), jax.ShapeDtypeStruct((T, D), F32),
              jax.ShapeDtypeStruct((T, N_EXPERTS), F32)]
    return specs, shapes


def outproj_even(a_out, o_f, o_b, proj, b_norm_g, w_out, x, ffn_g, w_router, tm=512):
    T, D = x.shape
    W = a_out.shape[1]
    row = lambda w, col=0: pl.BlockSpec((tm, w), lambda i: (i, col))
    full = lambda s: pl.BlockSpec(s, lambda i: (0, 0))
    out_specs, out_shapes = _router_out(T, D, tm)
    return pl.pallas_call(
        _outproj_even_kernel,
        grid=(T // tm,),
        in_specs=[row(W), row(W), row(W), row(W, 6), full((1, W)), full(w_out.shape), row(D),
                  full((1, D)), full(w_router.shape)],
        out_specs=out_specs,
        out_shape=out_shapes,
        compiler_params=_params("parallel"),
        name="outproj_even",
    )(a_out, o_f, o_b, proj, b_norm_g.reshape(1, W), w_out.astype(BF16), x, ffn_g.reshape(1, D), w_router)


def _outproj_odd_kernel(o1_ref, o2_ref, o3_ref, l1_ref, l2_ref, l3_ref, ex_ref, w_ref, x_ref, g_ref, wr_ref,
                        xo_ref, h_ref, aff_ref):
    ls = [l1_ref[...], l2_ref[...], l3_ref[...]]
    m = jnp.maximum(jnp.maximum(ls[0], ls[1]), ls[2])
    es = [jnp.exp(l - m) for l in ls]
    den = es[0] + es[1] + es[2]
    attn = None
    for e, o_ref in zip(es, (o1_ref, o2_ref, o3_ref)):
        wfull = _dot((e / den).astype(BF16), ex_ref[...])
        term = wfull * o_ref[...].astype(F32)
        attn = term if attn is None else attn + term
    mixed = _dot(attn.astype(BF16), w_ref[...])
    _router_epilogue(x_ref[...] + mixed, g_ref, wr_ref, xo_ref, h_ref, aff_ref)


def outproj_odd(outs, lses, w_o, x, ffn_g, w_router, tm=512):
    T, D = x.shape
    expand = jnp.asarray(np.kron(np.eye(C_HEADS), np.ones((1, C_HEAD_DIM))), BF16)
    row = lambda w: pl.BlockSpec((tm, w), lambda i: (i, 0))
    full = lambda s: pl.BlockSpec(s, lambda i: (0, 0))
    out_specs, out_shapes = _router_out(T, D, tm)
    return pl.pallas_call(
        _outproj_odd_kernel,
        grid=(T // tm,),
        in_specs=[row(D)] * 3 + [row(C_HEADS)] * 3 + [full(expand.shape), full(w_o.shape), row(D),
                                                      full((1, D)), full(w_router.shape)],
        out_specs=out_specs,
        out_shape=out_shapes,
        compiler_params=_params("parallel"),
        name="outproj_odd",
    )(*outs, *lses, expand, w_o.astype(BF16), x, ffn_g.reshape(1, D), w_router)


def _topk_kernel(aff_ref, upper_ref, strict_ref, pos_ref, start_ref, *, cap):
    a = aff_ref[0]
    E, NC, _ = a.shape
    bits = lax.bitcast_convert_type(a, I32)

    def count(mask):
        return jnp.sum(jnp.sum(mask.astype(F32), axis=2, keepdims=True), axis=1, keepdims=True)

    def search(i, thr):
        cand = thr | jnp.left_shift(jnp.int32(1), 30 - i)
        return jnp.where(count(bits >= cand) >= cap, cand, thr)

    thr = lax.fori_loop(0, 31, search, jnp.zeros((E, 1, 1), I32))
    gt = bits > thr
    eq = bits == thr
    need = cap - count(gt)

    def prefix(mask):
        m2 = mask.astype(BF16).reshape(E * NC, LANES)
        within = _dot(m2, upper_ref[...])
        total = within[:, LANES - 1:LANES].astype(BF16)
        tot_b = jnp.broadcast_to(total, (E * NC, LANES))
        starts = jnp.concatenate(
            [_dot(strict_ref[...], tot_b[e * NC:(e + 1) * NC]) for e in range(E)], axis=0)
        return (within + starts).reshape(E, NC, LANES), starts.reshape(E, NC, LANES)

    eq_incl, _ = prefix(eq)
    sel = gt | (eq & (eq_incl - 1.0 < need))
    sel_incl, starts = prefix(sel)
    pos_ref[0] = jnp.where(sel, sel_incl - 1.0, -1.0).astype(I32)
    start_ref[0] = starts.astype(I32)


def route_topk(aff_t, cap):
    B, E, S = aff_t.shape
    NC = S // TOKEN_CHUNK
    t = np.arange(LANES)
    upper = jnp.asarray(t[:, None] <= t[None, :], BF16)
    c = np.arange(NC)
    strict = jnp.asarray(c[None, :] < c[:, None], BF16)
    blk = pl.BlockSpec((1, E, NC, LANES), lambda b: (b, 0, 0, 0))
    pos, starts = pl.pallas_call(
        functools.partial(_topk_kernel, cap=cap),
        grid=(B,),
        in_specs=[blk, pl.BlockSpec(upper.shape, lambda b: (0, 0)), pl.BlockSpec(strict.shape, lambda b: (0, 0))],
        out_specs=[blk, blk],
        out_shape=[jax.ShapeDtypeStruct((B, E, NC, LANES), I32)] * 2,
        compiler_params=_params("parallel"),
        name="route_topk",
    )(aff_t.reshape(B, E, NC, LANES), upper, strict)
    return pos, starts[..., 0].reshape(-1)


def _window_base(start, cap):
    base = jnp.minimum((start // TOKEN_CHUNK) * TOKEN_CHUNK, cap - SLOT_WINDOW)
    return pl.multiple_of(base, TOKEN_CHUNK)


def _index_kernel(starts_ref, pos_ref, sel_ref, idx_ref, acc_ref):
    b, e = pl.program_id(0), pl.program_id(1)
    n_exp = pl.num_programs(1)
    cap = acc_ref.shape[0] * TOKEN_CHUNK
    NC = pos_ref.shape[2]
    acc_ref[...] = jnp.zeros_like(acc_ref)
    slot = lax.broadcasted_iota(I32, (SLOT_WINDOW, TOKEN_CHUNK), 0)

    def body(c, carry):
        base = _window_base(starts_ref[(b * n_exp + e) * NC + c], cap)
        onehot = jnp.where(slot + base == pos_ref[0, 0, pl.ds(c, 1), :], 1.0, 0.0).astype(BF16)
        hit = _dot_nt(sel_ref[...], onehot)
        tok = hit[0:1] + hit[1:2] * lax.convert_element_type(c * TOKEN_CHUNK, F32)
        j = base // TOKEN_CHUNK
        acc_ref[j, 0:1, :] += tok[:, 0:TOKEN_CHUNK]
        acc_ref[j + 1, 0:1, :] += tok[:, TOKEN_CHUNK:]
        return carry

    lax.fori_loop(0, NC, body, 0, unroll=8)
    idx_ref[0, 0] = acc_ref[...].astype(I32)


def moe_slot_tokens(starts, pos, cap):
    B, E, NC, _ = pos.shape
    nt = cap // TOKEN_CHUNK
    sel = np.zeros((8, TOKEN_CHUNK), np.float32)
    sel[0] = np.arange(TOKEN_CHUNK)
    sel[1] = 1.0
    idx = pl.pallas_call(
        _index_kernel,
        grid_spec=pltpu.PrefetchScalarGridSpec(
            num_scalar_prefetch=1,
            grid=(B, E),
            in_specs=[pl.BlockSpec((1, 1, NC, LANES), lambda b, e, s: (b, e, 0, 0)),
                      pl.BlockSpec((8, TOKEN_CHUNK), lambda b, e, s: (0, 0))],
            out_specs=pl.BlockSpec((1, 1, nt, 8, LANES), lambda b, e, s: (b, e, 0, 0, 0)),
            scratch_shapes=[pltpu.VMEM((nt, 8, LANES), F32)]),
        out_shape=jax.ShapeDtypeStruct((B, E, nt, 8, LANES), I32),
        compiler_params=_params("arbitrary", "arbitrary"),
        name="moe_slot_tokens",
    )(starts, pos, jnp.asarray(sel, BF16))
    return idx[:, :, :, 0, :].reshape(-1)


def _ffn_kernel(idx_ref, h_ref, wg_ref, wu_ref, wd_ref, y_ref, x_buf, sem, wgb_ref, wub_ref, wdb_ref, *, tr):
    e, b = pl.program_id(0), pl.program_id(1)
    n_exp, nb = pl.num_programs(0), pl.num_programs(1)
    cap = x_buf.shape[1]
    step = e * nb + b

    def gather(st, buf, start):
        eb = st // nb
        bb = st - eb * nb
        if not start:
            pltpu.make_async_copy(h_ref.at[bb, pl.ds(0, cap), :], x_buf.at[buf], sem.at[buf]).wait()
            return

        def body(i, carry):
            tok = idx_ref[(bb * n_exp + eb) * cap + i]
            pltpu.make_async_copy(h_ref.at[bb, pl.ds(tok, 1), :], x_buf.at[buf, pl.ds(i, 1), :],
                                  sem.at[buf]).start()
            return carry

        lax.fori_loop(0, cap, body, 0, unroll=8)

    cur = lax.rem(step, 2)

    @pl.when(step == 0)
    def _():
        gather(step, 0, True)

    @pl.when(step + 1 < n_exp * nb)
    def _():
        gather(step + 1, 1 - cur, True)

    @pl.when(b == 0)
    def _():
        wgb_ref[...] = wg_ref[0, 0].astype(BF16)
        wub_ref[...] = wu_ref[0, 0].astype(BF16)
        wdb_ref[...] = wd_ref[0, 0].astype(BF16)

    gather(step, cur, False)
    for r in range(cap // tr):
        rows = pl.ds(r * tr, tr)
        x = x_buf[cur, rows, :].astype(BF16)
        g = _dot(x, wgb_ref[...])
        u = _dot(x, wub_ref[...])
        mid = (g * jax.nn.sigmoid(g) * u).astype(BF16)
        y_ref[0, 0, rows, :] = _dot(mid, wdb_ref[...]).astype(BF16)


def moe_ffn(idx, h, w_gate, w_up, w_down, layer, cap, tr=256):
    B, S, D = h.shape
    E, F = w_gate.shape[1], w_gate.shape[-1]
    wspec = lambda r, c: pl.BlockSpec((1, 1, r, c), lambda e, b, s: (layer, e, 0, 0))
    return pl.pallas_call(
        functools.partial(_ffn_kernel, tr=min(tr, cap)),
        grid_spec=pltpu.PrefetchScalarGridSpec(
            num_scalar_prefetch=1,
            grid=(E, B),
            in_specs=[pl.BlockSpec(memory_space=pl.ANY), wspec(D, F), wspec(D, F), wspec(F, D)],
            out_specs=pl.BlockSpec((1, 1, cap, D), lambda e, b, s: (b, e, 0, 0)),
            scratch_shapes=[pltpu.VMEM((2, cap, D), F32), pltpu.SemaphoreType.DMA((2,)),
                            pltpu.VMEM((D, F), BF16), pltpu.VMEM((D, F), BF16), pltpu.VMEM((F, D), BF16)]),
        out_shape=jax.ShapeDtypeStruct((B, E, cap, D), BF16),
        compiler_params=_params("arbitrary", "arbitrary"),
        name="moe_ffn",
    )(idx, h, w_gate, w_up, w_down)


def _combine_kernel(starts_ref, post_ref, gate_ref, x_ref, y_ref, o_ref):
    b, c = pl.program_id(0), pl.program_id(2)
    NC = pl.num_programs(2)
    n_exp, cap = y_ref.shape[1], y_ref.shape[2]
    slot = lax.broadcasted_iota(I32, (TOKEN_CHUNK, SLOT_WINDOW), 1)
    acc = x_ref[0]
    for e in range(n_exp):
        base = _window_base(starts_ref[(b * n_exp + e) * NC + c], cap)
        onehot = jnp.where(slot + base == post_ref[0, :, e:e + 1], 1.0, 0.0).astype(BF16)
        acc = acc + _dot(onehot, y_ref[0, e, pl.ds(base, SLOT_WINDOW), :]) * gate_ref[0, :, e:e + 1]
    o_ref[0] = acc


def moe_combine(starts, pos_t, gate, x, y, dsplit=2):
    B, S, D = x.shape
    E, cap = y.shape[1], y.shape[2]
    dh = D // dsplit
    tok = lambda w: pl.BlockSpec((1, TOKEN_CHUNK, w), lambda b, j, c, s: (b, c, 0))
    res = pl.BlockSpec((1, TOKEN_CHUNK, dh), lambda b, j, c, s: (b, c, j))
    return pl.pallas_call(
        _combine_kernel,
        grid_spec=pltpu.PrefetchScalarGridSpec(
            num_scalar_prefetch=1,
            grid=(B, dsplit, S // TOKEN_CHUNK),
            in_specs=[tok(E), tok(E), res,
                      pl.BlockSpec((1, E, cap, dh), lambda b, j, c, s: (b, 0, 0, j))],
            out_specs=res),
        out_shape=jax.ShapeDtypeStruct((B, S, D), F32),
        compiler_params=_params("arbitrary", "arbitrary", "arbitrary"),
        name="moe_combine",
    )(starts, pos_t, gate, x, y)


def expert_choice_moe(x, h, aff, batch, w_gate, w_up, w_down, layer):
    T, D = x.shape
    S = T // batch
    cap = max(1, CAPACITY_FACTOR * S // N_EXPERTS)
    aff = aff.reshape(batch, S, N_EXPERTS)
    pos, starts = route_topk(jnp.swapaxes(aff, 1, 2), cap)
    idx = moe_slot_tokens(starts, pos, cap)
    y = moe_ffn(idx, h.reshape(batch, S, D), w_gate, w_up, w_down, layer, cap)
    pos_t = jnp.swapaxes(pos.reshape(batch, N_EXPERTS, S), 1, 2)
    return moe_combine(starts, pos_t, aff, x.reshape(batch, S, D), y).reshape(T, D)


ATTN_TQ = 128
ATTN_TK = ATTN_TQ + 2 * HALF_WINDOW


def _t5_bucket(rel):
    half_buckets = REL_BUCKETS // 2
    max_exact = half_buckets // 2
    n = jnp.abs(rel)
    scaled = (jnp.log(jnp.maximum(n, 1).astype(jnp.float32) / max_exact)
              / math.log(REL_MAX_DISTANCE / max_exact))
    large = jnp.minimum(max_exact + (scaled * (half_buckets - max_exact)).astype(jnp.int32),
                        half_buckets - 1)
    return jnp.where(rel > 0, half_buckets, 0) + jnp.where(n < max_exact, n, large)


def _bias_kernel(table_ref, bucket_ref, o_ref):
    h = pl.program_id(1)
    bucket = bucket_ref[0]
    acc = jnp.zeros(bucket.shape, F32)
    for bk in range(REL_BUCKETS):
        acc = jnp.where(bucket == bk, table_ref[bk * C_HEADS + h], acc)
    q = lax.broadcasted_iota(I32, bucket.shape, 0)
    kc = lax.broadcasted_iota(I32, bucket.shape, 1)
    o_ref[0, 0] = jnp.where(jnp.abs(kc - HALF_WINDOW - q) <= HALF_WINDOW, acc, NEG_INF)


def attention_bias(rel_bias):
    rel = np.arange(ATTN_TK)[None, :] - HALF_WINDOW - np.arange(ATTN_TQ)[:, None]
    buckets = jnp.stack([_t5_bucket(jnp.asarray(rel * d, I32)) for d in DILATIONS]).astype(I32)
    P = len(DILATIONS)
    return pl.pallas_call(
        _bias_kernel,
        grid_spec=pltpu.PrefetchScalarGridSpec(
            num_scalar_prefetch=1,
            grid=(P, C_HEADS),
            in_specs=[pl.BlockSpec((1, ATTN_TQ, ATTN_TK), lambda p, h, t: (p, 0, 0))],
            out_specs=pl.BlockSpec((1, 1, ATTN_TQ, ATTN_TK), lambda p, h, t: (p, h, 0, 0))),
        out_shape=jax.ShapeDtypeStruct((P, C_HEADS, ATTN_TQ, ATTN_TK), F32),
        compiler_params=_params("arbitrary", "arbitrary"),
        name="attention_bias",
    )(rel_bias.reshape(-1), buckets)


def _attn_kernel(q_ref, kp_ref, km_ref, kn_ref, vp_ref, vm_ref, vn_ref, bias_ref, o_ref, lse_ref, k_buf, v_buf):
    i = pl.program_id(1)
    last = pl.num_programs(1) - 1
    hw = HALF_WINDOW
    k_buf[0:hw] = kp_ref[0]
    k_buf[hw:hw + ATTN_TQ] = km_ref[0]
    k_buf[hw + ATTN_TQ:] = kn_ref[0]
    v_buf[0:hw] = vp_ref[0]
    v_buf[hw:hw + ATTN_TQ] = vm_ref[0]
    v_buf[hw + ATTN_TQ:] = vn_ref[0]
    kc = lax.broadcasted_iota(I32, (ATTN_TQ, ATTN_TK), 1)
    in_seq = jnp.logical_and(jnp.logical_or(i > 0, kc >= hw), jnp.logical_or(i < last, kc < hw + ATTN_TQ))
    scale = C_HEAD_DIM ** -0.5
    for h in range(C_HEADS):
        cols = slice(h * C_HEAD_DIM, (h + 1) * C_HEAD_DIM)
        s = _dot_nt(q_ref[0, :, cols] * scale, k_buf[:, cols]) + bias_ref[h]
        s = jnp.where(in_seq, s, NEG_INF)
        m = jnp.max(s, axis=-1, keepdims=True)
        p = jnp.exp(s - m)
        den = jnp.sum(p, axis=-1, keepdims=True)
        o_ref[0, :, cols] = _dot((p / den).astype(BF16), v_buf[:, cols]).astype(BF16)
        lse_ref[0, :, h:h + 1] = m + jnp.log(den)


def dilated_attention(qkv, bias):
    G, n, D3 = qkv.shape
    D = D3 // 3
    nb = n // HALF_WINDOW
    r = ATTN_TQ // HALF_WINDOW
    main = lambda col: pl.BlockSpec((1, ATTN_TQ, D), lambda g, i: (g, i, col))
    prev = lambda col: pl.BlockSpec((1, HALF_WINDOW, D), lambda g, i: (g, jnp.maximum(i * r - 1, 0), col))
    nxt = lambda col: pl.BlockSpec((1, HALF_WINDOW, D), lambda g, i: (g, jnp.minimum(i * r + r, nb - 1), col))
    return pl.pallas_call(
        _attn_kernel,
        grid=(G, n // ATTN_TQ),
        in_specs=[main(0), prev(1), main(1), nxt(1), prev(2), main(2), nxt(2),
                  pl.BlockSpec(bias.shape, lambda g, i: (0, 0, 0))],
        out_specs=[pl.BlockSpec((1, ATTN_TQ, D), lambda g, i: (g, i, 0)),
                   pl.BlockSpec((1, ATTN_TQ, C_HEADS), lambda g, i: (g, i, 0))],
        out_shape=[jax.ShapeDtypeStruct((G, n, D), BF16), jax.ShapeDtypeStruct((G, n, C_HEADS), F32)],
        scratch_shapes=[pltpu.VMEM((ATTN_TK, D), BF16), pltpu.VMEM((ATTN_TK, D), BF16)],
        compiler_params=_params("parallel", "parallel"),
        name="dilated_attention",
    )(qkv, qkv, qkv, qkv, qkv, qkv, qkv, bias)


def dilated_mixture(qkv, rel_bias, batch):
    T, D3 = qkv.shape
    S = T // batch
    bias = attention_bias(rel_bias)
    outs, lses = [], []
    for p, d in enumerate(DILATIONS):
        n = S // d
        strided = qkv.reshape(batch, n, d, D3).swapaxes(1, 2).reshape(batch * d, n, D3)
        o, lse = dilated_attention(strided, bias[p])
        unstride = lambda t: t.reshape(batch, d, n, t.shape[-1]).swapaxes(1, 2).reshape(T, t.shape[-1])
        outs.append(unstride(o))
        lses.append(unstride(lse))
    return outs, lses


def _final_norm_kernel(x_ref, g_ref, o_ref):
    o_ref[...] = _rms(x_ref[...], g_ref[...])


def final_rmsnorm(x, g, tm=512):
    T, D = x.shape
    return pl.pallas_call(
        _final_norm_kernel,
        grid=(T // tm,),
        in_specs=[pl.BlockSpec((tm, D), lambda i: (i, 0)), pl.BlockSpec((1, D), lambda i: (0, 0))],
        out_specs=pl.BlockSpec((tm, D), lambda i: (i, 0)),
        out_shape=jax.ShapeDtypeStruct((T, D), F32),
        compiler_params=_params("parallel"),
        name="final_norm",
    )(x, g.reshape(1, D))


def kernel(x, mix_norm, ffn_norm, final_norm, w_in_even, w_out_even, a_ln_g, a_ln_b, a_w_s, a_b_s, b_lb_table,
           b_norm_g, w_qkv_odd, w_o_odd, rel_bias, w_router, w_gate, w_up, w_down):
    B, S, D = x.shape
    depth = mix_norm.shape[0]
    xt = x.reshape(B * S, D)
    for layer in range(depth):
        j = layer // 2
        if layer % 2 == 0:
            proj = norm_matmul(xt, mix_norm[layer], w_in_even[j].astype(BF16))
            a_out = mixer_a(proj, a_ln_g[j], a_ln_b[j], a_w_s[j], a_b_s[j])
            o_f, o_b = mixer_b(proj, b_lb_table, layer, B)
            xt, h, aff = outproj_even(a_out, o_f, o_b, proj, b_norm_g[j], w_out_even[j], xt,
                                      ffn_norm[layer], w_router[layer])
        else:
            qkv = norm_matmul(xt, mix_norm[layer], w_qkv_odd[j].astype(BF16))
            outs, lses = dilated_mixture(qkv, rel_bias, B)
            xt, h, aff = outproj_odd(outs, lses, w_o_odd[j], xt, ffn_norm[layer], w_router[layer])
        xt = expert_choice_moe(xt, h, aff, B, w_gate, w_up, w_down, layer)
    return final_rmsnorm(xt, final_norm).reshape(B, S, D)
```

```python
import functools
import math

import numpy as np
import jax
import jax.numpy as jnp
from jax import lax
from jax.experimental import pallas as pl
from jax.experimental.pallas import tpu as pltpu

F32 = jnp.float32
BF16 = jnp.bfloat16
I32 = jnp.int32
EPS = 1e-6
NEG_INF = -1e30
HIGHEST = lax.Precision.HIGHEST

LANES = 128
VMEM_LIMIT = 56 * 1024 * 1024

A_GROUPS = 4
A_CHUNK = 128
B_HEADS = 4
B_DIM = 128
GLA_CHUNK = 128
C_HEADS = 16
C_HEAD_DIM = 64
HALF_WINDOW = 64
DILATIONS = (1, 4, 16)
REL_BUCKETS = 32
REL_MAX_DISTANCE = 1024
N_EXPERTS = 16
CAPACITY_FACTOR = 2
TOKEN_CHUNK = 128
SLOT_WINDOW = 256


def _params(*sem):
    return pltpu.CompilerParams(dimension_semantics=sem, vmem_limit_bytes=VMEM_LIMIT)


def _dot(a, b, **kw):
    return jnp.dot(a, b, preferred_element_type=F32, **kw)


def _dot_nt(a, b):
    return lax.dot_general(a, b, (((1,), (1,)), ((), ())), preferred_element_type=F32)


def _dot_tn(a, b):
    return lax.dot_general(a, b, (((0,), (0,)), ((), ())), preferred_element_type=F32)


def _dot_split01(a2, x):
    hi = x.astype(BF16)
    lo = (x - hi.astype(F32)).astype(BF16)
    return _dot(a2, jnp.concatenate([hi, lo], axis=0))


def _rms(x, g):
    return x * lax.rsqrt(jnp.mean(x * x, axis=-1, keepdims=True) + EPS) * g


def _norm_matmul_kernel(x_ref, g_ref, w_ref, o_ref, *, tn):
    h = _rms(x_ref[...], g_ref[...]).astype(BF16)
    for j in range(w_ref.shape[1] // tn):
        cols = slice(j * tn, (j + 1) * tn)
        o_ref[:, cols] = _dot(h, w_ref[:, cols]).astype(o_ref.dtype)


def norm_matmul(x, g, w, tm=1024, tn=512):
    T, D = x.shape
    N = w.shape[1]
    return pl.pallas_call(
        functools.partial(_norm_matmul_kernel, tn=tn),
        grid=(T // tm,),
        in_specs=[
            pl.BlockSpec((tm, D), lambda i: (i, 0)),
            pl.BlockSpec((1, D), lambda i: (0, 0)),
            pl.BlockSpec((D, N), lambda i: (0, 0)),
        ],
        out_specs=pl.BlockSpec((tm, N), lambda i: (i, 0)),
        out_shape=jax.ShapeDtypeStruct((T, N), BF16),
        compiler_params=_params("parallel"),
        name="norm_matmul",
    )(x, g.reshape(1, D), w)


def _mixa_kernel(u_ref, v_ref, lg_ref, lb_ref, ws_ref, bs_ref, o_ref):
    tm = u_ref.shape[0]
    u = jax.nn.gelu(u_ref[...].astype(F32))
    v = jax.nn.gelu(v_ref[...].astype(F32))
    mu = jnp.mean(v, axis=-1, keepdims=True)
    vc = v - mu
    vn = vc * lax.rsqrt(jnp.mean(vc * vc, axis=-1, keepdims=True) + EPS)
    vb = (vn * lg_ref[...] + lb_ref[...]).astype(BF16)
    for n in range(tm // A_CHUNK):
        rows = slice(n * A_CHUNK, (n + 1) * A_CHUNK)
        for g in range(A_GROUPS):
            cols = slice(g * LANES, (g + 1) * LANES)
            mixed = _dot(ws_ref[g], vb[rows, cols]) + bs_ref[:, g:g + 1]
            o_ref[rows, cols] = (u[rows, cols] * mixed).astype(BF16)


def mixer_a(proj, ln_g, ln_b, w_s, b_s, tm=512):
    T = proj.shape[0]
    W = A_GROUPS * LANES
    return pl.pallas_call(
        _mixa_kernel,
        grid=(T // tm,),
        in_specs=[
            pl.BlockSpec((tm, W), lambda i: (i, 0)),
            pl.BlockSpec((tm, W), lambda i: (i, 1)),
            pl.BlockSpec((1, W), lambda i: (0, 0)),
            pl.BlockSpec((1, W), lambda i: (0, 0)),
            pl.BlockSpec((A_GROUPS, A_CHUNK, A_CHUNK), lambda i: (0, 0, 0)),
            pl.BlockSpec((A_CHUNK, A_GROUPS), lambda i: (0, 0)),
        ],
        out_specs=pl.BlockSpec((tm, W), lambda i: (i, 0)),
        out_shape=jax.ShapeDtypeStruct((T, W), BF16),
        compiler_params=_params("parallel"),
        name="mixer_a",
    )(proj, proj, ln_g.reshape(1, W), ln_b.reshape(1, W), w_s.astype(BF16), b_s.T)


def _gla_consts(C, reverse):
    t = np.arange(C)[:, None]
    r = np.arange(C)[None, :]
    L = int(round(math.log2(C)))
    spans = [(r >= t) if reverse else (r <= t), (r < t) if reverse else (r > t)]
    level = np.where(np.eye(C, dtype=bool), L, -1).astype(np.int32)
    for l in range(L):
        bs = (t >> (l + 1)) << (l + 1)
        mid = bs + (1 << l)
        if reverse:
            act_q = t < mid
            span = np.where(act_q, (r >= t) & (r < mid), (r >= mid) & (r < t))
        else:
            act_q = t >= mid
            span = np.where(act_q, (r >= mid) & (r <= t), (r > t) & (r < mid))
        spans.append(span)
        level[(bs == bs.T) & act_q & ~act_q.T] = l
    return np.tile(np.concatenate(spans, axis=0).astype(np.float32), (1, 2)), level


def _gla_kernel(qf_ref, ff_ref, if_ref, qb_ref, fb_ref, ib_ref, tbl_ref, span_ref, level_ref,
                of_ref, ob_ref, state_ref, w_buf, q_buf, k_buf, v_buf, s_buf, p_buf, *, layer):
    @pl.when(pl.program_id(1) == 0)
    def _():
        state_ref[...] = jnp.zeros_like(state_ref)

    C = qf_ref.shape[0]
    L = span_ref.shape[1] // C - 2
    heads = [(d, h, slice(h * B_DIM, (h + 1) * B_DIM)) for d in range(2) for h in range(B_HEADS)]
    tots = []
    for d, (q_ref, f_ref, i_ref) in enumerate(((qf_ref, ff_ref, if_ref), (qb_ref, fb_ref, ib_ref))):
        tb = tbl_ref[d]
        e = jnp.exp(tb - jnp.max(tb, axis=0, keepdims=True))
        lb = jnp.sum(e[0:layer + 1], axis=0, keepdims=True) / jnp.sum(e, axis=0, keepdims=True)
        f = lb + (1.0 - lb) * jax.nn.sigmoid(f_ref[...].astype(F32))
        e2 = _dot_split01(span_ref[d], jnp.log2(f))
        tots.append(jnp.exp2(e2[0:1] if d == 1 else e2[C - 1:C]))
        w_buf[d] = jnp.exp2(e2).astype(BF16)
        q_buf[d] = jax.nn.silu(q_ref[...].astype(F32)).astype(BF16)
        k_buf[d] = (1.0 - f).astype(BF16)
        v_buf[d] = i_ref[...]
    for d, h, sl in heads:
        qh, kh = q_buf[d, :, sl], k_buf[d, :, sl]
        s_buf[d, h, L] = _dot_nt(qh, kh)
        for l in range(L):
            wl = w_buf[d, (l + 2) * C:(l + 3) * C, sl]
            s_buf[d, h, l] = _dot_nt(qh * wl, kh * wl)
    for d, h, sl in heads:
        level = level_ref[d]
        p = jnp.where(level == L, s_buf[d, h, L], 0.0)
        for l in range(L):
            p = jnp.where(level == l, s_buf[d, h, l], p)
        p_buf[d, h] = p.astype(BF16)
    for d, h, sl in heads:
        o_ref = of_ref if d == 0 else ob_ref
        st = state_ref[d, h]
        vh = v_buf[d, :, sl]
        o_ref[:, sl] = _dot(p_buf[d, h], vh) + _dot_nt(q_buf[d, :, sl] * w_buf[d, 0:C, sl], st.astype(BF16))
        state_ref[d, h] = st * tots[d][:, sl] + _dot_tn(vh, k_buf[d, :, sl] * w_buf[d, C:2 * C, sl])


def mixer_b(proj, b_lb_table, layer, batch, C=GLA_CHUNK):
    T = proj.shape[0]
    W = B_HEADS * B_DIM
    n = T // batch // C
    consts = [_gla_consts(C, rev) for rev in (False, True)]
    span = jnp.asarray(np.stack([c[0] for c in consts]), BF16)
    level = jnp.asarray(np.stack([c[1] for c in consts]))
    fwd = lambda col: pl.BlockSpec((C, W), lambda b, c: (b * n + c, col))
    bwd = lambda col: pl.BlockSpec((C, W), lambda b, c: (b * n + n - 1 - c, col))
    full = lambda a: pl.BlockSpec(a.shape, lambda b, c: (0,) * a.ndim)
    return pl.pallas_call(
        functools.partial(_gla_kernel, layer=layer),
        grid=(batch, n),
        in_specs=[fwd(2), fwd(3), fwd(5), bwd(2), bwd(4), bwd(5),
                  full(b_lb_table), full(span), full(level)],
        out_specs=[pl.BlockSpec((C, W), lambda b, c: (b * n + c, 0)),
                   pl.BlockSpec((C, W), lambda b, c: (b * n + n - 1 - c, 0))],
        out_shape=[jax.ShapeDtypeStruct((T, W), F32)] * 2,
        scratch_shapes=[pltpu.VMEM((2, B_HEADS, B_DIM, B_DIM), F32),
                        pltpu.VMEM((2, span.shape[1], W), BF16),
                        pltpu.VMEM((2, C, W), BF16), pltpu.VMEM((2, C, W), BF16), pltpu.VMEM((2, C, W), BF16),
                        pltpu.VMEM((2, B_HEADS, span.shape[1] // C - 1, C, C), F32),
                        pltpu.VMEM((2, B_HEADS, C, C), BF16)],
        compiler_params=_params("arbitrary", "arbitrary"),
        name="mixer_b",
    )(proj, proj, proj, proj, proj, proj, b_lb_table, span, level)


def _router_epilogue(x_new, g_ref, wr_ref, x_ref, h_ref, aff_ref):
    x_ref[...] = x_new
    h = _rms(x_new, g_ref[...])
    h_ref[...] = h
    w = wr_ref[...]
    h_hi, w_hi = h.astype(BF16), w.astype(BF16)
    h_lo, w_lo = (h - h_hi.astype(F32)).astype(BF16), (w - w_hi.astype(F32)).astype(BF16)
    logits = _dot(h_hi, w_hi) + _dot(h_lo, w_hi) + _dot(h_hi, w_lo)
    e = jnp.exp(logits - jnp.max(logits, axis=-1, keepdims=True))
    aff_ref[...] = e / jnp.sum(e, axis=-1, keepdims=True)


def _outproj_even_kernel(a_ref, of_ref, ob_ref, gate_ref, bng_ref, w_ref, x_ref, g_ref, wr_ref,
                         xo_ref, h_ref, aff_ref):
    o = of_ref[...] + ob_ref[...]
    parts = []
    for h in range(B_HEADS):
        oh = o[:, h * B_DIM:(h + 1) * B_DIM]
        parts.append(oh * lax.rsqrt(jnp.mean(oh * oh, axis=-1, keepdims=True) + EPS))
    on = jnp.concatenate(parts, axis=-1) * bng_ref[...] * jax.nn.sigmoid(gate_ref[...].astype(F32))
    wa = a_ref.shape[1]
    mixed = _dot(a_ref[...], w_ref[0:wa, :]) + _dot(on.astype(BF16), w_ref[wa:, :])
    _router_epilogue(x_ref[...] + mixed, g_ref, wr_ref, xo_ref, h_ref, aff_ref)


def _router_out(T, D, tm):
    specs = [pl.BlockSpec((tm, D), lambda i: (i, 0)), pl.BlockSpec((tm, D), lambda i: (i, 0)),
             pl.BlockSpec((tm, N_EXPERTS), lambda i: (i, 0))]
    shapes = [jax.ShapeDtypeStruct((T, D), F32), jax.ShapeDtypeStruct((T, D), F32),
              jax.ShapeDtypeStruct((T, N_EXPERTS), F32)]
    return specs, shapes


def outproj_even(a_out, o_f, o_b, proj, b_norm_g, w_out, x, ffn_g, w_router, tm=512):
    T, D = x.shape
    W = a_out.shape[1]
    row = lambda w, col=0: pl.BlockSpec((tm, w), lambda i: (i, col))
    full = lambda s: pl.BlockSpec(s, lambda i: (0, 0))
    out_specs, out_shapes = _router_out(T, D, tm)
    return pl.pallas_call(
        _outproj_even_kernel,
        grid=(T // tm,),
        in_specs=[row(W), row(W), row(W), row(W, 6), full((1, W)), full(w_out.shape), row(D),
                  full((1, D)), full(w_router.shape)],
        out_specs=out_specs,
        out_shape=out_shapes,
        compiler_params=_params("parallel"),
        name="outproj_even",
    )(a_out, o_f, o_b, proj, b_norm_g.reshape(1, W), w_out.astype(BF16), x, ffn_g.reshape(1, D), w_router)


def _outproj_odd_kernel(*refs):
    P = len(DILATIONS)
    o_refs, l_refs = refs[:P], refs[P:2 * P]
    ex_ref, w_ref, x_ref, g_ref, wr_ref, xo_ref, h_ref, aff_ref = refs[2 * P:]
    ls = [l_ref[:, 0:C_HEADS] for l_ref in l_refs]
    m = functools.reduce(jnp.maximum, ls)
    es = [jnp.exp(l - m) for l in ls]
    den = functools.reduce(jnp.add, es)
    attn = None
    for e, o_ref in zip(es, o_refs):
        wfull = _dot((e / den).astype(BF16), ex_ref[...])
        term = wfull * o_ref[...].astype(F32)
        attn = term if attn is None else attn + term
    mixed = _dot(attn.astype(BF16), w_ref[...])
    _router_epilogue(x_ref[...] + mixed, g_ref, wr_ref, xo_ref, h_ref, aff_ref)


def outproj_odd(outs, lses, w_o, x, ffn_g, w_router, tm=512):
    T, D = x.shape
    P = len(outs)
    expand = jnp.asarray(np.kron(np.eye(C_HEADS), np.ones((1, C_HEAD_DIM))), BF16)
    row = lambda w: pl.BlockSpec((tm, w), lambda i: (i, 0))
    full = lambda s: pl.BlockSpec(s, lambda i: (0, 0))
    out_specs, out_shapes = _router_out(T, D, tm)
    return pl.pallas_call(
        _outproj_odd_kernel,
        grid=(T // tm,),
        in_specs=[row(D)] * P + [row(LANES)] * P + [full(expand.shape), full(w_o.shape), row(D),
                                                    full((1, D)), full(w_router.shape)],
        out_specs=out_specs,
        out_shape=out_shapes,
        compiler_params=_params("parallel"),
        name="outproj_odd",
    )(*outs, *lses, expand, w_o.astype(BF16), x, ffn_g.reshape(1, D), w_router)


def _topk_kernel(aff_ref, upper_ref, strict_ref, pos_ref, start_ref, *, cap):
    a = aff_ref[0]
    E, NC, _ = a.shape
    bits = lax.bitcast_convert_type(a, I32)

    def count(mask):
        return jnp.sum(jnp.sum(mask.astype(F32), axis=2, keepdims=True), axis=1, keepdims=True)

    def search(i, thr):
        cand = thr | jnp.left_shift(jnp.int32(1), 30 - i)
        return jnp.where(count(bits >= cand) >= cap, cand, thr)

    thr = lax.fori_loop(0, 31, search, jnp.zeros((E, 1, 1), I32))
    gt = bits > thr
    eq = bits == thr
    need = cap - count(gt)

    def prefix(mask):
        m2 = mask.astype(BF16).reshape(E * NC, LANES)
        within = _dot(m2, upper_ref[...])
        total = within[:, LANES - 1:LANES].astype(BF16)
        tot_b = jnp.broadcast_to(total, (E * NC, LANES))
        starts = jnp.concatenate(
            [_dot(strict_ref[...], tot_b[e * NC:(e + 1) * NC]) for e in range(E)], axis=0)
        return (within + starts).reshape(E, NC, LANES), starts.reshape(E, NC, LANES)

    eq_incl, _ = prefix(eq)
    sel = gt | (eq & (eq_incl - 1.0 < need))
    sel_incl, starts = prefix(sel)
    pos_ref[0] = jnp.where(sel, sel_incl - 1.0, -1.0).astype(I32)
    start_ref[0] = starts.astype(I32)


def route_topk(aff_t, cap):
    B, E, S = aff_t.shape
    NC = S // TOKEN_CHUNK
    t = np.arange(LANES)
    upper = jnp.asarray(t[:, None] <= t[None, :], BF16)
    c = np.arange(NC)
    strict = jnp.asarray(c[None, :] < c[:, None], BF16)
    blk = pl.BlockSpec((1, E, NC, LANES), lambda b: (b, 0, 0, 0))
    pos, starts = pl.pallas_call(
        functools.partial(_topk_kernel, cap=cap),
        grid=(B,),
        in_specs=[blk, pl.BlockSpec(upper.shape, lambda b: (0, 0)), pl.BlockSpec(strict.shape, lambda b: (0, 0))],
        out_specs=[blk, blk],
        out_shape=[jax.ShapeDtypeStruct((B, E, NC, LANES), I32)] * 2,
        compiler_params=_params("parallel"),
        name="route_topk",
    )(aff_t.reshape(B, E, NC, LANES), upper, strict)
    return pos, starts[..., 0].reshape(-1)


def _window_base(start, cap):
    base = jnp.minimum((start // TOKEN_CHUNK) * TOKEN_CHUNK, cap - SLOT_WINDOW)
    return pl.multiple_of(base, TOKEN_CHUNK)


def _index_kernel(starts_ref, pos_ref, sel_ref, idx_ref, acc_ref):
    b, e = pl.program_id(0), pl.program_id(1)
    n_exp = pl.num_programs(1)
    cap = acc_ref.shape[0] * TOKEN_CHUNK
    NC = pos_ref.shape[2]
    acc_ref[...] = jnp.zeros_like(acc_ref)
    slot = lax.broadcasted_iota(I32, (SLOT_WINDOW, TOKEN_CHUNK), 0)

    def body(c, carry):
        base = _window_base(starts_ref[(b * n_exp + e) * NC + c], cap)
        onehot = jnp.where(slot + base == pos_ref[0, 0, pl.ds(c, 1), :], 1.0, 0.0).astype(BF16)
        hit = _dot_nt(sel_ref[...], onehot)
        tok = hit[0:1] + hit[1:2] * lax.convert_element_type(c * TOKEN_CHUNK, F32)
        j = base // TOKEN_CHUNK
        acc_ref[j, 0:1, :] += tok[:, 0:TOKEN_CHUNK]
        acc_ref[j + 1, 0:1, :] += tok[:, TOKEN_CHUNK:]
        return carry

    lax.fori_loop(0, NC, body, 0, unroll=8)
    idx_ref[0, 0] = acc_ref[...].astype(I32)


def moe_slot_tokens(starts, pos, cap):
    B, E, NC, _ = pos.shape
    nt = cap // TOKEN_CHUNK
    sel = np.zeros((8, TOKEN_CHUNK), np.float32)
    sel[0] = np.arange(TOKEN_CHUNK)
    sel[1] = 1.0
    idx = pl.pallas_call(
        _index_kernel,
        grid_spec=pltpu.PrefetchScalarGridSpec(
            num_scalar_prefetch=1,
            grid=(B, E),
            in_specs=[pl.BlockSpec((1, 1, NC, LANES), lambda b, e, s: (b, e, 0, 0)),
                      pl.BlockSpec((8, TOKEN_CHUNK), lambda b, e, s: (0, 0))],
            out_specs=pl.BlockSpec((1, 1, nt, 8, LANES), lambda b, e, s: (b, e, 0, 0, 0)),
            scratch_shapes=[pltpu.VMEM((nt, 8, LANES), F32)]),
        out_shape=jax.ShapeDtypeStruct((B, E, nt, 8, LANES), I32),
        compiler_params=_params("arbitrary", "arbitrary"),
        name="moe_slot_tokens",
    )(starts, pos, jnp.asarray(sel, BF16))
    return idx[:, :, :, 0, :].reshape(-1)


def _ffn_kernel(idx_ref, h_ref, wg_ref, wu_ref, wd_ref, y_ref, x_buf, sem, wgb_ref, wub_ref, wdb_ref, *, tr):
    e, b = pl.program_id(0), pl.program_id(1)
    n_exp, nb = pl.num_programs(0), pl.num_programs(1)
    cap = x_buf.shape[1]
    step = e * nb + b

    def gather(st, buf, start):
        eb = st // nb
        bb = st - eb * nb
        if not start:
            pltpu.make_async_copy(h_ref.at[bb, pl.ds(0, cap), :], x_buf.at[buf], sem.at[buf]).wait()
            return

        def body(i, carry):
            tok = idx_ref[(bb * n_exp + eb) * cap + i]
            pltpu.make_async_copy(h_ref.at[bb, pl.ds(tok, 1), :], x_buf.at[buf, pl.ds(i, 1), :],
                                  sem.at[buf]).start()
            return carry

        lax.fori_loop(0, cap, body, 0, unroll=8)

    cur = lax.rem(step, 2)

    @pl.when(step == 0)
    def _():
        gather(step, 0, True)

    @pl.when(step + 1 < n_exp * nb)
    def _():
        gather(step + 1, 1 - cur, True)

    @pl.when(b == 0)
    def _():
        wgb_ref[...] = wg_ref[0, 0].astype(BF16)
        wub_ref[...] = wu_ref[0, 0].astype(BF16)
        wdb_ref[...] = wd_ref[0, 0].astype(BF16)

    gather(step, cur, False)
    for r in range(cap // tr):
        rows = pl.ds(r * tr, tr)
        x = x_buf[cur, rows, :].astype(BF16)
        g = _dot(x, wgb_ref[...])
        u = _dot(x, wub_ref[...])
        mid = (g * jax.nn.sigmoid(g) * u).astype(BF16)
        y_ref[0, 0, rows, :] = _dot(mid, wdb_ref[...]).astype(BF16)


def moe_ffn(idx, h, w_gate, w_up, w_down, layer, cap, tr=256):
    B, S, D = h.shape
    E, F = w_gate.shape[1], w_gate.shape[-1]
    wspec = lambda r, c: pl.BlockSpec((1, 1, r, c), lambda e, b, s: (layer, e, 0, 0))
    return pl.pallas_call(
        functools.partial(_ffn_kernel, tr=min(tr, cap)),
        grid_spec=pltpu.PrefetchScalarGridSpec(
            num_scalar_prefetch=1,
            grid=(E, B),
            in_specs=[pl.BlockSpec(memory_space=pl.ANY), wspec(D, F), wspec(D, F), wspec(F, D)],
            out_specs=pl.BlockSpec((1, 1, cap, D), lambda e, b, s: (b, e, 0, 0)),
            scratch_shapes=[pltpu.VMEM((2, cap, D), F32), pltpu.SemaphoreType.DMA((2,)),
                            pltpu.VMEM((D, F), BF16), pltpu.VMEM((D, F), BF16), pltpu.VMEM((F, D), BF16)]),
        out_shape=jax.ShapeDtypeStruct((B, E, cap, D), BF16),
        compiler_params=_params("arbitrary", "arbitrary"),
        name="moe_ffn",
    )(idx, h, w_gate, w_up, w_down)


def _combine_kernel(starts_ref, post_ref, gate_ref, x_ref, y_ref, o_ref):
    b, c = pl.program_id(0), pl.program_id(2)
    NC = pl.num_programs(2)
    n_exp, cap = y_ref.shape[1], y_ref.shape[2]
    slot = lax.broadcasted_iota(I32, (TOKEN_CHUNK, SLOT_WINDOW), 1)
    acc = x_ref[0]
    for e in range(n_exp):
        base = _window_base(starts_ref[(b * n_exp + e) * NC + c], cap)
        onehot = jnp.where(slot + base == post_ref[0, :, e:e + 1], 1.0, 0.0).astype(BF16)
        acc = acc + _dot(onehot, y_ref[0, e, pl.ds(base, SLOT_WINDOW), :]) * gate_ref[0, :, e:e + 1]
    o_ref[0] = acc


def moe_combine(starts, pos_t, gate, x, y, dsplit=2):
    B, S, D = x.shape
    E, cap = y.shape[1], y.shape[2]
    dh = D // dsplit
    tok = lambda w: pl.BlockSpec((1, TOKEN_CHUNK, w), lambda b, j, c, s: (b, c, 0))
    res = pl.BlockSpec((1, TOKEN_CHUNK, dh), lambda b, j, c, s: (b, c, j))
    return pl.pallas_call(
        _combine_kernel,
        grid_spec=pltpu.PrefetchScalarGridSpec(
            num_scalar_prefetch=1,
            grid=(B, dsplit, S // TOKEN_CHUNK),
            in_specs=[tok(E), tok(E), res,
                      pl.BlockSpec((1, E, cap, dh), lambda b, j, c, s: (b, 0, 0, j))],
            out_specs=res),
        out_shape=jax.ShapeDtypeStruct((B, S, D), F32),
        compiler_params=_params("arbitrary", "arbitrary", "arbitrary"),
        name="moe_combine",
    )(starts, pos_t, gate, x, y)


def expert_choice_moe(x, h, aff, batch, w_gate, w_up, w_down, layer):
    T, D = x.shape
    S = T // batch
    cap = max(1, CAPACITY_FACTOR * S // N_EXPERTS)
    aff = aff.reshape(batch, S, N_EXPERTS)
    pos, starts = route_topk(jnp.swapaxes(aff, 1, 2), cap)
    idx = moe_slot_tokens(starts, pos, cap)
    y = moe_ffn(idx, h.reshape(batch, S, D), w_gate, w_up, w_down, layer, cap)
    pos_t = jnp.swapaxes(pos.reshape(batch, N_EXPERTS, S), 1, 2)
    return moe_combine(starts, pos_t, aff, x.reshape(batch, S, D), y).reshape(T, D)


ATTN_TQ = 128
ATTN_TK = ATTN_TQ + 2 * HALF_WINDOW


def _t5_bucket(rel):
    half_buckets = REL_BUCKETS // 2
    max_exact = half_buckets // 2
    n = jnp.abs(rel)
    scaled = (jnp.log(jnp.maximum(n, 1).astype(jnp.float32) / max_exact)
              / math.log(REL_MAX_DISTANCE / max_exact))
    large = jnp.minimum(max_exact + (scaled * (half_buckets - max_exact)).astype(jnp.int32),
                        half_buckets - 1)
    return jnp.where(rel > 0, half_buckets, 0) + jnp.where(n < max_exact, n, large)


def _bias_kernel(table_ref, bucket_ref, o_ref):
    v, h = pl.program_id(1), pl.program_id(2)
    bucket = bucket_ref[0]
    acc = jnp.zeros(bucket.shape, F32)
    for bk in range(REL_BUCKETS):
        acc = jnp.where(bucket == bk, table_ref[bk * C_HEADS + h], acc)
    q = lax.broadcasted_iota(I32, bucket.shape, 0)
    kc = lax.broadcasted_iota(I32, bucket.shape, 1)
    in_band = jnp.abs(kc - HALF_WINDOW - q) <= HALF_WINDOW
    in_seq = jnp.logical_and(jnp.logical_or(v > 0, kc >= HALF_WINDOW),
                             jnp.logical_or(v < 2, kc < HALF_WINDOW + ATTN_TQ))
    o_ref[0, 0, 0] = jnp.where(jnp.logical_and(in_band, in_seq), acc, NEG_INF)


def attention_bias(rel_bias):
    rel = np.arange(ATTN_TK)[None, :] - HALF_WINDOW - np.arange(ATTN_TQ)[:, None]
    buckets = jnp.stack([_t5_bucket(jnp.asarray(rel * d, I32)) for d in DILATIONS]).astype(I32)
    P = len(DILATIONS)
    return pl.pallas_call(
        _bias_kernel,
        grid_spec=pltpu.PrefetchScalarGridSpec(
            num_scalar_prefetch=1,
            grid=(P, 3, C_HEADS),
            in_specs=[pl.BlockSpec((1, ATTN_TQ, ATTN_TK), lambda p, v, h, t: (p, 0, 0))],
            out_specs=pl.BlockSpec((1, 1, 1, ATTN_TQ, ATTN_TK), lambda p, v, h, t: (p, v, h, 0, 0))),
        out_shape=jax.ShapeDtypeStruct((P, 3, C_HEADS, ATTN_TQ, ATTN_TK), F32),
        compiler_params=_params("arbitrary", "arbitrary", "arbitrary"),
        name="attention_bias",
    )(rel_bias.reshape(-1), buckets)


def _attn_kernel(q_ref, kp_ref, km_ref, kn_ref, vp_ref, vm_ref, vn_ref, bias_ref, o_ref, lse_ref,
                 k_buf, v_buf, s_buf, p_buf):
    i = pl.program_id(2)
    last = pl.num_programs(2) - 1
    hw = HALF_WINDOW
    k_buf[0:hw] = kp_ref[0]
    k_buf[hw:hw + ATTN_TQ] = km_ref[0]
    k_buf[hw + ATTN_TQ:] = kn_ref[0]
    v_buf[0:hw] = vp_ref[0]
    v_buf[hw:hw + ATTN_TQ] = vm_ref[0]
    v_buf[hw + ATTN_TQ:] = vn_ref[0]
    variant = jnp.where(i == 0, 0, jnp.where(i == last, 2, 1))
    first_head = lax.broadcasted_iota(I32, (ATTN_TQ, LANES), 1) < C_HEAD_DIM
    scale = C_HEAD_DIM ** -0.5
    for pair in range(C_HEADS // 2):
        cols = slice(pair * LANES, (pair + 1) * LANES)
        q = q_ref[0, :, cols] * scale
        for half in range(2):
            mine = first_head if half == 0 else jnp.logical_not(first_head)
            s_buf[2 * pair + half] = (_dot_nt(jnp.where(mine, q, jnp.zeros_like(q)), k_buf[:, cols])
                                      + bias_ref[variant, 2 * pair + half])
    lse_ref[...] = jnp.zeros_like(lse_ref)
    for h in range(C_HEADS):
        s = s_buf[h]
        m = jnp.max(s, axis=-1, keepdims=True)
        p = jnp.exp(s - m)
        den = jnp.sum(p, axis=-1, keepdims=True)
        p_buf[h] = (p / den).astype(BF16)
        lse_ref[0, :, h:h + 1] = m + jnp.log(den)
    for pair in range(C_HEADS // 2):
        cols = slice(pair * LANES, (pair + 1) * LANES)
        o_ref[0, :, cols] = jnp.where(first_head, _dot(p_buf[2 * pair], v_buf[:, cols]),
                                      _dot(p_buf[2 * pair + 1], v_buf[:, cols])).astype(BF16)


def dilated_attention(qkv, bias, batch, d):
    T, D3 = qkv.shape
    D = D3 // 3
    n = T // batch // d
    nb = n // HALF_WINDOW
    r = ATTN_TQ // HALF_WINDOW
    assert n // ATTN_TQ >= 2
    main = lambda c: pl.BlockSpec((1, ATTN_TQ, D), lambda b, j, i: (b, i, 3 * j + c))
    prev = lambda c: pl.BlockSpec((1, HALF_WINDOW, D), lambda b, j, i: (b, jnp.maximum(i * r - 1, 0), 3 * j + c))
    nxt = lambda c: pl.BlockSpec((1, HALF_WINDOW, D),
                                 lambda b, j, i: (b, jnp.minimum(i * r + r, nb - 1), 3 * j + c))
    view = qkv.reshape(batch, n, d * D3)
    o, lse = pl.pallas_call(
        _attn_kernel,
        grid=(batch, d, n // ATTN_TQ),
        in_specs=[main(0), prev(1), main(1), nxt(1), prev(2), main(2), nxt(2),
                  pl.BlockSpec(bias.shape, lambda b, j, i: (0, 0, 0, 0))],
        out_specs=[pl.BlockSpec((1, ATTN_TQ, D), lambda b, j, i: (b, i, j)),
                   pl.BlockSpec((1, ATTN_TQ, LANES), lambda b, j, i: (b, i, j))],
        out_shape=[jax.ShapeDtypeStruct((batch, n, d * D), BF16),
                   jax.ShapeDtypeStruct((batch, n, d * LANES), F32)],
        scratch_shapes=[pltpu.VMEM((ATTN_TK, D), BF16), pltpu.VMEM((ATTN_TK, D), BF16),
                        pltpu.VMEM((C_HEADS, ATTN_TQ, ATTN_TK), F32), pltpu.VMEM((C_HEADS, ATTN_TQ, ATTN_TK), BF16)],
        compiler_params=_params("parallel", "parallel", "parallel"),
        name="dilated_attention",
    )(view, view, view, view, view, view, view, bias)
    return o.reshape(T, D), lse.reshape(T, LANES)


def dilated_mixture(qkv, rel_bias, batch):
    bias = attention_bias(rel_bias)
    results = [dilated_attention(qkv, bias[p], batch, d) for p, d in enumerate(DILATIONS)]
    return [o for o, _ in results], [lse for _, lse in results]


def _final_norm_kernel(x_ref, g_ref, o_ref):
    o_ref[...] = _rms(x_ref[...], g_ref[...])


def final_rmsnorm(x, g, tm=512):
    T, D = x.shape
    return pl.pallas_call(
        _final_norm_kernel,
        grid=(T // tm,),
        in_specs=[pl.BlockSpec((tm, D), lambda i: (i, 0)), pl.BlockSpec((1, D), lambda i: (0, 0))],
        out_specs=pl.BlockSpec((tm, D), lambda i: (i, 0)),
        out_shape=jax.ShapeDtypeStruct((T, D), F32),
        compiler_params=_params("parallel"),
        name="final_norm",
    )(x, g.reshape(1, D))


def kernel(x, mix_norm, ffn_norm, final_norm, w_in_even, w_out_even, a_ln_g, a_ln_b, a_w_s, a_b_s, b_lb_table,
           b_norm_g, w_qkv_odd, w_o_odd, rel_bias, w_router, w_gate, w_up, w_down):
    B, S, D = x.shape
    depth = mix_norm.shape[0]
    xt = x.reshape(B * S, D)
    for layer in range(depth):
        j = layer // 2
        if layer % 2 == 0:
            proj = norm_matmul(xt, mix_norm[layer], w_in_even[j].astype(BF16))
            a_out = mixer_a(proj, a_ln_g[j], a_ln_b[j], a_w_s[j], a_b_s[j])
            o_f, o_b = mixer_b(proj, b_lb_table, layer, B)
            xt, h, aff = outproj_even(a_out, o_f, o_b, proj, b_norm_g[j], w_out_even[j], xt,
                                      ffn_norm[layer], w_router[layer])
        else:
            qkv = norm_matmul(xt, mix_norm[layer], w_qkv_odd[j].astype(BF16))
            outs, lses = dilated_mixture(qkv, rel_bias, B)
            xt, h, aff = outproj_odd(outs, lses, w_o_odd[j], xt, ffn_norm[layer], w_router[layer])
        xt = expert_choice_moe(xt, h, aff, B, w_gate, w_up, w_down, layer)
    return final_rmsnorm(xt, final_norm).reshape(B, S, D)
```

```python
import functools
import math

import numpy as np
import jax
import jax.numpy as jnp
from jax import lax
from jax.experimental import pallas as pl
from jax.experimental.pallas import tpu as pltpu

F32 = jnp.float32
BF16 = jnp.bfloat16
I32 = jnp.int32
EPS = 1e-6
NEG_INF = -1e30
HIGHEST = lax.Precision.HIGHEST

LANES = 128
VMEM_LIMIT = 56 * 1024 * 1024

A_GROUPS = 4
A_CHUNK = 128
B_HEADS = 4
B_DIM = 128
GLA_CHUNK = 128
C_HEADS = 16
C_HEAD_DIM = 64
HALF_WINDOW = 64
DILATIONS = (1, 4, 16)
REL_BUCKETS = 32
REL_MAX_DISTANCE = 1024
N_EXPERTS = 16
CAPACITY_FACTOR = 2
TOKEN_CHUNK = 128
SLOT_WINDOW = 256


def _params(*sem):
    return pltpu.CompilerParams(dimension_semantics=sem, vmem_limit_bytes=VMEM_LIMIT)


def _dot(a, b, **kw):
    return jnp.dot(a, b, preferred_element_type=F32, **kw)


def _dot_nt(a, b):
    return lax.dot_general(a, b, (((1,), (1,)), ((), ())), preferred_element_type=F32)


def _dot_tn(a, b):
    return lax.dot_general(a, b, (((0,), (0,)), ((), ())), preferred_element_type=F32)


def _dot_split01(a2, x):
    hi = x.astype(BF16)
    lo = (x - hi.astype(F32)).astype(BF16)
    return _dot(a2, jnp.concatenate([hi, lo], axis=0))


def _rms(x, g):
    return x * lax.rsqrt(jnp.mean(x * x, axis=-1, keepdims=True) + EPS) * g


def _norm_matmul_kernel(x_ref, g_ref, w_ref, *rest, tn, dilations):
    o_refs, scratch = rest[:len(dilations)], rest[len(dilations):]
    tm, N = x_ref.shape[0], w_ref.shape[1]
    h = _rms(x_ref[...], g_ref[...]).astype(BF16)
    for j in range(N // tn):
        res = _dot(h, w_ref[:, j * tn:(j + 1) * tn])
        if scratch:
            for t in range(tn // LANES):
                scratch[0][t] = res[:, t * LANES:(t + 1) * LANES]
        for o_ref, d in zip(o_refs, dilations):
            if d == 1:
                o_ref[0, :, j * tn:(j + 1) * tn] = res.astype(BF16)
                continue
            for r in range(d):
                for t in range(tn // LANES):
                    lane0 = r * N + j * tn + t * LANES
                    o_ref[0, :, lane0:lane0 + LANES] = (
                        scratch[0][t, pl.ds(r, tm // d, stride=d), :].astype(BF16))


def norm_matmul(x, g, w, batch, dilations=(1,), tm=1024, tn=512):
    T, D = x.shape
    N = w.shape[1]
    S = T // batch
    tiles = S // tm
    strided = any(d > 1 for d in dilations)
    return pl.pallas_call(
        functools.partial(_norm_matmul_kernel, tn=tn, dilations=dilations),
        grid=(T // tm,),
        in_specs=[
            pl.BlockSpec((tm, D), lambda i: (i, 0)),
            pl.BlockSpec((1, D), lambda i: (0, 0)),
            pl.BlockSpec((D, N), lambda i: (0, 0)),
        ],
        out_specs=[pl.BlockSpec((1, tm // d, d * N), lambda i: (i // tiles, i % tiles, 0)) for d in dilations],
        out_shape=[jax.ShapeDtypeStruct((batch, S // d, d * N), BF16) for d in dilations],
        scratch_shapes=[pltpu.VMEM((tn // LANES, tm, LANES), F32)] if strided else [],
        compiler_params=_params("parallel"),
        name="norm_matmul",
    )(x, g.reshape(1, D), w)


def _mixa_kernel(u_ref, v_ref, lg_ref, lb_ref, ws_ref, bs_ref, o_ref):
    tm = u_ref.shape[0]
    u = jax.nn.gelu(u_ref[...].astype(F32))
    v = jax.nn.gelu(v_ref[...].astype(F32))
    mu = jnp.mean(v, axis=-1, keepdims=True)
    vc = v - mu
    vn = vc * lax.rsqrt(jnp.mean(vc * vc, axis=-1, keepdims=True) + EPS)
    vb = (vn * lg_ref[...] + lb_ref[...]).astype(BF16)
    for n in range(tm // A_CHUNK):
        rows = slice(n * A_CHUNK, (n + 1) * A_CHUNK)
        for g in range(A_GROUPS):
            cols = slice(g * LANES, (g + 1) * LANES)
            mixed = _dot(ws_ref[g], vb[rows, cols]) + bs_ref[:, g:g + 1]
            o_ref[rows, cols] = (u[rows, cols] * mixed).astype(BF16)


def mixer_a(proj, ln_g, ln_b, w_s, b_s, tm=512):
    T = proj.shape[0]
    W = A_GROUPS * LANES
    return pl.pallas_call(
        _mixa_kernel,
        grid=(T // tm,),
        in_specs=[
            pl.BlockSpec((tm, W), lambda i: (i, 0)),
            pl.BlockSpec((tm, W), lambda i: (i, 1)),
            pl.BlockSpec((1, W), lambda i: (0, 0)),
            pl.BlockSpec((1, W), lambda i: (0, 0)),
            pl.BlockSpec((A_GROUPS, A_CHUNK, A_CHUNK), lambda i: (0, 0, 0)),
            pl.BlockSpec((A_CHUNK, A_GROUPS), lambda i: (0, 0)),
        ],
        out_specs=pl.BlockSpec((tm, W), lambda i: (i, 0)),
        out_shape=jax.ShapeDtypeStruct((T, W), BF16),
        compiler_params=_params("parallel"),
        name="mixer_a",
    )(proj, proj, ln_g.reshape(1, W), ln_b.reshape(1, W), w_s.astype(BF16), b_s.T)


def _gla_consts(C, reverse):
    t = np.arange(C)[:, None]
    r = np.arange(C)[None, :]
    L = int(round(math.log2(C)))
    spans = [(r >= t) if reverse else (r <= t), (r < t) if reverse else (r > t)]
    level = np.where(np.eye(C, dtype=bool), L, -1).astype(np.int32)
    for l in range(L):
        bs = (t >> (l + 1)) << (l + 1)
        mid = bs + (1 << l)
        if reverse:
            act_q = t < mid
            span = np.where(act_q, (r >= t) & (r < mid), (r >= mid) & (r < t))
        else:
            act_q = t >= mid
            span = np.where(act_q, (r >= mid) & (r <= t), (r > t) & (r < mid))
        spans.append(span)
        level[(bs == bs.T) & act_q & ~act_q.T] = l
    return np.tile(np.concatenate(spans, axis=0).astype(np.float32), (1, 2)), level


def _gla_kernel(qf_ref, ff_ref, if_ref, qb_ref, fb_ref, ib_ref, tbl_ref, span_ref, level_ref,
                of_ref, ob_ref, state_ref, w_buf, q_buf, k_buf, v_buf, s_buf, p_buf, *, layer):
    @pl.when(pl.program_id(1) == 0)
    def _():
        state_ref[...] = jnp.zeros_like(state_ref)

    C = qf_ref.shape[0]
    L = span_ref.shape[1] // C - 2
    heads = [(d, h, slice(h * B_DIM, (h + 1) * B_DIM)) for d in range(2) for h in range(B_HEADS)]
    tots = []
    for d, (q_ref, f_ref, i_ref) in enumerate(((qf_ref, ff_ref, if_ref), (qb_ref, fb_ref, ib_ref))):
        tb = tbl_ref[d]
        e = jnp.exp(tb - jnp.max(tb, axis=0, keepdims=True))
        lb = jnp.sum(e[0:layer + 1], axis=0, keepdims=True) / jnp.sum(e, axis=0, keepdims=True)
        f = lb + (1.0 - lb) * jax.nn.sigmoid(f_ref[...].astype(F32))
        e2 = _dot_split01(span_ref[d], jnp.log2(f))
        tots.append(jnp.exp2(e2[0:1] if d == 1 else e2[C - 1:C]))
        w_buf[d] = jnp.exp2(e2).astype(BF16)
        q_buf[d] = jax.nn.silu(q_ref[...].astype(F32)).astype(BF16)
        k_buf[d] = (1.0 - f).astype(BF16)
        v_buf[d] = i_ref[...]
    for d, h, sl in heads:
        qh, kh = q_buf[d, :, sl], k_buf[d, :, sl]
        s_buf[d, h, L] = _dot_nt(qh, kh)
        for l in range(L):
            wl = w_buf[d, (l + 2) * C:(l + 3) * C, sl]
            s_buf[d, h, l] = _dot_nt(qh * wl, kh * wl)
    for d, h, sl in heads:
        level = level_ref[d]
        p = jnp.where(level == L, s_buf[d, h, L], 0.0)
        for l in range(L):
            p = jnp.where(level == l, s_buf[d, h, l], p)
        p_buf[d, h] = p.astype(BF16)
    for d, h, sl in heads:
        o_ref = of_ref if d == 0 else ob_ref
        st = state_ref[d, h]
        vh = v_buf[d, :, sl]
        o_ref[:, sl] = _dot(p_buf[d, h], vh) + _dot_nt(q_buf[d, :, sl] * w_buf[d, 0:C, sl], st.astype(BF16))
        state_ref[d, h] = st * tots[d][:, sl] + _dot_tn(vh, k_buf[d, :, sl] * w_buf[d, C:2 * C, sl])


def mixer_b(proj, b_lb_table, layer, batch, C=GLA_CHUNK):
    T = proj.shape[0]
    W = B_HEADS * B_DIM
    n = T // batch // C
    consts = [_gla_consts(C, rev) for rev in (False, True)]
    span = jnp.asarray(np.stack([c[0] for c in consts]), BF16)
    level = jnp.asarray(np.stack([c[1] for c in consts]))
    fwd = lambda col: pl.BlockSpec((C, W), lambda b, c: (b * n + c, col))
    bwd = lambda col: pl.BlockSpec((C, W), lambda b, c: (b * n + n - 1 - c, col))
    full = lambda a: pl.BlockSpec(a.shape, lambda b, c: (0,) * a.ndim)
    return pl.pallas_call(
        functools.partial(_gla_kernel, layer=layer),
        grid=(batch, n),
        in_specs=[fwd(2), fwd(3), fwd(5), bwd(2), bwd(4), bwd(5),
                  full(b_lb_table), full(span), full(level)],
        out_specs=[pl.BlockSpec((C, W), lambda b, c: (b * n + c, 0)),
                   pl.BlockSpec((C, W), lambda b, c: (b * n + n - 1 - c, 0))],
        out_shape=[jax.ShapeDtypeStruct((T, W), F32)] * 2,
        scratch_shapes=[pltpu.VMEM((2, B_HEADS, B_DIM, B_DIM), F32),
                        pltpu.VMEM((2, span.shape[1], W), BF16),
                        pltpu.VMEM((2, C, W), BF16), pltpu.VMEM((2, C, W), BF16), pltpu.VMEM((2, C, W), BF16),
                        pltpu.VMEM((2, B_HEADS, span.shape[1] // C - 1, C, C), F32),
                        pltpu.VMEM((2, B_HEADS, C, C), BF16)],
        compiler_params=_params("arbitrary", "arbitrary"),
        name="mixer_b",
    )(proj, proj, proj, proj, proj, proj, b_lb_table, span, level)


def _router_epilogue(x_new, g_ref, wr_ref, x_ref, h_ref, aff_ref):
    x_ref[...] = x_new
    h = _rms(x_new, g_ref[...])
    h_ref[...] = h
    w = wr_ref[...]
    h_hi, w_hi = h.astype(BF16), w.astype(BF16)
    h_lo, w_lo = (h - h_hi.astype(F32)).astype(BF16), (w - w_hi.astype(F32)).astype(BF16)
    logits = _dot(h_hi, w_hi) + _dot(h_lo, w_hi) + _dot(h_hi, w_lo)
    e = jnp.exp(logits - jnp.max(logits, axis=-1, keepdims=True))
    aff_ref[...] = e / jnp.sum(e, axis=-1, keepdims=True)


def _outproj_even_kernel(a_ref, of_ref, ob_ref, gate_ref, bng_ref, w_ref, x_ref, g_ref, wr_ref,
                         xo_ref, h_ref, aff_ref):
    o = of_ref[...] + ob_ref[...]
    parts = []
    for h in range(B_HEADS):
        oh = o[:, h * B_DIM:(h + 1) * B_DIM]
        parts.append(oh * lax.rsqrt(jnp.mean(oh * oh, axis=-1, keepdims=True) + EPS))
    on = jnp.concatenate(parts, axis=-1) * bng_ref[...] * jax.nn.sigmoid(gate_ref[...].astype(F32))
    wa = a_ref.shape[1]
    mixed = _dot(a_ref[...], w_ref[0:wa, :]) + _dot(on.astype(BF16), w_ref[wa:, :])
    _router_epilogue(x_ref[...] + mixed, g_ref, wr_ref, xo_ref, h_ref, aff_ref)


def _router_out(T, D, tm):
    specs = [pl.BlockSpec((tm, D), lambda i: (i, 0)), pl.BlockSpec((tm, D), lambda i: (i, 0)),
             pl.BlockSpec((tm, N_EXPERTS), lambda i: (i, 0))]
    shapes = [jax.ShapeDtypeStruct((T, D), F32), jax.ShapeDtypeStruct((T, D), F32),
              jax.ShapeDtypeStruct((T, N_EXPERTS), F32)]
    return specs, shapes


def outproj_even(a_out, o_f, o_b, proj, b_norm_g, w_out, x, ffn_g, w_router, tm=512):
    T, D = x.shape
    W = a_out.shape[1]
    row = lambda w, col=0: pl.BlockSpec((tm, w), lambda i: (i, col))
    full = lambda s: pl.BlockSpec(s, lambda i: (0, 0))
    out_specs, out_shapes = _router_out(T, D, tm)
    return pl.pallas_call(
        _outproj_even_kernel,
        grid=(T // tm,),
        in_specs=[row(W), row(W), row(W), row(W, 6), full((1, W)), full(w_out.shape), row(D),
                  full((1, D)), full(w_router.shape)],
        out_specs=out_specs,
        out_shape=out_shapes,
        compiler_params=_params("parallel"),
        name="outproj_even",
    )(a_out, o_f, o_b, proj, b_norm_g.reshape(1, W), w_out.astype(BF16), x, ffn_g.reshape(1, D), w_router)


def _outproj_odd_kernel(*refs):
    P = len(DILATIONS)
    o_refs, l_refs = refs[:P], refs[P:2 * P]
    ex_ref, w_ref, x_ref, g_ref, wr_ref, xo_ref, h_ref, aff_ref, o_buf, l_buf = refs[2 * P:]
    tm = x_ref.shape[0]

    def token_order(ref, buf, d):
        if d == 1:
            return ref[0].astype(F32)
        tiles = buf.shape[0]
        for r in range(d):
            for t in range(tiles):
                lane0 = (r * tiles + t) * LANES
                buf[t, pl.ds(r, tm // d, stride=d), :] = ref[0, :, lane0:lane0 + LANES].astype(F32)
        return jnp.concatenate([buf[t] for t in range(tiles)], axis=-1)

    ls = [token_order(l_ref, l_buf, d)[:, 0:C_HEADS] for l_ref, d in zip(l_refs, DILATIONS)]
    m = functools.reduce(jnp.maximum, ls)
    es = [jnp.exp(l - m) for l in ls]
    den = functools.reduce(jnp.add, es)
    attn = None
    for e, o_ref, d in zip(es, o_refs, DILATIONS):
        wfull = _dot((e / den).astype(BF16), ex_ref[...])
        term = wfull * token_order(o_ref, o_buf, d)
        attn = term if attn is None else attn + term
    mixed = _dot(attn.astype(BF16), w_ref[...])
    _router_epilogue(x_ref[...] + mixed, g_ref, wr_ref, xo_ref, h_ref, aff_ref)


def outproj_odd(outs, lses, w_o, x, ffn_g, w_router, tm=512):
    T, D = x.shape
    tiles = outs[0].shape[1] // tm
    expand = jnp.asarray(np.kron(np.eye(C_HEADS), np.ones((1, C_HEAD_DIM))), BF16)
    dil = lambda w: [pl.BlockSpec((1, tm // d, d * w), lambda i: (i // tiles, i % tiles, 0)) for d in DILATIONS]
    full = lambda s: pl.BlockSpec(s, lambda i: (0, 0))
    out_specs, out_shapes = _router_out(T, D, tm)
    return pl.pallas_call(
        _outproj_odd_kernel,
        grid=(T // tm,),
        in_specs=dil(D) + dil(LANES) + [full(expand.shape), full(w_o.shape),
                                        pl.BlockSpec((tm, D), lambda i: (i, 0)), full((1, D)),
                                        full(w_router.shape)],
        out_specs=out_specs,
        out_shape=out_shapes,
        scratch_shapes=[pltpu.VMEM((D // LANES, tm, LANES), F32), pltpu.VMEM((1, tm, LANES), F32)],
        compiler_params=_params("parallel"),
        name="outproj_odd",
    )(*outs, *lses, expand, w_o.astype(BF16), x, ffn_g.reshape(1, D), w_router)


def _topk_kernel(aff_ref, upper_ref, strict_ref, pos_ref, tile_ref, *, cap):
    a = aff_ref[0]
    E, NC, _ = a.shape

    def count(mask):
        return jnp.sum(jnp.sum(mask.astype(F32), axis=2, keepdims=True), axis=1, keepdims=True)

    def as_float(bits):
        return lax.bitcast_convert_type(bits, jnp.float32)

    def search(i, thr):
        cand = thr | jnp.left_shift(jnp.int32(1), 30 - i)
        return jnp.where(count(a >= as_float(cand)) >= cap, cand, thr)

    thr = lax.fori_loop(0, 31, search, jnp.zeros((E, 1, 1), I32))
    gt = a >= as_float(thr + 1)
    eq = jnp.logical_and(a >= as_float(thr), jnp.logical_not(gt))
    need = cap - count(gt)

    def prefix(mask):
        m2 = mask.astype(BF16).reshape(E * NC, LANES)
        within = _dot(m2, upper_ref[...])
        total = within[:, LANES - 1:LANES].astype(BF16)
        tot_b = jnp.broadcast_to(total, (E * NC, LANES))
        starts = jnp.concatenate(
            [_dot(strict_ref[...], tot_b[e * NC:(e + 1) * NC]) for e in range(E)], axis=0)
        return (within + starts).reshape(E, NC, LANES), starts.reshape(E, NC, LANES)

    eq_incl, _ = prefix(eq)
    sel = gt | (eq & (eq_incl - 1.0 < need))
    sel_incl, starts = prefix(sel)
    pos_ref[0] = jnp.where(sel, sel_incl - 1.0, -1.0).astype(I32)
    tile = jnp.minimum(jnp.floor(starts * (1.0 / TOKEN_CHUNK)), float((cap - SLOT_WINDOW) // TOKEN_CHUNK))
    tile_ref[0] = tile.astype(I32)


def route_topk(aff_t, cap):
    B, E, S = aff_t.shape
    NC = S // TOKEN_CHUNK
    t = np.arange(LANES)
    upper = jnp.asarray(t[:, None] <= t[None, :], BF16)
    c = np.arange(NC)
    strict = jnp.asarray(c[None, :] < c[:, None], BF16)
    blk = pl.BlockSpec((1, E, NC, LANES), lambda b: (b, 0, 0, 0))
    pos, tiles = pl.pallas_call(
        functools.partial(_topk_kernel, cap=cap),
        grid=(B,),
        in_specs=[blk, pl.BlockSpec(upper.shape, lambda b: (0, 0)), pl.BlockSpec(strict.shape, lambda b: (0, 0))],
        out_specs=[blk, blk],
        out_shape=[jax.ShapeDtypeStruct((B, E, NC, LANES), I32)] * 2,
        compiler_params=_params("parallel"),
        name="route_topk",
    )(aff_t.reshape(B, E, NC, LANES), upper, strict)
    return pos, tiles[..., 0].reshape(-1)


def _index_kernel(tiles_ref, pos_ref, sel_ref, idx_ref, acc_ref):
    b, e = pl.program_id(0), pl.program_id(1)
    n_exp = pl.num_programs(1)
    NC = pos_ref.shape[2]
    acc_ref[...] = jnp.zeros_like(acc_ref)
    slot = lax.broadcasted_iota(I32, (SLOT_WINDOW, TOKEN_CHUNK), 0)

    def body(c, carry):
        j = tiles_ref[(b * n_exp + e) * NC + c]
        onehot = jnp.where(slot + j * TOKEN_CHUNK == pos_ref[0, 0, pl.ds(c, 1), :], 1.0, 0.0).astype(BF16)
        hit = _dot_nt(sel_ref[...], onehot)
        tok = hit[0:1] + hit[1:2] * lax.convert_element_type(c * TOKEN_CHUNK, F32)
        acc_ref[j, 0:1, :] += tok[:, 0:TOKEN_CHUNK]
        acc_ref[j + 1, 0:1, :] += tok[:, TOKEN_CHUNK:]
        return carry

    lax.fori_loop(0, NC, body, 0, unroll=8)
    idx_ref[0, 0] = acc_ref[...].astype(I32)


def moe_slot_tokens(tiles, pos, cap):
    B, E, NC, _ = pos.shape
    nt = cap // TOKEN_CHUNK
    sel = np.zeros((8, TOKEN_CHUNK), np.float32)
    sel[0] = np.arange(TOKEN_CHUNK)
    sel[1] = 1.0
    idx = pl.pallas_call(
        _index_kernel,
        grid_spec=pltpu.PrefetchScalarGridSpec(
            num_scalar_prefetch=1,
            grid=(B, E),
            in_specs=[pl.BlockSpec((1, 1, NC, LANES), lambda b, e, s: (b, e, 0, 0)),
                      pl.BlockSpec((8, TOKEN_CHUNK), lambda b, e, s: (0, 0))],
            out_specs=pl.BlockSpec((1, 1, nt, 8, LANES), lambda b, e, s: (b, e, 0, 0, 0)),
            scratch_shapes=[pltpu.VMEM((nt, 8, LANES), F32)]),
        out_shape=jax.ShapeDtypeStruct((B, E, nt, 8, LANES), I32),
        compiler_params=_params("arbitrary", "arbitrary"),
        name="moe_slot_tokens",
    )(tiles, pos, jnp.asarray(sel, BF16))
    return idx[:, :, :, 0, :].reshape(-1)


def _ffn_kernel(idx_ref, h_ref, wg_ref, wu_ref, wd_ref, y_ref, x_buf, sem, wgb_ref, wub_ref, wdb_ref, *, tr):
    e, b = pl.program_id(0), pl.program_id(1)
    n_exp, nb = pl.num_programs(0), pl.num_programs(1)
    cap = x_buf.shape[1]
    step = e * nb + b

    def row_copy(st, buf, i):
        eb = st // nb
        bb = st - eb * nb
        tok = idx_ref[(bb * n_exp + eb) * cap + i]
        return pltpu.make_async_copy(h_ref.at[bb, pl.ds(tok, 1), :], x_buf.at[buf, pl.ds(i, 1), :], sem.at[buf])

    cur = jnp.bitwise_and(step, 1)

    @pl.when(step == 0)
    def _():
        lax.fori_loop(0, cap, lambda i, c: (row_copy(step, 0, i).start(), c)[1], 0, unroll=8)

    @pl.when(b == 0)
    def _():
        wgb_ref[...] = wg_ref[0, 0].astype(BF16)
        wub_ref[...] = wu_ref[0, 0].astype(BF16)
        wdb_ref[...] = wd_ref[0, 0].astype(BF16)

    pltpu.make_async_copy(h_ref.at[0, pl.ds(0, cap), :], x_buf.at[cur], sem.at[cur]).wait()

    def compute(prefetch):
        for r in range(cap // tr):
            rows = pl.ds(r * tr, tr)
            x = x_buf[cur, rows, :].astype(BF16)
            if prefetch:
                for i in range(r * tr, (r + 1) * tr):
                    row_copy(step + 1, 1 - cur, i).start()
            g = _dot(x, wgb_ref[...])
            u = _dot(x, wub_ref[...])
            mid = (g * jax.nn.sigmoid(g) * u).astype(BF16)
            y_ref[0, 0, rows, :] = _dot(mid, wdb_ref[...]).astype(BF16)

    is_last = step + 1 == n_exp * nb
    pl.when(jnp.logical_not(is_last))(lambda: compute(True))
    pl.when(is_last)(lambda: compute(False))


def moe_ffn(idx, h, w_gate, w_up, w_down, layer, cap, tr=256):
    B, S, D = h.shape
    E, F = w_gate.shape[1], w_gate.shape[-1]
    wspec = lambda r, c: pl.BlockSpec((1, 1, r, c), lambda e, b, s: (layer, e, 0, 0))
    return pl.pallas_call(
        functools.partial(_ffn_kernel, tr=min(tr, cap)),
        grid_spec=pltpu.PrefetchScalarGridSpec(
            num_scalar_prefetch=1,
            grid=(E, B),
            in_specs=[pl.BlockSpec(memory_space=pl.ANY), wspec(D, F), wspec(D, F), wspec(F, D)],
            out_specs=pl.BlockSpec((1, 1, cap, D), lambda e, b, s: (b, e, 0, 0)),
            scratch_shapes=[pltpu.VMEM((2, cap, D), F32), pltpu.SemaphoreType.DMA((2,)),
                            pltpu.VMEM((D, F), BF16), pltpu.VMEM((D, F), BF16), pltpu.VMEM((F, D), BF16)]),
        out_shape=jax.ShapeDtypeStruct((B, E, cap, D), BF16),
        compiler_params=_params("arbitrary", "arbitrary"),
        name="moe_ffn",
    )(idx, h, w_gate, w_up, w_down)


def _combine_kernel(tiles_ref, post_ref, gate_ref, x_ref, y_ref, o_ref):
    b, c = pl.program_id(0), pl.program_id(2)
    NC = pl.num_programs(2)
    n_exp = y_ref.shape[1]
    slot = lax.broadcasted_iota(I32, (TOKEN_CHUNK, SLOT_WINDOW), 1)
    acc = x_ref[0]
    for e in range(n_exp):
        base = pl.multiple_of(tiles_ref[(b * n_exp + e) * NC + c] * TOKEN_CHUNK, TOKEN_CHUNK)
        onehot = jnp.where(slot + base == post_ref[0, :, e:e + 1], 1.0, 0.0).astype(BF16)
        acc = acc + _dot(onehot, y_ref[0, e, pl.ds(base, SLOT_WINDOW), :]) * gate_ref[0, :, e:e + 1]
    o_ref[0] = acc


def moe_combine(tiles, pos_t, gate, x, y, dsplit=2):
    B, S, D = x.shape
    E, cap = y.shape[1], y.shape[2]
    dh = D // dsplit
    tok = lambda w: pl.BlockSpec((1, TOKEN_CHUNK, w), lambda b, j, c, s: (b, c, 0))
    res = pl.BlockSpec((1, TOKEN_CHUNK, dh), lambda b, j, c, s: (b, c, j))
    return pl.pallas_call(
        _combine_kernel,
        grid_spec=pltpu.PrefetchScalarGridSpec(
            num_scalar_prefetch=1,
            grid=(B, dsplit, S // TOKEN_CHUNK),
            in_specs=[tok(E), tok(E), res,
                      pl.BlockSpec((1, E, cap, dh), lambda b, j, c, s: (b, 0, 0, j))],
            out_specs=res),
        out_shape=jax.ShapeDtypeStruct((B, S, D), F32),
        compiler_params=_params("arbitrary", "arbitrary", "arbitrary"),
        name="moe_combine",
    )(tiles, pos_t, gate, x, y)


def expert_choice_moe(x, h, aff, batch, w_gate, w_up, w_down, layer):
    T, D = x.shape
    S = T // batch
    cap = max(1, CAPACITY_FACTOR * S // N_EXPERTS)
    aff = aff.reshape(batch, S, N_EXPERTS)
    pos, tiles = route_topk(jnp.swapaxes(aff, 1, 2), cap)
    idx = moe_slot_tokens(tiles, pos, cap)
    y = moe_ffn(idx, h.reshape(batch, S, D), w_gate, w_up, w_down, layer, cap)
    pos_t = jnp.swapaxes(pos.reshape(batch, N_EXPERTS, S), 1, 2)
    return moe_combine(tiles, pos_t, aff, x.reshape(batch, S, D), y).reshape(T, D)


ATTN_TQ = 128
ATTN_TK = ATTN_TQ + 2 * HALF_WINDOW


def _t5_bucket(rel):
    half_buckets = REL_BUCKETS // 2
    max_exact = half_buckets // 2
    n = jnp.abs(rel)
    scaled = (jnp.log(jnp.maximum(n, 1).astype(jnp.float32) / max_exact)
              / math.log(REL_MAX_DISTANCE / max_exact))
    large = jnp.minimum(max_exact + jnp.floor(scaled * (half_buckets - max_exact)).astype(jnp.int32),
                        half_buckets - 1)
    return jnp.where(rel > 0, half_buckets, 0) + jnp.where(n < max_exact, n, large)


def _bias_kernel(table_ref, bucket_ref, o_ref):
    v, h = pl.program_id(1), pl.program_id(2)
    bucket = bucket_ref[0]
    acc = jnp.zeros(bucket.shape, F32)
    for bk in range(REL_BUCKETS):
        acc = jnp.where(bucket == bk, table_ref[bk * C_HEADS + h], acc)
    q = lax.broadcasted_iota(I32, bucket.shape, 0)
    kc = lax.broadcasted_iota(I32, bucket.shape, 1)
    in_band = jnp.abs(kc - HALF_WINDOW - q) <= HALF_WINDOW
    in_seq = jnp.logical_and(jnp.logical_or(v > 0, kc >= HALF_WINDOW),
                             jnp.logical_or(v < 2, kc < HALF_WINDOW + ATTN_TQ))
    o_ref[0, 0, 0] = jnp.where(jnp.logical_and(in_band, in_seq), acc, NEG_INF)


def attention_bias(rel_bias):
    rel = np.arange(ATTN_TK)[None, :] - HALF_WINDOW - np.arange(ATTN_TQ)[:, None]
    buckets = jnp.stack([_t5_bucket(jnp.asarray(rel * d, I32)) for d in DILATIONS]).astype(I32)
    P = len(DILATIONS)
    return pl.pallas_call(
        _bias_kernel,
        grid_spec=pltpu.PrefetchScalarGridSpec(
            num_scalar_prefetch=1,
            grid=(P, 3, C_HEADS),
            in_specs=[pl.BlockSpec((1, ATTN_TQ, ATTN_TK), lambda p, v, h, t: (p, 0, 0))],
            out_specs=pl.BlockSpec((1, 1, 1, ATTN_TQ, ATTN_TK), lambda p, v, h, t: (p, v, h, 0, 0))),
        out_shape=jax.ShapeDtypeStruct((P, 3, C_HEADS, ATTN_TQ, ATTN_TK), F32),
        compiler_params=_params("arbitrary", "arbitrary", "arbitrary"),
        name="attention_bias",
    )(rel_bias.reshape(-1), buckets)


def _attn_kernel(q_ref, kp_ref, km_ref, kn_ref, vp_ref, vm_ref, vn_ref, bias_ref, o_ref, lse_ref,
                 k_buf, v_buf, s_buf, p_buf):
    i = pl.program_id(2)
    last = pl.num_programs(2) - 1
    hw = HALF_WINDOW
    k_buf[0:hw] = kp_ref[0]
    k_buf[hw:hw + ATTN_TQ] = km_ref[0]
    k_buf[hw + ATTN_TQ:] = kn_ref[0]
    v_buf[0:hw] = vp_ref[0]
    v_buf[hw:hw + ATTN_TQ] = vm_ref[0]
    v_buf[hw + ATTN_TQ:] = vn_ref[0]
    variant = jnp.where(i == 0, 0, jnp.where(i == last, 2, 1))
    first_head = lax.broadcasted_iota(I32, (ATTN_TQ, LANES), 1) < C_HEAD_DIM
    scale = C_HEAD_DIM ** -0.5
    for pair in range(C_HEADS // 2):
        cols = slice(pair * LANES, (pair + 1) * LANES)
        q = q_ref[0, :, cols] * scale
        for half in range(2):
            mine = first_head if half == 0 else jnp.logical_not(first_head)
            s_buf[2 * pair + half] = (_dot_nt(jnp.where(mine, q, jnp.zeros_like(q)), k_buf[:, cols])
                                      + bias_ref[variant, 2 * pair + half])
    lse_ref[...] = jnp.zeros_like(lse_ref)
    for h in range(C_HEADS):
        s = s_buf[h]
        m = jnp.max(s, axis=-1, keepdims=True)
        p = jnp.exp(s - m)
        den = jnp.sum(p, axis=-1, keepdims=True)
        p_buf[h] = (p / den).astype(BF16)
        lse_ref[0, :, h:h + 1] = m + jnp.log(den)
    for pair in range(C_HEADS // 2):
        cols = slice(pair * LANES, (pair + 1) * LANES)
        o_ref[0, :, cols] = jnp.where(first_head, _dot(p_buf[2 * pair], v_buf[:, cols]),
                                      _dot(p_buf[2 * pair + 1], v_buf[:, cols])).astype(BF16)


def dilated_attention(view, bias, d):
    batch, n, D3 = view.shape
    D = D3 // d // 3
    nb = n // HALF_WINDOW
    r = ATTN_TQ // HALF_WINDOW
    assert n // ATTN_TQ >= 2
    main = lambda c: pl.BlockSpec((1, ATTN_TQ, D), lambda b, j, i: (b, i, 3 * j + c))
    prev = lambda c: pl.BlockSpec((1, HALF_WINDOW, D), lambda b, j, i: (b, jnp.maximum(i * r - 1, 0), 3 * j + c))
    nxt = lambda c: pl.BlockSpec((1, HALF_WINDOW, D),
                                 lambda b, j, i: (b, jnp.minimum(i * r + r, nb - 1), 3 * j + c))
    return pl.pallas_call(
        _attn_kernel,
        grid=(batch, d, n // ATTN_TQ),
        in_specs=[main(0), prev(1), main(1), nxt(1), prev(2), main(2), nxt(2),
                  pl.BlockSpec(bias.shape, lambda b, j, i: (0, 0, 0, 0))],
        out_specs=[pl.BlockSpec((1, ATTN_TQ, D), lambda b, j, i: (b, i, j)),
                   pl.BlockSpec((1, ATTN_TQ, LANES), lambda b, j, i: (b, i, j))],
        out_shape=[jax.ShapeDtypeStruct((batch, n, d * D), BF16),
                   jax.ShapeDtypeStruct((batch, n, d * LANES), F32)],
        scratch_shapes=[pltpu.VMEM((ATTN_TK, D), BF16), pltpu.VMEM((ATTN_TK, D), BF16),
                        pltpu.VMEM((C_HEADS, ATTN_TQ, ATTN_TK), F32), pltpu.VMEM((C_HEADS, ATTN_TQ, ATTN_TK), BF16)],
        compiler_params=_params("parallel", "parallel", "parallel"),
        name="dilated_attention",
    )(view, view, view, view, view, view, view, bias)


def dilated_mixture(views, rel_bias):
    bias = attention_bias(rel_bias)
    results = [dilated_attention(view, bias[p], d) for p, (view, d) in enumerate(zip(views, DILATIONS))]
    return [o for o, _ in results], [lse for _, lse in results]


def _final_norm_kernel(x_ref, g_ref, o_ref):
    o_ref[...] = _rms(x_ref[...], g_ref[...])


def final_rmsnorm(x, g, tm=512):
    T, D = x.shape
    return pl.pallas_call(
        _final_norm_kernel,
        grid=(T // tm,),
        in_specs=[pl.BlockSpec((tm, D), lambda i: (i, 0)), pl.BlockSpec((1, D), lambda i: (0, 0))],
        out_specs=pl.BlockSpec((tm, D), lambda i: (i, 0)),
        out_shape=jax.ShapeDtypeStruct((T, D), F32),
        compiler_params=_params("parallel"),
        name="final_norm",
    )(x, g.reshape(1, D))


def kernel(x, mix_norm, ffn_norm, final_norm, w_in_even, w_out_even, a_ln_g, a_ln_b, a_w_s, a_b_s, b_lb_table,
           b_norm_g, w_qkv_odd, w_o_odd, rel_bias, w_router, w_gate, w_up, w_down):
    B, S, D = x.shape
    depth = mix_norm.shape[0]
    xt = x.reshape(B * S, D)
    for layer in range(depth):
        j = layer // 2
        if layer % 2 == 0:
            proj = norm_matmul(xt, mix_norm[layer], w_in_even[j].astype(BF16), B)[0].reshape(B * S, -1)
            a_out = mixer_a(proj, a_ln_g[j], a_ln_b[j], a_w_s[j], a_b_s[j])
            o_f, o_b = mixer_b(proj, b_lb_table, layer, B)
            xt, h, aff = outproj_even(a_out, o_f, o_b, proj, b_norm_g[j], w_out_even[j], xt,
                                      ffn_norm[layer], w_router[layer])
        else:
            views = norm_matmul(xt, mix_norm[layer], w_qkv_odd[j].astype(BF16), B, DILATIONS, tm=512)
            outs, lses = dilated_mixture(views, rel_bias)
            xt, h, aff = outproj_odd(outs, lses, w_o_odd[j], xt, ffn_norm[layer], w_router[layer])
        xt = expert_choice_moe(xt, h, aff, B, w_gate, w_up, w_down, layer)
    return final_rmsnorm(xt, final_norm).reshape(B, S, D)
```

```python
import functools
import math

import numpy as np
import jax
import jax.numpy as jnp
from jax import lax
from jax.experimental import pallas as pl
from jax.experimental.pallas import tpu as pltpu

F32 = jnp.float32
BF16 = jnp.bfloat16
I32 = jnp.int32
EPS = 1e-6
NEG_INF = -1e30
HIGHEST = lax.Precision.HIGHEST

LANES = 128
VMEM_LIMIT = 56 * 1024 * 1024

A_GROUPS = 4
A_CHUNK = 128
B_HEADS = 4
B_DIM = 128
GLA_CHUNK = 128
C_HEADS = 16
C_HEAD_DIM = 64
HALF_WINDOW = 64
DILATIONS = (1, 4, 16)
REL_BUCKETS = 32
REL_MAX_DISTANCE = 1024
N_EXPERTS = 16
CAPACITY_FACTOR = 2
TOKEN_CHUNK = 128
SLOT_WINDOW = 256


def _params(*sem):
    return pltpu.CompilerParams(dimension_semantics=sem, vmem_limit_bytes=VMEM_LIMIT)


def _dot(a, b, **kw):
    return jnp.dot(a, b, preferred_element_type=F32, **kw)


def _dot_nt(a, b):
    return lax.dot_general(a, b, (((1,), (1,)), ((), ())), preferred_element_type=F32)


def _dot_tn(a, b):
    return lax.dot_general(a, b, (((0,), (0,)), ((), ())), preferred_element_type=F32)


def _dot_split01(a2, x):
    hi = x.astype(BF16)
    lo = (x - hi.astype(F32)).astype(BF16)
    return _dot(a2, jnp.concatenate([hi, lo], axis=0))


def _rms(x, g):
    return x * lax.rsqrt(jnp.mean(x * x, axis=-1, keepdims=True) + EPS) * g


def _norm_matmul_kernel(x_ref, g_ref, w_ref, *rest, tn, dilations):
    o_refs, scratch = rest[:len(dilations)], rest[len(dilations):]
    tm, N = x_ref.shape[0], w_ref.shape[1]
    h = _rms(x_ref[...], g_ref[...]).astype(BF16)
    for j in range(N // tn):
        res = _dot(h, w_ref[:, j * tn:(j + 1) * tn])
        if scratch:
            for t in range(tn // LANES):
                scratch[0][t] = res[:, t * LANES:(t + 1) * LANES]
        for o_ref, d in zip(o_refs, dilations):
            if d == 1:
                o_ref[0, :, j * tn:(j + 1) * tn] = res.astype(BF16)
                continue
            for r in range(d):
                for t in range(tn // LANES):
                    lane0 = r * N + j * tn + t * LANES
                    o_ref[0, :, lane0:lane0 + LANES] = (
                        scratch[0][t, pl.ds(r, tm // d, stride=d), :].astype(BF16))


def norm_matmul(x, g, w, batch, dilations=(1,), tm=1024, tn=512):
    T, D = x.shape
    N = w.shape[1]
    S = T // batch
    tiles = S // tm
    strided = any(d > 1 for d in dilations)
    return pl.pallas_call(
        functools.partial(_norm_matmul_kernel, tn=tn, dilations=dilations),
        grid=(T // tm,),
        in_specs=[
            pl.BlockSpec((tm, D), lambda i: (i, 0)),
            pl.BlockSpec((1, D), lambda i: (0, 0)),
            pl.BlockSpec((D, N), lambda i: (0, 0)),
        ],
        out_specs=[pl.BlockSpec((1, tm // d, d * N), lambda i: (i // tiles, i % tiles, 0)) for d in dilations],
        out_shape=[jax.ShapeDtypeStruct((batch, S // d, d * N), BF16) for d in dilations],
        scratch_shapes=[pltpu.VMEM((tn // LANES, tm, LANES), F32)] if strided else [],
        compiler_params=_params("parallel"),
        name="norm_matmul",
    )(x, g.reshape(1, D), w)


def _mixa_kernel(u_ref, v_ref, lg_ref, lb_ref, ws_ref, bs_ref, o_ref):
    tm = u_ref.shape[0]
    u = jax.nn.gelu(u_ref[...].astype(F32))
    v = jax.nn.gelu(v_ref[...].astype(F32))
    mu = jnp.mean(v, axis=-1, keepdims=True)
    vc = v - mu
    vn = vc * lax.rsqrt(jnp.mean(vc * vc, axis=-1, keepdims=True) + EPS)
    vb = (vn * lg_ref[...] + lb_ref[...]).astype(BF16)
    for n in range(tm // A_CHUNK):
        rows = slice(n * A_CHUNK, (n + 1) * A_CHUNK)
        for g in range(A_GROUPS):
            cols = slice(g * LANES, (g + 1) * LANES)
            mixed = _dot(ws_ref[g], vb[rows, cols]) + bs_ref[:, g:g + 1]
            o_ref[rows, cols] = (u[rows, cols] * mixed).astype(BF16)


def mixer_a(proj, ln_g, ln_b, w_s, b_s, tm=512):
    T = proj.shape[0]
    W = A_GROUPS * LANES
    return pl.pallas_call(
        _mixa_kernel,
        grid=(T // tm,),
        in_specs=[
            pl.BlockSpec((tm, W), lambda i: (i, 0)),
            pl.BlockSpec((tm, W), lambda i: (i, 1)),
            pl.BlockSpec((1, W), lambda i: (0, 0)),
            pl.BlockSpec((1, W), lambda i: (0, 0)),
            pl.BlockSpec((A_GROUPS, A_CHUNK, A_CHUNK), lambda i: (0, 0, 0)),
            pl.BlockSpec((A_CHUNK, A_GROUPS), lambda i: (0, 0)),
        ],
        out_specs=pl.BlockSpec((tm, W), lambda i: (i, 0)),
        out_shape=jax.ShapeDtypeStruct((T, W), BF16),
        compiler_params=_params("parallel"),
        name="mixer_a",
    )(proj, proj, ln_g.reshape(1, W), ln_b.reshape(1, W), w_s.astype(BF16), b_s.T)


def _gla_consts(C, reverse):
    t = np.arange(C)[:, None]
    r = np.arange(C)[None, :]
    L = int(round(math.log2(C)))
    spans = [(r >= t) if reverse else (r <= t)]
    level = np.where(np.eye(C, dtype=bool), L, -1).astype(np.int32)
    for l in range(L):
        bs = (t >> (l + 1)) << (l + 1)
        mid = bs + (1 << l)
        if reverse:
            act_q = t < mid
            span = np.where(act_q, (r >= t) & (r < mid), (r >= mid) & (r < t))
        else:
            act_q = t >= mid
            span = np.where(act_q, (r >= mid) & (r <= t), (r > t) & (r < mid))
        if l > 0:
            spans.append(span)
        level[(bs == bs.T) & act_q & ~act_q.T] = l
    return np.tile(np.concatenate(spans, axis=0).astype(np.float32), (1, 2)), level


def _gla_kernel(qf_ref, ff_ref, if_ref, qb_ref, fb_ref, ib_ref, tbl_ref, span_ref, level_ref,
                of_ref, ob_ref, state_ref, w_buf, q_buf, k_buf, v_buf, s_buf, p_buf, *, layer):
    @pl.when(pl.program_id(1) == 0)
    def _():
        state_ref[...] = jnp.zeros_like(state_ref)

    C = qf_ref.shape[0]
    L = span_ref.shape[1] // C
    heads = [(d, h, slice(h * B_DIM, (h + 1) * B_DIM)) for d in range(2) for h in range(B_HEADS)]
    odd_row = jnp.bitwise_and(lax.broadcasted_iota(I32, (C, qf_ref.shape[1]), 0), 1) == 1
    tots = []
    for d, (q_ref, f_ref, i_ref) in enumerate(((qf_ref, ff_ref, if_ref), (qb_ref, fb_ref, ib_ref))):
        tb = tbl_ref[d]
        e = jnp.exp(tb - jnp.max(tb, axis=0, keepdims=True))
        lb = jnp.sum(e[0:layer + 1], axis=0, keepdims=True) / jnp.sum(e, axis=0, keepdims=True)
        f = lb + (1.0 - lb) * jax.nn.sigmoid(f_ref[...].astype(F32))
        e2 = _dot_split01(span_ref[d], jnp.log2(f))
        tot2 = e2[0:1] if d == 1 else e2[C - 1:C]
        tots.append(jnp.exp2(tot2))
        w_buf[d, 0:C] = jnp.exp2(e2[0:C]).astype(BF16)
        w_buf[d, C:2 * C] = jnp.exp2(tot2 - e2[0:C]).astype(BF16)
        w_buf[d, 2 * C:3 * C] = jnp.where(odd_row if d == 0 else jnp.logical_not(odd_row), f, 1.0).astype(BF16)
        w_buf[d, 3 * C:] = jnp.exp2(e2[C:]).astype(BF16)
        q_buf[d] = jax.nn.silu(q_ref[...].astype(F32)).astype(BF16)
        k_buf[d] = (1.0 - f).astype(BF16)
        v_buf[d] = i_ref[...]
    for d, h, sl in heads:
        qh, kh = q_buf[d, :, sl], k_buf[d, :, sl]
        s_buf[d, h, L] = _dot_nt(qh, kh)
        for l in range(L):
            wl = w_buf[d, (l + 2) * C:(l + 3) * C, sl]
            s_buf[d, h, l] = _dot_nt(qh * wl, kh * wl)
    for d, h, sl in heads:
        level = level_ref[d]
        p = jnp.where(level == L, s_buf[d, h, L], 0.0)
        for l in range(L):
            p = jnp.where(level == l, s_buf[d, h, l], p)
        p_buf[d, h] = p.astype(BF16)
    for d, h, sl in heads:
        o_ref = of_ref if d == 0 else ob_ref
        st = state_ref[d, h]
        vh = v_buf[d, :, sl]
        o_ref[:, sl] = _dot(p_buf[d, h], vh) + _dot_nt(q_buf[d, :, sl] * w_buf[d, 0:C, sl], st.astype(BF16))
        state_ref[d, h] = st * tots[d][:, sl] + _dot_tn(vh, k_buf[d, :, sl] * w_buf[d, C:2 * C, sl])


def mixer_b(proj, b_lb_table, layer, batch, C=GLA_CHUNK):
    T = proj.shape[0]
    W = B_HEADS * B_DIM
    n = T // batch // C
    consts = [_gla_consts(C, rev) for rev in (False, True)]
    span = jnp.asarray(np.stack([c[0] for c in consts]), BF16)
    level = jnp.asarray(np.stack([c[1] for c in consts]))
    fwd = lambda col: pl.BlockSpec((C, W), lambda b, c: (b * n + c, col))
    bwd = lambda col: pl.BlockSpec((C, W), lambda b, c: (b * n + n - 1 - c, col))
    full = lambda a: pl.BlockSpec(a.shape, lambda b, c: (0,) * a.ndim)
    return pl.pallas_call(
        functools.partial(_gla_kernel, layer=layer),
        grid=(batch, n),
        in_specs=[fwd(2), fwd(3), fwd(5), bwd(2), bwd(4), bwd(5),
                  full(b_lb_table), full(span), full(level)],
        out_specs=[pl.BlockSpec((C, W), lambda b, c: (b * n + c, 0)),
                   pl.BlockSpec((C, W), lambda b, c: (b * n + n - 1 - c, 0))],
        out_shape=[jax.ShapeDtypeStruct((T, W), F32)] * 2,
        scratch_shapes=[pltpu.VMEM((2, B_HEADS, B_DIM, B_DIM), F32),
                        pltpu.VMEM((2, span.shape[1] + 2 * C, W), BF16),
                        pltpu.VMEM((2, C, W), BF16), pltpu.VMEM((2, C, W), BF16), pltpu.VMEM((2, C, W), BF16),
                        pltpu.VMEM((2, B_HEADS, span.shape[1] // C + 1, C, C), F32),
                        pltpu.VMEM((2, B_HEADS, C, C), BF16)],
        compiler_params=_params("arbitrary", "arbitrary"),
        name="mixer_b",
    )(proj, proj, proj, proj, proj, proj, b_lb_table, span, level)


def _router_epilogue(x_new, g_ref, wr_ref, x_ref, h_ref, aff_ref):
    x_ref[...] = x_new
    h = _rms(x_new, g_ref[...])
    h_ref[...] = h
    w = wr_ref[...]
    h_hi, w_hi = h.astype(BF16), w.astype(BF16)
    h_lo, w_lo = (h - h_hi.astype(F32)).astype(BF16), (w - w_hi.astype(F32)).astype(BF16)
    logits = _dot(h_hi, w_hi) + _dot(h_lo, w_hi) + _dot(h_hi, w_lo)
    e = jnp.exp(logits - jnp.max(logits, axis=-1, keepdims=True))
    aff_ref[...] = e / jnp.sum(e, axis=-1, keepdims=True)


def _outproj_even_kernel(a_ref, of_ref, ob_ref, gate_ref, bng_ref, w_ref, x_ref, g_ref, wr_ref,
                         xo_ref, h_ref, aff_ref):
    o = of_ref[...] + ob_ref[...]
    parts = []
    for h in range(B_HEADS):
        oh = o[:, h * B_DIM:(h + 1) * B_DIM]
        parts.append(oh * lax.rsqrt(jnp.mean(oh * oh, axis=-1, keepdims=True) + EPS))
    on = jnp.concatenate(parts, axis=-1) * bng_ref[...] * jax.nn.sigmoid(gate_ref[...].astype(F32))
    wa = a_ref.shape[1]
    mixed = _dot(a_ref[...], w_ref[0:wa, :]) + _dot(on.astype(BF16), w_ref[wa:, :])
    _router_epilogue(x_ref[...] + mixed, g_ref, wr_ref, xo_ref, h_ref, aff_ref)


def _router_out(T, D, tm):
    specs = [pl.BlockSpec((tm, D), lambda i: (i, 0)), pl.BlockSpec((tm, D), lambda i: (i, 0)),
             pl.BlockSpec((tm, N_EXPERTS), lambda i: (i, 0))]
    shapes = [jax.ShapeDtypeStruct((T, D), F32), jax.ShapeDtypeStruct((T, D), F32),
              jax.ShapeDtypeStruct((T, N_EXPERTS), F32)]
    return specs, shapes


def outproj_even(a_out, o_f, o_b, proj, b_norm_g, w_out, x, ffn_g, w_router, tm=512):
    T, D = x.shape
    W = a_out.shape[1]
    row = lambda w, col=0: pl.BlockSpec((tm, w), lambda i: (i, col))
    full = lambda s: pl.BlockSpec(s, lambda i: (0, 0))
    out_specs, out_shapes = _router_out(T, D, tm)
    return pl.pallas_call(
        _outproj_even_kernel,
        grid=(T // tm,),
        in_specs=[row(W), row(W), row(W), row(W, 6), full((1, W)), full(w_out.shape), row(D),
                  full((1, D)), full(w_router.shape)],
        out_specs=out_specs,
        out_shape=out_shapes,
        compiler_params=_params("parallel"),
        name="outproj_even",
    )(a_out, o_f, o_b, proj, b_norm_g.reshape(1, W), w_out.astype(BF16), x, ffn_g.reshape(1, D), w_router)


def _outproj_odd_kernel(*refs):
    P = len(DILATIONS)
    o_refs, l_refs = refs[:P], refs[P:2 * P]
    ex_ref, w_ref, x_ref, g_ref, wr_ref, xo_ref, h_ref, aff_ref, o_buf, l_buf = refs[2 * P:]
    tm = x_ref.shape[0]

    def token_order(ref, buf, d):
        if d == 1:
            return ref[0].astype(F32)
        tiles = buf.shape[0]
        for r in range(d):
            for t in range(tiles):
                lane0 = (r * tiles + t) * LANES
                buf[t, pl.ds(r, tm // d, stride=d), :] = ref[0, :, lane0:lane0 + LANES].astype(F32)
        return jnp.concatenate([buf[t] for t in range(tiles)], axis=-1)

    ls = [token_order(l_ref, l_buf, d)[:, 0:C_HEADS] for l_ref, d in zip(l_refs, DILATIONS)]
    m = functools.reduce(jnp.maximum, ls)
    es = [jnp.exp(l - m) for l in ls]
    den = functools.reduce(jnp.add, es)
    attn = None
    for e, o_ref, d in zip(es, o_refs, DILATIONS):
        wfull = _dot((e / den).astype(BF16), ex_ref[...])
        term = wfull * token_order(o_ref, o_buf, d)
        attn = term if attn is None else attn + term
    mixed = _dot(attn.astype(BF16), w_ref[...])
    _router_epilogue(x_ref[...] + mixed, g_ref, wr_ref, xo_ref, h_ref, aff_ref)


def outproj_odd(outs, lses, w_o, x, ffn_g, w_router, tm=512):
    T, D = x.shape
    tiles = outs[0].shape[1] // tm
    expand = jnp.asarray(np.kron(np.eye(C_HEADS), np.ones((1, C_HEAD_DIM))), BF16)
    dil = lambda w: [pl.BlockSpec((1, tm // d, d * w), lambda i: (i // tiles, i % tiles, 0)) for d in DILATIONS]
    full = lambda s: pl.BlockSpec(s, lambda i: (0, 0))
    out_specs, out_shapes = _router_out(T, D, tm)
    return pl.pallas_call(
        _outproj_odd_kernel,
        grid=(T // tm,),
        in_specs=dil(D) + dil(LANES) + [full(expand.shape), full(w_o.shape),
                                        pl.BlockSpec((tm, D), lambda i: (i, 0)), full((1, D)),
                                        full(w_router.shape)],
        out_specs=out_specs,
        out_shape=out_shapes,
        scratch_shapes=[pltpu.VMEM((D // LANES, tm, LANES), F32), pltpu.VMEM((1, tm, LANES), F32)],
        compiler_params=_params("parallel"),
        name="outproj_odd",
    )(*outs, *lses, expand, w_o.astype(BF16), x, ffn_g.reshape(1, D), w_router)


def _topk_kernel(aff_ref, upper_ref, strict_ref, pos_ref, tile_ref, *, cap):
    a = aff_ref[0]
    E, NC, _ = a.shape

    def count(mask):
        return jnp.sum(jnp.sum(mask.astype(F32), axis=2, keepdims=True), axis=1, keepdims=True)

    def as_float(bits):
        return lax.bitcast_convert_type(bits, jnp.float32)

    def search(i, thr):
        cand = thr | jnp.left_shift(jnp.int32(1), 30 - i)
        return jnp.where(count(a >= as_float(cand)) >= cap, cand, thr)

    thr = lax.fori_loop(0, 31, search, jnp.zeros((E, 1, 1), I32))
    gt = a >= as_float(thr + 1)
    eq = jnp.logical_and(a >= as_float(thr), jnp.logical_not(gt))
    need = cap - count(gt)

    def prefix(mask):
        m2 = mask.astype(BF16).reshape(E * NC, LANES)
        within = _dot(m2, upper_ref[...])
        total = within[:, LANES - 1:LANES].astype(BF16)
        tot_b = jnp.broadcast_to(total, (E * NC, LANES))
        starts = jnp.concatenate(
            [_dot(strict_ref[...], tot_b[e * NC:(e + 1) * NC]) for e in range(E)], axis=0)
        return (within + starts).reshape(E, NC, LANES), starts.reshape(E, NC, LANES)

    eq_incl, _ = prefix(eq)
    sel = gt | (eq & (eq_incl - 1.0 < need))
    sel_incl, starts = prefix(sel)
    pos_ref[0] = jnp.where(sel, sel_incl - 1.0, -1.0).astype(I32)
    tile = jnp.minimum(jnp.floor(starts * (1.0 / TOKEN_CHUNK)), float((cap - SLOT_WINDOW) // TOKEN_CHUNK))
    tile_ref[0] = tile.astype(I32)


def route_topk(aff_t, cap):
    B, E, S = aff_t.shape
    NC = S // TOKEN_CHUNK
    t = np.arange(LANES)
    upper = jnp.asarray(t[:, None] <= t[None, :], BF16)
    c = np.arange(NC)
    strict = jnp.asarray(c[None, :] < c[:, None], BF16)
    blk = pl.BlockSpec((1, E, NC, LANES), lambda b: (b, 0, 0, 0))
    pos, tiles = pl.pallas_call(
        functools.partial(_topk_kernel, cap=cap),
        grid=(B,),
        in_specs=[blk, pl.BlockSpec(upper.shape, lambda b: (0, 0)), pl.BlockSpec(strict.shape, lambda b: (0, 0))],
        out_specs=[blk, blk],
        out_shape=[jax.ShapeDtypeStruct((B, E, NC, LANES), I32)] * 2,
        compiler_params=_params("parallel"),
        name="route_topk",
    )(aff_t.reshape(B, E, NC, LANES), upper, strict)
    return pos, tiles[..., 0].reshape(-1)


def _index_kernel(tiles_ref, pos_ref, sel_ref, idx_ref, acc_ref):
    b, e = pl.program_id(0), pl.program_id(1)
    n_exp = pl.num_programs(1)
    NC = pos_ref.shape[2]
    acc_ref[...] = jnp.zeros_like(acc_ref)
    slot = lax.broadcasted_iota(I32, (SLOT_WINDOW, TOKEN_CHUNK), 0)

    def body(c, carry):
        j = tiles_ref[(b * n_exp + e) * NC + c]
        onehot = jnp.where(slot + j * TOKEN_CHUNK == pos_ref[0, 0, pl.ds(c, 1), :], 1.0, 0.0).astype(BF16)
        hit = _dot_nt(sel_ref[...], onehot)
        tok = hit[0:1] + hit[1:2] * lax.convert_element_type(c * TOKEN_CHUNK, F32)
        acc_ref[j, 0:1, :] += tok[:, 0:TOKEN_CHUNK]
        acc_ref[j + 1, 0:1, :] += tok[:, TOKEN_CHUNK:]
        return carry

    lax.fori_loop(0, NC, body, 0, unroll=8)
    idx_ref[0, 0] = acc_ref[...].astype(I32)


def moe_slot_tokens(tiles, pos, cap):
    B, E, NC, _ = pos.shape
    nt = cap // TOKEN_CHUNK
    sel = np.zeros((8, TOKEN_CHUNK), np.float32)
    sel[0] = np.arange(TOKEN_CHUNK)
    sel[1] = 1.0
    idx = pl.pallas_call(
        _index_kernel,
        grid_spec=pltpu.PrefetchScalarGridSpec(
            num_scalar_prefetch=1,
            grid=(B, E),
            in_specs=[pl.BlockSpec((1, 1, NC, LANES), lambda b, e, s: (b, e, 0, 0)),
                      pl.BlockSpec((8, TOKEN_CHUNK), lambda b, e, s: (0, 0))],
            out_specs=pl.BlockSpec((1, 1, nt, 8, LANES), lambda b, e, s: (b, e, 0, 0, 0)),
            scratch_shapes=[pltpu.VMEM((nt, 8, LANES), F32)]),
        out_shape=jax.ShapeDtypeStruct((B, E, nt, 8, LANES), I32),
        compiler_params=_params("arbitrary", "arbitrary"),
        name="moe_slot_tokens",
    )(tiles, pos, jnp.asarray(sel, BF16))
    return idx[:, :, :, 0, :].reshape(-1)


def _ffn_kernel(idx_ref, h_ref, wg_ref, wu_ref, wd_ref, y_ref, x_buf, sem, wgb_ref, wub_ref, wdb_ref, *, tr):
    e, b = pl.program_id(0), pl.program_id(1)
    n_exp, nb = pl.num_programs(0), pl.num_programs(1)
    cap = x_buf.shape[1]
    step = e * nb + b

    def row_copy(st, buf, i):
        eb = st // nb
        bb = st - eb * nb
        tok = idx_ref[(bb * n_exp + eb) * cap + i]
        return pltpu.make_async_copy(h_ref.at[bb, pl.ds(tok, 1), :], x_buf.at[buf, pl.ds(i, 1), :], sem.at[buf])

    cur = jnp.bitwise_and(step, 1)

    @pl.when(step == 0)
    def _():
        lax.fori_loop(0, cap, lambda i, c: (row_copy(step, 0, i).start(), c)[1], 0, unroll=8)

    @pl.when(b == 0)
    def _():
        wgb_ref[...] = wg_ref[0, 0].astype(BF16)
        wub_ref[...] = wu_ref[0, 0].astype(BF16)
        wdb_ref[...] = wd_ref[0, 0].astype(BF16)

    pltpu.make_async_copy(h_ref.at[0, pl.ds(0, cap), :], x_buf.at[cur], sem.at[cur]).wait()

    def compute(prefetch):
        for r in range(cap // tr):
            rows = pl.ds(r * tr, tr)
            x = x_buf[cur, rows, :].astype(BF16)
            if prefetch:
                for i in range(r * tr, (r + 1) * tr):
                    row_copy(step + 1, 1 - cur, i).start()
            g = _dot(x, wgb_ref[...])
            u = _dot(x, wub_ref[...])
            mid = (g * jax.nn.sigmoid(g) * u).astype(BF16)
            y_ref[0, 0, rows, :] = _dot(mid, wdb_ref[...]).astype(BF16)

    is_last = step + 1 == n_exp * nb
    pl.when(jnp.logical_not(is_last))(lambda: compute(True))
    pl.when(is_last)(lambda: compute(False))


def moe_ffn(idx, h, w_gate, w_up, w_down, layer, cap, tr=256):
    B, S, D = h.shape
    E, F = w_gate.shape[1], w_gate.shape[-1]
    wspec = lambda r, c: pl.BlockSpec((1, 1, r, c), lambda e, b, s: (layer, e, 0, 0))
    return pl.pallas_call(
        functools.partial(_ffn_kernel, tr=min(tr, cap)),
        grid_spec=pltpu.PrefetchScalarGridSpec(
            num_scalar_prefetch=1,
            grid=(E, B),
            in_specs=[pl.BlockSpec(memory_space=pl.ANY), wspec(D, F), wspec(D, F), wspec(F, D)],
            out_specs=pl.BlockSpec((1, 1, cap, D), lambda e, b, s: (b, e, 0, 0)),
            scratch_shapes=[pltpu.VMEM((2, cap, D), F32), pltpu.SemaphoreType.DMA((2,)),
                            pltpu.VMEM((D, F), BF16), pltpu.VMEM((D, F), BF16), pltpu.VMEM((F, D), BF16)]),
        out_shape=jax.ShapeDtypeStruct((B, E, cap, D), BF16),
        compiler_params=_params("arbitrary", "arbitrary"),
        name="moe_ffn",
    )(idx, h, w_gate, w_up, w_down)


def _combine_kernel(tiles_ref, post_ref, gate_ref, x_ref, y_ref, o_ref):
    b, i = pl.program_id(0), pl.program_id(2)
    n_exp = y_ref.shape[1]
    per_step = x_ref.shape[1] // TOKEN_CHUNK
    NC = pl.num_programs(2) * per_step
    slot = lax.broadcasted_iota(I32, (TOKEN_CHUNK, SLOT_WINDOW), 1)
    for k in range(per_step):
        rows = slice(k * TOKEN_CHUNK, (k + 1) * TOKEN_CHUNK)
        acc = x_ref[0, rows, :]
        for e in range(n_exp):
            tile = tiles_ref[(b * n_exp + e) * NC + i * per_step + k]
            base = pl.multiple_of(tile * TOKEN_CHUNK, TOKEN_CHUNK)
            onehot = jnp.where(slot + base == post_ref[0, rows, e:e + 1], 1.0, 0.0).astype(BF16)
            acc = acc + _dot(onehot, y_ref[0, e, pl.ds(base, SLOT_WINDOW), :]) * gate_ref[0, rows, e:e + 1]
        o_ref[0, rows, :] = acc


def moe_combine(tiles, pos_t, gate, x, y, dsplit=2, tm=512):
    B, S, D = x.shape
    E, cap = y.shape[1], y.shape[2]
    dh = D // dsplit
    tok = lambda w: pl.BlockSpec((1, tm, w), lambda b, j, c, s: (b, c, 0))
    res = pl.BlockSpec((1, tm, dh), lambda b, j, c, s: (b, c, j))
    return pl.pallas_call(
        _combine_kernel,
        grid_spec=pltpu.PrefetchScalarGridSpec(
            num_scalar_prefetch=1,
            grid=(B, dsplit, S // tm),
            in_specs=[tok(E), tok(E), res,
                      pl.BlockSpec((1, E, cap, dh), lambda b, j, c, s: (b, 0, 0, j))],
            out_specs=res),
        out_shape=jax.ShapeDtypeStruct((B, S, D), F32),
        compiler_params=_params("arbitrary", "arbitrary", "arbitrary"),
        name="moe_combine",
    )(tiles, pos_t, gate, x, y)


def expert_choice_moe(x, h, aff, batch, w_gate, w_up, w_down, layer):
    T, D = x.shape
    S = T // batch
    cap = max(1, CAPACITY_FACTOR * S // N_EXPERTS)
    aff = aff.reshape(batch, S, N_EXPERTS)
    pos, tiles = route_topk(jnp.swapaxes(aff, 1, 2), cap)
    idx = moe_slot_tokens(tiles, pos, cap)
    y = moe_ffn(idx, h.reshape(batch, S, D), w_gate, w_up, w_down, layer, cap)
    pos_t = jnp.swapaxes(pos.reshape(batch, N_EXPERTS, S), 1, 2)
    return moe_combine(tiles, pos_t, aff, x.reshape(batch, S, D), y).reshape(T, D)


ATTN_TQ = 128
ATTN_TK = ATTN_TQ + 2 * HALF_WINDOW


def _t5_bucket(rel):
    half_buckets = REL_BUCKETS // 2
    max_exact = half_buckets // 2
    n = jnp.abs(rel)
    scaled = (jnp.log(jnp.maximum(n, 1).astype(jnp.float32) / max_exact)
              / math.log(REL_MAX_DISTANCE / max_exact))
    large = jnp.minimum(max_exact + jnp.floor(scaled * (half_buckets - max_exact)).astype(jnp.int32),
                        half_buckets - 1)
    return jnp.where(rel > 0, half_buckets, 0) + jnp.where(n < max_exact, n, large)


def _bias_kernel(table_ref, bucket_ref, o_ref):
    h = pl.program_id(1)
    bucket = bucket_ref[0]
    acc = jnp.zeros(bucket.shape, F32)
    for bk in range(REL_BUCKETS):
        acc = jnp.where(bucket == bk, table_ref[bk * C_HEADS + h], acc)
    q = lax.broadcasted_iota(I32, bucket.shape, 0)
    kc = lax.broadcasted_iota(I32, bucket.shape, 1)
    middle = jnp.where(jnp.abs(kc - HALF_WINDOW - q) <= HALF_WINDOW, acc, NEG_INF)
    o_ref[0, 0, 0] = jnp.where(kc >= HALF_WINDOW, middle, NEG_INF)
    o_ref[0, 1, 0] = middle
    o_ref[0, 2, 0] = jnp.where(kc < HALF_WINDOW + ATTN_TQ, middle, NEG_INF)


def attention_bias(rel_bias):
    rel = np.arange(ATTN_TK)[None, :] - HALF_WINDOW - np.arange(ATTN_TQ)[:, None]
    buckets = jnp.stack([_t5_bucket(jnp.asarray(rel * d, I32)) for d in DILATIONS]).astype(I32)
    P = len(DILATIONS)
    return pl.pallas_call(
        _bias_kernel,
        grid_spec=pltpu.PrefetchScalarGridSpec(
            num_scalar_prefetch=1,
            grid=(P, C_HEADS),
            in_specs=[pl.BlockSpec((1, ATTN_TQ, ATTN_TK), lambda p, h, t: (p, 0, 0))],
            out_specs=pl.BlockSpec((1, 3, 1, ATTN_TQ, ATTN_TK), lambda p, h, t: (p, 0, h, 0, 0))),
        out_shape=jax.ShapeDtypeStruct((P, 3, C_HEADS, ATTN_TQ, ATTN_TK), F32),
        compiler_params=_params("arbitrary", "arbitrary"),
        name="attention_bias",
    )(rel_bias.reshape(-1), buckets)


def _attn_kernel(q_ref, kp_ref, km_ref, kn_ref, vp_ref, vm_ref, vn_ref, bias_ref, o_ref, lse_ref,
                 k_buf, v_buf, s_buf, p_buf):
    i = pl.program_id(2)
    last = pl.num_programs(2) - 1
    hw = HALF_WINDOW
    k_buf[0:hw] = kp_ref[0]
    k_buf[hw:hw + ATTN_TQ] = km_ref[0]
    k_buf[hw + ATTN_TQ:] = kn_ref[0]
    v_buf[0:hw] = vp_ref[0]
    v_buf[hw:hw + ATTN_TQ] = vm_ref[0]
    v_buf[hw + ATTN_TQ:] = vn_ref[0]
    variant = jnp.where(i == 0, 0, jnp.where(i == last, 2, 1))
    first_head = lax.broadcasted_iota(I32, (ATTN_TQ, LANES), 1) < C_HEAD_DIM
    scale = C_HEAD_DIM ** -0.5
    for pair in range(C_HEADS // 2):
        cols = slice(pair * LANES, (pair + 1) * LANES)
        q = q_ref[0, :, cols] * scale
        for half in range(2):
            mine = first_head if half == 0 else jnp.logical_not(first_head)
            s_buf[2 * pair + half] = (_dot_nt(jnp.where(mine, q, jnp.zeros_like(q)), k_buf[:, cols])
                                      + bias_ref[variant, 2 * pair + half])
    lse_ref[...] = jnp.zeros_like(lse_ref)
    for h in range(C_HEADS):
        s = s_buf[h]
        m = jnp.max(s, axis=-1, keepdims=True)
        p = jnp.exp(s - m)
        den = jnp.sum(p, axis=-1, keepdims=True)
        p_buf[h] = (p / den).astype(BF16)
        lse_ref[0, :, h:h + 1] = m + jnp.log(den)
    for pair in range(C_HEADS // 2):
        cols = slice(pair * LANES, (pair + 1) * LANES)
        o_ref[0, :, cols] = jnp.where(first_head, _dot(p_buf[2 * pair], v_buf[:, cols]),
                                      _dot(p_buf[2 * pair + 1], v_buf[:, cols])).astype(BF16)


def dilated_attention(view, bias, d):
    batch, n, D3 = view.shape
    D = D3 // d // 3
    nb = n // HALF_WINDOW
    r = ATTN_TQ // HALF_WINDOW
    assert n // ATTN_TQ >= 2
    main = lambda c: pl.BlockSpec((1, ATTN_TQ, D), lambda b, j, i: (b, i, 3 * j + c))
    prev = lambda c: pl.BlockSpec((1, HALF_WINDOW, D), lambda b, j, i: (b, jnp.maximum(i * r - 1, 0), 3 * j + c))
    nxt = lambda c: pl.BlockSpec((1, HALF_WINDOW, D),
                                 lambda b, j, i: (b, jnp.minimum(i * r + r, nb - 1), 3 * j + c))
    return pl.pallas_call(
        _attn_kernel,
        grid=(batch, d, n // ATTN_TQ),
        in_specs=[main(0), prev(1), main(1), nxt(1), prev(2), main(2), nxt(2),
                  pl.BlockSpec(bias.shape, lambda b, j, i: (0, 0, 0, 0))],
        out_specs=[pl.BlockSpec((1, ATTN_TQ, D), lambda b, j, i: (b, i, j)),
                   pl.BlockSpec((1, ATTN_TQ, LANES), lambda b, j, i: (b, i, j))],
        out_shape=[jax.ShapeDtypeStruct((batch, n, d * D), BF16),
                   jax.ShapeDtypeStruct((batch, n, d * LANES), F32)],
        scratch_shapes=[pltpu.VMEM((ATTN_TK, D), BF16), pltpu.VMEM((ATTN_TK, D), BF16),
                        pltpu.VMEM((C_HEADS, ATTN_TQ, ATTN_TK), F32), pltpu.VMEM((C_HEADS, ATTN_TQ, ATTN_TK), BF16)],
        compiler_params=_params("parallel", "parallel", "parallel"),
        name="dilated_attention",
    )(view, view, view, view, view, view, view, bias)


def dilated_mixture(views, rel_bias):
    bias = attention_bias(rel_bias)
    results = [dilated_attention(view, bias[p], d) for p, (view, d) in enumerate(zip(views, DILATIONS))]
    return [o for o, _ in results], [lse for _, lse in results]


def _final_norm_kernel(x_ref, g_ref, o_ref):
    o_ref[...] = _rms(x_ref[...], g_ref[...])


def final_rmsnorm(x, g, tm=512):
    T, D = x.shape
    return pl.pallas_call(
        _final_norm_kernel,
        grid=(T // tm,),
        in_specs=[pl.BlockSpec((tm, D), lambda i: (i, 0)), pl.BlockSpec((1, D), lambda i: (0, 0))],
        out_specs=pl.BlockSpec((tm, D), lambda i: (i, 0)),
        out_shape=jax.ShapeDtypeStruct((T, D), F32),
        compiler_params=_params("parallel"),
        name="final_norm",
    )(x, g.reshape(1, D))


def kernel(x, mix_norm, ffn_norm, final_norm, w_in_even, w_out_even, a_ln_g, a_ln_b, a_w_s, a_b_s, b_lb_table,
           b_norm_g, w_qkv_odd, w_o_odd, rel_bias, w_router, w_gate, w_up, w_down):
    B, S, D = x.shape
    depth = mix_norm.shape[0]
    xt = x.reshape(B * S, D)
    for layer in range(depth):
        j = layer // 2
        if layer % 2 == 0:
            proj = norm_matmul(xt, mix_norm[layer], w_in_even[j].astype(BF16), B)[0].reshape(B * S, -1)
            a_out = mixer_a(proj, a_ln_g[j], a_ln_b[j], a_w_s[j], a_b_s[j])
            o_f, o_b = mixer_b(proj, b_lb_table, layer, B)
            xt, h, aff = outproj_even(a_out, o_f, o_b, proj, b_norm_g[j], w_out_even[j], xt,
                                      ffn_norm[layer], w_router[layer])
        else:
            views = norm_matmul(xt, mix_norm[layer], w_qkv_odd[j].astype(BF16), B, DILATIONS, tm=512)
            outs, lses = dilated_mixture(views, rel_bias)
            xt, h, aff = outproj_odd(outs, lses, w_o_odd[j], xt, ffn_norm[layer], w_router[layer])
        xt = expert_choice_moe(xt, h, aff, B, w_gate, w_up, w_down, layer)
    return final_rmsnorm(xt, final_norm).reshape(B, S, D)
```

```python
import functools
import math

import numpy as np
import jax
import jax.numpy as jnp
from jax import lax
from jax.experimental import pallas as pl
from jax.experimental.pallas import tpu as pltpu

F32 = jnp.float32
BF16 = jnp.bfloat16
I32 = jnp.int32
EPS = 1e-6
NEG_INF = -1e30
HIGHEST = lax.Precision.HIGHEST

LANES = 128
VMEM_LIMIT = 56 * 1024 * 1024

A_GROUPS = 4
A_CHUNK = 128
B_HEADS = 4
B_DIM = 128
GLA_CHUNK = 128
C_HEADS = 16
C_HEAD_DIM = 64
HALF_WINDOW = 64
DILATIONS = (1, 4, 16)
REL_BUCKETS = 32
REL_MAX_DISTANCE = 1024
N_EXPERTS = 16
CAPACITY_FACTOR = 2
TOKEN_CHUNK = 128
SLOT_WINDOW = 256


def _params(*sem):
    return pltpu.CompilerParams(dimension_semantics=sem, vmem_limit_bytes=VMEM_LIMIT)


def _dot(a, b, **kw):
    return jnp.dot(a, b, preferred_element_type=F32, **kw)


def _dot_nt(a, b):
    return lax.dot_general(a, b, (((1,), (1,)), ((), ())), preferred_element_type=F32)


def _dot_tn(a, b):
    return lax.dot_general(a, b, (((0,), (0,)), ((), ())), preferred_element_type=F32)


def _dot_split01(a2, x):
    hi = x.astype(BF16)
    lo = (x - hi.astype(F32)).astype(BF16)
    return _dot(a2, jnp.concatenate([hi, lo], axis=0))


def _rms(x, g):
    return x * lax.rsqrt(jnp.mean(x * x, axis=-1, keepdims=True) + EPS) * g


def _norm_matmul_kernel(x_ref, g_ref, w_ref, *rest, tn, dilations):
    o_refs, scratch = rest[:len(dilations)], rest[len(dilations):]
    tm, N = x_ref.shape[0], w_ref.shape[1]
    h = _rms(x_ref[...], g_ref[...]).astype(BF16)
    for j in range(N // tn):
        res = _dot(h, w_ref[:, j * tn:(j + 1) * tn])
        if scratch:
            for t in range(tn // LANES):
                scratch[0][t] = res[:, t * LANES:(t + 1) * LANES]
        for o_ref, d in zip(o_refs, dilations):
            if d == 1:
                o_ref[0, :, j * tn:(j + 1) * tn] = res.astype(BF16)
                continue
            for r in range(d):
                for t in range(tn // LANES):
                    lane0 = r * N + j * tn + t * LANES
                    o_ref[0, :, lane0:lane0 + LANES] = (
                        scratch[0][t, pl.ds(r, tm // d, stride=d), :].astype(BF16))


def norm_matmul(x, g, w, batch, dilations=(1,), tm=1024, tn=512):
    T, D = x.shape
    N = w.shape[1]
    S = T // batch
    tiles = S // tm
    strided = any(d > 1 for d in dilations)
    return pl.pallas_call(
        functools.partial(_norm_matmul_kernel, tn=tn, dilations=dilations),
        grid=(T // tm,),
        in_specs=[
            pl.BlockSpec((tm, D), lambda i: (i, 0)),
            pl.BlockSpec((1, D), lambda i: (0, 0)),
            pl.BlockSpec((D, N), lambda i: (0, 0)),
        ],
        out_specs=[pl.BlockSpec((1, tm // d, d * N), lambda i: (i // tiles, i % tiles, 0)) for d in dilations],
        out_shape=[jax.ShapeDtypeStruct((batch, S // d, d * N), BF16) for d in dilations],
        scratch_shapes=[pltpu.VMEM((tn // LANES, tm, LANES), F32)] if strided else [],
        compiler_params=_params("parallel"),
        name="norm_matmul",
    )(x, g.reshape(1, D), w)


def _mixa_kernel(u_ref, v_ref, lg_ref, lb_ref, ws_ref, bs_ref, o_ref):
    tm = u_ref.shape[0]
    u = jax.nn.gelu(u_ref[...].astype(F32))
    v = jax.nn.gelu(v_ref[...].astype(F32))
    mu = jnp.mean(v, axis=-1, keepdims=True)
    vc = v - mu
    vn = vc * lax.rsqrt(jnp.mean(vc * vc, axis=-1, keepdims=True) + EPS)
    vb = (vn * lg_ref[...] + lb_ref[...]).astype(BF16)
    for n in range(tm // A_CHUNK):
        rows = slice(n * A_CHUNK, (n + 1) * A_CHUNK)
        for g in range(A_GROUPS):
            cols = slice(g * LANES, (g + 1) * LANES)
            mixed = _dot(ws_ref[g], vb[rows, cols]) + bs_ref[:, g:g + 1]
            o_ref[rows, cols] = (u[rows, cols] * mixed).astype(BF16)


def mixer_a(proj, ln_g, ln_b, w_s, b_s, tm=512):
    T = proj.shape[0]
    W = A_GROUPS * LANES
    return pl.pallas_call(
        _mixa_kernel,
        grid=(T // tm,),
        in_specs=[
            pl.BlockSpec((tm, W), lambda i: (i, 0)),
            pl.BlockSpec((tm, W), lambda i: (i, 1)),
            pl.BlockSpec((1, W), lambda i: (0, 0)),
            pl.BlockSpec((1, W), lambda i: (0, 0)),
            pl.BlockSpec((A_GROUPS, A_CHUNK, A_CHUNK), lambda i: (0, 0, 0)),
            pl.BlockSpec((A_CHUNK, A_GROUPS), lambda i: (0, 0)),
        ],
        out_specs=pl.BlockSpec((tm, W), lambda i: (i, 0)),
        out_shape=jax.ShapeDtypeStruct((T, W), BF16),
        compiler_params=_params("parallel"),
        name="mixer_a",
    )(proj, proj, ln_g.reshape(1, W), ln_b.reshape(1, W), w_s.astype(BF16), b_s.T)


def _gla_consts(C, reverse):
    t = np.arange(C)[:, None]
    r = np.arange(C)[None, :]
    L = int(round(math.log2(C)))
    spans = [(r >= t) if reverse else (r <= t)]
    level = np.where(np.eye(C, dtype=bool), L, -1).astype(np.int32)
    for l in range(L):
        bs = (t >> (l + 1)) << (l + 1)
        mid = bs + (1 << l)
        if reverse:
            act_q = t < mid
            span = np.where(act_q, (r >= t) & (r < mid), (r >= mid) & (r < t))
        else:
            act_q = t >= mid
            span = np.where(act_q, (r >= mid) & (r <= t), (r > t) & (r < mid))
        if l > 0:
            spans.append(span)
        level[(bs == bs.T) & act_q & ~act_q.T] = l
    return np.tile(np.concatenate(spans, axis=0).astype(np.float32), (1, 2)), level


def _gla_kernel(qf_ref, ff_ref, if_ref, qb_ref, fb_ref, ib_ref, tbl_ref, span_ref, level_ref,
                of_ref, ob_ref, state_ref, w_buf, q_buf, k_buf, v_buf, s_buf, p_buf, *, layer, C):
    @pl.when(pl.program_id(1) == 0)
    def _():
        state_ref[...] = jnp.zeros_like(state_ref)

    L = span_ref.shape[1] // C
    subs = qf_ref.shape[0] // C
    odd_row = jnp.bitwise_and(lax.broadcasted_iota(I32, (C, qf_ref.shape[1]), 0), 1) == 1

    def chunk_rows(sub, d):
        k = sub if d == 0 else subs - 1 - sub
        return slice(k * C, (k + 1) * C)

    groups = [(sub, d) for sub in range(subs) for d in range(2)]
    heads = [(sub * 2 + d, sub, d, h, slice(h * B_DIM, (h + 1) * B_DIM)) for sub, d in groups for h in range(B_HEADS)]
    tots = {}
    for sub, d in groups:
        g = sub * 2 + d
        q_ref, f_ref, i_ref = (qf_ref, ff_ref, if_ref) if d == 0 else (qb_ref, fb_ref, ib_ref)
        rows = chunk_rows(sub, d)
        tb = tbl_ref[d]
        e = jnp.exp(tb - jnp.max(tb, axis=0, keepdims=True))
        lb = jnp.sum(e[0:layer + 1], axis=0, keepdims=True) / jnp.sum(e, axis=0, keepdims=True)
        f = lb + (1.0 - lb) * jax.nn.sigmoid(f_ref[rows, :].astype(F32))
        e2 = _dot_split01(span_ref[d], jnp.log2(f))
        tot2 = e2[0:1] if d == 1 else e2[C - 1:C]
        tots[g] = jnp.exp2(tot2)
        w_buf[g, 0:C] = jnp.exp2(e2[0:C]).astype(BF16)
        w_buf[g, C:2 * C] = jnp.exp2(tot2 - e2[0:C]).astype(BF16)
        w_buf[g, 2 * C:3 * C] = jnp.where(odd_row if d == 0 else jnp.logical_not(odd_row), f, 1.0).astype(BF16)
        w_buf[g, 3 * C:] = jnp.exp2(e2[C:]).astype(BF16)
        q_buf[g] = jax.nn.silu(q_ref[rows, :].astype(F32)).astype(BF16)
        k_buf[g] = (1.0 - f).astype(BF16)
        v_buf[g] = i_ref[rows, :]
    for g, sub, d, h, sl in heads:
        qh, kh = q_buf[g, :, sl], k_buf[g, :, sl]
        s_buf[g, h, L] = _dot_nt(qh, kh)
        for l in range(L):
            wl = w_buf[g, (l + 2) * C:(l + 3) * C, sl]
            s_buf[g, h, l] = _dot_nt(qh * wl, kh * wl)
    for g, sub, d, h, sl in heads:
        level = level_ref[d]
        p = jnp.where(level == L, s_buf[g, h, L], 0.0)
        for l in range(L):
            p = jnp.where(level == l, s_buf[g, h, l], p)
        p_buf[g, h] = p.astype(BF16)
    for g, sub, d, h, sl in heads:
        o_ref = of_ref if d == 0 else ob_ref
        st = state_ref[d, h]
        vh = v_buf[g, :, sl]
        o_ref[chunk_rows(sub, d), sl] = (_dot(p_buf[g, h], vh)
                                         + _dot_nt(q_buf[g, :, sl] * w_buf[g, 0:C, sl], st.astype(BF16)))
        state_ref[d, h] = st * tots[g][:, sl] + _dot_tn(vh, k_buf[g, :, sl] * w_buf[g, C:2 * C, sl])


def mixer_b(proj, b_lb_table, layer, batch, C=GLA_CHUNK, subs=2):
    T = proj.shape[0]
    W = B_HEADS * B_DIM
    R = C * subs
    n = T // batch // R
    consts = [_gla_consts(C, rev) for rev in (False, True)]
    span = jnp.asarray(np.stack([c[0] for c in consts]), BF16)
    level = jnp.asarray(np.stack([c[1] for c in consts]))
    fwd = lambda col: pl.BlockSpec((R, W), lambda b, c: (b * n + c, col))
    bwd = lambda col: pl.BlockSpec((R, W), lambda b, c: (b * n + n - 1 - c, col))
    full = lambda a: pl.BlockSpec(a.shape, lambda b, c: (0,) * a.ndim)
    G = 2 * subs
    return pl.pallas_call(
        functools.partial(_gla_kernel, layer=layer, C=C),
        grid=(batch, n),
        in_specs=[fwd(2), fwd(3), fwd(5), bwd(2), bwd(4), bwd(5),
                  full(b_lb_table), full(span), full(level)],
        out_specs=[pl.BlockSpec((R, W), lambda b, c: (b * n + c, 0)),
                   pl.BlockSpec((R, W), lambda b, c: (b * n + n - 1 - c, 0))],
        out_shape=[jax.ShapeDtypeStruct((T, W), F32)] * 2,
        scratch_shapes=[pltpu.VMEM((2, B_HEADS, B_DIM, B_DIM), F32),
                        pltpu.VMEM((G, span.shape[1] + 2 * C, W), BF16),
                        pltpu.VMEM((G, C, W), BF16), pltpu.VMEM((G, C, W), BF16), pltpu.VMEM((G, C, W), BF16),
                        pltpu.VMEM((G, B_HEADS, span.shape[1] // C + 1, C, C), F32),
                        pltpu.VMEM((G, B_HEADS, C, C), BF16)],
        compiler_params=_params("arbitrary", "arbitrary"),
        name="mixer_b",
    )(proj, proj, proj, proj, proj, proj, b_lb_table, span, level)


def _router_epilogue(x_new, g_ref, wr_ref, x_ref, h_ref, aff_ref):
    x_ref[...] = x_new
    h = _rms(x_new, g_ref[...])
    h_ref[...] = h
    w = wr_ref[...]
    h_hi, w_hi = h.astype(BF16), w.astype(BF16)
    h_lo, w_lo = (h - h_hi.astype(F32)).astype(BF16), (w - w_hi.astype(F32)).astype(BF16)
    logits = _dot(h_hi, w_hi) + _dot(h_lo, w_hi) + _dot(h_hi, w_lo)
    e = jnp.exp(logits - jnp.max(logits, axis=-1, keepdims=True))
    aff_ref[...] = e / jnp.sum(e, axis=-1, keepdims=True)


def _outproj_even_kernel(a_ref, of_ref, ob_ref, gate_ref, bng_ref, w_ref, x_ref, g_ref, wr_ref,
                         xo_ref, h_ref, aff_ref):
    o = of_ref[...] + ob_ref[...]
    parts = []
    for h in range(B_HEADS):
        oh = o[:, h * B_DIM:(h + 1) * B_DIM]
        parts.append(oh * lax.rsqrt(jnp.mean(oh * oh, axis=-1, keepdims=True) + EPS))
    on = jnp.concatenate(parts, axis=-1) * bng_ref[...] * jax.nn.sigmoid(gate_ref[...].astype(F32))
    wa = a_ref.shape[1]
    mixed = _dot(a_ref[...], w_ref[0:wa, :]) + _dot(on.astype(BF16), w_ref[wa:, :])
    _router_epilogue(x_ref[...] + mixed, g_ref, wr_ref, xo_ref, h_ref, aff_ref)


def _router_out(T, D, tm):
    specs = [pl.BlockSpec((tm, D), lambda i: (i, 0)), pl.BlockSpec((tm, D), lambda i: (i, 0)),
             pl.BlockSpec((tm, N_EXPERTS), lambda i: (i, 0))]
    shapes = [jax.ShapeDtypeStruct((T, D), F32), jax.ShapeDtypeStruct((T, D), F32),
              jax.ShapeDtypeStruct((T, N_EXPERTS), F32)]
    return specs, shapes


def outproj_even(a_out, o_f, o_b, proj, b_norm_g, w_out, x, ffn_g, w_router, tm=512):
    T, D = x.shape
    W = a_out.shape[1]
    row = lambda w, col=0: pl.BlockSpec((tm, w), lambda i: (i, col))
    full = lambda s: pl.BlockSpec(s, lambda i: (0, 0))
    out_specs, out_shapes = _router_out(T, D, tm)
    return pl.pallas_call(
        _outproj_even_kernel,
        grid=(T // tm,),
        in_specs=[row(W), row(W), row(W), row(W, 6), full((1, W)), full(w_out.shape), row(D),
                  full((1, D)), full(w_router.shape)],
        out_specs=out_specs,
        out_shape=out_shapes,
        compiler_params=_params("parallel"),
        name="outproj_even",
    )(a_out, o_f, o_b, proj, b_norm_g.reshape(1, W), w_out.astype(BF16), x, ffn_g.reshape(1, D), w_router)


def _outproj_odd_kernel(*refs):
    P = len(DILATIONS)
    o_refs, l_refs = refs[:P], refs[P:2 * P]
    ex_ref, w_ref, x_ref, g_ref, wr_ref, xo_ref, h_ref, aff_ref, o_buf, l_buf = refs[2 * P:]
    tm = x_ref.shape[0]

    def token_order(ref, buf, d):
        if d == 1:
            return ref[0].astype(F32)
        tiles = buf.shape[0]
        for r in range(d):
            for t in range(tiles):
                lane0 = (r * tiles + t) * LANES
                buf[t, pl.ds(r, tm // d, stride=d), :] = ref[0, :, lane0:lane0 + LANES].astype(F32)
        return jnp.concatenate([buf[t] for t in range(tiles)], axis=-1)

    ls = [token_order(l_ref, l_buf, d)[:, 0:C_HEADS] for l_ref, d in zip(l_refs, DILATIONS)]
    m = functools.reduce(jnp.maximum, ls)
    es = [jnp.exp(l - m) for l in ls]
    den = functools.reduce(jnp.add, es)
    attn = None
    for e, o_ref, d in zip(es, o_refs, DILATIONS):
        wfull = _dot((e / den).astype(BF16), ex_ref[...])
        term = wfull * token_order(o_ref, o_buf, d)
        attn = term if attn is None else attn + term
    mixed = _dot(attn.astype(BF16), w_ref[...])
    _router_epilogue(x_ref[...] + mixed, g_ref, wr_ref, xo_ref, h_ref, aff_ref)


def outproj_odd(outs, lses, w_o, x, ffn_g, w_router, tm=512):
    T, D = x.shape
    tiles = outs[0].shape[1] // tm
    expand = jnp.asarray(np.kron(np.eye(C_HEADS), np.ones((1, C_HEAD_DIM))), BF16)
    dil = lambda w: [pl.BlockSpec((1, tm // d, d * w), lambda i: (i // tiles, i % tiles, 0)) for d in DILATIONS]
    full = lambda s: pl.BlockSpec(s, lambda i: (0, 0))
    out_specs, out_shapes = _router_out(T, D, tm)
    return pl.pallas_call(
        _outproj_odd_kernel,
        grid=(T // tm,),
        in_specs=dil(D) + dil(LANES) + [full(expand.shape), full(w_o.shape),
                                        pl.BlockSpec((tm, D), lambda i: (i, 0)), full((1, D)),
                                        full(w_router.shape)],
        out_specs=out_specs,
        out_shape=out_shapes,
        scratch_shapes=[pltpu.VMEM((D // LANES, tm, LANES), F32), pltpu.VMEM((1, tm, LANES), F32)],
        compiler_params=_params("parallel"),
        name="outproj_odd",
    )(*outs, *lses, expand, w_o.astype(BF16), x, ffn_g.reshape(1, D), w_router)


def _topk_kernel(aff_ref, upper_ref, strict_ref, pos_ref, tile_ref, *, cap):
    a = aff_ref[0]
    E, NC, _ = a.shape

    def count(mask):
        return jnp.sum(jnp.sum(mask.astype(F32), axis=2, keepdims=True), axis=1, keepdims=True)

    def as_float(bits):
        return lax.bitcast_convert_type(bits, jnp.float32)

    def search(i, thr):
        cand = thr | jnp.left_shift(jnp.int32(1), 30 - i)
        return jnp.where(count(a >= as_float(cand)) >= cap, cand, thr)

    thr = lax.fori_loop(0, 31, search, jnp.zeros((E, 1, 1), I32))
    gt = a >= as_float(thr + 1)
    eq = jnp.logical_and(a >= as_float(thr), jnp.logical_not(gt))
    need = cap - count(gt)

    def prefix(mask):
        m2 = mask.astype(BF16).reshape(E * NC, LANES)
        within = _dot(m2, upper_ref[...])
        total = within[:, LANES - 1:LANES].astype(BF16)
        tot_b = jnp.broadcast_to(total, (E * NC, LANES))
        starts = jnp.concatenate(
            [_dot(strict_ref[...], tot_b[e * NC:(e + 1) * NC]) for e in range(E)], axis=0)
        return (within + starts).reshape(E, NC, LANES), starts.reshape(E, NC, LANES)

    eq_incl, _ = prefix(eq)
    sel = gt | (eq & (eq_incl - 1.0 < need))
    sel_incl, starts = prefix(sel)
    pos_ref[0] = jnp.where(sel, sel_incl - 1.0, -1.0).astype(I32)
    tile = jnp.minimum(jnp.floor(starts * (1.0 / TOKEN_CHUNK)), float((cap - SLOT_WINDOW) // TOKEN_CHUNK))
    tile_ref[0] = tile.astype(I32)


def route_topk(aff_t, cap):
    B, E, S = aff_t.shape
    NC = S // TOKEN_CHUNK
    t = np.arange(LANES)
    upper = jnp.asarray(t[:, None] <= t[None, :], BF16)
    c = np.arange(NC)
    strict = jnp.asarray(c[None, :] < c[:, None], BF16)
    blk = pl.BlockSpec((1, E, NC, LANES), lambda b: (b, 0, 0, 0))
    pos, tiles = pl.pallas_call(
        functools.partial(_topk_kernel, cap=cap),
        grid=(B,),
        in_specs=[blk, pl.BlockSpec(upper.shape, lambda b: (0, 0)), pl.BlockSpec(strict.shape, lambda b: (0, 0))],
        out_specs=[blk, blk],
        out_shape=[jax.ShapeDtypeStruct((B, E, NC, LANES), I32)] * 2,
        compiler_params=_params("parallel"),
        name="route_topk",
    )(aff_t.reshape(B, E, NC, LANES), upper, strict)
    return pos, tiles[..., 0].reshape(-1)


def _index_kernel(tiles_ref, pos_ref, sel_ref, idx_ref, acc_ref):
    b, e = pl.program_id(0), pl.program_id(1)
    n_exp = pl.num_programs(1)
    NC = pos_ref.shape[2]
    acc_ref[...] = jnp.zeros_like(acc_ref)
    slot = lax.broadcasted_iota(I32, (SLOT_WINDOW, TOKEN_CHUNK), 0)

    def body(c, carry):
        j = tiles_ref[(b * n_exp + e) * NC + c]
        onehot = jnp.where(slot + j * TOKEN_CHUNK == pos_ref[0, 0, pl.ds(c, 1), :], 1.0, 0.0).astype(BF16)
        hit = _dot_nt(sel_ref[...], onehot)
        tok = hit[0:1] + hit[1:2] * lax.convert_element_type(c * TOKEN_CHUNK, F32)
        acc_ref[j, 0:1, :] += tok[:, 0:TOKEN_CHUNK]
        acc_ref[j + 1, 0:1, :] += tok[:, TOKEN_CHUNK:]
        return carry

    lax.fori_loop(0, NC, body, 0, unroll=32)
    idx_ref[0, 0] = acc_ref[...].astype(I32)


def moe_slot_tokens(tiles, pos, cap):
    B, E, NC, _ = pos.shape
    nt = cap // TOKEN_CHUNK
    sel = np.zeros((8, TOKEN_CHUNK), np.float32)
    sel[0] = np.arange(TOKEN_CHUNK)
    sel[1] = 1.0
    idx = pl.pallas_call(
        _index_kernel,
        grid_spec=pltpu.PrefetchScalarGridSpec(
            num_scalar_prefetch=1,
            grid=(B, E),
            in_specs=[pl.BlockSpec((1, 1, NC, LANES), lambda b, e, s: (b, e, 0, 0)),
                      pl.BlockSpec((8, TOKEN_CHUNK), lambda b, e, s: (0, 0))],
            out_specs=pl.BlockSpec((1, 1, nt, 8, LANES), lambda b, e, s: (b, e, 0, 0, 0)),
            scratch_shapes=[pltpu.VMEM((nt, 8, LANES), F32)]),
        out_shape=jax.ShapeDtypeStruct((B, E, nt, 8, LANES), I32),
        compiler_params=_params("arbitrary", "arbitrary"),
        name="moe_slot_tokens",
    )(tiles, pos, jnp.asarray(sel, BF16))
    return idx[:, :, :, 0, :].reshape(-1)


def _ffn_kernel(idx_ref, h_ref, wg_ref, wu_ref, wd_ref, y_ref, x_buf, sem, wgb_ref, wub_ref, wdb_ref, *, tr):
    e, b = pl.program_id(0), pl.program_id(1)
    n_exp, nb = pl.num_programs(0), pl.num_programs(1)
    cap = x_buf.shape[1]
    step = e * nb + b

    def row_copy(st, buf, i):
        eb = st // nb
        bb = st - eb * nb
        tok = idx_ref[(bb * n_exp + eb) * cap + i]
        return pltpu.make_async_copy(h_ref.at[bb, pl.ds(tok, 1), :], x_buf.at[buf, pl.ds(i, 1), :], sem.at[buf])

    cur = jnp.bitwise_and(step, 1)

    @pl.when(step == 0)
    def _():
        lax.fori_loop(0, cap, lambda i, c: (row_copy(step, 0, i).start(), c)[1], 0, unroll=8)

    @pl.when(b == 0)
    def _():
        wgb_ref[...] = wg_ref[0, 0].astype(BF16)
        wub_ref[...] = wu_ref[0, 0].astype(BF16)
        wdb_ref[...] = wd_ref[0, 0].astype(BF16)

    pltpu.make_async_copy(h_ref.at[0, pl.ds(0, cap), :], x_buf.at[cur], sem.at[cur]).wait()

    def compute(prefetch):
        for r in range(cap // tr):
            rows = pl.ds(r * tr, tr)
            x = x_buf[cur, rows, :].astype(BF16)
            if prefetch:
                for i in range(r * tr, (r + 1) * tr):
                    row_copy(step + 1, 1 - cur, i).start()
            g = _dot(x, wgb_ref[...])
            u = _dot(x, wub_ref[...])
            mid = (g * jax.nn.sigmoid(g) * u).astype(BF16)
            y_ref[0, 0, rows, :] = _dot(mid, wdb_ref[...]).astype(BF16)

    is_last = step + 1 == n_exp * nb
    pl.when(jnp.logical_not(is_last))(lambda: compute(True))
    pl.when(is_last)(lambda: compute(False))


def moe_ffn(idx, h, w_gate, w_up, w_down, layer, cap, tr=256):
    B, S, D = h.shape
    E, F = w_gate.shape[1], w_gate.shape[-1]
    wspec = lambda r, c: pl.BlockSpec((1, 1, r, c), lambda e, b, s: (layer, e, 0, 0))
    return pl.pallas_call(
        functools.partial(_ffn_kernel, tr=min(tr, cap)),
        grid_spec=pltpu.PrefetchScalarGridSpec(
            num_scalar_prefetch=1,
            grid=(E, B),
            in_specs=[pl.BlockSpec(memory_space=pl.ANY), wspec(D, F), wspec(D, F), wspec(F, D)],
            out_specs=pl.BlockSpec((1, 1, cap, D), lambda e, b, s: (b, e, 0, 0)),
            scratch_shapes=[pltpu.VMEM((2, cap, D), F32), pltpu.SemaphoreType.DMA((2,)),
                            pltpu.VMEM((D, F), BF16), pltpu.VMEM((D, F), BF16), pltpu.VMEM((F, D), BF16)]),
        out_shape=jax.ShapeDtypeStruct((B, E, cap, D), BF16),
        compiler_params=_params("arbitrary", "arbitrary"),
        name="moe_ffn",
    )(idx, h, w_gate, w_up, w_down)


def _combine_kernel(tiles_ref, post_ref, gate_ref, x_ref, y_ref, o_ref):
    b, i = pl.program_id(0), pl.program_id(2)
    n_exp = y_ref.shape[1]
    per_step = x_ref.shape[1] // TOKEN_CHUNK
    NC = pl.num_programs(2) * per_step
    slot = lax.broadcasted_iota(I32, (TOKEN_CHUNK, SLOT_WINDOW), 1)
    for k in range(per_step):
        rows = slice(k * TOKEN_CHUNK, (k + 1) * TOKEN_CHUNK)
        acc = x_ref[0, rows, :]
        for e in range(n_exp):
            tile = tiles_ref[(b * n_exp + e) * NC + i * per_step + k]
            base = pl.multiple_of(tile * TOKEN_CHUNK, TOKEN_CHUNK)
            onehot = jnp.where(slot + base == post_ref[0, rows, e:e + 1], 1.0, 0.0).astype(BF16)
            acc = acc + _dot(onehot, y_ref[0, e, pl.ds(base, SLOT_WINDOW), :]) * gate_ref[0, rows, e:e + 1]
        o_ref[0, rows, :] = acc


def moe_combine(tiles, pos_t, gate, x, y, dsplit=2, tm=512):
    B, S, D = x.shape
    E, cap = y.shape[1], y.shape[2]
    dh = D // dsplit
    tok = lambda w: pl.BlockSpec((1, tm, w), lambda b, j, c, s: (b, c, 0))
    res = pl.BlockSpec((1, tm, dh), lambda b, j, c, s: (b, c, j))
    return pl.pallas_call(
        _combine_kernel,
        grid_spec=pltpu.PrefetchScalarGridSpec(
            num_scalar_prefetch=1,
            grid=(B, dsplit, S // tm),
            in_specs=[tok(E), tok(E), res,
                      pl.BlockSpec((1, E, cap, dh), lambda b, j, c, s: (b, 0, 0, j))],
            out_specs=res),
        out_shape=jax.ShapeDtypeStruct((B, S, D), F32),
        compiler_params=_params("arbitrary", "arbitrary", "arbitrary"),
        name="moe_combine",
    )(tiles, pos_t, gate, x, y)


def expert_choice_moe(x, h, aff, batch, w_gate, w_up, w_down, layer):
    T, D = x.shape
    S = T // batch
    cap = max(1, CAPACITY_FACTOR * S // N_EXPERTS)
    aff = aff.reshape(batch, S, N_EXPERTS)
    pos, tiles = route_topk(jnp.swapaxes(aff, 1, 2), cap)
    idx = moe_slot_tokens(tiles, pos, cap)
    y = moe_ffn(idx, h.reshape(batch, S, D), w_gate, w_up, w_down, layer, cap)
    pos_t = jnp.swapaxes(pos.reshape(batch, N_EXPERTS, S), 1, 2)
    return moe_combine(tiles, pos_t, aff, x.reshape(batch, S, D), y).reshape(T, D)


ATTN_TQ = 128
ATTN_TK = ATTN_TQ + 2 * HALF_WINDOW
ATTN_SUB = 2


def _t5_bucket(rel):
    half_buckets = REL_BUCKETS // 2
    max_exact = half_buckets // 2
    n = jnp.abs(rel)
    scaled = (jnp.log(jnp.maximum(n, 1).astype(jnp.float32) / max_exact)
              / math.log(REL_MAX_DISTANCE / max_exact))
    large = jnp.minimum(max_exact + jnp.floor(scaled * (half_buckets - max_exact)).astype(jnp.int32),
                        half_buckets - 1)
    return jnp.where(rel > 0, half_buckets, 0) + jnp.where(n < max_exact, n, large)


def _bias_kernel(table_ref, bucket_ref, o_ref):
    h = pl.program_id(1)
    bucket = bucket_ref[0]
    acc = jnp.zeros(bucket.shape, F32)
    for bk in range(REL_BUCKETS):
        acc = jnp.where(bucket == bk, table_ref[bk * C_HEADS + h], acc)
    q = lax.broadcasted_iota(I32, bucket.shape, 0)
    kc = lax.broadcasted_iota(I32, bucket.shape, 1)
    middle = jnp.where(jnp.abs(kc - HALF_WINDOW - q) <= HALF_WINDOW, acc, NEG_INF)
    o_ref[0, 0, 0] = jnp.where(kc >= HALF_WINDOW, middle, NEG_INF)
    o_ref[0, 1, 0] = middle
    o_ref[0, 2, 0] = jnp.where(kc < HALF_WINDOW + ATTN_TQ, middle, NEG_INF)


def attention_bias(rel_bias):
    rel = np.arange(ATTN_TK)[None, :] - HALF_WINDOW - np.arange(ATTN_TQ)[:, None]
    buckets = jnp.stack([_t5_bucket(jnp.asarray(rel * d, I32)) for d in DILATIONS]).astype(I32)
    P = len(DILATIONS)
    return pl.pallas_call(
        _bias_kernel,
        grid_spec=pltpu.PrefetchScalarGridSpec(
            num_scalar_prefetch=1,
            grid=(P, C_HEADS),
            in_specs=[pl.BlockSpec((1, ATTN_TQ, ATTN_TK), lambda p, h, t: (p, 0, 0))],
            out_specs=pl.BlockSpec((1, 3, 1, ATTN_TQ, ATTN_TK), lambda p, h, t: (p, 0, h, 0, 0))),
        out_shape=jax.ShapeDtypeStruct((P, 3, C_HEADS, ATTN_TQ, ATTN_TK), F32),
        compiler_params=_params("arbitrary", "arbitrary"),
        name="attention_bias",
    )(rel_bias.reshape(-1), buckets)


def _attn_kernel(q_ref, kp_ref, km_ref, kn_ref, vp_ref, vm_ref, vn_ref, bias_ref, o_ref, lse_ref,
                 k_buf, v_buf, s_buf, p_buf):
    i = pl.program_id(2)
    last = pl.num_programs(2) - 1
    hw = HALF_WINDOW
    rows = q_ref.shape[1]
    k_buf[0:hw] = kp_ref[0]
    k_buf[hw:hw + rows] = km_ref[0]
    k_buf[hw + rows:] = kn_ref[0]
    v_buf[0:hw] = vp_ref[0]
    v_buf[hw:hw + rows] = vm_ref[0]
    v_buf[hw + rows:] = vn_ref[0]
    first_head = lax.broadcasted_iota(I32, (ATTN_TQ, LANES), 1) < C_HEAD_DIM
    scale = C_HEAD_DIM ** -0.5
    lse_ref[...] = jnp.zeros_like(lse_ref)
    for sub in range(rows // ATTN_TQ):
        qrows = slice(sub * ATTN_TQ, (sub + 1) * ATTN_TQ)
        krows = slice(sub * ATTN_TQ, sub * ATTN_TQ + ATTN_TK)
        if sub == 0:
            variant = jnp.where(i == 0, 0, 1)
        elif sub == rows // ATTN_TQ - 1:
            variant = jnp.where(i == last, 2, 1)
        else:
            variant = 1
        for pair in range(C_HEADS // 2):
            cols = slice(pair * LANES, (pair + 1) * LANES)
            q = q_ref[0, qrows, cols] * scale
            for half in range(2):
                mine = first_head if half == 0 else jnp.logical_not(first_head)
                s_buf[2 * pair + half] = (_dot_nt(jnp.where(mine, q, jnp.zeros_like(q)), k_buf[krows, cols])
                                          + bias_ref[variant, 2 * pair + half])
        for h in range(C_HEADS):
            s = s_buf[h]
            m = jnp.max(s, axis=-1, keepdims=True)
            p = jnp.exp(s - m)
            den = jnp.sum(p, axis=-1, keepdims=True)
            p_buf[h] = (p / den).astype(BF16)
            lse_ref[0, qrows, h:h + 1] = m + jnp.log(den)
        for pair in range(C_HEADS // 2):
            cols = slice(pair * LANES, (pair + 1) * LANES)
            o_ref[0, qrows, cols] = jnp.where(first_head, _dot(p_buf[2 * pair], v_buf[krows, cols]),
                                              _dot(p_buf[2 * pair + 1], v_buf[krows, cols])).astype(BF16)


def dilated_attention(view, bias, d):
    batch, n, D3 = view.shape
    D = D3 // d // 3
    rows = ATTN_TQ * ATTN_SUB
    nb = n // HALF_WINDOW
    r = rows // HALF_WINDOW
    assert n // rows >= 1 and ATTN_SUB >= 2
    main = lambda c: pl.BlockSpec((1, rows, D), lambda b, j, i: (b, i, 3 * j + c))
    prev = lambda c: pl.BlockSpec((1, HALF_WINDOW, D), lambda b, j, i: (b, jnp.maximum(i * r - 1, 0), 3 * j + c))
    nxt = lambda c: pl.BlockSpec((1, HALF_WINDOW, D),
                                 lambda b, j, i: (b, jnp.minimum(i * r + r, nb - 1), 3 * j + c))
    return pl.pallas_call(
        _attn_kernel,
        grid=(batch, d, n // rows),
        in_specs=[main(0), prev(1), main(1), nxt(1), prev(2), main(2), nxt(2),
                  pl.BlockSpec(bias.shape, lambda b, j, i: (0, 0, 0, 0))],
        out_specs=[pl.BlockSpec((1, rows, D), lambda b, j, i: (b, i, j)),
                   pl.BlockSpec((1, rows, LANES), lambda b, j, i: (b, i, j))],
        out_shape=[jax.ShapeDtypeStruct((batch, n, d * D), BF16),
                   jax.ShapeDtypeStruct((batch, n, d * LANES), F32)],
        scratch_shapes=[pltpu.VMEM((rows + 2 * HALF_WINDOW, D), BF16), pltpu.VMEM((rows + 2 * HALF_WINDOW, D), BF16),
                        pltpu.VMEM((C_HEADS, ATTN_TQ, ATTN_TK), F32), pltpu.VMEM((C_HEADS, ATTN_TQ, ATTN_TK), BF16)],
        compiler_params=_params("parallel", "parallel", "parallel"),
        name="dilated_attention",
    )(view, view, view, view, view, view, view, bias)


def dilated_mixture(views, rel_bias):
    bias = attention_bias(rel_bias)
    results = [dilated_attention(view, bias[p], d) for p, (view, d) in enumerate(zip(views, DILATIONS))]
    return [o for o, _ in results], [lse for _, lse in results]


def _final_norm_kernel(x_ref, g_ref, o_ref):
    o_ref[...] = _rms(x_ref[...], g_ref[...])


def final_rmsnorm(x, g, tm=512):
    T, D = x.shape
    return pl.pallas_call(
        _final_norm_kernel,
        grid=(T // tm,),
        in_specs=[pl.BlockSpec((tm, D), lambda i: (i, 0)), pl.BlockSpec((1, D), lambda i: (0, 0))],
        out_specs=pl.BlockSpec((tm, D), lambda i: (i, 0)),
        out_shape=jax.ShapeDtypeStruct((T, D), F32),
        compiler_params=_params("parallel"),
        name="final_norm",
    )(x, g.reshape(1, D))


def kernel(x, mix_norm, ffn_norm, final_norm, w_in_even, w_out_even, a_ln_g, a_ln_b, a_w_s, a_b_s, b_lb_table,
           b_norm_g, w_qkv_odd, w_o_odd, rel_bias, w_router, w_gate, w_up, w_down):
    B, S, D = x.shape
    depth = mix_norm.shape[0]
    xt = x.reshape(B * S, D)
    for layer in range(depth):
        j = layer // 2
        if layer % 2 == 0:
            proj = norm_matmul(xt, mix_norm[layer], w_in_even[j].astype(BF16), B)[0].reshape(B * S, -1)
            a_out = mixer_a(proj, a_ln_g[j], a_ln_b[j], a_w_s[j], a_b_s[j])
            o_f, o_b = mixer_b(proj, b_lb_table, layer, B)
            xt, h, aff = outproj_even(a_out, o_f, o_b, proj, b_norm_g[j], w_out_even[j], xt,
                                      ffn_norm[layer], w_router[layer])
        else:
            views = norm_matmul(xt, mix_norm[layer], w_qkv_odd[j].astype(BF16), B, DILATIONS, tm=512)
            outs, lses = dilated_mixture(views, rel_bias)
            xt, h, aff = outproj_odd(outs, lses, w_o_odd[j], xt, ffn_norm[layer], w_router[layer])
        xt = expert_choice_moe(xt, h, aff, B, w_gate, w_up, w_down, layer)
    return final_rmsnorm(xt, final_norm).reshape(B, S, D)
```

```python
import functools
import math

import numpy as np
import jax
import jax.numpy as jnp
from jax import lax
from jax.experimental import pallas as pl
from jax.experimental.pallas import tpu as pltpu

F32 = jnp.float32
BF16 = jnp.bfloat16
I32 = jnp.int32
EPS = 1e-6
NEG_INF = -1e30
HIGHEST = lax.Precision.HIGHEST

LANES = 128
VMEM_LIMIT = 56 * 1024 * 1024

A_GROUPS = 4
A_CHUNK = 128
B_HEADS = 4
B_DIM = 128
GLA_CHUNK = 128
C_HEADS = 16
C_HEAD_DIM = 64
HALF_WINDOW = 64
DILATIONS = (1, 4, 16)
REL_BUCKETS = 32
REL_MAX_DISTANCE = 1024
N_EXPERTS = 16
CAPACITY_FACTOR = 2
TOKEN_CHUNK = 128
SLOT_WINDOW = 256


def _params(*sem):
    return pltpu.CompilerParams(dimension_semantics=sem, vmem_limit_bytes=VMEM_LIMIT)


def _dot(a, b, **kw):
    return jnp.dot(a, b, preferred_element_type=F32, **kw)


def _dot_nt(a, b):
    return lax.dot_general(a, b, (((1,), (1,)), ((), ())), preferred_element_type=F32)


def _dot_tn(a, b):
    return lax.dot_general(a, b, (((0,), (0,)), ((), ())), preferred_element_type=F32)


def _dot_split01(a2, x):
    hi = x.astype(BF16)
    lo = (x - hi.astype(F32)).astype(BF16)
    return _dot(a2, jnp.concatenate([hi, lo], axis=0))


def _rms(x, g):
    return x * lax.rsqrt(jnp.mean(x * x, axis=-1, keepdims=True) + EPS) * g


def _norm_matmul_kernel(x_ref, g_ref, w_ref, *rest, tn, dilations):
    o_refs, scratch = rest[:len(dilations)], rest[len(dilations):]
    tm, N = x_ref.shape[0], w_ref.shape[1]
    h = _rms(x_ref[...], g_ref[...]).astype(BF16)
    for j in range(N // tn):
        res = _dot(h, w_ref[:, j * tn:(j + 1) * tn])
        if scratch:
            for t in range(tn // LANES):
                scratch[0][t] = res[:, t * LANES:(t + 1) * LANES]
        for o_ref, d in zip(o_refs, dilations):
            if d == 1:
                o_ref[0, :, j * tn:(j + 1) * tn] = res.astype(BF16)
                continue
            for r in range(d):
                for t in range(tn // LANES):
                    lane0 = r * N + j * tn + t * LANES
                    o_ref[0, :, lane0:lane0 + LANES] = (
                        scratch[0][t, pl.ds(r, tm // d, stride=d), :].astype(BF16))


def norm_matmul(x, g, w, batch, dilations=(1,), tm=1024, tn=512):
    T, D = x.shape
    N = w.shape[1]
    S = T // batch
    tiles = S // tm
    strided = any(d > 1 for d in dilations)
    return pl.pallas_call(
        functools.partial(_norm_matmul_kernel, tn=tn, dilations=dilations),
        grid=(T // tm,),
        in_specs=[
            pl.BlockSpec((tm, D), lambda i: (i, 0)),
            pl.BlockSpec((1, D), lambda i: (0, 0)),
            pl.BlockSpec((D, N), lambda i: (0, 0)),
        ],
        out_specs=[pl.BlockSpec((1, tm // d, d * N), lambda i: (i // tiles, i % tiles, 0)) for d in dilations],
        out_shape=[jax.ShapeDtypeStruct((batch, S // d, d * N), BF16) for d in dilations],
        scratch_shapes=[pltpu.VMEM((tn // LANES, tm, LANES), F32)] if strided else [],
        compiler_params=_params("parallel"),
        name="norm_matmul",
    )(x, g.reshape(1, D), w)


def _mixa_kernel(u_ref, v_ref, lg_ref, lb_ref, ws_ref, bs_ref, o_ref):
    tm = u_ref.shape[0]
    u = jax.nn.gelu(u_ref[...].astype(F32))
    v = jax.nn.gelu(v_ref[...].astype(F32))
    mu = jnp.mean(v, axis=-1, keepdims=True)
    vc = v - mu
    vn = vc * lax.rsqrt(jnp.mean(vc * vc, axis=-1, keepdims=True) + EPS)
    vb = (vn * lg_ref[...] + lb_ref[...]).astype(BF16)
    for n in range(tm // A_CHUNK):
        rows = slice(n * A_CHUNK, (n + 1) * A_CHUNK)
        for g in range(A_GROUPS):
            cols = slice(g * LANES, (g + 1) * LANES)
            mixed = _dot(ws_ref[g], vb[rows, cols]) + bs_ref[:, g:g + 1]
            o_ref[rows, cols] = (u[rows, cols] * mixed).astype(BF16)


def mixer_a(proj, ln_g, ln_b, w_s, b_s, tm=512):
    T = proj.shape[0]
    W = A_GROUPS * LANES
    return pl.pallas_call(
        _mixa_kernel,
        grid=(T // tm,),
        in_specs=[
            pl.BlockSpec((tm, W), lambda i: (i, 0)),
            pl.BlockSpec((tm, W), lambda i: (i, 1)),
            pl.BlockSpec((1, W), lambda i: (0, 0)),
            pl.BlockSpec((1, W), lambda i: (0, 0)),
            pl.BlockSpec((A_GROUPS, A_CHUNK, A_CHUNK), lambda i: (0, 0, 0)),
            pl.BlockSpec((A_CHUNK, A_GROUPS), lambda i: (0, 0)),
        ],
        out_specs=pl.BlockSpec((tm, W), lambda i: (i, 0)),
        out_shape=jax.ShapeDtypeStruct((T, W), BF16),
        compiler_params=_params("parallel"),
        name="mixer_a",
    )(proj, proj, ln_g.reshape(1, W), ln_b.reshape(1, W), w_s.astype(BF16), b_s.T)


def _gla_consts(C, reverse):
    t = np.arange(C)[:, None]
    r = np.arange(C)[None, :]
    L = int(round(math.log2(C)))
    spans = [(r >= t) if reverse else (r <= t)]
    level = np.where(np.eye(C, dtype=bool), L, -1).astype(np.int32)
    for l in range(L):
        bs = (t >> (l + 1)) << (l + 1)
        mid = bs + (1 << l)
        if reverse:
            act_q = t < mid
            span = np.where(act_q, (r >= t) & (r < mid), (r >= mid) & (r < t))
        else:
            act_q = t >= mid
            span = np.where(act_q, (r >= mid) & (r <= t), (r > t) & (r < mid))
        if l > 0:
            spans.append(span)
        level[(bs == bs.T) & act_q & ~act_q.T] = l
    return np.tile(np.concatenate(spans, axis=0).astype(np.float32), (1, 2)), level


def _gla_kernel(qf_ref, ff_ref, if_ref, qb_ref, fb_ref, ib_ref, tbl_ref, span_ref, level_ref,
                of_ref, ob_ref, state_ref, w_buf, q_buf, k_buf, v_buf, s_buf, p_buf, *, layer, C):
    @pl.when(pl.program_id(1) == 0)
    def _():
        state_ref[...] = jnp.zeros_like(state_ref)

    L = span_ref.shape[1] // C
    subs = qf_ref.shape[0] // C
    odd_row = jnp.bitwise_and(lax.broadcasted_iota(I32, (C, qf_ref.shape[1]), 0), 1) == 1

    def chunk_rows(sub, d):
        k = sub if d == 0 else subs - 1 - sub
        return slice(k * C, (k + 1) * C)

    groups = [(sub, d) for sub in range(subs) for d in range(2)]
    heads = [(sub * 2 + d, sub, d, h, slice(h * B_DIM, (h + 1) * B_DIM)) for sub, d in groups for h in range(B_HEADS)]
    tots = {}
    for sub, d in groups:
        g = sub * 2 + d
        q_ref, f_ref, i_ref = (qf_ref, ff_ref, if_ref) if d == 0 else (qb_ref, fb_ref, ib_ref)
        rows = chunk_rows(sub, d)
        tb = tbl_ref[d]
        e = jnp.exp(tb - jnp.max(tb, axis=0, keepdims=True))
        lb = jnp.sum(e[0:layer + 1], axis=0, keepdims=True) / jnp.sum(e, axis=0, keepdims=True)
        f = lb + (1.0 - lb) * jax.nn.sigmoid(f_ref[rows, :].astype(F32))
        e2 = _dot_split01(span_ref[d], jnp.log2(f))
        tot2 = e2[0:1] if d == 1 else e2[C - 1:C]
        tots[g] = jnp.exp2(tot2)
        w_buf[g, 0:C] = jnp.exp2(e2[0:C]).astype(BF16)
        w_buf[g, C:2 * C] = jnp.exp2(tot2 - e2[0:C]).astype(BF16)
        w_buf[g, 2 * C:3 * C] = jnp.where(odd_row if d == 0 else jnp.logical_not(odd_row), f, 1.0).astype(BF16)
        w_buf[g, 3 * C:] = jnp.exp2(e2[C:]).astype(BF16)
        q_buf[g] = jax.nn.silu(q_ref[rows, :].astype(F32)).astype(BF16)
        k_buf[g] = (1.0 - f).astype(BF16)
        v_buf[g] = i_ref[rows, :]
    for g, sub, d, h, sl in heads:
        qh, kh = q_buf[g, :, sl], k_buf[g, :, sl]
        s_buf[g, h, L] = _dot_nt(qh, kh)
        for l in range(L):
            wl = w_buf[g, (l + 2) * C:(l + 3) * C, sl]
            s_buf[g, h, l] = _dot_nt(qh * wl, kh * wl)
    for g, sub, d, h, sl in heads:
        level = level_ref[d]
        p = jnp.where(level == L, s_buf[g, h, L], 0.0)
        for l in range(L):
            p = jnp.where(level == l, s_buf[g, h, l], p)
        p_buf[g, h] = p.astype(BF16)
    for g, sub, d, h, sl in heads:
        o_ref = of_ref if d == 0 else ob_ref
        st = state_ref[d, h]
        vh = v_buf[g, :, sl]
        o_ref[chunk_rows(sub, d), sl] = (_dot(p_buf[g, h], vh)
                                         + _dot_nt(q_buf[g, :, sl] * w_buf[g, 0:C, sl], st.astype(BF16)))
        state_ref[d, h] = st * tots[g][:, sl] + _dot_tn(vh, k_buf[g, :, sl] * w_buf[g, C:2 * C, sl])


def mixer_b(proj, b_lb_table, layer, batch, C=GLA_CHUNK, subs=2):
    T = proj.shape[0]
    W = B_HEADS * B_DIM
    R = C * subs
    n = T // batch // R
    consts = [_gla_consts(C, rev) for rev in (False, True)]
    span = jnp.asarray(np.stack([c[0] for c in consts]), BF16)
    level = jnp.asarray(np.stack([c[1] for c in consts]))
    fwd = lambda col: pl.BlockSpec((R, W), lambda b, c: (b * n + c, col))
    bwd = lambda col: pl.BlockSpec((R, W), lambda b, c: (b * n + n - 1 - c, col))
    full = lambda a: pl.BlockSpec(a.shape, lambda b, c: (0,) * a.ndim)
    G = 2 * subs
    return pl.pallas_call(
        functools.partial(_gla_kernel, layer=layer, C=C),
        grid=(batch, n),
        in_specs=[fwd(2), fwd(3), fwd(5), bwd(2), bwd(4), bwd(5),
                  full(b_lb_table), full(span), full(level)],
        out_specs=[pl.BlockSpec((R, W), lambda b, c: (b * n + c, 0)),
                   pl.BlockSpec((R, W), lambda b, c: (b * n + n - 1 - c, 0))],
        out_shape=[jax.ShapeDtypeStruct((T, W), F32)] * 2,
        scratch_shapes=[pltpu.VMEM((2, B_HEADS, B_DIM, B_DIM), F32),
                        pltpu.VMEM((G, span.shape[1] + 2 * C, W), BF16),
                        pltpu.VMEM((G, C, W), BF16), pltpu.VMEM((G, C, W), BF16), pltpu.VMEM((G, C, W), BF16),
                        pltpu.VMEM((G, B_HEADS, span.shape[1] // C + 1, C, C), F32),
                        pltpu.VMEM((G, B_HEADS, C, C), BF16)],
        compiler_params=_params("arbitrary", "arbitrary"),
        name="mixer_b",
    )(proj, proj, proj, proj, proj, proj, b_lb_table, span, level)


def _router_epilogue(x_new, g_ref, wr_ref, x_ref, h_ref, aff_ref):
    x_ref[...] = x_new
    h = _rms(x_new, g_ref[...])
    h_ref[...] = h
    w = wr_ref[...]
    h_hi, w_hi = h.astype(BF16), w.astype(BF16)
    h_lo, w_lo = (h - h_hi.astype(F32)).astype(BF16), (w - w_hi.astype(F32)).astype(BF16)
    logits = _dot(h_hi, w_hi) + _dot(h_lo, w_hi) + _dot(h_hi, w_lo)
    e = jnp.exp(logits - jnp.max(logits, axis=-1, keepdims=True))
    aff_ref[...] = e / jnp.sum(e, axis=-1, keepdims=True)


def _outproj_even_kernel(a_ref, of_ref, ob_ref, gate_ref, bng_ref, w_ref, x_ref, g_ref, wr_ref,
                         xo_ref, h_ref, aff_ref):
    o = of_ref[...] + ob_ref[...]
    parts = []
    for h in range(B_HEADS):
        oh = o[:, h * B_DIM:(h + 1) * B_DIM]
        parts.append(oh * lax.rsqrt(jnp.mean(oh * oh, axis=-1, keepdims=True) + EPS))
    on = jnp.concatenate(parts, axis=-1) * bng_ref[...] * jax.nn.sigmoid(gate_ref[...].astype(F32))
    wa = a_ref.shape[1]
    mixed = _dot(a_ref[...], w_ref[0:wa, :]) + _dot(on.astype(BF16), w_ref[wa:, :])
    _router_epilogue(x_ref[...] + mixed, g_ref, wr_ref, xo_ref, h_ref, aff_ref)


def _router_out(T, D, tm):
    specs = [pl.BlockSpec((tm, D), lambda i: (i, 0)), pl.BlockSpec((tm, D), lambda i: (i, 0)),
             pl.BlockSpec((tm, N_EXPERTS), lambda i: (i, 0))]
    shapes = [jax.ShapeDtypeStruct((T, D), F32), jax.ShapeDtypeStruct((T, D), F32),
              jax.ShapeDtypeStruct((T, N_EXPERTS), F32)]
    return specs, shapes


def outproj_even(a_out, o_f, o_b, proj, b_norm_g, w_out, x, ffn_g, w_router, tm=512):
    T, D = x.shape
    W = a_out.shape[1]
    row = lambda w, col=0: pl.BlockSpec((tm, w), lambda i: (i, col))
    full = lambda s: pl.BlockSpec(s, lambda i: (0, 0))
    out_specs, out_shapes = _router_out(T, D, tm)
    return pl.pallas_call(
        _outproj_even_kernel,
        grid=(T // tm,),
        in_specs=[row(W), row(W), row(W), row(W, 6), full((1, W)), full(w_out.shape), row(D),
                  full((1, D)), full(w_router.shape)],
        out_specs=out_specs,
        out_shape=out_shapes,
        compiler_params=_params("parallel"),
        name="outproj_even",
    )(a_out, o_f, o_b, proj, b_norm_g.reshape(1, W), w_out.astype(BF16), x, ffn_g.reshape(1, D), w_router)


def _outproj_odd_kernel(*refs):
    P = len(DILATIONS)
    o_refs, l_refs = refs[:P], refs[P:2 * P]
    ex_ref, w_ref, x_ref, g_ref, wr_ref, xo_ref, h_ref, aff_ref, o_buf, l_buf = refs[2 * P:]
    tm = x_ref.shape[0]

    def token_order(ref, buf, d):
        if d == 1:
            return ref[0].astype(F32)
        tiles = buf.shape[0]
        for r in range(d):
            for t in range(tiles):
                lane0 = (r * tiles + t) * LANES
                buf[t, pl.ds(r, tm // d, stride=d), :] = ref[0, :, lane0:lane0 + LANES].astype(F32)
        return jnp.concatenate([buf[t] for t in range(tiles)], axis=-1)

    ls = [token_order(l_ref, l_buf, d)[:, 0:C_HEADS] for l_ref, d in zip(l_refs, DILATIONS)]
    m = functools.reduce(jnp.maximum, ls)
    es = [jnp.exp(l - m) for l in ls]
    den = functools.reduce(jnp.add, es)
    attn = None
    for e, o_ref, d in zip(es, o_refs, DILATIONS):
        wfull = _dot((e / den).astype(BF16), ex_ref[...])
        term = wfull * token_order(o_ref, o_buf, d)
        attn = term if attn is None else attn + term
    mixed = _dot(attn.astype(BF16), w_ref[...])
    _router_epilogue(x_ref[...] + mixed, g_ref, wr_ref, xo_ref, h_ref, aff_ref)


def outproj_odd(outs, lses, w_o, x, ffn_g, w_router, tm=512):
    T, D = x.shape
    tiles = outs[0].shape[1] // tm
    expand = jnp.asarray(np.kron(np.eye(C_HEADS), np.ones((1, C_HEAD_DIM))), BF16)
    dil = lambda w: [pl.BlockSpec((1, tm // d, d * w), lambda i: (i // tiles, i % tiles, 0)) for d in DILATIONS]
    full = lambda s: pl.BlockSpec(s, lambda i: (0, 0))
    out_specs, out_shapes = _router_out(T, D, tm)
    return pl.pallas_call(
        _outproj_odd_kernel,
        grid=(T // tm,),
        in_specs=dil(D) + dil(LANES) + [full(expand.shape), full(w_o.shape),
                                        pl.BlockSpec((tm, D), lambda i: (i, 0)), full((1, D)),
                                        full(w_router.shape)],
        out_specs=out_specs,
        out_shape=out_shapes,
        scratch_shapes=[pltpu.VMEM((D // LANES, tm, LANES), F32), pltpu.VMEM((1, tm, LANES), F32)],
        compiler_params=_params("parallel"),
        name="outproj_odd",
    )(*outs, *lses, expand, w_o.astype(BF16), x, ffn_g.reshape(1, D), w_router)


def _topk_kernel(aff_ref, upper_ref, strict_ref, pos_ref, tile_ref, *, cap):
    a = aff_ref[0]
    E, NC, _ = a.shape

    def count(mask):
        return jnp.sum(jnp.sum(mask.astype(F32), axis=2, keepdims=True), axis=1, keepdims=True)

    def as_float(bits):
        return lax.bitcast_convert_type(bits, jnp.float32)

    def search(i, thr):
        cand = thr | jnp.left_shift(jnp.int32(1), 30 - i)
        return jnp.where(count(a >= as_float(cand)) >= cap, cand, thr)

    thr = lax.fori_loop(0, 31, search, jnp.zeros((E, 1, 1), I32))
    gt = a >= as_float(thr + 1)
    eq = jnp.logical_and(a >= as_float(thr), jnp.logical_not(gt))
    need = cap - count(gt)

    def prefix(mask):
        m2 = mask.astype(BF16).reshape(E * NC, LANES)
        within = _dot(m2, upper_ref[...])
        total = within[:, LANES - 1:LANES].astype(BF16)
        tot_b = jnp.broadcast_to(total, (E * NC, LANES))
        starts = jnp.concatenate(
            [_dot(strict_ref[...], tot_b[e * NC:(e + 1) * NC]) for e in range(E)], axis=0)
        return (within + starts).reshape(E, NC, LANES), starts.reshape(E, NC, LANES)

    eq_incl, _ = prefix(eq)
    sel = gt | (eq & (eq_incl - 1.0 < need))
    sel_incl, starts = prefix(sel)
    pos_ref[0] = jnp.where(sel, sel_incl - 1.0, -1.0).astype(I32)
    tile = jnp.minimum(jnp.floor(starts * (1.0 / TOKEN_CHUNK)), float((cap - SLOT_WINDOW) // TOKEN_CHUNK))
    tile_ref[0] = tile.astype(I32)


def route_topk(aff_t, cap):
    B, E, S = aff_t.shape
    NC = S // TOKEN_CHUNK
    t = np.arange(LANES)
    upper = jnp.asarray(t[:, None] <= t[None, :], BF16)
    c = np.arange(NC)
    strict = jnp.asarray(c[None, :] < c[:, None], BF16)
    blk = pl.BlockSpec((1, E, NC, LANES), lambda b: (b, 0, 0, 0))
    pos, tiles = pl.pallas_call(
        functools.partial(_topk_kernel, cap=cap),
        grid=(B,),
        in_specs=[blk, pl.BlockSpec(upper.shape, lambda b: (0, 0)), pl.BlockSpec(strict.shape, lambda b: (0, 0))],
        out_specs=[blk, blk],
        out_shape=[jax.ShapeDtypeStruct((B, E, NC, LANES), I32)] * 2,
        compiler_params=_params("parallel"),
        name="route_topk",
    )(aff_t.reshape(B, E, NC, LANES), upper, strict)
    return pos, tiles[..., 0].reshape(-1)


def _index_kernel(tiles_ref, pos_ref, sel_ref, idx_ref, acc_ref):
    b, e = pl.program_id(0), pl.program_id(1)
    n_exp = pl.num_programs(1)
    NC = pos_ref.shape[2]
    acc_ref[...] = jnp.zeros_like(acc_ref)
    slot = lax.broadcasted_iota(I32, (SLOT_WINDOW, TOKEN_CHUNK), 0)

    def body(c, carry):
        j = tiles_ref[(b * n_exp + e) * NC + c]
        onehot = jnp.where(slot + j * TOKEN_CHUNK == pos_ref[0, 0, pl.ds(c, 1), :], 1.0, 0.0).astype(BF16)
        hit = _dot_nt(sel_ref[...], onehot)
        tok = hit[0:1] + hit[1:2] * lax.convert_element_type(c * TOKEN_CHUNK, F32)
        acc_ref[j, 0:1, :] += tok[:, 0:TOKEN_CHUNK]
        acc_ref[j + 1, 0:1, :] += tok[:, TOKEN_CHUNK:]
        return carry

    lax.fori_loop(0, NC, body, 0, unroll=32)
    idx_ref[0, 0] = acc_ref[...].astype(I32)


def moe_slot_tokens(tiles, pos, cap):
    B, E, NC, _ = pos.shape
    nt = cap // TOKEN_CHUNK
    sel = np.zeros((8, TOKEN_CHUNK), np.float32)
    sel[0] = np.arange(TOKEN_CHUNK)
    sel[1] = 1.0
    idx = pl.pallas_call(
        _index_kernel,
        grid_spec=pltpu.PrefetchScalarGridSpec(
            num_scalar_prefetch=1,
            grid=(B, E),
            in_specs=[pl.BlockSpec((1, 1, NC, LANES), lambda b, e, s: (b, e, 0, 0)),
                      pl.BlockSpec((8, TOKEN_CHUNK), lambda b, e, s: (0, 0))],
            out_specs=pl.BlockSpec((1, 1, nt, 8, LANES), lambda b, e, s: (b, e, 0, 0, 0)),
            scratch_shapes=[pltpu.VMEM((nt, 8, LANES), F32)]),
        out_shape=jax.ShapeDtypeStruct((B, E, nt, 8, LANES), I32),
        compiler_params=_params("arbitrary", "arbitrary"),
        name="moe_slot_tokens",
    )(tiles, pos, jnp.asarray(sel, BF16))
    return idx[:, :, :, 0, :].reshape(-1)


def _ffn_kernel(idx_ref, h_ref, wg_ref, wu_ref, wd_ref, y_ref, x_buf, sem, wgb_ref, wub_ref, wdb_ref, *, tr):
    e, b = pl.program_id(0), pl.program_id(1)
    n_exp, nb = pl.num_programs(0), pl.num_programs(1)
    cap = x_buf.shape[1]
    step = e * nb + b

    def row_copy(st, buf, i):
        eb = st // nb
        bb = st - eb * nb
        tok = idx_ref[(bb * n_exp + eb) * cap + i]
        return pltpu.make_async_copy(h_ref.at[bb, pl.ds(tok, 1), :], x_buf.at[buf, pl.ds(i, 1), :], sem.at[buf])

    cur = jnp.bitwise_and(step, 1)

    @pl.when(step == 0)
    def _():
        lax.fori_loop(0, cap, lambda i, c: (row_copy(step, 0, i).start(), c)[1], 0, unroll=8)

    @pl.when(b == 0)
    def _():
        wgb_ref[...] = wg_ref[0, 0].astype(BF16)
        wub_ref[...] = wu_ref[0, 0].astype(BF16)
        wdb_ref[...] = wd_ref[0, 0].astype(BF16)

    pltpu.make_async_copy(h_ref.at[0, pl.ds(0, cap), :], x_buf.at[cur], sem.at[cur]).wait()

    def compute(prefetch):
        for r in range(cap // tr):
            rows = pl.ds(r * tr, tr)
            x = x_buf[cur, rows, :].astype(BF16)
            if prefetch:
                for i in range(r * tr, (r + 1) * tr):
                    row_copy(step + 1, 1 - cur, i).start()
            g = _dot(x, wgb_ref[...])
            u = _dot(x, wub_ref[...])
            mid = (g * jax.nn.sigmoid(g) * u).astype(BF16)
            y_ref[0, 0, rows, :] = _dot(mid, wdb_ref[...]).astype(BF16)

    is_last = step + 1 == n_exp * nb
    pl.when(jnp.logical_not(is_last))(lambda: compute(True))
    pl.when(is_last)(lambda: compute(False))


def moe_ffn(idx, h, w_gate, w_up, w_down, layer, cap, tr=256):
    B, S, D = h.shape
    E, F = w_gate.shape[1], w_gate.shape[-1]
    wspec = lambda r, c: pl.BlockSpec((1, 1, r, c), lambda e, b, s: (layer, e, 0, 0))
    return pl.pallas_call(
        functools.partial(_ffn_kernel, tr=min(tr, cap)),
        grid_spec=pltpu.PrefetchScalarGridSpec(
            num_scalar_prefetch=1,
            grid=(E, B),
            in_specs=[pl.BlockSpec(memory_space=pl.ANY), wspec(D, F), wspec(D, F), wspec(F, D)],
            out_specs=pl.BlockSpec((1, 1, cap, D), lambda e, b, s: (b, e, 0, 0)),
            scratch_shapes=[pltpu.VMEM((2, cap, D), F32), pltpu.SemaphoreType.DMA((2,)),
                            pltpu.VMEM((D, F), BF16), pltpu.VMEM((D, F), BF16), pltpu.VMEM((F, D), BF16)]),
        out_shape=jax.ShapeDtypeStruct((B, E, cap, D), BF16),
        compiler_params=_params("arbitrary", "arbitrary"),
        name="moe_ffn",
    )(idx, h, w_gate, w_up, w_down)


def _combine_kernel(tiles_ref, post_ref, gate_ref, x_ref, y_ref, *rest):
    g_ref, o_ref = rest if len(rest) == 2 else (None, rest[0])
    b, i = pl.program_id(0), pl.program_id(1)
    n_exp = y_ref.shape[1]
    per_step = x_ref.shape[1] // TOKEN_CHUNK
    NC = pl.num_programs(1) * per_step
    slot = lax.broadcasted_iota(I32, (TOKEN_CHUNK, SLOT_WINDOW), 1)
    for k in range(per_step):
        rows = slice(k * TOKEN_CHUNK, (k + 1) * TOKEN_CHUNK)
        acc = x_ref[0, rows, :]
        for e in range(n_exp):
            tile = tiles_ref[(b * n_exp + e) * NC + i * per_step + k]
            base = pl.multiple_of(tile * TOKEN_CHUNK, TOKEN_CHUNK)
            onehot = jnp.where(slot + base == post_ref[0, rows, e:e + 1], 1.0, 0.0).astype(BF16)
            acc = acc + _dot(onehot, y_ref[0, e, pl.ds(base, SLOT_WINDOW), :]) * gate_ref[0, rows, e:e + 1]
        o_ref[0, rows, :] = acc if g_ref is None else _rms(acc, g_ref[...])


def moe_combine(tiles, pos_t, gate, x, y, out_norm=None, tm=512):
    B, S, D = x.shape
    E, cap = y.shape[1], y.shape[2]
    tok = lambda w: pl.BlockSpec((1, tm, w), lambda b, c, s: (b, c, 0))
    in_specs = [tok(E), tok(E), tok(D),
                pl.BlockSpec((1, E, cap, D), lambda b, c, s: (b, 0, 0, 0), pipeline_mode=pl.Buffered(1))]
    args = [tiles, pos_t, gate, x, y]
    if out_norm is not None:
        in_specs.append(pl.BlockSpec((1, D), lambda b, c, s: (0, 0)))
        args.append(out_norm.reshape(1, D))
    return pl.pallas_call(
        _combine_kernel,
        grid_spec=pltpu.PrefetchScalarGridSpec(
            num_scalar_prefetch=1,
            grid=(B, S // tm),
            in_specs=in_specs,
            out_specs=tok(D)),
        out_shape=jax.ShapeDtypeStruct((B, S, D), F32),
        compiler_params=_params("arbitrary", "arbitrary"),
        name="moe_combine",
    )(*args)


def expert_choice_moe(x, h, aff, batch, w_gate, w_up, w_down, layer, out_norm=None):
    T, D = x.shape
    S = T // batch
    cap = max(1, CAPACITY_FACTOR * S // N_EXPERTS)
    aff = aff.reshape(batch, S, N_EXPERTS)
    pos, tiles = route_topk(jnp.swapaxes(aff, 1, 2), cap)
    idx = moe_slot_tokens(tiles, pos, cap)
    y = moe_ffn(idx, h.reshape(batch, S, D), w_gate, w_up, w_down, layer, cap)
    pos_t = jnp.swapaxes(pos.reshape(batch, N_EXPERTS, S), 1, 2)
    return moe_combine(tiles, pos_t, aff, x.reshape(batch, S, D), y, out_norm).reshape(T, D)


ATTN_TQ = 128
ATTN_TK = ATTN_TQ + 2 * HALF_WINDOW
ATTN_SUB = 2


def _t5_bucket(rel):
    half_buckets = REL_BUCKETS // 2
    max_exact = half_buckets // 2
    n = jnp.abs(rel)
    scaled = (jnp.log(jnp.maximum(n, 1).astype(jnp.float32) / max_exact)
              / math.log(REL_MAX_DISTANCE / max_exact))
    large = jnp.minimum(max_exact + jnp.floor(scaled * (half_buckets - max_exact)).astype(jnp.int32),
                        half_buckets - 1)
    return jnp.where(rel > 0, half_buckets, 0) + jnp.where(n < max_exact, n, large)


def _bias_kernel(table_ref, bucket_ref, o_ref):
    h = pl.program_id(1)
    bucket = bucket_ref[0]
    acc = jnp.zeros(bucket.shape, F32)
    for bk in range(REL_BUCKETS):
        acc = jnp.where(bucket == bk, table_ref[bk * C_HEADS + h], acc)
    q = lax.broadcasted_iota(I32, bucket.shape, 0)
    kc = lax.broadcasted_iota(I32, bucket.shape, 1)
    middle = jnp.where(jnp.abs(kc - HALF_WINDOW - q) <= HALF_WINDOW, acc, NEG_INF)
    o_ref[0, 0, 0] = jnp.where(kc >= HALF_WINDOW, middle, NEG_INF)
    o_ref[0, 1, 0] = middle
    o_ref[0, 2, 0] = jnp.where(kc < HALF_WINDOW + ATTN_TQ, middle, NEG_INF)


def attention_bias(rel_bias):
    rel = np.arange(ATTN_TK)[None, :] - HALF_WINDOW - np.arange(ATTN_TQ)[:, None]
    buckets = jnp.stack([_t5_bucket(jnp.asarray(rel * d, I32)) for d in DILATIONS]).astype(I32)
    P = len(DILATIONS)
    return pl.pallas_call(
        _bias_kernel,
        grid_spec=pltpu.PrefetchScalarGridSpec(
            num_scalar_prefetch=1,
            grid=(P, C_HEADS),
            in_specs=[pl.BlockSpec((1, ATTN_TQ, ATTN_TK), lambda p, h, t: (p, 0, 0))],
            out_specs=pl.BlockSpec((1, 3, 1, ATTN_TQ, ATTN_TK), lambda p, h, t: (p, 0, h, 0, 0))),
        out_shape=jax.ShapeDtypeStruct((P, 3, C_HEADS, ATTN_TQ, ATTN_TK), F32),
        compiler_params=_params("arbitrary", "arbitrary"),
        name="attention_bias",
    )(rel_bias.reshape(-1), buckets)


def _attn_kernel(q_ref, kp_ref, km_ref, kn_ref, vp_ref, vm_ref, vn_ref, bias_ref, o_ref, lse_ref,
                 k_buf, v_buf, s_buf, p_buf):
    i = pl.program_id(2)
    last = pl.num_programs(2) - 1
    hw = HALF_WINDOW
    rows = q_ref.shape[1]
    k_buf[0:hw] = kp_ref[0]
    k_buf[hw:hw + rows] = km_ref[0]
    k_buf[hw + rows:] = kn_ref[0]
    v_buf[0:hw] = vp_ref[0]
    v_buf[hw:hw + rows] = vm_ref[0]
    v_buf[hw + rows:] = vn_ref[0]
    first_head = lax.broadcasted_iota(I32, (ATTN_TQ, LANES), 1) < C_HEAD_DIM
    scale = C_HEAD_DIM ** -0.5
    lse_ref[...] = jnp.zeros_like(lse_ref)
    for sub in range(rows // ATTN_TQ):
        qrows = slice(sub * ATTN_TQ, (sub + 1) * ATTN_TQ)
        krows = slice(sub * ATTN_TQ, sub * ATTN_TQ + ATTN_TK)
        if sub == 0:
            variant = jnp.where(i == 0, 0, 1)
        elif sub == rows // ATTN_TQ - 1:
            variant = jnp.where(i == last, 2, 1)
        else:
            variant = 1
        for pair in range(C_HEADS // 2):
            cols = slice(pair * LANES, (pair + 1) * LANES)
            q = q_ref[0, qrows, cols] * scale
            for half in range(2):
                mine = first_head if half == 0 else jnp.logical_not(first_head)
                s_buf[2 * pair + half] = (_dot_nt(jnp.where(mine, q, jnp.zeros_like(q)), k_buf[krows, cols])
                                          + bias_ref[variant, 2 * pair + half])
        for h in range(C_HEADS):
            s = s_buf[h]
            m = jnp.max(s, axis=-1, keepdims=True)
            p = jnp.exp(s - m)
            den = jnp.sum(p, axis=-1, keepdims=True)
            p_buf[h] = (p / den).astype(BF16)
            lse_ref[0, qrows, h:h + 1] = m + jnp.log(den)
        for pair in range(C_HEADS // 2):
            cols = slice(pair * LANES, (pair + 1) * LANES)
            o_ref[0, qrows, cols] = jnp.where(first_head, _dot(p_buf[2 * pair], v_buf[krows, cols]),
                                              _dot(p_buf[2 * pair + 1], v_buf[krows, cols])).astype(BF16)


def dilated_attention(view, bias, d):
    batch, n, D3 = view.shape
    D = D3 // d // 3
    rows = ATTN_TQ * ATTN_SUB
    nb = n // HALF_WINDOW
    r = rows // HALF_WINDOW
    assert n // rows >= 1 and ATTN_SUB >= 2
    main = lambda c: pl.BlockSpec((1, rows, D), lambda b, j, i: (b, i, 3 * j + c))
    prev = lambda c: pl.BlockSpec((1, HALF_WINDOW, D), lambda b, j, i: (b, jnp.maximum(i * r - 1, 0), 3 * j + c))
    nxt = lambda c: pl.BlockSpec((1, HALF_WINDOW, D),
                                 lambda b, j, i: (b, jnp.minimum(i * r + r, nb - 1), 3 * j + c))
    return pl.pallas_call(
        _attn_kernel,
        grid=(batch, d, n // rows),
        in_specs=[main(0), prev(1), main(1), nxt(1), prev(2), main(2), nxt(2),
                  pl.BlockSpec(bias.shape, lambda b, j, i: (0, 0, 0, 0))],
        out_specs=[pl.BlockSpec((1, rows, D), lambda b, j, i: (b, i, j)),
                   pl.BlockSpec((1, rows, LANES), lambda b, j, i: (b, i, j))],
        out_shape=[jax.ShapeDtypeStruct((batch, n, d * D), BF16),
                   jax.ShapeDtypeStruct((batch, n, d * LANES), F32)],
        scratch_shapes=[pltpu.VMEM((rows + 2 * HALF_WINDOW, D), BF16), pltpu.VMEM((rows + 2 * HALF_WINDOW, D), BF16),
                        pltpu.VMEM((C_HEADS, ATTN_TQ, ATTN_TK), F32), pltpu.VMEM((C_HEADS, ATTN_TQ, ATTN_TK), BF16)],
        compiler_params=_params("parallel", "parallel", "parallel"),
        name="dilated_attention",
    )(view, view, view, view, view, view, view, bias)


def dilated_mixture(views, rel_bias):
    bias = attention_bias(rel_bias)
    results = [dilated_attention(view, bias[p], d) for p, (view, d) in enumerate(zip(views, DILATIONS))]
    return [o for o, _ in results], [lse for _, lse in results]


def kernel(x, mix_norm, ffn_norm, final_norm, w_in_even, w_out_even, a_ln_g, a_ln_b, a_w_s, a_b_s, b_lb_table,
           b_norm_g, w_qkv_odd, w_o_odd, rel_bias, w_router, w_gate, w_up, w_down):
    B, S, D = x.shape
    depth = mix_norm.shape[0]
    xt = x.reshape(B * S, D)
    for layer in range(depth):
        j = layer // 2
        if layer % 2 == 0:
            proj = norm_matmul(xt, mix_norm[layer], w_in_even[j].astype(BF16), B)[0].reshape(B * S, -1)
            a_out = mixer_a(proj, a_ln_g[j], a_ln_b[j], a_w_s[j], a_b_s[j])
            o_f, o_b = mixer_b(proj, b_lb_table, layer, B)
            xt, h, aff = outproj_even(a_out, o_f, o_b, proj, b_norm_g[j], w_out_even[j], xt,
                                      ffn_norm[layer], w_router[layer])
        else:
            views = norm_matmul(xt, mix_norm[layer], w_qkv_odd[j].astype(BF16), B, DILATIONS, tm=512)
            outs, lses = dilated_mixture(views, rel_bias)
            xt, h, aff = outproj_odd(outs, lses, w_o_odd[j], xt, ffn_norm[layer], w_router[layer])
        xt = expert_choice_moe(xt, h, aff, B, w_gate, w_up, w_down, layer,
                               out_norm=final_norm if layer == depth - 1 else None)
    return xt.reshape(B, S, D)
```

```python
import functools
import math

import numpy as np
import jax
import jax.numpy as jnp
from jax import lax
from jax.experimental import pallas as pl
from jax.experimental.pallas import tpu as pltpu

F32 = jnp.float32
BF16 = jnp.bfloat16
I32 = jnp.int32
EPS = 1e-6
NEG_INF = -1e30
HIGHEST = lax.Precision.HIGHEST

LANES = 128
VMEM_LIMIT = 56 * 1024 * 1024

A_GROUPS = 4
A_CHUNK = 128
B_HEADS = 4
B_DIM = 128
GLA_CHUNK = 128
C_HEADS = 16
C_HEAD_DIM = 64
HALF_WINDOW = 64
DILATIONS = (1, 4, 16)
REL_BUCKETS = 32
REL_MAX_DISTANCE = 1024
N_EXPERTS = 16
CAPACITY_FACTOR = 2
TOKEN_CHUNK = 128
SLOT_WINDOW = 256


def _params(*sem):
    return pltpu.CompilerParams(dimension_semantics=sem, vmem_limit_bytes=VMEM_LIMIT)


def _dot(a, b, **kw):
    return jnp.dot(a, b, preferred_element_type=F32, **kw)


def _dot_nt(a, b):
    return lax.dot_general(a, b, (((1,), (1,)), ((), ())), preferred_element_type=F32)


def _dot_tn(a, b):
    return lax.dot_general(a, b, (((0,), (0,)), ((), ())), preferred_element_type=F32)


def _dot_split01(a2, x):
    hi = x.astype(BF16)
    lo = (x - hi.astype(F32)).astype(BF16)
    return _dot(a2, jnp.concatenate([hi, lo], axis=0))


def _rms(x, g):
    return x * lax.rsqrt(jnp.mean(x * x, axis=-1, keepdims=True) + EPS) * g


def _norm_matmul_kernel(x_ref, g_ref, w_ref, *rest, tn, dilations):
    o_refs, scratch = rest[:len(dilations)], rest[len(dilations):]
    tm, N = x_ref.shape[0], w_ref.shape[1]
    h = _rms(x_ref[...], g_ref[...]).astype(BF16)
    for j in range(N // tn):
        res = _dot(h, w_ref[:, j * tn:(j + 1) * tn])
        if scratch:
            for t in range(tn // LANES):
                scratch[0][t] = res[:, t * LANES:(t + 1) * LANES]
        for o_ref, d in zip(o_refs, dilations):
            if d == 1:
                o_ref[0, :, j * tn:(j + 1) * tn] = res.astype(BF16)
                continue
            for r in range(d):
                for t in range(tn // LANES):
                    lane0 = r * N + j * tn + t * LANES
                    o_ref[0, :, lane0:lane0 + LANES] = (
                        scratch[0][t, pl.ds(r, tm // d, stride=d), :].astype(BF16))


def norm_matmul(x, g, w, batch, dilations=(1,), tm=1024, tn=512):
    T, D = x.shape
    N = w.shape[1]
    S = T // batch
    tiles = S // tm
    strided = any(d > 1 for d in dilations)
    return pl.pallas_call(
        functools.partial(_norm_matmul_kernel, tn=tn, dilations=dilations),
        grid=(T // tm,),
        in_specs=[
            pl.BlockSpec((tm, D), lambda i: (i, 0)),
            pl.BlockSpec((1, D), lambda i: (0, 0)),
            pl.BlockSpec((D, N), lambda i: (0, 0)),
        ],
        out_specs=[pl.BlockSpec((1, tm // d, d * N), lambda i: (i // tiles, i % tiles, 0)) for d in dilations],
        out_shape=[jax.ShapeDtypeStruct((batch, S // d, d * N), BF16) for d in dilations],
        scratch_shapes=[pltpu.VMEM((tn // LANES, tm, LANES), F32)] if strided else [],
        compiler_params=_params("parallel"),
        name="norm_matmul",
    )(x, g.reshape(1, D), w)


def _mixa_kernel(u_ref, v_ref, lg_ref, lb_ref, ws_ref, bs_ref, o_ref):
    tm = u_ref.shape[0]
    u = jax.nn.gelu(u_ref[...].astype(F32))
    v = jax.nn.gelu(v_ref[...].astype(F32))
    mu = jnp.mean(v, axis=-1, keepdims=True)
    vc = v - mu
    vn = vc * lax.rsqrt(jnp.mean(vc * vc, axis=-1, keepdims=True) + EPS)
    vb = (vn * lg_ref[...] + lb_ref[...]).astype(BF16)
    for n in range(tm // A_CHUNK):
        rows = slice(n * A_CHUNK, (n + 1) * A_CHUNK)
        for g in range(A_GROUPS):
            cols = slice(g * LANES, (g + 1) * LANES)
            mixed = _dot(ws_ref[g], vb[rows, cols]) + bs_ref[:, g:g + 1]
            o_ref[rows, cols] = (u[rows, cols] * mixed).astype(BF16)


def mixer_a(proj, ln_g, ln_b, w_s, b_s, tm=512):
    T = proj.shape[0]
    W = A_GROUPS * LANES
    return pl.pallas_call(
        _mixa_kernel,
        grid=(T // tm,),
        in_specs=[
            pl.BlockSpec((tm, W), lambda i: (i, 0)),
            pl.BlockSpec((tm, W), lambda i: (i, 1)),
            pl.BlockSpec((1, W), lambda i: (0, 0)),
            pl.BlockSpec((1, W), lambda i: (0, 0)),
            pl.BlockSpec((A_GROUPS, A_CHUNK, A_CHUNK), lambda i: (0, 0, 0)),
            pl.BlockSpec((A_CHUNK, A_GROUPS), lambda i: (0, 0)),
        ],
        out_specs=pl.BlockSpec((tm, W), lambda i: (i, 0)),
        out_shape=jax.ShapeDtypeStruct((T, W), BF16),
        compiler_params=_params("parallel"),
        name="mixer_a",
    )(proj, proj, ln_g.reshape(1, W), ln_b.reshape(1, W), w_s.astype(BF16), b_s.T)


def _gla_consts(C, reverse):
    t = np.arange(C)[:, None]
    r = np.arange(C)[None, :]
    L = int(round(math.log2(C)))
    spans = [(r >= t) if reverse else (r <= t)]
    level = np.where(np.eye(C, dtype=bool), L, -1).astype(np.int32)
    for l in range(L):
        bs = (t >> (l + 1)) << (l + 1)
        mid = bs + (1 << l)
        if reverse:
            act_q = t < mid
            span = np.where(act_q, (r >= t) & (r < mid), (r >= mid) & (r < t))
        else:
            act_q = t >= mid
            span = np.where(act_q, (r >= mid) & (r <= t), (r > t) & (r < mid))
        if l > 0:
            spans.append(span)
        level[(bs == bs.T) & act_q & ~act_q.T] = l
    return np.tile(np.concatenate(spans, axis=0).astype(np.float32), (1, 2)), level


def _gla_kernel(qf_ref, ff_ref, if_ref, qb_ref, fb_ref, ib_ref, tbl_ref, span_ref, level_ref,
                of_ref, ob_ref, state_ref, w_buf, q_buf, k_buf, v_buf, s_buf, p_buf, *, layer, C):
    @pl.when(pl.program_id(1) == 0)
    def _():
        state_ref[...] = jnp.zeros_like(state_ref)

    L = span_ref.shape[1] // C
    subs = qf_ref.shape[0] // C
    odd_row = jnp.bitwise_and(lax.broadcasted_iota(I32, (C, qf_ref.shape[1]), 0), 1) == 1

    def chunk_rows(sub, d):
        k = sub if d == 0 else subs - 1 - sub
        return slice(k * C, (k + 1) * C)

    groups = [(sub, d) for sub in range(subs) for d in range(2)]
    heads = [(sub * 2 + d, sub, d, h, slice(h * B_DIM, (h + 1) * B_DIM)) for sub, d in groups for h in range(B_HEADS)]
    tots = {}
    for sub, d in groups:
        g = sub * 2 + d
        q_ref, f_ref, i_ref = (qf_ref, ff_ref, if_ref) if d == 0 else (qb_ref, fb_ref, ib_ref)
        rows = chunk_rows(sub, d)
        tb = tbl_ref[d]
        e = jnp.exp(tb - jnp.max(tb, axis=0, keepdims=True))
        lb = jnp.sum(e[0:layer + 1], axis=0, keepdims=True) / jnp.sum(e, axis=0, keepdims=True)
        f = lb + (1.0 - lb) * jax.nn.sigmoid(f_ref[rows, :].astype(F32))
        e2 = _dot_split01(span_ref[d], jnp.log2(f))
        tot2 = e2[0:1] if d == 1 else e2[C - 1:C]
        tots[g] = jnp.exp2(tot2)
        w_buf[g, 0:C] = jnp.exp2(e2[0:C]).astype(BF16)
        w_buf[g, C:2 * C] = jnp.exp2(tot2 - e2[0:C]).astype(BF16)
        w_buf[g, 2 * C:3 * C] = jnp.where(odd_row if d == 0 else jnp.logical_not(odd_row), f, 1.0).astype(BF16)
        w_buf[g, 3 * C:] = jnp.exp2(e2[C:]).astype(BF16)
        q_buf[g] = jax.nn.silu(q_ref[rows, :].astype(F32)).astype(BF16)
        k_buf[g] = (1.0 - f).astype(BF16)
        v_buf[g] = i_ref[rows, :]
    for g, sub, d, h, sl in heads:
        qh, kh = q_buf[g, :, sl], k_buf[g, :, sl]
        s_buf[g, h, L] = _dot_nt(qh, kh)
        for l in range(L):
            wl = w_buf[g, (l + 2) * C:(l + 3) * C, sl]
            s_buf[g, h, l] = _dot_nt(qh * wl, kh * wl)
    for g, sub, d, h, sl in heads:
        level = level_ref[d]
        p = jnp.where(level == L, s_buf[g, h, L], 0.0)
        for l in range(L):
            p = jnp.where(level == l, s_buf[g, h, l], p)
        p_buf[g, h] = p.astype(BF16)
    for g, sub, d, h, sl in heads:
        o_ref = of_ref if d == 0 else ob_ref
        st = state_ref[d, h]
        vh = v_buf[g, :, sl]
        o_ref[chunk_rows(sub, d), sl] = (_dot(p_buf[g, h], vh)
                                         + _dot_nt(q_buf[g, :, sl] * w_buf[g, 0:C, sl], st.astype(BF16)))
        state_ref[d, h] = st * tots[g][:, sl] + _dot_tn(vh, k_buf[g, :, sl] * w_buf[g, C:2 * C, sl])


def mixer_b(proj, b_lb_table, layer, batch, C=GLA_CHUNK, subs=2):
    T = proj.shape[0]
    W = B_HEADS * B_DIM
    R = C * subs
    n = T // batch // R
    consts = [_gla_consts(C, rev) for rev in (False, True)]
    span = jnp.asarray(np.stack([c[0] for c in consts]), BF16)
    level = jnp.asarray(np.stack([c[1] for c in consts]))
    fwd = lambda col: pl.BlockSpec((R, W), lambda b, c: (b * n + c, col))
    bwd = lambda col: pl.BlockSpec((R, W), lambda b, c: (b * n + n - 1 - c, col))
    full = lambda a: pl.BlockSpec(a.shape, lambda b, c: (0,) * a.ndim)
    G = 2 * subs
    return pl.pallas_call(
        functools.partial(_gla_kernel, layer=layer, C=C),
        grid=(batch, n),
        in_specs=[fwd(2), fwd(3), fwd(5), bwd(2), bwd(4), bwd(5),
                  full(b_lb_table), full(span), full(level)],
        out_specs=[pl.BlockSpec((R, W), lambda b, c: (b * n + c, 0)),
                   pl.BlockSpec((R, W), lambda b, c: (b * n + n - 1 - c, 0))],
        out_shape=[jax.ShapeDtypeStruct((T, W), F32)] * 2,
        scratch_shapes=[pltpu.VMEM((2, B_HEADS, B_DIM, B_DIM), F32),
                        pltpu.VMEM((G, span.shape[1] + 2 * C, W), BF16),
                        pltpu.VMEM((G, C, W), BF16), pltpu.VMEM((G, C, W), BF16), pltpu.VMEM((G, C, W), BF16),
                        pltpu.VMEM((G, B_HEADS, span.shape[1] // C + 1, C, C), F32),
                        pltpu.VMEM((G, B_HEADS, C, C), BF16)],
        compiler_params=_params("arbitrary", "arbitrary"),
        name="mixer_b",
    )(proj, proj, proj, proj, proj, proj, b_lb_table, span, level)


def _router_epilogue(x_new, g_ref, wr_ref, x_ref, h_ref, aff_ref):
    x_ref[...] = x_new
    h = _rms(x_new, g_ref[...])
    h_ref[...] = h
    w = wr_ref[...]
    n_exp = w.shape[0]
    h_hi, w_hi = h.astype(BF16), w.astype(BF16)
    h_lo, w_lo = (h - h_hi.astype(F32)).astype(BF16), (w - w_hi.astype(F32)).astype(BF16)
    both = _dot_nt(jnp.concatenate([w_hi, w_lo], axis=0), h_hi)
    logits = both[0:n_exp] + both[n_exp:] + _dot_nt(w_hi, h_lo)
    e = jnp.exp(logits - jnp.max(logits, axis=0, keepdims=True))
    aff_ref[...] = e / jnp.sum(e, axis=0, keepdims=True)


def _outproj_even_kernel(a_ref, of_ref, ob_ref, gate_ref, bng_ref, w_ref, x_ref, g_ref, wr_ref,
                         xo_ref, h_ref, aff_ref):
    o = of_ref[...] + ob_ref[...]
    parts = []
    for h in range(B_HEADS):
        oh = o[:, h * B_DIM:(h + 1) * B_DIM]
        parts.append(oh * lax.rsqrt(jnp.mean(oh * oh, axis=-1, keepdims=True) + EPS))
    on = jnp.concatenate(parts, axis=-1) * bng_ref[...] * jax.nn.sigmoid(gate_ref[...].astype(F32))
    wa = a_ref.shape[1]
    mixed = _dot(a_ref[...], w_ref[0:wa, :]) + _dot(on.astype(BF16), w_ref[wa:, :])
    _router_epilogue(x_ref[...] + mixed, g_ref, wr_ref, xo_ref, h_ref, aff_ref)


def _router_out(T, D, tm):
    specs = [pl.BlockSpec((tm, D), lambda i: (i, 0)), pl.BlockSpec((tm, D), lambda i: (i, 0)),
             pl.BlockSpec((N_EXPERTS, tm), lambda i: (0, i))]
    shapes = [jax.ShapeDtypeStruct((T, D), F32), jax.ShapeDtypeStruct((T, D), F32),
              jax.ShapeDtypeStruct((N_EXPERTS, T), F32)]
    return specs, shapes


def outproj_even(a_out, o_f, o_b, proj, b_norm_g, w_out, x, ffn_g, w_router, tm=512):
    T, D = x.shape
    W = a_out.shape[1]
    row = lambda w, col=0: pl.BlockSpec((tm, w), lambda i: (i, col))
    full = lambda s: pl.BlockSpec(s, lambda i: (0, 0))
    out_specs, out_shapes = _router_out(T, D, tm)
    return pl.pallas_call(
        _outproj_even_kernel,
        grid=(T // tm,),
        in_specs=[row(W), row(W), row(W), row(W, 6), full((1, W)), full(w_out.shape), row(D),
                  full((1, D)), full(w_router.T.shape)],
        out_specs=out_specs,
        out_shape=out_shapes,
        compiler_params=_params("parallel"),
        name="outproj_even",
    )(a_out, o_f, o_b, proj, b_norm_g.reshape(1, W), w_out.astype(BF16), x, ffn_g.reshape(1, D), w_router.T)


def _outproj_odd_kernel(*refs):
    P = len(DILATIONS)
    o_refs, l_refs = refs[:P], refs[P:2 * P]
    ex_ref, w_ref, x_ref, g_ref, wr_ref, xo_ref, h_ref, aff_ref, o_buf, l_buf = refs[2 * P:]
    tm = x_ref.shape[0]

    def token_order(ref, buf, d):
        if d == 1:
            return ref[0].astype(F32)
        tiles = buf.shape[0]
        for r in range(d):
            for t in range(tiles):
                lane0 = (r * tiles + t) * LANES
                buf[t, pl.ds(r, tm // d, stride=d), :] = ref[0, :, lane0:lane0 + LANES].astype(F32)
        return jnp.concatenate([buf[t] for t in range(tiles)], axis=-1)

    ls = [token_order(l_ref, l_buf, d)[:, 0:C_HEADS] for l_ref, d in zip(l_refs, DILATIONS)]
    m = functools.reduce(jnp.maximum, ls)
    es = [jnp.exp(l - m) for l in ls]
    den = functools.reduce(jnp.add, es)
    attn = None
    for e, o_ref, d in zip(es, o_refs, DILATIONS):
        wfull = _dot((e / den).astype(BF16), ex_ref[...])
        term = wfull * token_order(o_ref, o_buf, d)
        attn = term if attn is None else attn + term
    mixed = _dot(attn.astype(BF16), w_ref[...])
    _router_epilogue(x_ref[...] + mixed, g_ref, wr_ref, xo_ref, h_ref, aff_ref)


def outproj_odd(outs, lses, w_o, x, ffn_g, w_router, tm=512):
    T, D = x.shape
    tiles = outs[0].shape[1] // tm
    expand = jnp.asarray(np.kron(np.eye(C_HEADS), np.ones((1, C_HEAD_DIM))), BF16)
    dil = lambda w: [pl.BlockSpec((1, tm // d, d * w), lambda i: (i // tiles, i % tiles, 0)) for d in DILATIONS]
    full = lambda s: pl.BlockSpec(s, lambda i: (0, 0))
    out_specs, out_shapes = _router_out(T, D, tm)
    return pl.pallas_call(
        _outproj_odd_kernel,
        grid=(T // tm,),
        in_specs=dil(D) + dil(LANES) + [full(expand.shape), full(w_o.shape),
                                        pl.BlockSpec((tm, D), lambda i: (i, 0)), full((1, D)),
                                        full(w_router.T.shape)],
        out_specs=out_specs,
        out_shape=out_shapes,
        scratch_shapes=[pltpu.VMEM((D // LANES, tm, LANES), F32), pltpu.VMEM((1, tm, LANES), F32)],
        compiler_params=_params("parallel"),
        name="outproj_odd",
    )(*outs, *lses, expand, w_o.astype(BF16), x, ffn_g.reshape(1, D), w_router.T)


def _topk_kernel(aff_ref, upper_ref, strict_ref, pos_ref, tile_ref, *, cap):
    a = aff_ref[...]
    E, NC, _ = a.shape

    def count(mask):
        return jnp.sum(jnp.sum(mask.astype(F32), axis=2, keepdims=True), axis=1, keepdims=True)

    def as_float(bits):
        return lax.bitcast_convert_type(bits, jnp.float32)

    def search(i, thr):
        cand = thr | jnp.left_shift(jnp.int32(1), 30 - i)
        return jnp.where(count(a >= as_float(cand)) >= cap, cand, thr)

    thr = lax.fori_loop(0, 31, search, jnp.zeros((E, 1, 1), I32))
    gt = a >= as_float(thr + 1)
    eq = jnp.logical_and(a >= as_float(thr), jnp.logical_not(gt))
    need = cap - count(gt)

    def prefix(mask):
        m2 = mask.astype(BF16).reshape(E * NC, LANES)
        within = _dot(m2, upper_ref[...])
        total = within[:, LANES - 1:LANES].astype(BF16)
        tot_b = jnp.broadcast_to(total, (E * NC, LANES))
        starts = jnp.concatenate(
            [_dot(strict_ref[...], tot_b[e * NC:(e + 1) * NC]) for e in range(E)], axis=0)
        return (within + starts).reshape(E, NC, LANES), starts.reshape(E, NC, LANES)

    eq_incl, _ = prefix(eq)
    sel = gt | (eq & (eq_incl - 1.0 < need))
    sel_incl, starts = prefix(sel)
    pos_ref[0] = jnp.where(sel, sel_incl - 1.0, -1.0).astype(I32)
    tile = jnp.minimum(jnp.floor(starts * (1.0 / TOKEN_CHUNK)), float((cap - SLOT_WINDOW) // TOKEN_CHUNK))
    tile_ref[0] = tile.astype(I32)


def route_topk(aff_t, batch, cap):
    E, T = aff_t.shape
    B, S = batch, T // batch
    NC = S // TOKEN_CHUNK
    t = np.arange(LANES)
    upper = jnp.asarray(t[:, None] <= t[None, :], BF16)
    c = np.arange(NC)
    strict = jnp.asarray(c[None, :] < c[:, None], BF16)
    blk = pl.BlockSpec((1, E, NC, LANES), lambda b: (b, 0, 0, 0))
    pos, tiles = pl.pallas_call(
        functools.partial(_topk_kernel, cap=cap),
        grid=(B,),
        in_specs=[pl.BlockSpec((E, NC, LANES), lambda b: (0, b, 0)),
                  pl.BlockSpec(upper.shape, lambda b: (0, 0)), pl.BlockSpec(strict.shape, lambda b: (0, 0))],
        out_specs=[blk, blk],
        out_shape=[jax.ShapeDtypeStruct((B, E, NC, LANES), I32)] * 2,
        compiler_params=_params("parallel"),
        name="route_topk",
    )(aff_t.reshape(E, B * NC, LANES), upper, strict)
    return pos, tiles[..., 0].reshape(-1)


def _index_kernel(tiles_ref, pos_ref, sel_ref, idx_ref, acc_ref):
    b, e = pl.program_id(0), pl.program_id(1)
    n_exp = pl.num_programs(1)
    NC = pos_ref.shape[2]
    acc_ref[...] = jnp.zeros_like(acc_ref)
    slot = lax.broadcasted_iota(I32, (SLOT_WINDOW, TOKEN_CHUNK), 0)

    def body(c, carry):
        j = tiles_ref[(b * n_exp + e) * NC + c]
        onehot = jnp.where(slot + j * TOKEN_CHUNK == pos_ref[0, 0, pl.ds(c, 1), :], 1.0, 0.0).astype(BF16)
        hit = _dot_nt(sel_ref[...], onehot)
        tok = hit[0:1] + hit[1:2] * lax.convert_element_type(c * TOKEN_CHUNK, F32)
        acc_ref[j, 0:1, :] += tok[:, 0:TOKEN_CHUNK]
        acc_ref[j + 1, 0:1, :] += tok[:, TOKEN_CHUNK:]
        return carry

    lax.fori_loop(0, NC, body, 0, unroll=32)
    idx_ref[0, 0] = acc_ref[...].astype(I32)


def moe_slot_tokens(tiles, pos, cap):
    B, E, NC, _ = pos.shape
    nt = cap // TOKEN_CHUNK
    sel = np.zeros((8, TOKEN_CHUNK), np.float32)
    sel[0] = np.arange(TOKEN_CHUNK)
    sel[1] = 1.0
    idx = pl.pallas_call(
        _index_kernel,
        grid_spec=pltpu.PrefetchScalarGridSpec(
            num_scalar_prefetch=1,
            grid=(B, E),
            in_specs=[pl.BlockSpec((1, 1, NC, LANES), lambda b, e, s: (b, e, 0, 0)),
                      pl.BlockSpec((8, TOKEN_CHUNK), lambda b, e, s: (0, 0))],
            out_specs=pl.BlockSpec((1, 1, nt, 8, LANES), lambda b, e, s: (b, e, 0, 0, 0)),
            scratch_shapes=[pltpu.VMEM((nt, 8, LANES), F32)]),
        out_shape=jax.ShapeDtypeStruct((B, E, nt, 8, LANES), I32),
        compiler_params=_params("arbitrary", "arbitrary"),
        name="moe_slot_tokens",
    )(tiles, pos, jnp.asarray(sel, BF16))
    return idx[:, :, :, 0, :].reshape(-1)


def _ffn_kernel(idx_ref, h_ref, wg_ref, wu_ref, wd_ref, y_ref, x_buf, sem, wgb_ref, wub_ref, wdb_ref, *, tr):
    e, b = pl.program_id(0), pl.program_id(1)
    n_exp, nb = pl.num_programs(0), pl.num_programs(1)
    cap = x_buf.shape[1]
    step = e * nb + b

    def row_copy(st, buf, i):
        eb = st // nb
        bb = st - eb * nb
        tok = idx_ref[(bb * n_exp + eb) * cap + i]
        return pltpu.make_async_copy(h_ref.at[bb, pl.ds(tok, 1), :], x_buf.at[buf, pl.ds(i, 1), :], sem.at[buf])

    cur = jnp.bitwise_and(step, 1)

    @pl.when(step == 0)
    def _():
        lax.fori_loop(0, cap, lambda i, c: (row_copy(step, 0, i).start(), c)[1], 0, unroll=8)

    @pl.when(b == 0)
    def _():
        wgb_ref[...] = wg_ref[0, 0].astype(BF16)
        wub_ref[...] = wu_ref[0, 0].astype(BF16)
        wdb_ref[...] = wd_ref[0, 0].astype(BF16)

    pltpu.make_async_copy(h_ref.at[0, pl.ds(0, cap), :], x_buf.at[cur], sem.at[cur]).wait()

    def compute(prefetch):
        for r in range(cap // tr):
            rows = pl.ds(r * tr, tr)
            x = x_buf[cur, rows, :].astype(BF16)
            if prefetch:
                for i in range(r * tr, (r + 1) * tr):
                    row_copy(step + 1, 1 - cur, i).start()
            g = _dot(x, wgb_ref[...])
            u = _dot(x, wub_ref[...])
            mid = (g * jax.nn.sigmoid(g) * u).astype(BF16)
            y_ref[0, 0, rows, :] = _dot(mid, wdb_ref[...]).astype(BF16)

    is_last = step + 1 == n_exp * nb
    pl.when(jnp.logical_not(is_last))(lambda: compute(True))
    pl.when(is_last)(lambda: compute(False))


def moe_ffn(idx, h, w_gate, w_up, w_down, layer, cap, tr=256):
    B, S, D = h.shape
    E, F = w_gate.shape[1], w_gate.shape[-1]
    wspec = lambda r, c: pl.BlockSpec((1, 1, r, c), lambda e, b, s: (layer, e, 0, 0))
    return pl.pallas_call(
        functools.partial(_ffn_kernel, tr=min(tr, cap)),
        grid_spec=pltpu.PrefetchScalarGridSpec(
            num_scalar_prefetch=1,
            grid=(E, B),
            in_specs=[pl.BlockSpec(memory_space=pl.ANY), wspec(D, F), wspec(D, F), wspec(F, D)],
            out_specs=pl.BlockSpec((1, 1, cap, D), lambda e, b, s: (b, e, 0, 0)),
            scratch_shapes=[pltpu.VMEM((2, cap, D), F32), pltpu.SemaphoreType.DMA((2,)),
                            pltpu.VMEM((D, F), BF16), pltpu.VMEM((D, F), BF16), pltpu.VMEM((F, D), BF16)]),
        out_shape=jax.ShapeDtypeStruct((B, E, cap, D), BF16),
        compiler_params=_params("arbitrary", "arbitrary"),
        name="moe_ffn",
    )(idx, h, w_gate, w_up, w_down)


def _combine_kernel(tiles_ref, post_ref, gate_ref, x_ref, y_ref, *rest):
    g_ref, o_ref = rest if len(rest) == 2 else (None, rest[0])
    b, i = pl.program_id(0), pl.program_id(1)
    n_exp = y_ref.shape[1]
    per_step = x_ref.shape[1] // TOKEN_CHUNK
    NC = pl.num_programs(1) * per_step
    slot = lax.broadcasted_iota(I32, (TOKEN_CHUNK, SLOT_WINDOW), 1)
    for k in range(per_step):
        rows = slice(k * TOKEN_CHUNK, (k + 1) * TOKEN_CHUNK)
        acc = x_ref[0, rows, :]
        for e in range(n_exp):
            tile = tiles_ref[(b * n_exp + e) * NC + i * per_step + k]
            base = pl.multiple_of(tile * TOKEN_CHUNK, TOKEN_CHUNK)
            onehot = jnp.where(slot + base == post_ref[0, rows, e:e + 1], 1.0, 0.0).astype(BF16)
            acc = acc + _dot(onehot, y_ref[0, e, pl.ds(base, SLOT_WINDOW), :]) * gate_ref[0, rows, e:e + 1]
        o_ref[0, rows, :] = acc if g_ref is None else _rms(acc, g_ref[...])


def moe_combine(tiles, pos_t, gate, x, y, out_norm=None, tm=512):
    B, S, D = x.shape
    E, cap = y.shape[1], y.shape[2]
    tok = lambda w: pl.BlockSpec((1, tm, w), lambda b, c, s: (b, c, 0))
    in_specs = [tok(E), tok(E), tok(D),
                pl.BlockSpec((1, E, cap, D), lambda b, c, s: (b, 0, 0, 0), pipeline_mode=pl.Buffered(1))]
    args = [tiles, pos_t, gate, x, y]
    if out_norm is not None:
        in_specs.append(pl.BlockSpec((1, D), lambda b, c, s: (0, 0)))
        args.append(out_norm.reshape(1, D))
    return pl.pallas_call(
        _combine_kernel,
        grid_spec=pltpu.PrefetchScalarGridSpec(
            num_scalar_prefetch=1,
            grid=(B, S // tm),
            in_specs=in_specs,
            out_specs=tok(D)),
        out_shape=jax.ShapeDtypeStruct((B, S, D), F32),
        compiler_params=_params("arbitrary", "arbitrary"),
        name="moe_combine",
    )(*args)


def expert_choice_moe(x, h, aff_t, batch, w_gate, w_up, w_down, layer, out_norm=None):
    T, D = x.shape
    S = T // batch
    cap = max(1, CAPACITY_FACTOR * S // N_EXPERTS)
    aff = aff_t.T.reshape(batch, S, N_EXPERTS)
    pos, tiles = route_topk(aff_t, batch, cap)
    idx = moe_slot_tokens(tiles, pos, cap)
    y = moe_ffn(idx, h.reshape(batch, S, D), w_gate, w_up, w_down, layer, cap)
    pos_t = jnp.swapaxes(pos.reshape(batch, N_EXPERTS, S), 1, 2)
    return moe_combine(tiles, pos_t, aff, x.reshape(batch, S, D), y, out_norm).reshape(T, D)


ATTN_TQ = 128
ATTN_TK = ATTN_TQ + 2 * HALF_WINDOW
ATTN_SUB = 2


def _t5_bucket(rel):
    half_buckets = REL_BUCKETS // 2
    max_exact = half_buckets // 2
    n = jnp.abs(rel)
    scaled = (jnp.log(jnp.maximum(n, 1).astype(jnp.float32) / max_exact)
              / math.log(REL_MAX_DISTANCE / max_exact))
    large = jnp.minimum(max_exact + jnp.floor(scaled * (half_buckets - max_exact)).astype(jnp.int32),
                        half_buckets - 1)
    return jnp.where(rel > 0, half_buckets, 0) + jnp.where(n < max_exact, n, large)


def _bias_kernel(table_ref, bucket_ref, o_ref):
    h = pl.program_id(1)
    bucket = bucket_ref[0]
    acc = jnp.zeros(bucket.shape, F32)
    for bk in range(REL_BUCKETS):
        acc = jnp.where(bucket == bk, table_ref[bk * C_HEADS + h], acc)
    q = lax.broadcasted_iota(I32, bucket.shape, 0)
    kc = lax.broadcasted_iota(I32, bucket.shape, 1)
    middle = jnp.where(jnp.abs(kc - HALF_WINDOW - q) <= HALF_WINDOW, acc, NEG_INF)
    o_ref[0, 0, 0] = jnp.where(kc >= HALF_WINDOW, middle, NEG_INF)
    o_ref[0, 1, 0] = middle
    o_ref[0, 2, 0] = jnp.where(kc < HALF_WINDOW + ATTN_TQ, middle, NEG_INF)


def attention_bias(rel_bias):
    rel = np.arange(ATTN_TK)[None, :] - HALF_WINDOW - np.arange(ATTN_TQ)[:, None]
    buckets = jnp.stack([_t5_bucket(jnp.asarray(rel * d, I32)) for d in DILATIONS]).astype(I32)
    P = len(DILATIONS)
    return pl.pallas_call(
        _bias_kernel,
        grid_spec=pltpu.PrefetchScalarGridSpec(
            num_scalar_prefetch=1,
            grid=(P, C_HEADS),
            in_specs=[pl.BlockSpec((1, ATTN_TQ, ATTN_TK), lambda p, h, t: (p, 0, 0))],
            out_specs=pl.BlockSpec((1, 3, 1, ATTN_TQ, ATTN_TK), lambda p, h, t: (p, 0, h, 0, 0))),
        out_shape=jax.ShapeDtypeStruct((P, 3, C_HEADS, ATTN_TQ, ATTN_TK), F32),
        compiler_params=_params("arbitrary", "arbitrary"),
        name="attention_bias",
    )(rel_bias.reshape(-1), buckets)


def _attn_kernel(q_ref, kp_ref, km_ref, kn_ref, vp_ref, vm_ref, vn_ref, bias_ref, o_ref, lse_ref,
                 k_buf, v_buf, s_buf, p_buf):
    i = pl.program_id(2)
    last = pl.num_programs(2) - 1
    hw = HALF_WINDOW
    rows = q_ref.shape[1]
    k_buf[0:hw] = kp_ref[0]
    k_buf[hw:hw + rows] = km_ref[0]
    k_buf[hw + rows:] = kn_ref[0]
    v_buf[0:hw] = vp_ref[0]
    v_buf[hw:hw + rows] = vm_ref[0]
    v_buf[hw + rows:] = vn_ref[0]
    first_head = lax.broadcasted_iota(I32, (ATTN_TQ, LANES), 1) < C_HEAD_DIM
    scale = C_HEAD_DIM ** -0.5
    lse_ref[...] = jnp.zeros_like(lse_ref)
    for sub in range(rows // ATTN_TQ):
        qrows = slice(sub * ATTN_TQ, (sub + 1) * ATTN_TQ)
        krows = slice(sub * ATTN_TQ, sub * ATTN_TQ + ATTN_TK)
        if sub == 0:
            variant = jnp.where(i == 0, 0, 1)
        elif sub == rows // ATTN_TQ - 1:
            variant = jnp.where(i == last, 2, 1)
        else:
            variant = 1
        for pair in range(C_HEADS // 2):
            cols = slice(pair * LANES, (pair + 1) * LANES)
            q = q_ref[0, qrows, cols] * scale
            for half in range(2):
                mine = first_head if half == 0 else jnp.logical_not(first_head)
                s_buf[2 * pair + half] = (_dot_nt(jnp.where(mine, q, jnp.zeros_like(q)), k_buf[krows, cols])
                                          + bias_ref[variant, 2 * pair + half])
        for h in range(C_HEADS):
            s = s_buf[h]
            m = jnp.max(s, axis=-1, keepdims=True)
            p = jnp.exp(s - m)
            den = jnp.sum(p, axis=-1, keepdims=True)
            p_buf[h] = (p / den).astype(BF16)
            lse_ref[0, qrows, h:h + 1] = m + jnp.log(den)
        for pair in range(C_HEADS // 2):
            cols = slice(pair * LANES, (pair + 1) * LANES)
            o_ref[0, qrows, cols] = jnp.where(first_head, _dot(p_buf[2 * pair], v_buf[krows, cols]),
                                              _dot(p_buf[2 * pair + 1], v_buf[krows, cols])).astype(BF16)


def dilated_attention(view, bias, d):
    batch, n, D3 = view.shape
    D = D3 // d // 3
    rows = ATTN_TQ * ATTN_SUB
    nb = n // HALF_WINDOW
    r = rows // HALF_WINDOW
    assert n // rows >= 1 and ATTN_SUB >= 2
    main = lambda c: pl.BlockSpec((1, rows, D), lambda b, j, i: (b, i, 3 * j + c))
    prev = lambda c: pl.BlockSpec((1, HALF_WINDOW, D), lambda b, j, i: (b, jnp.maximum(i * r - 1, 0), 3 * j + c))
    nxt = lambda c: pl.BlockSpec((1, HALF_WINDOW, D),
                                 lambda b, j, i: (b, jnp.minimum(i * r + r, nb - 1), 3 * j + c))
    return pl.pallas_call(
        _attn_kernel,
        grid=(batch, d, n // rows),
        in_specs=[main(0), prev(1), main(1), nxt(1), prev(2), main(2), nxt(2),
                  pl.BlockSpec(bias.shape, lambda b, j, i: (0, 0, 0, 0))],
        out_specs=[pl.BlockSpec((1, rows, D), lambda b, j, i: (b, i, j)),
                   pl.BlockSpec((1, rows, LANES), lambda b, j, i: (b, i, j))],
        out_shape=[jax.ShapeDtypeStruct((batch, n, d * D), BF16),
                   jax.ShapeDtypeStruct((batch, n, d * LANES), F32)],
        scratch_shapes=[pltpu.VMEM((rows + 2 * HALF_WINDOW, D), BF16), pltpu.VMEM((rows + 2 * HALF_WINDOW, D), BF16),
                        pltpu.VMEM((C_HEADS, ATTN_TQ, ATTN_TK), F32), pltpu.VMEM((C_HEADS, ATTN_TQ, ATTN_TK), BF16)],
        compiler_params=_params("parallel", "parallel", "parallel"),
        name="dilated_attention",
    )(view, view, view, view, view, view, view, bias)


def dilated_mixture(views, rel_bias):
    bias = attention_bias(rel_bias)
    results = [dilated_attention(view, bias[p], d) for p, (view, d) in enumerate(zip(views, DILATIONS))]
    return [o for o, _ in results], [lse for _, lse in results]


def kernel(x, mix_norm, ffn_norm, final_norm, w_in_even, w_out_even, a_ln_g, a_ln_b, a_w_s, a_b_s, b_lb_table,
           b_norm_g, w_qkv_odd, w_o_odd, rel_bias, w_router, w_gate, w_up, w_down):
    B, S, D = x.shape
    depth = mix_norm.shape[0]
    xt = x.reshape(B * S, D)
    for layer in range(depth):
        j = layer // 2
        if layer % 2 == 0:
            proj = norm_matmul(xt, mix_norm[layer], w_in_even[j].astype(BF16), B)[0].reshape(B * S, -1)
            a_out = mixer_a(proj, a_ln_g[j], a_ln_b[j], a_w_s[j], a_b_s[j])
            o_f, o_b = mixer_b(proj, b_lb_table, layer, B)
            xt, h, aff = outproj_even(a_out, o_f, o_b, proj, b_norm_g[j], w_out_even[j], xt,
                                      ffn_norm[layer], w_router[layer])
        else:
            views = norm_matmul(xt, mix_norm[layer], w_qkv_odd[j].astype(BF16), B, DILATIONS, tm=512)
            outs, lses = dilated_mixture(views, rel_bias)
            xt, h, aff = outproj_odd(outs, lses, w_o_odd[j], xt, ffn_norm[layer], w_router[layer])
        xt = expert_choice_moe(xt, h, aff, B, w_gate, w_up, w_down, layer,
                               out_norm=final_norm if layer == depth - 1 else None)
    return xt.reshape(B, S, D)
```

```python
import functools
import math

import numpy as np
import jax
import jax.numpy as jnp
from jax import lax
from jax.experimental import pallas as pl
from jax.experimental.pallas import tpu as pltpu

F32 = jnp.float32
BF16 = jnp.bfloat16
I32 = jnp.int32
EPS = 1e-6
NEG_INF = -1e30
HIGHEST = lax.Precision.HIGHEST

LANES = 128
VMEM_LIMIT = 56 * 1024 * 1024

A_GROUPS = 4
A_CHUNK = 128
B_HEADS = 4
B_DIM = 128
GLA_CHUNK = 128
C_HEADS = 16
C_HEAD_DIM = 64
HALF_WINDOW = 64
DILATIONS = (1, 4, 16)
REL_BUCKETS = 32
REL_MAX_DISTANCE = 1024
N_EXPERTS = 16
CAPACITY_FACTOR = 2
TOKEN_CHUNK = 128
SLOT_WINDOW = 256


def _params(*sem):
    return pltpu.CompilerParams(dimension_semantics=sem, vmem_limit_bytes=VMEM_LIMIT)


def _dot(a, b, **kw):
    return jnp.dot(a, b, preferred_element_type=F32, **kw)


def _dot_nt(a, b):
    return lax.dot_general(a, b, (((1,), (1,)), ((), ())), preferred_element_type=F32)


def _dot_tn(a, b):
    return lax.dot_general(a, b, (((0,), (0,)), ((), ())), preferred_element_type=F32)


def _dot_split01(a2, x):
    hi = x.astype(BF16)
    lo = (x - hi.astype(F32)).astype(BF16)
    return _dot(a2, jnp.concatenate([hi, lo], axis=0))


def _rms(x, g):
    return x * lax.rsqrt(jnp.mean(x * x, axis=-1, keepdims=True) + EPS) * g


def _norm_matmul_kernel(x_ref, g_ref, w_ref, *rest, tn, dilations):
    o_refs, scratch = rest[:len(dilations)], rest[len(dilations):]
    tm, N = x_ref.shape[0], w_ref.shape[1]
    h = _rms(x_ref[...], g_ref[...]).astype(BF16)
    for j in range(N // tn):
        res = _dot(h, w_ref[:, j * tn:(j + 1) * tn])
        if scratch:
            for t in range(tn // LANES):
                scratch[0][t] = res[:, t * LANES:(t + 1) * LANES]
        for o_ref, d in zip(o_refs, dilations):
            if d == 1:
                o_ref[0, :, j * tn:(j + 1) * tn] = res.astype(BF16)
                continue
            for r in range(d):
                for t in range(tn // LANES):
                    lane0 = r * N + j * tn + t * LANES
                    o_ref[0, :, lane0:lane0 + LANES] = (
                        scratch[0][t, pl.ds(r, tm // d, stride=d), :].astype(BF16))


def norm_matmul(x, g, w, batch, dilations=(1,), tm=1024, tn=512):
    T, D = x.shape
    N = w.shape[1]
    S = T // batch
    tiles = S // tm
    strided = any(d > 1 for d in dilations)
    return pl.pallas_call(
        functools.partial(_norm_matmul_kernel, tn=tn, dilations=dilations),
        grid=(T // tm,),
        in_specs=[
            pl.BlockSpec((tm, D), lambda i: (i, 0)),
            pl.BlockSpec((1, D), lambda i: (0, 0)),
            pl.BlockSpec((D, N), lambda i: (0, 0)),
        ],
        out_specs=[pl.BlockSpec((1, tm // d, d * N), lambda i: (i // tiles, i % tiles, 0)) for d in dilations],
        out_shape=[jax.ShapeDtypeStruct((batch, S // d, d * N), BF16) for d in dilations],
        scratch_shapes=[pltpu.VMEM((tn // LANES, tm, LANES), F32)] if strided else [],
        compiler_params=_params("parallel"),
        name="norm_matmul",
    )(x, g.reshape(1, D), w)


def _spatial_gating(u_ref, v_ref, lg_ref, lb_ref, ws_ref, bs_ref):
    tm = u_ref.shape[0]
    u = jax.nn.gelu(u_ref[...].astype(F32))
    v = jax.nn.gelu(v_ref[...].astype(F32))
    mu = jnp.mean(v, axis=-1, keepdims=True)
    vc = v - mu
    vn = vc * lax.rsqrt(jnp.mean(vc * vc, axis=-1, keepdims=True) + EPS)
    vb = (vn * lg_ref[...] + lb_ref[...]).astype(BF16)
    rows_out = []
    for n in range(tm // A_CHUNK):
        rows = slice(n * A_CHUNK, (n + 1) * A_CHUNK)
        groups = []
        for g in range(A_GROUPS):
            cols = slice(g * LANES, (g + 1) * LANES)
            mixed = _dot(ws_ref[g], vb[rows, cols]) + bs_ref[:, g:g + 1]
            groups.append((u[rows, cols] * mixed).astype(BF16))
        rows_out.append(jnp.concatenate(groups, axis=-1))
    return jnp.concatenate(rows_out, axis=0)


def _gla_consts(C, reverse):
    t = np.arange(C)[:, None]
    r = np.arange(C)[None, :]
    L = int(round(math.log2(C)))
    spans = [(r >= t) if reverse else (r <= t)]
    level = np.where(np.eye(C, dtype=bool), L, -1).astype(np.int32)
    for l in range(L):
        bs = (t >> (l + 1)) << (l + 1)
        mid = bs + (1 << l)
        if reverse:
            act_q = t < mid
            span = np.where(act_q, (r >= t) & (r < mid), (r >= mid) & (r < t))
        else:
            act_q = t >= mid
            span = np.where(act_q, (r >= mid) & (r <= t), (r > t) & (r < mid))
        if l > 0:
            spans.append(span)
        level[(bs == bs.T) & act_q & ~act_q.T] = l
    return np.tile(np.concatenate(spans, axis=0).astype(np.float32), (1, 2)), level


def _gla_kernel(qf_ref, ff_ref, if_ref, qb_ref, fb_ref, ib_ref, tbl_ref, span_ref, level_ref,
                of_ref, ob_ref, state_ref, w_buf, q_buf, k_buf, v_buf, s_buf, p_buf, *, layer, C):
    @pl.when(pl.program_id(1) == 0)
    def _():
        state_ref[...] = jnp.zeros_like(state_ref)

    L = span_ref.shape[1] // C
    subs = qf_ref.shape[0] // C
    odd_row = jnp.bitwise_and(lax.broadcasted_iota(I32, (C, qf_ref.shape[1]), 0), 1) == 1

    def chunk_rows(sub, d):
        k = sub if d == 0 else subs - 1 - sub
        return slice(k * C, (k + 1) * C)

    groups = [(sub, d) for sub in range(subs) for d in range(2)]
    heads = [(sub * 2 + d, sub, d, h, slice(h * B_DIM, (h + 1) * B_DIM)) for sub, d in groups for h in range(B_HEADS)]
    tots = {}
    for sub, d in groups:
        g = sub * 2 + d
        q_ref, f_ref, i_ref = (qf_ref, ff_ref, if_ref) if d == 0 else (qb_ref, fb_ref, ib_ref)
        rows = chunk_rows(sub, d)
        tb = tbl_ref[d]
        e = jnp.exp(tb - jnp.max(tb, axis=0, keepdims=True))
        lb = jnp.sum(e[0:layer + 1], axis=0, keepdims=True) / jnp.sum(e, axis=0, keepdims=True)
        f = lb + (1.0 - lb) * jax.nn.sigmoid(f_ref[rows, :].astype(F32))
        e2 = _dot_split01(span_ref[d], jnp.log2(f))
        tot2 = e2[0:1] if d == 1 else e2[C - 1:C]
        tots[g] = jnp.exp2(tot2)
        w_buf[g, 0:C] = jnp.exp2(e2[0:C]).astype(BF16)
        w_buf[g, C:2 * C] = jnp.exp2(tot2 - e2[0:C]).astype(BF16)
        w_buf[g, 2 * C:3 * C] = jnp.where(odd_row if d == 0 else jnp.logical_not(odd_row), f, 1.0).astype(BF16)
        w_buf[g, 3 * C:] = jnp.exp2(e2[C:]).astype(BF16)
        q_buf[g] = jax.nn.silu(q_ref[rows, :].astype(F32)).astype(BF16)
        k_buf[g] = (1.0 - f).astype(BF16)
        v_buf[g] = i_ref[rows, :]
    for g, sub, d, h, sl in heads:
        qh, kh = q_buf[g, :, sl], k_buf[g, :, sl]
        s_buf[g, h, L] = _dot_nt(qh, kh)
        for l in range(L):
            wl = w_buf[g, (l + 2) * C:(l + 3) * C, sl]
            s_buf[g, h, l] = _dot_nt(qh * wl, kh * wl)
    for g, sub, d, h, sl in heads:
        level = level_ref[d]
        p = jnp.where(level == L, s_buf[g, h, L], 0.0)
        for l in range(L):
            p = jnp.where(level == l, s_buf[g, h, l], p)
        p_buf[g, h] = p.astype(BF16)
    for g, sub, d, h, sl in heads:
        o_ref = of_ref if d == 0 else ob_ref
        st = state_ref[d, h]
        vh = v_buf[g, :, sl]
        o_ref[chunk_rows(sub, d), sl] = (_dot(p_buf[g, h], vh) + _dot_nt(
            q_buf[g, :, sl] * w_buf[g, 0:C, sl], st.astype(BF16))).astype(o_ref.dtype)
        state_ref[d, h] = st * tots[g][:, sl] + _dot_tn(vh, k_buf[g, :, sl] * w_buf[g, C:2 * C, sl])


def mixer_b(proj, b_lb_table, layer, batch, C=GLA_CHUNK, subs=2):
    T = proj.shape[0]
    W = B_HEADS * B_DIM
    R = C * subs
    n = T // batch // R
    consts = [_gla_consts(C, rev) for rev in (False, True)]
    span = jnp.asarray(np.stack([c[0] for c in consts]), BF16)
    level = jnp.asarray(np.stack([c[1] for c in consts]))
    fwd = lambda col: pl.BlockSpec((R, W), lambda b, c: (b * n + c, col))
    bwd = lambda col: pl.BlockSpec((R, W), lambda b, c: (b * n + n - 1 - c, col))
    full = lambda a: pl.BlockSpec(a.shape, lambda b, c: (0,) * a.ndim)
    G = 2 * subs
    return pl.pallas_call(
        functools.partial(_gla_kernel, layer=layer, C=C),
        grid=(batch, n),
        in_specs=[fwd(2), fwd(3), fwd(5), bwd(2), bwd(4), bwd(5),
                  full(b_lb_table), full(span), full(level)],
        out_specs=[pl.BlockSpec((R, W), lambda b, c: (b * n + c, 0)),
                   pl.BlockSpec((R, W), lambda b, c: (b * n + n - 1 - c, 0))],
        out_shape=[jax.ShapeDtypeStruct((T, W), BF16)] * 2,
        scratch_shapes=[pltpu.VMEM((2, B_HEADS, B_DIM, B_DIM), F32),
                        pltpu.VMEM((G, span.shape[1] + 2 * C, W), BF16),
                        pltpu.VMEM((G, C, W), BF16), pltpu.VMEM((G, C, W), BF16), pltpu.VMEM((G, C, W), BF16),
                        pltpu.VMEM((G, B_HEADS, span.shape[1] // C + 1, C, C), F32),
                        pltpu.VMEM((G, B_HEADS, C, C), BF16)],
        compiler_params=_params("arbitrary", "arbitrary"),
        name="mixer_b",
    )(proj, proj, proj, proj, proj, proj, b_lb_table, span, level)


def _router_epilogue(x_new, g_ref, wr_ref, x_ref, h_ref, aff_ref):
    x_ref[...] = x_new
    h = _rms(x_new, g_ref[...])
    h_ref[...] = h
    w = wr_ref[...]
    n_exp = w.shape[0]
    h_hi, w_hi = h.astype(BF16), w.astype(BF16)
    h_lo, w_lo = (h - h_hi.astype(F32)).astype(BF16), (w - w_hi.astype(F32)).astype(BF16)
    both = _dot_nt(jnp.concatenate([w_hi, w_lo], axis=0), h_hi)
    logits = both[0:n_exp] + both[n_exp:] + _dot_nt(w_hi, h_lo)
    e = jnp.exp(logits - jnp.max(logits, axis=0, keepdims=True))
    aff_ref[...] = e / jnp.sum(e, axis=0, keepdims=True)


def _outproj_even_kernel(u_ref, v_ref, lg_ref, lb_ref, ws_ref, bs_ref, of_ref, ob_ref, gate_ref, bng_ref, w_ref,
                         x_ref, g_ref, wr_ref, xo_ref, h_ref, aff_ref):
    a = _spatial_gating(u_ref, v_ref, lg_ref, lb_ref, ws_ref, bs_ref)
    o = of_ref[...].astype(F32) + ob_ref[...].astype(F32)
    parts = []
    for h in range(B_HEADS):
        oh = o[:, h * B_DIM:(h + 1) * B_DIM]
        parts.append(oh * lax.rsqrt(jnp.mean(oh * oh, axis=-1, keepdims=True) + EPS))
    on = jnp.concatenate(parts, axis=-1) * bng_ref[...] * jax.nn.sigmoid(gate_ref[...].astype(F32))
    wa = a.shape[1]
    mixed = _dot(a, w_ref[0:wa, :]) + _dot(on.astype(BF16), w_ref[wa:, :])
    _router_epilogue(x_ref[...] + mixed, g_ref, wr_ref, xo_ref, h_ref, aff_ref)


def _router_out(T, D, tm):
    specs = [pl.BlockSpec((tm, D), lambda i: (i, 0)), pl.BlockSpec((tm, D), lambda i: (i, 0)),
             pl.BlockSpec((N_EXPERTS, tm), lambda i: (0, i))]
    shapes = [jax.ShapeDtypeStruct((T, D), F32), jax.ShapeDtypeStruct((T, D), F32),
              jax.ShapeDtypeStruct((N_EXPERTS, T), F32)]
    return specs, shapes


def outproj_even(proj, a_ln_g, a_ln_b, a_w_s, a_b_s, o_f, o_b, b_norm_g, w_out, x, ffn_g, w_router, tm=512):
    T, D = x.shape
    W = o_f.shape[1]
    row = lambda w, col=0: pl.BlockSpec((tm, w), lambda i: (i, col))
    full = lambda s: pl.BlockSpec(s, lambda i: (0,) * len(s))
    out_specs, out_shapes = _router_out(T, D, tm)
    return pl.pallas_call(
        _outproj_even_kernel,
        grid=(T // tm,),
        in_specs=[row(W, 0), row(W, 1), full((1, W)), full((1, W)), full(a_w_s.shape), full(a_b_s.T.shape),
                  row(W), row(W), row(W, 6), full((1, W)), full(w_out.shape), row(D),
                  full((1, D)), full(w_router.T.shape)],
        out_specs=out_specs,
        out_shape=out_shapes,
        compiler_params=_params("parallel"),
        name="outproj_even",
    )(proj, proj, a_ln_g.reshape(1, W), a_ln_b.reshape(1, W), a_w_s.astype(BF16), a_b_s.T,
      o_f, o_b, proj, b_norm_g.reshape(1, W), w_out.astype(BF16), x, ffn_g.reshape(1, D), w_router.T)


def _outproj_odd_kernel(*refs):
    P = len(DILATIONS)
    o_refs, l_refs = refs[:P], refs[P:2 * P]
    ex_ref, w_ref, x_ref, g_ref, wr_ref, xo_ref, h_ref, aff_ref, o_buf, l_buf = refs[2 * P:]
    tm = x_ref.shape[0]

    def token_order(ref, buf, d):
        if d == 1:
            return ref[0].astype(F32)
        tiles = buf.shape[0]
        for r in range(d):
            for t in range(tiles):
                lane0 = (r * tiles + t) * LANES
                buf[t, pl.ds(r, tm // d, stride=d), :] = ref[0, :, lane0:lane0 + LANES].astype(F32)
        return jnp.concatenate([buf[t] for t in range(tiles)], axis=-1)

    ls = [token_order(l_ref, l_buf, d)[:, 0:C_HEADS] for l_ref, d in zip(l_refs, DILATIONS)]
    m = functools.reduce(jnp.maximum, ls)
    es = [jnp.exp(l - m) for l in ls]
    den = functools.reduce(jnp.add, es)
    attn = None
    for e, o_ref, d in zip(es, o_refs, DILATIONS):
        wfull = _dot((e / den).astype(BF16), ex_ref[...])
        term = wfull * token_order(o_ref, o_buf, d)
        attn = term if attn is None else attn + term
    mixed = _dot(attn.astype(BF16), w_ref[...])
    _router_epilogue(x_ref[...] + mixed, g_ref, wr_ref, xo_ref, h_ref, aff_ref)


def outproj_odd(outs, lses, w_o, x, ffn_g, w_router, tm=512):
    T, D = x.shape
    tiles = outs[0].shape[1] // tm
    expand = jnp.asarray(np.kron(np.eye(C_HEADS), np.ones((1, C_HEAD_DIM))), BF16)
    dil = lambda w: [pl.BlockSpec((1, tm // d, d * w), lambda i: (i // tiles, i % tiles, 0)) for d in DILATIONS]
    full = lambda s: pl.BlockSpec(s, lambda i: (0, 0))
    out_specs, out_shapes = _router_out(T, D, tm)
    return pl.pallas_call(
        _outproj_odd_kernel,
        grid=(T // tm,),
        in_specs=dil(D) + dil(LANES) + [full(expand.shape), full(w_o.shape),
                                        pl.BlockSpec((tm, D), lambda i: (i, 0)), full((1, D)),
                                        full(w_router.T.shape)],
        out_specs=out_specs,
        out_shape=out_shapes,
        scratch_shapes=[pltpu.VMEM((D // LANES, tm, LANES), F32), pltpu.VMEM((1, tm, LANES), F32)],
        compiler_params=_params("parallel"),
        name="outproj_odd",
    )(*outs, *lses, expand, w_o.astype(BF16), x, ffn_g.reshape(1, D), w_router.T)


def _topk_kernel(aff_ref, upper_ref, strict_ref, pos_ref, tile_ref, *, cap):
    a = aff_ref[...]
    E, NC, _ = a.shape

    def count(mask):
        return jnp.sum(jnp.sum(mask.astype(F32), axis=2, keepdims=True), axis=1, keepdims=True)

    def as_float(bits):
        return lax.bitcast_convert_type(bits, jnp.float32)

    def search(i, thr):
        cand = thr | jnp.left_shift(jnp.int32(1), 30 - i)
        return jnp.where(count(a >= as_float(cand)) >= cap, cand, thr)

    thr = lax.fori_loop(0, 31, search, jnp.zeros((E, 1, 1), I32))
    gt = a >= as_float(thr + 1)
    eq = jnp.logical_and(a >= as_float(thr), jnp.logical_not(gt))
    need = cap - count(gt)

    def prefix(mask):
        m2 = mask.astype(BF16).reshape(E * NC, LANES)
        within = _dot(m2, upper_ref[...])
        total = within[:, LANES - 1:LANES].astype(BF16)
        tot_b = jnp.broadcast_to(total, (E * NC, LANES))
        starts = jnp.concatenate(
            [_dot(strict_ref[...], tot_b[e * NC:(e + 1) * NC]) for e in range(E)], axis=0)
        return (within + starts).reshape(E, NC, LANES), starts.reshape(E, NC, LANES)

    eq_incl, _ = prefix(eq)
    sel = gt | (eq & (eq_incl - 1.0 < need))
    sel_incl, starts = prefix(sel)
    pos_ref[0] = jnp.where(sel, sel_incl - 1.0, -1.0).astype(I32)
    tile = jnp.minimum(jnp.floor(starts * (1.0 / TOKEN_CHUNK)), float((cap - SLOT_WINDOW) // TOKEN_CHUNK))
    tile_ref[0] = tile.astype(I32)


def route_topk(aff_t, batch, cap):
    E, T = aff_t.shape
    B, S = batch, T // batch
    NC = S // TOKEN_CHUNK
    t = np.arange(LANES)
    upper = jnp.asarray(t[:, None] <= t[None, :], BF16)
    c = np.arange(NC)
    strict = jnp.asarray(c[None, :] < c[:, None], BF16)
    blk = pl.BlockSpec((1, E, NC, LANES), lambda b: (b, 0, 0, 0))
    pos, tiles = pl.pallas_call(
        functools.partial(_topk_kernel, cap=cap),
        grid=(B,),
        in_specs=[pl.BlockSpec((E, NC, LANES), lambda b: (0, b, 0)),
                  pl.BlockSpec(upper.shape, lambda b: (0, 0)), pl.BlockSpec(strict.shape, lambda b: (0, 0))],
        out_specs=[blk, blk],
        out_shape=[jax.ShapeDtypeStruct((B, E, NC, LANES), I32)] * 2,
        compiler_params=_params("parallel"),
        name="route_topk",
    )(aff_t.reshape(E, B * NC, LANES), upper, strict)
    return pos, tiles[..., 0].reshape(-1)


def _index_kernel(tiles_ref, pos_ref, sel_ref, idx_ref, acc_ref):
    b, e = pl.program_id(0), pl.program_id(1)
    n_exp = pl.num_programs(1)
    NC = pos_ref.shape[2]
    acc_ref[...] = jnp.zeros_like(acc_ref)
    slot = lax.broadcasted_iota(I32, (SLOT_WINDOW, TOKEN_CHUNK), 0)

    def body(c, carry):
        j = tiles_ref[(b * n_exp + e) * NC + c]
        onehot = jnp.where(slot + j * TOKEN_CHUNK == pos_ref[0, 0, pl.ds(c, 1), :], 1.0, 0.0).astype(BF16)
        hit = _dot_nt(sel_ref[...], onehot)
        tok = hit[0:1] + hit[1:2] * lax.convert_element_type(c * TOKEN_CHUNK, F32)
        acc_ref[j, 0:1, :] += tok[:, 0:TOKEN_CHUNK]
        acc_ref[j + 1, 0:1, :] += tok[:, TOKEN_CHUNK:]
        return carry

    lax.fori_loop(0, NC, body, 0, unroll=32)
    idx_ref[0, 0] = acc_ref[...].astype(I32)


def moe_slot_tokens(tiles, pos, cap):
    B, E, NC, _ = pos.shape
    nt = cap // TOKEN_CHUNK
    sel = np.zeros((8, TOKEN_CHUNK), np.float32)
    sel[0] = np.arange(TOKEN_CHUNK)
    sel[1] = 1.0
    idx = pl.pallas_call(
        _index_kernel,
        grid_spec=pltpu.PrefetchScalarGridSpec(
            num_scalar_prefetch=1,
            grid=(B, E),
            in_specs=[pl.BlockSpec((1, 1, NC, LANES), lambda b, e, s: (b, e, 0, 0)),
                      pl.BlockSpec((8, TOKEN_CHUNK), lambda b, e, s: (0, 0))],
            out_specs=pl.BlockSpec((1, 1, nt, 8, LANES), lambda b, e, s: (b, e, 0, 0, 0)),
            scratch_shapes=[pltpu.VMEM((nt, 8, LANES), F32)]),
        out_shape=jax.ShapeDtypeStruct((B, E, nt, 8, LANES), I32),
        compiler_params=_params("arbitrary", "arbitrary"),
        name="moe_slot_tokens",
    )(tiles, pos, jnp.asarray(sel, BF16))
    return idx[:, :, :, 0, :].reshape(-1)


def _ffn_kernel(idx_ref, h_ref, wg_ref, wu_ref, wd_ref, y_ref, x_buf, sem, wgb_ref, wub_ref, wdb_ref, *, tr):
    e, b = pl.program_id(0), pl.program_id(1)
    n_exp, nb = pl.num_programs(0), pl.num_programs(1)
    cap = x_buf.shape[1]
    step = e * nb + b

    def row_copy(st, buf, i):
        eb = st // nb
        bb = st - eb * nb
        tok = idx_ref[(bb * n_exp + eb) * cap + i]
        return pltpu.make_async_copy(h_ref.at[bb, pl.ds(tok, 1), :], x_buf.at[buf, pl.ds(i, 1), :], sem.at[buf])

    cur = jnp.bitwise_and(step, 1)

    @pl.when(step == 0)
    def _():
        lax.fori_loop(0, cap, lambda i, c: (row_copy(step, 0, i).start(), c)[1], 0, unroll=8)

    @pl.when(b == 0)
    def _():
        wgb_ref[...] = wg_ref[0, 0].astype(BF16)
        wub_ref[...] = wu_ref[0, 0].astype(BF16)
        wdb_ref[...] = wd_ref[0, 0].astype(BF16)

    pltpu.make_async_copy(h_ref.at[0, pl.ds(0, cap), :], x_buf.at[cur], sem.at[cur]).wait()

    def compute(prefetch):
        for r in range(cap // tr):
            rows = pl.ds(r * tr, tr)
            x = x_buf[cur, rows, :].astype(BF16)
            if prefetch:
                for i in range(r * tr, (r + 1) * tr):
                    row_copy(step + 1, 1 - cur, i).start()
            g = _dot(x, wgb_ref[...])
            u = _dot(x, wub_ref[...])
            mid = (g * jax.nn.sigmoid(g) * u).astype(BF16)
            y_ref[0, 0, rows, :] = _dot(mid, wdb_ref[...]).astype(BF16)

    is_last = step + 1 == n_exp * nb
    pl.when(jnp.logical_not(is_last))(lambda: compute(True))
    pl.when(is_last)(lambda: compute(False))


def moe_ffn(idx, h, w_gate, w_up, w_down, layer, cap, tr=256):
    B, S, D = h.shape
    E, F = w_gate.shape[1], w_gate.shape[-1]
    wspec = lambda r, c: pl.BlockSpec((1, 1, r, c), lambda e, b, s: (layer, e, 0, 0))
    return pl.pallas_call(
        functools.partial(_ffn_kernel, tr=min(tr, cap)),
        grid_spec=pltpu.PrefetchScalarGridSpec(
            num_scalar_prefetch=1,
            grid=(E, B),
            in_specs=[pl.BlockSpec(memory_space=pl.ANY), wspec(D, F), wspec(D, F), wspec(F, D)],
            out_specs=pl.BlockSpec((1, 1, cap, D), lambda e, b, s: (b, e, 0, 0)),
            scratch_shapes=[pltpu.VMEM((2, cap, D), F32), pltpu.SemaphoreType.DMA((2,)),
                            pltpu.VMEM((D, F), BF16), pltpu.VMEM((D, F), BF16), pltpu.VMEM((F, D), BF16)]),
        out_shape=jax.ShapeDtypeStruct((B, E, cap, D), BF16),
        compiler_params=_params("arbitrary", "arbitrary"),
        name="moe_ffn",
    )(idx, h, w_gate, w_up, w_down)


def _combine_kernel(tiles_ref, post_ref, gate_ref, x_ref, y_ref, *rest):
    g_ref, o_ref = rest if len(rest) == 2 else (None, rest[0])
    b, i = pl.program_id(0), pl.program_id(1)
    n_exp = y_ref.shape[1]
    per_step = x_ref.shape[1] // TOKEN_CHUNK
    NC = pl.num_programs(1) * per_step
    slot = lax.broadcasted_iota(I32, (TOKEN_CHUNK, SLOT_WINDOW), 1)
    for k in range(per_step):
        rows = slice(k * TOKEN_CHUNK, (k + 1) * TOKEN_CHUNK)
        acc = x_ref[0, rows, :]
        for e in range(n_exp):
            tile = tiles_ref[(b * n_exp + e) * NC + i * per_step + k]
            base = pl.multiple_of(tile * TOKEN_CHUNK, TOKEN_CHUNK)
            onehot = jnp.where(slot + base == post_ref[0, rows, e:e + 1], 1.0, 0.0).astype(BF16)
            acc = acc + _dot(onehot, y_ref[0, e, pl.ds(base, SLOT_WINDOW), :]) * gate_ref[0, rows, e:e + 1]
        o_ref[0, rows, :] = acc if g_ref is None else _rms(acc, g_ref[...])


def moe_combine(tiles, pos_t, gate, x, y, out_norm=None, tm=512):
    B, S, D = x.shape
    E, cap = y.shape[1], y.shape[2]
    tok = lambda w: pl.BlockSpec((1, tm, w), lambda b, c, s: (b, c, 0))
    in_specs = [tok(E), tok(E), tok(D),
                pl.BlockSpec((1, E, cap, D), lambda b, c, s: (b, 0, 0, 0), pipeline_mode=pl.Buffered(1))]
    args = [tiles, pos_t, gate, x, y]
    if out_norm is not None:
        in_specs.append(pl.BlockSpec((1, D), lambda b, c, s: (0, 0)))
        args.append(out_norm.reshape(1, D))
    return pl.pallas_call(
        _combine_kernel,
        grid_spec=pltpu.PrefetchScalarGridSpec(
            num_scalar_prefetch=1,
            grid=(B, S // tm),
            in_specs=in_specs,
            out_specs=tok(D)),
        out_shape=jax.ShapeDtypeStruct((B, S, D), F32),
        compiler_params=_params("arbitrary", "arbitrary"),
        name="moe_combine",
    )(*args)


def expert_choice_moe(x, h, aff_t, batch, w_gate, w_up, w_down, layer, out_norm=None):
    T, D = x.shape
    S = T // batch
    cap = max(1, CAPACITY_FACTOR * S // N_EXPERTS)
    aff = aff_t.T.reshape(batch, S, N_EXPERTS)
    pos, tiles = route_topk(aff_t, batch, cap)
    idx = moe_slot_tokens(tiles, pos, cap)
    y = moe_ffn(idx, h.reshape(batch, S, D), w_gate, w_up, w_down, layer, cap)
    pos_t = jnp.swapaxes(pos.reshape(batch, N_EXPERTS, S), 1, 2)
    return moe_combine(tiles, pos_t, aff, x.reshape(batch, S, D), y, out_norm).reshape(T, D)


ATTN_TQ = 128
ATTN_TK = ATTN_TQ + 2 * HALF_WINDOW
ATTN_SUB = 2


def _t5_bucket(rel):
    half_buckets = REL_BUCKETS // 2
    max_exact = half_buckets // 2
    n = jnp.abs(rel)
    scaled = (jnp.log(jnp.maximum(n, 1).astype(jnp.float32) / max_exact)
              / math.log(REL_MAX_DISTANCE / max_exact))
    large = jnp.minimum(max_exact + jnp.floor(scaled * (half_buckets - max_exact)).astype(jnp.int32),
                        half_buckets - 1)
    return jnp.where(rel > 0, half_buckets, 0) + jnp.where(n < max_exact, n, large)


def _bias_kernel(table_ref, bucket_ref, o_ref):
    h = pl.program_id(1)
    bucket = bucket_ref[0]
    acc = jnp.zeros(bucket.shape, F32)
    for bk in range(REL_BUCKETS):
        acc = jnp.where(bucket == bk, table_ref[bk * C_HEADS + h], acc)
    q = lax.broadcasted_iota(I32, bucket.shape, 0)
    kc = lax.broadcasted_iota(I32, bucket.shape, 1)
    middle = jnp.where(jnp.abs(kc - HALF_WINDOW - q) <= HALF_WINDOW, acc, NEG_INF)
    o_ref[0, 0, 0] = jnp.where(kc >= HALF_WINDOW, middle, NEG_INF)
    o_ref[0, 1, 0] = middle
    o_ref[0, 2, 0] = jnp.where(kc < HALF_WINDOW + ATTN_TQ, middle, NEG_INF)


def attention_bias(rel_bias):
    rel = np.arange(ATTN_TK)[None, :] - HALF_WINDOW - np.arange(ATTN_TQ)[:, None]
    buckets = jnp.stack([_t5_bucket(jnp.asarray(rel * d, I32)) for d in DILATIONS]).astype(I32)
    P = len(DILATIONS)
    return pl.pallas_call(
        _bias_kernel,
        grid_spec=pltpu.PrefetchScalarGridSpec(
            num_scalar_prefetch=1,
            grid=(P, C_HEADS),
            in_specs=[pl.BlockSpec((1, ATTN_TQ, ATTN_TK), lambda p, h, t: (p, 0, 0))],
            out_specs=pl.BlockSpec((1, 3, 1, ATTN_TQ, ATTN_TK), lambda p, h, t: (p, 0, h, 0, 0))),
        out_shape=jax.ShapeDtypeStruct((P, 3, C_HEADS, ATTN_TQ, ATTN_TK), F32),
        compiler_params=_params("arbitrary", "arbitrary"),
        name="attention_bias",
    )(rel_bias.reshape(-1), buckets)


def _attn_kernel(q_ref, kp_ref, km_ref, kn_ref, vp_ref, vm_ref, vn_ref, bias_ref, o_ref, lse_ref,
                 k_buf, v_buf, s_buf, p_buf):
    i = pl.program_id(2)
    last = pl.num_programs(2) - 1
    hw = HALF_WINDOW
    rows = q_ref.shape[1]
    k_buf[0:hw] = kp_ref[0]
    k_buf[hw:hw + rows] = km_ref[0]
    k_buf[hw + rows:] = kn_ref[0]
    v_buf[0:hw] = vp_ref[0]
    v_buf[hw:hw + rows] = vm_ref[0]
    v_buf[hw + rows:] = vn_ref[0]
    first_head = lax.broadcasted_iota(I32, (ATTN_TQ, LANES), 1) < C_HEAD_DIM
    scale = C_HEAD_DIM ** -0.5
    lse_ref[...] = jnp.zeros_like(lse_ref)
    for sub in range(rows // ATTN_TQ):
        qrows = slice(sub * ATTN_TQ, (sub + 1) * ATTN_TQ)
        krows = slice(sub * ATTN_TQ, sub * ATTN_TQ + ATTN_TK)
        if sub == 0:
            variant = jnp.where(i == 0, 0, 1)
        elif sub == rows // ATTN_TQ - 1:
            variant = jnp.where(i == last, 2, 1)
        else:
            variant = 1
        for pair in range(C_HEADS // 2):
            cols = slice(pair * LANES, (pair + 1) * LANES)
            q = q_ref[0, qrows, cols] * scale
            for half in range(2):
                mine = first_head if half == 0 else jnp.logical_not(first_head)
                s_buf[2 * pair + half] = (_dot_nt(jnp.where(mine, q, jnp.zeros_like(q)), k_buf[krows, cols])
                                          + bias_ref[variant, 2 * pair + half])
        for h in range(C_HEADS):
            s = s_buf[h]
            m = jnp.max(s, axis=-1, keepdims=True)
            p = jnp.exp(s - m)
            den = jnp.sum(p, axis=-1, keepdims=True)
            p_buf[h] = (p / den).astype(BF16)
            lse_ref[0, qrows, h:h + 1] = m + jnp.log(den)
        for pair in range(C_HEADS // 2):
            cols = slice(pair * LANES, (pair + 1) * LANES)
            o_ref[0, qrows, cols] = jnp.where(first_head, _dot(p_buf[2 * pair], v_buf[krows, cols]),
                                              _dot(p_buf[2 * pair + 1], v_buf[krows, cols])).astype(BF16)


def dilated_attention(view, bias, d):
    batch, n, D3 = view.shape
    D = D3 // d // 3
    rows = ATTN_TQ * ATTN_SUB
    nb = n // HALF_WINDOW
    r = rows // HALF_WINDOW
    assert n // rows >= 1 and ATTN_SUB >= 2
    main = lambda c: pl.BlockSpec((1, rows, D), lambda b, j, i: (b, i, 3 * j + c))
    prev = lambda c: pl.BlockSpec((1, HALF_WINDOW, D), lambda b, j, i: (b, jnp.maximum(i * r - 1, 0), 3 * j + c))
    nxt = lambda c: pl.BlockSpec((1, HALF_WINDOW, D),
                                 lambda b, j, i: (b, jnp.minimum(i * r + r, nb - 1), 3 * j + c))
    return pl.pallas_call(
        _attn_kernel,
        grid=(batch, d, n // rows),
        in_specs=[main(0), prev(1), main(1), nxt(1), prev(2), main(2), nxt(2),
                  pl.BlockSpec(bias.shape, lambda b, j, i: (0, 0, 0, 0))],
        out_specs=[pl.BlockSpec((1, rows, D), lambda b, j, i: (b, i, j)),
                   pl.BlockSpec((1, rows, LANES), lambda b, j, i: (b, i, j))],
        out_shape=[jax.ShapeDtypeStruct((batch, n, d * D), BF16),
                   jax.ShapeDtypeStruct((batch, n, d * LANES), F32)],
        scratch_shapes=[pltpu.VMEM((rows + 2 * HALF_WINDOW, D), BF16), pltpu.VMEM((rows + 2 * HALF_WINDOW, D), BF16),
                        pltpu.VMEM((C_HEADS, ATTN_TQ, ATTN_TK), F32), pltpu.VMEM((C_HEADS, ATTN_TQ, ATTN_TK), BF16)],
        compiler_params=_params("parallel", "parallel", "parallel"),
        name="dilated_attention",
    )(view, view, view, view, view, view, view, bias)


def dilated_mixture(views, rel_bias):
    bias = attention_bias(rel_bias)
    results = [dilated_attention(view, bias[p], d) for p, (view, d) in enumerate(zip(views, DILATIONS))]
    return [o for o, _ in results], [lse for _, lse in results]


def kernel(x, mix_norm, ffn_norm, final_norm, w_in_even, w_out_even, a_ln_g, a_ln_b, a_w_s, a_b_s, b_lb_table,
           b_norm_g, w_qkv_odd, w_o_odd, rel_bias, w_router, w_gate, w_up, w_down):
    B, S, D = x.shape
    depth = mix_norm.shape[0]
    xt = x.reshape(B * S, D)
    for layer in range(depth):
        j = layer // 2
        if layer % 2 == 0:
            proj = norm_matmul(xt, mix_norm[layer], w_in_even[j].astype(BF16), B)[0].reshape(B * S, -1)
            o_f, o_b = mixer_b(proj, b_lb_table, layer, B)
            xt, h, aff = outproj_even(proj, a_ln_g[j], a_ln_b[j], a_w_s[j], a_b_s[j], o_f, o_b, b_norm_g[j],
                                      w_out_even[j], xt, ffn_norm[layer], w_router[layer])
        else:
            views = norm_matmul(xt, mix_norm[layer], w_qkv_odd[j].astype(BF16), B, DILATIONS, tm=512)
            outs, lses = dilated_mixture(views, rel_bias)
            xt, h, aff = outproj_odd(outs, lses, w_o_odd[j], xt, ffn_norm[layer], w_router[layer])
        xt = expert_choice_moe(xt, h, aff, B, w_gate, w_up, w_down, layer,
                               out_norm=final_norm if layer == depth - 1 else None)
    return xt.reshape(B, S, D)
```

```python
import functools
import math

import numpy as np
import jax
import jax.numpy as jnp
from jax import lax
from jax.experimental import pallas as pl
from jax.experimental.pallas import tpu as pltpu

F32 = jnp.float32
BF16 = jnp.bfloat16
I32 = jnp.int32
EPS = 1e-6
NEG_INF = -1e30
HIGHEST = lax.Precision.HIGHEST

LANES = 128
VMEM_LIMIT = 56 * 1024 * 1024

A_GROUPS = 4
A_CHUNK = 128
B_HEADS = 4
B_DIM = 128
GLA_CHUNK = 128
C_HEADS = 16
C_HEAD_DIM = 64
HALF_WINDOW = 64
DILATIONS = (1, 4, 16)
REL_BUCKETS = 32
REL_MAX_DISTANCE = 1024
N_EXPERTS = 16
CAPACITY_FACTOR = 2
TOKEN_CHUNK = 128
SLOT_WINDOW = 256


def _params(*sem):
    return pltpu.CompilerParams(dimension_semantics=sem, vmem_limit_bytes=VMEM_LIMIT)


def _dot(a, b, **kw):
    return jnp.dot(a, b, preferred_element_type=F32, **kw)


def _dot_nt(a, b):
    return lax.dot_general(a, b, (((1,), (1,)), ((), ())), preferred_element_type=F32)


def _dot_tn(a, b):
    return lax.dot_general(a, b, (((0,), (0,)), ((), ())), preferred_element_type=F32)


def _dot_split01(a2, x):
    hi = x.astype(BF16)
    lo = (x - hi.astype(F32)).astype(BF16)
    return _dot(a2, jnp.concatenate([hi, lo], axis=0))


def _rms(x, g):
    return x * lax.rsqrt(jnp.mean(x * x, axis=-1, keepdims=True) + EPS) * g


def _norm_matmul_kernel(x_ref, g_ref, w_ref, *rest, tn, dilations):
    o_refs, scratch = rest[:len(dilations)], rest[len(dilations):]
    tm, N = x_ref.shape[0], w_ref.shape[1]
    h = _rms(x_ref[...], g_ref[...]).astype(BF16)
    for j in range(N // tn):
        res = _dot(h, w_ref[:, j * tn:(j + 1) * tn])
        if scratch:
            for t in range(tn // LANES):
                scratch[0][t] = res[:, t * LANES:(t + 1) * LANES]
        for o_ref, d in zip(o_refs, dilations):
            if d == 1:
                o_ref[0, :, j * tn:(j + 1) * tn] = res.astype(BF16)
                continue
            for r in range(d):
                for t in range(tn // LANES):
                    lane0 = r * N + j * tn + t * LANES
                    o_ref[0, :, lane0:lane0 + LANES] = (
                        scratch[0][t, pl.ds(r, tm // d, stride=d), :].astype(BF16))


def norm_matmul(x, g, w, batch, dilations=(1,), tm=1024, tn=512):
    T, D = x.shape
    N = w.shape[1]
    S = T // batch
    tiles = S // tm
    strided = any(d > 1 for d in dilations)
    return pl.pallas_call(
        functools.partial(_norm_matmul_kernel, tn=tn, dilations=dilations),
        grid=(T // tm,),
        in_specs=[
            pl.BlockSpec((tm, D), lambda i: (i, 0)),
            pl.BlockSpec((1, D), lambda i: (0, 0)),
            pl.BlockSpec((D, N), lambda i: (0, 0)),
        ],
        out_specs=[pl.BlockSpec((1, tm // d, d * N), lambda i: (i // tiles, i % tiles, 0)) for d in dilations],
        out_shape=[jax.ShapeDtypeStruct((batch, S // d, d * N), BF16) for d in dilations],
        scratch_shapes=[pltpu.VMEM((tn // LANES, tm, LANES), F32)] if strided else [],
        compiler_params=_params("parallel"),
        name="norm_matmul",
    )(x, g.reshape(1, D), w)


def _spatial_gating(u_ref, v_ref, lg_ref, lb_ref, ws_ref, bs_ref):
    tm = u_ref.shape[0]
    u = jax.nn.gelu(u_ref[...].astype(F32))
    v = jax.nn.gelu(v_ref[...].astype(F32))
    mu = jnp.mean(v, axis=-1, keepdims=True)
    vc = v - mu
    vn = vc * lax.rsqrt(jnp.mean(vc * vc, axis=-1, keepdims=True) + EPS)
    vb = (vn * lg_ref[...] + lb_ref[...]).astype(BF16)
    rows_out = []
    for n in range(tm // A_CHUNK):
        rows = slice(n * A_CHUNK, (n + 1) * A_CHUNK)
        groups = []
        for g in range(A_GROUPS):
            cols = slice(g * LANES, (g + 1) * LANES)
            mixed = _dot(ws_ref[g], vb[rows, cols]) + bs_ref[:, g:g + 1]
            groups.append((u[rows, cols] * mixed).astype(BF16))
        rows_out.append(jnp.concatenate(groups, axis=-1))
    return jnp.concatenate(rows_out, axis=0)


def _gla_consts(C, reverse):
    t = np.arange(C)[:, None]
    r = np.arange(C)[None, :]
    L = int(round(math.log2(C)))
    spans = [(r >= t) if reverse else (r <= t)]
    level = np.where(np.eye(C, dtype=bool), L, -1).astype(np.int32)
    for l in range(L):
        bs = (t >> (l + 1)) << (l + 1)
        mid = bs + (1 << l)
        if reverse:
            act_q = t < mid
            span = np.where(act_q, (r >= t) & (r < mid), (r >= mid) & (r < t))
        else:
            act_q = t >= mid
            span = np.where(act_q, (r >= mid) & (r <= t), (r > t) & (r < mid))
        if l > 0:
            spans.append(span)
        level[(bs == bs.T) & act_q & ~act_q.T] = l
    return np.tile(np.concatenate(spans, axis=0).astype(np.float32), (1, 2)), level


def _gla_kernel(qf_ref, ff_ref, if_ref, qb_ref, fb_ref, ib_ref, tbl_ref, span_ref, level_ref,
                of_ref, ob_ref, state_ref, w_buf, q_buf, k_buf, v_buf, s_buf, p_buf, *, layer, C):
    @pl.when(pl.program_id(1) == 0)
    def _():
        state_ref[...] = jnp.zeros_like(state_ref)

    L = span_ref.shape[1] // C
    subs = qf_ref.shape[0] // C
    odd_row = jnp.bitwise_and(lax.broadcasted_iota(I32, (C, qf_ref.shape[1]), 0), 1) == 1

    def chunk_rows(sub, d):
        k = sub if d == 0 else subs - 1 - sub
        return slice(k * C, (k + 1) * C)

    groups = [(sub, d) for sub in range(subs) for d in range(2)]
    heads = [(sub * 2 + d, sub, d, h, slice(h * B_DIM, (h + 1) * B_DIM)) for sub, d in groups for h in range(B_HEADS)]
    tots = {}
    for sub, d in groups:
        g = sub * 2 + d
        q_ref, f_ref, i_ref = (qf_ref, ff_ref, if_ref) if d == 0 else (qb_ref, fb_ref, ib_ref)
        rows = chunk_rows(sub, d)
        tb = tbl_ref[d]
        e = jnp.exp(tb - jnp.max(tb, axis=0, keepdims=True))
        lb = jnp.sum(e[0:layer + 1], axis=0, keepdims=True) / jnp.sum(e, axis=0, keepdims=True)
        f = lb + (1.0 - lb) * jax.nn.sigmoid(f_ref[rows, :].astype(F32))
        e2 = _dot_split01(span_ref[d], jnp.log2(f))
        tot2 = e2[0:1] if d == 1 else e2[C - 1:C]
        tots[g] = jnp.exp2(tot2)
        w_buf[g, 0:C] = jnp.exp2(e2[0:C]).astype(BF16)
        w_buf[g, C:2 * C] = jnp.exp2(tot2 - e2[0:C]).astype(BF16)
        w_buf[g, 2 * C:3 * C] = jnp.where(odd_row if d == 0 else jnp.logical_not(odd_row), f, 1.0).astype(BF16)
        w_buf[g, 3 * C:] = jnp.exp2(e2[C:]).astype(BF16)
        q_buf[g] = jax.nn.silu(q_ref[rows, :].astype(F32)).astype(BF16)
        k_buf[g] = (1.0 - f).astype(BF16)
        v_buf[g] = i_ref[rows, :]
    for g, sub, d, h, sl in heads:
        qh, kh = q_buf[g, :, sl], k_buf[g, :, sl]
        s_buf[g, h, L] = _dot_nt(qh, kh)
        for l in range(L):
            wl = w_buf[g, (l + 2) * C:(l + 3) * C, sl]
            s_buf[g, h, l] = _dot_nt(qh * wl, kh * wl)
    for g, sub, d, h, sl in heads:
        level = level_ref[d]
        p = jnp.where(level == L, s_buf[g, h, L], 0.0)
        for l in range(L):
            p = jnp.where(level == l, s_buf[g, h, l], p)
        p_buf[g, h] = p.astype(BF16)
    for g, sub, d, h, sl in heads:
        o_ref = of_ref if d == 0 else ob_ref
        st = state_ref[d, h]
        vh = v_buf[g, :, sl]
        o_ref[chunk_rows(sub, d), sl] = (_dot(p_buf[g, h], vh) + _dot_nt(
            q_buf[g, :, sl] * w_buf[g, 0:C, sl], st.astype(BF16))).astype(o_ref.dtype)
        state_ref[d, h] = st * tots[g][:, sl] + _dot_tn(vh, k_buf[g, :, sl] * w_buf[g, C:2 * C, sl])


def mixer_b(proj, b_lb_table, layer, batch, C=GLA_CHUNK, subs=2):
    T = proj.shape[0]
    W = B_HEADS * B_DIM
    R = C * subs
    n = T // batch // R
    consts = [_gla_consts(C, rev) for rev in (False, True)]
    span = jnp.asarray(np.stack([c[0] for c in consts]), BF16)
    level = jnp.asarray(np.stack([c[1] for c in consts]))
    fwd = lambda col: pl.BlockSpec((R, W), lambda b, c: (b * n + c, col))
    bwd = lambda col: pl.BlockSpec((R, W), lambda b, c: (b * n + n - 1 - c, col))
    full = lambda a: pl.BlockSpec(a.shape, lambda b, c: (0,) * a.ndim)
    G = 2 * subs
    return pl.pallas_call(
        functools.partial(_gla_kernel, layer=layer, C=C),
        grid=(batch, n),
        in_specs=[fwd(2), fwd(3), fwd(5), bwd(2), bwd(4), bwd(5),
                  full(b_lb_table), full(span), full(level)],
        out_specs=[pl.BlockSpec((R, W), lambda b, c: (b * n + c, 0)),
                   pl.BlockSpec((R, W), lambda b, c: (b * n + n - 1 - c, 0))],
        out_shape=[jax.ShapeDtypeStruct((T, W), BF16)] * 2,
        scratch_shapes=[pltpu.VMEM((2, B_HEADS, B_DIM, B_DIM), F32),
                        pltpu.VMEM((G, span.shape[1] + 2 * C, W), BF16),
                        pltpu.VMEM((G, C, W), BF16), pltpu.VMEM((G, C, W), BF16), pltpu.VMEM((G, C, W), BF16),
                        pltpu.VMEM((G, B_HEADS, span.shape[1] // C + 1, C, C), F32),
                        pltpu.VMEM((G, B_HEADS, C, C), BF16)],
        compiler_params=_params("arbitrary", "arbitrary"),
        name="mixer_b",
    )(proj, proj, proj, proj, proj, proj, b_lb_table, span, level)


def _router_epilogue(x_new, g_ref, wr_ref, x_ref, h_ref, aff_ref):
    x_ref[...] = x_new
    h = _rms(x_new, g_ref[...])
    h_ref[...] = h
    w = wr_ref[...]
    n_exp = w.shape[0]
    h_hi, w_hi = h.astype(BF16), w.astype(BF16)
    h_lo, w_lo = (h - h_hi.astype(F32)).astype(BF16), (w - w_hi.astype(F32)).astype(BF16)
    both = _dot_nt(jnp.concatenate([w_hi, w_lo], axis=0), h_hi)
    logits = both[0:n_exp] + both[n_exp:] + _dot_nt(w_hi, h_lo)
    e = jnp.exp(logits - jnp.max(logits, axis=0, keepdims=True))
    aff_ref[...] = e / jnp.sum(e, axis=0, keepdims=True)


def _outproj_even_kernel(u_ref, v_ref, lg_ref, lb_ref, ws_ref, bs_ref, of_ref, ob_ref, gate_ref, bng_ref, w_ref,
                         x_ref, g_ref, wr_ref, xo_ref, h_ref, aff_ref):
    a = _spatial_gating(u_ref, v_ref, lg_ref, lb_ref, ws_ref, bs_ref)
    o = of_ref[...].astype(F32) + ob_ref[...].astype(F32)
    parts = []
    for h in range(B_HEADS):
        oh = o[:, h * B_DIM:(h + 1) * B_DIM]
        parts.append(oh * lax.rsqrt(jnp.mean(oh * oh, axis=-1, keepdims=True) + EPS))
    on = jnp.concatenate(parts, axis=-1) * bng_ref[...] * jax.nn.sigmoid(gate_ref[...].astype(F32))
    wa = a.shape[1]
    mixed = _dot(a, w_ref[0:wa, :]) + _dot(on.astype(BF16), w_ref[wa:, :])
    _router_epilogue(x_ref[...] + mixed, g_ref, wr_ref, xo_ref, h_ref, aff_ref)


def _router_out(T, D, tm):
    specs = [pl.BlockSpec((tm, D), lambda i: (i, 0)), pl.BlockSpec((tm, D), lambda i: (i, 0)),
             pl.BlockSpec((N_EXPERTS, tm), lambda i: (0, i))]
    shapes = [jax.ShapeDtypeStruct((T, D), F32), jax.ShapeDtypeStruct((T, D), F32),
              jax.ShapeDtypeStruct((N_EXPERTS, T), F32)]
    return specs, shapes


def outproj_even(proj, a_ln_g, a_ln_b, a_w_s, a_b_s, o_f, o_b, b_norm_g, w_out, x, ffn_g, w_router, tm=512):
    T, D = x.shape
    W = o_f.shape[1]
    row = lambda w, col=0: pl.BlockSpec((tm, w), lambda i: (i, col))
    full = lambda s: pl.BlockSpec(s, lambda i: (0,) * len(s))
    out_specs, out_shapes = _router_out(T, D, tm)
    return pl.pallas_call(
        _outproj_even_kernel,
        grid=(T // tm,),
        in_specs=[row(W, 0), row(W, 1), full((1, W)), full((1, W)), full(a_w_s.shape), full(a_b_s.T.shape),
                  row(W), row(W), row(W, 6), full((1, W)), full(w_out.shape), row(D),
                  full((1, D)), full(w_router.T.shape)],
        out_specs=out_specs,
        out_shape=out_shapes,
        compiler_params=_params("parallel"),
        name="outproj_even",
    )(proj, proj, a_ln_g.reshape(1, W), a_ln_b.reshape(1, W), a_w_s.astype(BF16), a_b_s.T,
      o_f, o_b, proj, b_norm_g.reshape(1, W), w_out.astype(BF16), x, ffn_g.reshape(1, D), w_router.T)


def _outproj_odd_kernel(*refs):
    P = len(DILATIONS)
    o_refs, l_refs = refs[:P], refs[P:2 * P]
    ex_ref, w_ref, x_ref, g_ref, wr_ref, xo_ref, h_ref, aff_ref, o_buf, l_buf = refs[2 * P:]
    tm = x_ref.shape[0]

    def token_order(ref, buf, d):
        if d == 1:
            return ref[0].astype(F32)
        tiles = buf.shape[0]
        for r in range(d):
            for t in range(tiles):
                lane0 = (r * tiles + t) * LANES
                buf[t, pl.ds(r, tm // d, stride=d), :] = ref[0, :, lane0:lane0 + LANES].astype(F32)
        return jnp.concatenate([buf[t] for t in range(tiles)], axis=-1)

    ls = [token_order(l_ref, l_buf, d)[:, 0:C_HEADS] for l_ref, d in zip(l_refs, DILATIONS)]
    m = functools.reduce(jnp.maximum, ls)
    es = [jnp.exp(l - m) for l in ls]
    den = functools.reduce(jnp.add, es)
    attn = None
    for e, o_ref, d in zip(es, o_refs, DILATIONS):
        wfull = _dot((e / den).astype(BF16), ex_ref[...])
        term = wfull * token_order(o_ref, o_buf, d)
        attn = term if attn is None else attn + term
    mixed = _dot(attn.astype(BF16), w_ref[...])
    _router_epilogue(x_ref[...] + mixed, g_ref, wr_ref, xo_ref, h_ref, aff_ref)


def outproj_odd(outs, lses, w_o, x, ffn_g, w_router, tm=512):
    T, D = x.shape
    tiles = outs[0].shape[1] // tm
    expand = jnp.asarray(np.kron(np.eye(C_HEADS), np.ones((1, C_HEAD_DIM))), BF16)
    dil = lambda w: [pl.BlockSpec((1, tm // d, d * w), lambda i: (i // tiles, i % tiles, 0)) for d in DILATIONS]
    full = lambda s: pl.BlockSpec(s, lambda i: (0, 0))
    out_specs, out_shapes = _router_out(T, D, tm)
    return pl.pallas_call(
        _outproj_odd_kernel,
        grid=(T // tm,),
        in_specs=dil(D) + dil(LANES) + [full(expand.shape), full(w_o.shape),
                                        pl.BlockSpec((tm, D), lambda i: (i, 0)), full((1, D)),
                                        full(w_router.T.shape)],
        out_specs=out_specs,
        out_shape=out_shapes,
        scratch_shapes=[pltpu.VMEM((D // LANES, tm, LANES), F32), pltpu.VMEM((1, tm, LANES), F32)],
        compiler_params=_params("parallel"),
        name="outproj_odd",
    )(*outs, *lses, expand, w_o.astype(BF16), x, ffn_g.reshape(1, D), w_router.T)


def _topk_kernel(aff_ref, upper_ref, strict_ref, pos_ref, tile_ref, *, cap):
    a = aff_ref[...]
    E, NC, _ = a.shape

    def count(mask):
        return jnp.sum(jnp.sum(mask.astype(F32), axis=2, keepdims=True), axis=1, keepdims=True)

    def as_float(bits):
        return lax.bitcast_convert_type(bits, jnp.float32)

    def search(i, thr):
        cand = thr | jnp.left_shift(jnp.int32(1), 30 - i)
        return jnp.where(count(a >= as_float(cand)) >= cap, cand, thr)

    thr = lax.fori_loop(0, 31, search, jnp.zeros((E, 1, 1), I32))
    gt = a >= as_float(thr + 1)
    eq = jnp.logical_and(a >= as_float(thr), jnp.logical_not(gt))
    need = cap - count(gt)

    def prefix(mask):
        m2 = mask.astype(BF16).reshape(E * NC, LANES)
        within = _dot(m2, upper_ref[...])
        total = within[:, LANES - 1:LANES].astype(BF16)
        tot_b = jnp.broadcast_to(total, (E * NC, LANES))
        starts = jnp.concatenate(
            [_dot(strict_ref[...], tot_b[e * NC:(e + 1) * NC]) for e in range(E)], axis=0)
        return (within + starts).reshape(E, NC, LANES), starts.reshape(E, NC, LANES)

    eq_incl, _ = prefix(eq)
    sel = gt | (eq & (eq_incl - 1.0 < need))
    sel_incl, starts = prefix(sel)
    pos_ref[0] = jnp.where(sel, sel_incl - 1.0, -1.0).astype(I32)
    tile = jnp.minimum(jnp.floor(starts * (1.0 / TOKEN_CHUNK)), float((cap - SLOT_WINDOW) // TOKEN_CHUNK))
    tile_ref[0] = tile.astype(I32)


def route_topk(aff_t, batch, cap):
    E, T = aff_t.shape
    B, S = batch, T // batch
    NC = S // TOKEN_CHUNK
    t = np.arange(LANES)
    upper = jnp.asarray(t[:, None] <= t[None, :], BF16)
    c = np.arange(NC)
    strict = jnp.asarray(c[None, :] < c[:, None], BF16)
    blk = pl.BlockSpec((1, E, NC, LANES), lambda b: (b, 0, 0, 0))
    pos, tiles = pl.pallas_call(
        functools.partial(_topk_kernel, cap=cap),
        grid=(B,),
        in_specs=[pl.BlockSpec((E, NC, LANES), lambda b: (0, b, 0)),
                  pl.BlockSpec(upper.shape, lambda b: (0, 0)), pl.BlockSpec(strict.shape, lambda b: (0, 0))],
        out_specs=[blk, blk],
        out_shape=[jax.ShapeDtypeStruct((B, E, NC, LANES), I32)] * 2,
        compiler_params=_params("parallel"),
        name="route_topk",
    )(aff_t.reshape(E, B * NC, LANES), upper, strict)
    return pos, tiles[..., 0].reshape(-1)


def _index_kernel(tiles_ref, pos_ref, sel_ref, idx_ref, acc_ref):
    b, e = pl.program_id(0), pl.program_id(1)
    n_exp = pl.num_programs(1)
    NC = pos_ref.shape[2]
    acc_ref[...] = jnp.zeros_like(acc_ref)
    slot = lax.broadcasted_iota(I32, (SLOT_WINDOW, TOKEN_CHUNK), 0)

    def body(c, carry):
        j = tiles_ref[(b * n_exp + e) * NC + c]
        onehot = jnp.where(slot + j * TOKEN_CHUNK == pos_ref[0, 0, pl.ds(c, 1), :], 1.0, 0.0).astype(BF16)
        hit = _dot_nt(sel_ref[...], onehot)
        tok = hit[0:1] + hit[1:2] * lax.convert_element_type(c * TOKEN_CHUNK, F32)
        acc_ref[j, 0:1, :] += tok[:, 0:TOKEN_CHUNK]
        acc_ref[j + 1, 0:1, :] += tok[:, TOKEN_CHUNK:]
        return carry

    lax.fori_loop(0, NC, body, 0, unroll=32)
    idx_ref[0, 0] = acc_ref[...].astype(I32)


def moe_slot_tokens(tiles, pos, cap):
    B, E, NC, _ = pos.shape
    nt = cap // TOKEN_CHUNK
    sel = np.zeros((8, TOKEN_CHUNK), np.float32)
    sel[0] = np.arange(TOKEN_CHUNK)
    sel[1] = 1.0
    idx = pl.pallas_call(
        _index_kernel,
        grid_spec=pltpu.PrefetchScalarGridSpec(
            num_scalar_prefetch=1,
            grid=(B, E),
            in_specs=[pl.BlockSpec((1, 1, NC, LANES), lambda b, e, s: (b, e, 0, 0)),
                      pl.BlockSpec((8, TOKEN_CHUNK), lambda b, e, s: (0, 0))],
            out_specs=pl.BlockSpec((1, 1, nt, 8, LANES), lambda b, e, s: (b, e, 0, 0, 0)),
            scratch_shapes=[pltpu.VMEM((nt, 8, LANES), F32)]),
        out_shape=jax.ShapeDtypeStruct((B, E, nt, 8, LANES), I32),
        compiler_params=_params("arbitrary", "arbitrary"),
        name="moe_slot_tokens",
    )(tiles, pos, jnp.asarray(sel, BF16))
    return idx[:, :, :, 0, :].reshape(-1)


def _ffn_kernel(idx_ref, h_ref, wg_ref, wu_ref, wd_ref, y_ref, x_buf, sem, wgb_ref, wub_ref, wdb_ref, *, tr):
    e, b = pl.program_id(0), pl.program_id(1)
    n_exp, nb = pl.num_programs(0), pl.num_programs(1)
    cap = x_buf.shape[1]
    step = e * nb + b

    def row_copy(st, buf, i):
        eb = st // nb
        bb = st - eb * nb
        tok = idx_ref[(bb * n_exp + eb) * cap + i]
        return pltpu.make_async_copy(h_ref.at[bb, pl.ds(tok, 1), :], x_buf.at[buf, pl.ds(i, 1), :], sem.at[buf])

    cur = jnp.bitwise_and(step, 1)

    @pl.when(step == 0)
    def _():
        lax.fori_loop(0, cap, lambda i, c: (row_copy(step, 0, i).start(), c)[1], 0, unroll=8)

    @pl.when(b == 0)
    def _():
        wgb_ref[...] = wg_ref[0, 0].astype(BF16)
        wub_ref[...] = wu_ref[0, 0].astype(BF16)
        wdb_ref[...] = wd_ref[0, 0].astype(BF16)

    pltpu.make_async_copy(h_ref.at[0, pl.ds(0, cap), :], x_buf.at[cur], sem.at[cur]).wait()

    def compute(prefetch):
        for r in range(cap // tr):
            rows = pl.ds(r * tr, tr)
            x = x_buf[cur, rows, :].astype(BF16)
            if prefetch:
                for i in range(r * tr, (r + 1) * tr):
                    row_copy(step + 1, 1 - cur, i).start()
            g = _dot(x, wgb_ref[...])
            u = _dot(x, wub_ref[...])
            mid = (g * jax.nn.sigmoid(g) * u).astype(BF16)
            y_ref[0, 0, rows, :] = _dot(mid, wdb_ref[...]).astype(BF16)

    is_last = step + 1 == n_exp * nb
    pl.when(jnp.logical_not(is_last))(lambda: compute(True))
    pl.when(is_last)(lambda: compute(False))


def moe_ffn(idx, h, w_gate, w_up, w_down, layer, cap, tr=256):
    B, S, D = h.shape
    E, F = w_gate.shape[1], w_gate.shape[-1]
    wspec = lambda r, c: pl.BlockSpec((1, 1, r, c), lambda e, b, s: (layer, e, 0, 0))
    return pl.pallas_call(
        functools.partial(_ffn_kernel, tr=min(tr, cap)),
        grid_spec=pltpu.PrefetchScalarGridSpec(
            num_scalar_prefetch=1,
            grid=(E, B),
            in_specs=[pl.BlockSpec(memory_space=pl.ANY), wspec(D, F), wspec(D, F), wspec(F, D)],
            out_specs=pl.BlockSpec((1, 1, cap, D), lambda e, b, s: (b, e, 0, 0)),
            scratch_shapes=[pltpu.VMEM((2, cap, D), F32), pltpu.SemaphoreType.DMA((2,)),
                            pltpu.VMEM((D, F), BF16), pltpu.VMEM((D, F), BF16), pltpu.VMEM((F, D), BF16)]),
        out_shape=jax.ShapeDtypeStruct((B, E, cap, D), BF16),
        compiler_params=_params("arbitrary", "arbitrary"),
        name="moe_ffn",
    )(idx, h, w_gate, w_up, w_down)


def _combine_kernel(tiles_ref, post_ref, gate_ref, x_ref, y_ref, *rest):
    g_ref, o_ref = rest if len(rest) == 2 else (None, rest[0])
    b, i = pl.program_id(0), pl.program_id(1)
    n_exp = y_ref.shape[1]
    per_step = x_ref.shape[1] // TOKEN_CHUNK
    NC = pl.num_programs(1) * per_step
    slot = lax.broadcasted_iota(I32, (TOKEN_CHUNK, SLOT_WINDOW), 1)
    for k in range(per_step):
        rows = slice(k * TOKEN_CHUNK, (k + 1) * TOKEN_CHUNK)
        acc = x_ref[0, rows, :]
        for e in range(n_exp):
            tile = tiles_ref[(b * n_exp + e) * NC + i * per_step + k]
            base = pl.multiple_of(tile * TOKEN_CHUNK, TOKEN_CHUNK)
            select = jnp.where(slot + base == post_ref[0, rows, e:e + 1], gate_ref[0, rows, e:e + 1], 0.0)
            acc = acc + _dot(select.astype(BF16), y_ref[0, e, pl.ds(base, SLOT_WINDOW), :])
        o_ref[0, rows, :] = acc if g_ref is None else _rms(acc, g_ref[...])


def moe_combine(tiles, pos_t, gate, x, y, out_norm=None, tm=512):
    B, S, D = x.shape
    E, cap = y.shape[1], y.shape[2]
    tok = lambda w: pl.BlockSpec((1, tm, w), lambda b, c, s: (b, c, 0))
    in_specs = [tok(E), tok(E), tok(D),
                pl.BlockSpec((1, E, cap, D), lambda b, c, s: (b, 0, 0, 0), pipeline_mode=pl.Buffered(1))]
    args = [tiles, pos_t, gate, x, y]
    if out_norm is not None:
        in_specs.append(pl.BlockSpec((1, D), lambda b, c, s: (0, 0)))
        args.append(out_norm.reshape(1, D))
    return pl.pallas_call(
        _combine_kernel,
        grid_spec=pltpu.PrefetchScalarGridSpec(
            num_scalar_prefetch=1,
            grid=(B, S // tm),
            in_specs=in_specs,
            out_specs=tok(D)),
        out_shape=jax.ShapeDtypeStruct((B, S, D), F32),
        compiler_params=_params("arbitrary", "arbitrary"),
        name="moe_combine",
    )(*args)


def expert_choice_moe(x, h, aff_t, batch, w_gate, w_up, w_down, layer, out_norm=None):
    T, D = x.shape
    S = T // batch
    cap = max(1, CAPACITY_FACTOR * S // N_EXPERTS)
    aff = aff_t.T.reshape(batch, S, N_EXPERTS)
    pos, tiles = route_topk(aff_t, batch, cap)
    idx = moe_slot_tokens(tiles, pos, cap)
    y = moe_ffn(idx, h.reshape(batch, S, D), w_gate, w_up, w_down, layer, cap)
    pos_t = jnp.swapaxes(pos.reshape(batch, N_EXPERTS, S), 1, 2)
    return moe_combine(tiles, pos_t, aff, x.reshape(batch, S, D), y, out_norm).reshape(T, D)


ATTN_TQ = 128
ATTN_TK = ATTN_TQ + 2 * HALF_WINDOW
ATTN_SUB = 2


def _t5_bucket(rel):
    half_buckets = REL_BUCKETS // 2
    max_exact = half_buckets // 2
    n = jnp.abs(rel)
    scaled = (jnp.log(jnp.maximum(n, 1).astype(jnp.float32) / max_exact)
              / math.log(REL_MAX_DISTANCE / max_exact))
    large = jnp.minimum(max_exact + jnp.floor(scaled * (half_buckets - max_exact)).astype(jnp.int32),
                        half_buckets - 1)
    return jnp.where(rel > 0, half_buckets, 0) + jnp.where(n < max_exact, n, large)


def _bias_kernel(table_ref, bucket_ref, o_ref):
    h = pl.program_id(1)
    bucket = bucket_ref[0]
    acc = jnp.zeros(bucket.shape, F32)
    for bk in range(REL_BUCKETS):
        acc = jnp.where(bucket == bk, table_ref[bk * C_HEADS + h], acc)
    q = lax.broadcasted_iota(I32, bucket.shape, 0)
    kc = lax.broadcasted_iota(I32, bucket.shape, 1)
    middle = jnp.where(jnp.abs(kc - HALF_WINDOW - q) <= HALF_WINDOW, acc, NEG_INF)
    o_ref[0, 0, 0] = jnp.where(kc >= HALF_WINDOW, middle, NEG_INF)
    o_ref[0, 1, 0] = middle
    o_ref[0, 2, 0] = jnp.where(kc < HALF_WINDOW + ATTN_TQ, middle, NEG_INF)


def attention_bias(rel_bias):
    rel = np.arange(ATTN_TK)[None, :] - HALF_WINDOW - np.arange(ATTN_TQ)[:, None]
    buckets = jnp.stack([_t5_bucket(jnp.asarray(rel * d, I32)) for d in DILATIONS]).astype(I32)
    P = len(DILATIONS)
    return pl.pallas_call(
        _bias_kernel,
        grid_spec=pltpu.PrefetchScalarGridSpec(
            num_scalar_prefetch=1,
            grid=(P, C_HEADS),
            in_specs=[pl.BlockSpec((1, ATTN_TQ, ATTN_TK), lambda p, h, t: (p, 0, 0))],
            out_specs=pl.BlockSpec((1, 3, 1, ATTN_TQ, ATTN_TK), lambda p, h, t: (p, 0, h, 0, 0))),
        out_shape=jax.ShapeDtypeStruct((P, 3, C_HEADS, ATTN_TQ, ATTN_TK), F32),
        compiler_params=_params("arbitrary", "arbitrary"),
        name="attention_bias",
    )(rel_bias.reshape(-1), buckets)


def _attn_kernel(q_ref, kp_ref, km_ref, kn_ref, vp_ref, vm_ref, vn_ref, bias_ref, o_ref, lse_ref,
                 k_buf, v_buf, s_buf, p_buf):
    i = pl.program_id(2)
    last = pl.num_programs(2) - 1
    hw = HALF_WINDOW
    rows = q_ref.shape[1]
    k_buf[0:hw] = kp_ref[0]
    k_buf[hw:hw + rows] = km_ref[0]
    k_buf[hw + rows:] = kn_ref[0]
    v_buf[0:hw] = vp_ref[0]
    v_buf[hw:hw + rows] = vm_ref[0]
    v_buf[hw + rows:] = vn_ref[0]
    first_head = lax.broadcasted_iota(I32, (ATTN_TQ, LANES), 1) < C_HEAD_DIM
    scale = C_HEAD_DIM ** -0.5
    lse_ref[...] = jnp.zeros_like(lse_ref)
    for sub in range(rows // ATTN_TQ):
        qrows = slice(sub * ATTN_TQ, (sub + 1) * ATTN_TQ)
        krows = slice(sub * ATTN_TQ, sub * ATTN_TQ + ATTN_TK)
        if sub == 0:
            variant = jnp.where(i == 0, 0, 1)
        elif sub == rows // ATTN_TQ - 1:
            variant = jnp.where(i == last, 2, 1)
        else:
            variant = 1
        for pair in range(C_HEADS // 2):
            cols = slice(pair * LANES, (pair + 1) * LANES)
            q = q_ref[0, qrows, cols] * scale
            for half in range(2):
                mine = first_head if half == 0 else jnp.logical_not(first_head)
                s_buf[2 * pair + half] = (_dot_nt(jnp.where(mine, q, jnp.zeros_like(q)), k_buf[krows, cols])
                                          + bias_ref[variant, 2 * pair + half])
        for h in range(C_HEADS):
            s = s_buf[h]
            m = jnp.max(s, axis=-1, keepdims=True)
            p = jnp.exp(s - m)
            den = jnp.sum(p, axis=-1, keepdims=True)
            p_buf[h] = (p / den).astype(BF16)
            lse_ref[0, qrows, h:h + 1] = m + jnp.log(den)
        for pair in range(C_HEADS // 2):
            cols = slice(pair * LANES, (pair + 1) * LANES)
            o_ref[0, qrows, cols] = jnp.where(first_head, _dot(p_buf[2 * pair], v_buf[krows, cols]),
                                              _dot(p_buf[2 * pair + 1], v_buf[krows, cols])).astype(BF16)


def dilated_attention(view, bias, d):
    batch, n, D3 = view.shape
    D = D3 // d // 3
    rows = ATTN_TQ * ATTN_SUB
    nb = n // HALF_WINDOW
    r = rows // HALF_WINDOW
    assert n // rows >= 1 and ATTN_SUB >= 2
    main = lambda c: pl.BlockSpec((1, rows, D), lambda b, j, i: (b, i, 3 * j + c))
    prev = lambda c: pl.BlockSpec((1, HALF_WINDOW, D), lambda b, j, i: (b, jnp.maximum(i * r - 1, 0), 3 * j + c))
    nxt = lambda c: pl.BlockSpec((1, HALF_WINDOW, D),
                                 lambda b, j, i: (b, jnp.minimum(i * r + r, nb - 1), 3 * j + c))
    return pl.pallas_call(
        _attn_kernel,
        grid=(batch, d, n // rows),
        in_specs=[main(0), prev(1), main(1), nxt(1), prev(2), main(2), nxt(2),
                  pl.BlockSpec(bias.shape, lambda b, j, i: (0, 0, 0, 0))],
        out_specs=[pl.BlockSpec((1, rows, D), lambda b, j, i: (b, i, j)),
                   pl.BlockSpec((1, rows, LANES), lambda b, j, i: (b, i, j))],
        out_shape=[jax.ShapeDtypeStruct((batch, n, d * D), BF16),
                   jax.ShapeDtypeStruct((batch, n, d * LANES), F32)],
        scratch_shapes=[pltpu.VMEM((rows + 2 * HALF_WINDOW, D), BF16), pltpu.VMEM((rows + 2 * HALF_WINDOW, D), BF16),
                        pltpu.VMEM((C_HEADS, ATTN_TQ, ATTN_TK), F32), pltpu.VMEM((C_HEADS, ATTN_TQ, ATTN_TK), BF16)],
        compiler_params=_params("parallel", "parallel", "parallel"),
        name="dilated_attention",
    )(view, view, view, view, view, view, view, bias)


def dilated_mixture(views, rel_bias):
    bias = attention_bias(rel_bias)
    results = [dilated_attention(view, bias[p], d) for p, (view, d) in enumerate(zip(views, DILATIONS))]
    return [o for o, _ in results], [lse for _, lse in results]


def kernel(x, mix_norm, ffn_norm, final_norm, w_in_even, w_out_even, a_ln_g, a_ln_b, a_w_s, a_b_s, b_lb_table,
           b_norm_g, w_qkv_odd, w_o_odd, rel_bias, w_router, w_gate, w_up, w_down):
    B, S, D = x.shape
    depth = mix_norm.shape[0]
    xt = x.reshape(B * S, D)
    for layer in range(depth):
        j = layer // 2
        if layer % 2 == 0:
            proj = norm_matmul(xt, mix_norm[layer], w_in_even[j].astype(BF16), B)[0].reshape(B * S, -1)
            o_f, o_b = mixer_b(proj, b_lb_table, layer, B)
            xt, h, aff = outproj_even(proj, a_ln_g[j], a_ln_b[j], a_w_s[j], a_b_s[j], o_f, o_b, b_norm_g[j],
                                      w_out_even[j], xt, ffn_norm[layer], w_router[layer])
        else:
            views = norm_matmul(xt, mix_norm[layer], w_qkv_odd[j].astype(BF16), B, DILATIONS, tm=512)
            outs, lses = dilated_mixture(views, rel_bias)
            xt, h, aff = outproj_odd(outs, lses, w_o_odd[j], xt, ffn_norm[layer], w_router[layer])
        xt = expert_choice_moe(xt, h, aff, B, w_gate, w_up, w_down, layer,
                               out_norm=final_norm if layer == depth - 1 else None)
    return xt.reshape(B, S, D)
```

```python
import functools
import math

import numpy as np
import jax
import jax.numpy as jnp
from jax import lax
from jax.experimental import pallas as pl
from jax.experimental.pallas import tpu as pltpu

F32 = jnp.float32
BF16 = jnp.bfloat16
I32 = jnp.int32
EPS = 1e-6
NEG_INF = -1e30

LANES = 128
VMEM_LIMIT = 56 * 1024 * 1024

A_GROUPS = 4
A_CHUNK = 128
B_HEADS = 4
B_DIM = 128
GLA_CHUNK = 128
C_HEADS = 16
C_HEAD_DIM = 64
HALF_WINDOW = 64
DILATIONS = (1, 4, 16)
REL_BUCKETS = 32
REL_MAX_DISTANCE = 1024
N_EXPERTS = 16
CAPACITY_FACTOR = 2
TOKEN_CHUNK = 128
SLOT_WINDOW = 256


def _params(*sem):
    return pltpu.CompilerParams(dimension_semantics=sem, vmem_limit_bytes=VMEM_LIMIT)


def _dot(a, b, **kw):
    return jnp.dot(a, b, preferred_element_type=F32, **kw)


def _dot_nt(a, b):
    return lax.dot_general(a, b, (((1,), (1,)), ((), ())), preferred_element_type=F32)


def _dot_tn(a, b):
    return lax.dot_general(a, b, (((0,), (0,)), ((), ())), preferred_element_type=F32)


def _dot_split01(a2, x):
    hi = x.astype(BF16)
    lo = (x - hi.astype(F32)).astype(BF16)
    return _dot(a2, jnp.concatenate([hi, lo], axis=0))


def _sigmoid(x):
    return 0.5 * jnp.tanh(0.5 * x) + 0.5


def _rms(x, g):
    return x * lax.rsqrt(jnp.mean(x * x, axis=-1, keepdims=True) + EPS) * g


def _norm_matmul_kernel(x_ref, g_ref, w_ref, *rest, tn, dilations):
    o_refs, scratch = rest[:len(dilations)], rest[len(dilations):]
    tm, N = x_ref.shape[0], w_ref.shape[1]
    h = _rms(x_ref[...], g_ref[...]).astype(BF16)
    for j in range(N // tn):
        res = _dot(h, w_ref[:, j * tn:(j + 1) * tn])
        if scratch:
            for t in range(tn // LANES):
                scratch[0][t] = res[:, t * LANES:(t + 1) * LANES]
        for o_ref, d in zip(o_refs, dilations):
            if d == 1:
                o_ref[0, :, j * tn:(j + 1) * tn] = res.astype(BF16)
                continue
            for r in range(d):
                for t in range(tn // LANES):
                    lane0 = r * N + j * tn + t * LANES
                    o_ref[0, :, lane0:lane0 + LANES] = (
                        scratch[0][t, pl.ds(r, tm // d, stride=d), :].astype(BF16))


def norm_matmul(x, g, w, batch, dilations=(1,), tm=1024, tn=512):
    T, D = x.shape
    N = w.shape[1]
    S = T // batch
    tiles = S // tm
    strided = any(d > 1 for d in dilations)
    return pl.pallas_call(
        functools.partial(_norm_matmul_kernel, tn=tn, dilations=dilations),
        grid=(T // tm,),
        in_specs=[
            pl.BlockSpec((tm, D), lambda i: (i, 0)),
            pl.BlockSpec((1, D), lambda i: (0, 0)),
            pl.BlockSpec((D, N), lambda i: (0, 0)),
        ],
        out_specs=[pl.BlockSpec((1, tm // d, d * N), lambda i: (i // tiles, i % tiles, 0)) for d in dilations],
        out_shape=[jax.ShapeDtypeStruct((batch, S // d, d * N), BF16) for d in dilations],
        scratch_shapes=[pltpu.VMEM((tn // LANES, tm, LANES), F32)] if strided else [],
        compiler_params=_params("parallel"),
        name="norm_matmul",
    )(x, g.reshape(1, D), w)


def _spatial_gating(u_ref, v_ref, lg_ref, lb_ref, ws_ref, bs_ref):
    tm = u_ref.shape[0]
    u = jax.nn.gelu(u_ref[...].astype(F32))
    v = jax.nn.gelu(v_ref[...].astype(F32))
    mu = jnp.mean(v, axis=-1, keepdims=True)
    vc = v - mu
    vn = vc * lax.rsqrt(jnp.mean(vc * vc, axis=-1, keepdims=True) + EPS)
    vb = (vn * lg_ref[...] + lb_ref[...]).astype(BF16)
    rows_out = []
    for n in range(tm // A_CHUNK):
        rows = slice(n * A_CHUNK, (n + 1) * A_CHUNK)
        groups = []
        for g in range(A_GROUPS):
            cols = slice(g * LANES, (g + 1) * LANES)
            mixed = _dot(ws_ref[g], vb[rows, cols]) + bs_ref[:, g:g + 1]
            groups.append((u[rows, cols] * mixed).astype(BF16))
        rows_out.append(jnp.concatenate(groups, axis=-1))
    return jnp.concatenate(rows_out, axis=0)


def _gla_consts(C, reverse):
    t = np.arange(C)[:, None]
    r = np.arange(C)[None, :]
    L = int(round(math.log2(C)))
    spans = [(r >= t) if reverse else (r <= t)]
    level = np.where(np.eye(C, dtype=bool), L, -1).astype(np.int32)
    for l in range(L):
        bs = (t >> (l + 1)) << (l + 1)
        mid = bs + (1 << l)
        if reverse:
            act_q = t < mid
            span = np.where(act_q, (r >= t) & (r < mid), (r >= mid) & (r < t))
        else:
            act_q = t >= mid
            span = np.where(act_q, (r >= mid) & (r <= t), (r > t) & (r < mid))
        if l > 0:
            spans.append(span)
        level[(bs == bs.T) & act_q & ~act_q.T] = l
    return np.tile(np.concatenate(spans, axis=0).astype(np.float32), (1, 2)), level


def _gla_kernel(qf_ref, ff_ref, if_ref, qb_ref, fb_ref, ib_ref, tbl_ref, span_ref, level_ref,
                of_ref, ob_ref, state_ref, w_buf, q_buf, k_buf, v_buf, s_buf, p_buf, *, layer, C):
    @pl.when(pl.program_id(1) == 0)
    def _():
        state_ref[...] = jnp.zeros_like(state_ref)

    L = span_ref.shape[1] // C
    subs = qf_ref.shape[0] // C
    odd_row = jnp.bitwise_and(lax.broadcasted_iota(I32, (C, qf_ref.shape[1]), 0), 1) == 1

    def chunk_rows(sub, d):
        k = sub if d == 0 else subs - 1 - sub
        return slice(k * C, (k + 1) * C)

    groups = [(sub, d) for sub in range(subs) for d in range(2)]
    heads = [(sub * 2 + d, sub, d, h, slice(h * B_DIM, (h + 1) * B_DIM)) for sub, d in groups for h in range(B_HEADS)]
    tots = {}
    for sub, d in groups:
        g = sub * 2 + d
        q_ref, f_ref, i_ref = (qf_ref, ff_ref, if_ref) if d == 0 else (qb_ref, fb_ref, ib_ref)
        rows = chunk_rows(sub, d)
        tb = tbl_ref[d]
        e = jnp.exp(tb - jnp.max(tb, axis=0, keepdims=True))
        lb = jnp.sum(e[0:layer + 1], axis=0, keepdims=True) / jnp.sum(e, axis=0, keepdims=True)
        f = lb + (1.0 - lb) * _sigmoid(f_ref[rows, :].astype(F32))
        e2 = _dot_split01(span_ref[d], jnp.log2(f))
        tot2 = e2[0:1] if d == 1 else e2[C - 1:C]
        tots[g] = jnp.exp2(tot2)
        w_buf[g, 0:C] = jnp.exp2(e2[0:C]).astype(BF16)
        w_buf[g, C:2 * C] = jnp.exp2(tot2 - e2[0:C]).astype(BF16)
        w_buf[g, 2 * C:3 * C] = jnp.where(odd_row if d == 0 else jnp.logical_not(odd_row), f, 1.0).astype(BF16)
        w_buf[g, 3 * C:] = jnp.exp2(e2[C:]).astype(BF16)
        qr = q_ref[rows, :].astype(F32)
        q_buf[g] = (qr * _sigmoid(qr)).astype(BF16)
        k_buf[g] = (1.0 - f).astype(BF16)
        v_buf[g] = i_ref[rows, :]
    for g, sub, d, h, sl in heads:
        qh, kh = q_buf[g, :, sl], k_buf[g, :, sl]
        s_buf[g, h, L] = _dot_nt(qh, kh)
        for l in range(L):
            wl = w_buf[g, (l + 2) * C:(l + 3) * C, sl]
            s_buf[g, h, l] = _dot_nt(qh * wl, kh * wl)
    for g, sub, d, h, sl in heads:
        level = level_ref[d]
        p = jnp.where(level == L, s_buf[g, h, L], 0.0)
        for l in range(L):
            p = jnp.where(level == l, s_buf[g, h, l], p)
        p_buf[g, h] = p.astype(BF16)
    for g, sub, d, h, sl in heads:
        o_ref = of_ref if d == 0 else ob_ref
        st = state_ref[d, h]
        vh = v_buf[g, :, sl]
        o_ref[chunk_rows(sub, d), sl] = (_dot(p_buf[g, h], vh) + _dot_nt(
            q_buf[g, :, sl] * w_buf[g, 0:C, sl], st.astype(BF16))).astype(o_ref.dtype)
        state_ref[d, h] = st * tots[g][:, sl] + _dot_tn(vh, k_buf[g, :, sl] * w_buf[g, C:2 * C, sl])


def mixer_b(proj, b_lb_table, layer, batch, C=GLA_CHUNK, subs=4):
    T = proj.shape[0]
    W = B_HEADS * B_DIM
    R = C * subs
    n = T // batch // R
    consts = [_gla_consts(C, rev) for rev in (False, True)]
    span = jnp.asarray(np.stack([c[0] for c in consts]), BF16)
    level = jnp.asarray(np.stack([c[1] for c in consts]))
    fwd = lambda col: pl.BlockSpec((R, W), lambda b, c: (b * n + c, col))
    bwd = lambda col: pl.BlockSpec((R, W), lambda b, c: (b * n + n - 1 - c, col))
    full = lambda a: pl.BlockSpec(a.shape, lambda b, c: (0,) * a.ndim)
    G = 2 * subs
    return pl.pallas_call(
        functools.partial(_gla_kernel, layer=layer, C=C),
        grid=(batch, n),
        in_specs=[fwd(2), fwd(3), fwd(5), bwd(2), bwd(4), bwd(5),
                  full(b_lb_table), full(span), full(level)],
        out_specs=[pl.BlockSpec((R, W), lambda b, c: (b * n + c, 0)),
                   pl.BlockSpec((R, W), lambda b, c: (b * n + n - 1 - c, 0))],
        out_shape=[jax.ShapeDtypeStruct((T, W), BF16)] * 2,
        scratch_shapes=[pltpu.VMEM((2, B_HEADS, B_DIM, B_DIM), F32),
                        pltpu.VMEM((G, span.shape[1] + 2 * C, W), BF16),
                        pltpu.VMEM((G, C, W), BF16), pltpu.VMEM((G, C, W), BF16), pltpu.VMEM((G, C, W), BF16),
                        pltpu.VMEM((G, B_HEADS, span.shape[1] // C + 1, C, C), F32),
                        pltpu.VMEM((G, B_HEADS, C, C), BF16)],
        compiler_params=_params("arbitrary", "arbitrary"),
        name="mixer_b",
    )(proj, proj, proj, proj, proj, proj, b_lb_table, span, level)


def _router_epilogue(x_new, g_ref, wr_ref, x_ref, h_ref, aff_ref):
    x_ref[...] = x_new
    h = _rms(x_new, g_ref[...])
    h_ref[...] = h
    w = wr_ref[...]
    n_exp = w.shape[0]
    h_hi, w_hi = h.astype(BF16), w.astype(BF16)
    h_lo, w_lo = (h - h_hi.astype(F32)).astype(BF16), (w - w_hi.astype(F32)).astype(BF16)
    both = _dot_nt(jnp.concatenate([w_hi, w_lo], axis=0), h_hi)
    logits = both[0:n_exp] + both[n_exp:] + _dot_nt(w_hi, h_lo)
    e = jnp.exp(logits - jnp.max(logits, axis=0, keepdims=True))
    aff_ref[...] = e / jnp.sum(e, axis=0, keepdims=True)


def _outproj_even_kernel(u_ref, v_ref, lg_ref, lb_ref, ws_ref, bs_ref, of_ref, ob_ref, gate_ref, bng_ref, w_ref,
                         x_ref, g_ref, wr_ref, xo_ref, h_ref, aff_ref):
    a = _spatial_gating(u_ref, v_ref, lg_ref, lb_ref, ws_ref, bs_ref)
    o = of_ref[...].astype(F32) + ob_ref[...].astype(F32)
    parts = []
    for h in range(B_HEADS):
        oh = o[:, h * B_DIM:(h + 1) * B_DIM]
        parts.append(oh * lax.rsqrt(jnp.mean(oh * oh, axis=-1, keepdims=True) + EPS))
    on = jnp.concatenate(parts, axis=-1) * bng_ref[...] * jax.nn.sigmoid(gate_ref[...].astype(F32))
    wa = a.shape[1]
    mixed = _dot(a, w_ref[0:wa, :]) + _dot(on.astype(BF16), w_ref[wa:, :])
    _router_epilogue(x_ref[...] + mixed, g_ref, wr_ref, xo_ref, h_ref, aff_ref)


def _router_out(T, D, tm):
    specs = [pl.BlockSpec((tm, D), lambda i: (i, 0)), pl.BlockSpec((tm, D), lambda i: (i, 0)),
             pl.BlockSpec((N_EXPERTS, tm), lambda i: (0, i))]
    shapes = [jax.ShapeDtypeStruct((T, D), F32), jax.ShapeDtypeStruct((T, D), F32),
              jax.ShapeDtypeStruct((N_EXPERTS, T), F32)]
    return specs, shapes


def outproj_even(proj, a_ln_g, a_ln_b, a_w_s, a_b_s, o_f, o_b, b_norm_g, w_out, x, ffn_g, w_router, tm=512):
    T, D = x.shape
    W = o_f.shape[1]
    row = lambda w, col=0: pl.BlockSpec((tm, w), lambda i: (i, col))
    full = lambda s: pl.BlockSpec(s, lambda i: (0,) * len(s))
    out_specs, out_shapes = _router_out(T, D, tm)
    return pl.pallas_call(
        _outproj_even_kernel,
        grid=(T // tm,),
        in_specs=[row(W, 0), row(W, 1), full((1, W)), full((1, W)), full(a_w_s.shape), full(a_b_s.T.shape),
                  row(W), row(W), row(W, 6), full((1, W)), full(w_out.shape), row(D),
                  full((1, D)), full(w_router.T.shape)],
        out_specs=out_specs,
        out_shape=out_shapes,
        compiler_params=_params("parallel"),
        name="outproj_even",
    )(proj, proj, a_ln_g.reshape(1, W), a_ln_b.reshape(1, W), a_w_s.astype(BF16), a_b_s.T,
      o_f, o_b, proj, b_norm_g.reshape(1, W), w_out.astype(BF16), x, ffn_g.reshape(1, D), w_router.T)


def _outproj_odd_kernel(*refs):
    P = len(DILATIONS)
    o_refs, l_refs = refs[:P], refs[P:2 * P]
    ex_ref, w_ref, x_ref, g_ref, wr_ref, xo_ref, h_ref, aff_ref, o_buf, l_buf = refs[2 * P:]
    tm = x_ref.shape[0]

    def token_order(ref, buf, d):
        if d == 1:
            return ref[0].astype(F32)
        tiles = buf.shape[0]
        for r in range(d):
            for t in range(tiles):
                lane0 = (r * tiles + t) * LANES
                buf[t, pl.ds(r, tm // d, stride=d), :] = ref[0, :, lane0:lane0 + LANES].astype(F32)
        return jnp.concatenate([buf[t] for t in range(tiles)], axis=-1)

    ls = [token_order(l_ref, l_buf, d)[:, 0:C_HEADS] for l_ref, d in zip(l_refs, DILATIONS)]
    m = functools.reduce(jnp.maximum, ls)
    es = [jnp.exp(l - m) for l in ls]
    den = functools.reduce(jnp.add, es)
    attn = None
    for e, o_ref, d in zip(es, o_refs, DILATIONS):
        wfull = _dot((e / den).astype(BF16), ex_ref[...])
        term = wfull * token_order(o_ref, o_buf, d)
        attn = term if attn is None else attn + term
    mixed = _dot(attn.astype(BF16), w_ref[...])
    _router_epilogue(x_ref[...] + mixed, g_ref, wr_ref, xo_ref, h_ref, aff_ref)


def outproj_odd(outs, lses, w_o, x, ffn_g, w_router, tm=512):
    T, D = x.shape
    tiles = outs[0].shape[1] // tm
    expand = jnp.asarray(np.kron(np.eye(C_HEADS), np.ones((1, C_HEAD_DIM))), BF16)
    dil = lambda w: [pl.BlockSpec((1, tm // d, d * w), lambda i: (i // tiles, i % tiles, 0)) for d in DILATIONS]
    full = lambda s: pl.BlockSpec(s, lambda i: (0, 0))
    out_specs, out_shapes = _router_out(T, D, tm)
    return pl.pallas_call(
        _outproj_odd_kernel,
        grid=(T // tm,),
        in_specs=dil(D) + dil(LANES) + [full(expand.shape), full(w_o.shape),
                                        pl.BlockSpec((tm, D), lambda i: (i, 0)), full((1, D)),
                                        full(w_router.T.shape)],
        out_specs=out_specs,
        out_shape=out_shapes,
        scratch_shapes=[pltpu.VMEM((D // LANES, tm, LANES), F32), pltpu.VMEM((1, tm, LANES), F32)],
        compiler_params=_params("parallel"),
        name="outproj_odd",
    )(*outs, *lses, expand, w_o.astype(BF16), x, ffn_g.reshape(1, D), w_router.T)


def _topk_kernel(aff_ref, upper_ref, strict_ref, pos_ref, tile_ref, *, cap):
    a = aff_ref[...]
    E, NC, _ = a.shape

    def count(mask):
        return jnp.sum(jnp.sum(mask.astype(F32), axis=2, keepdims=True), axis=1, keepdims=True)

    def as_float(bits):
        return lax.bitcast_convert_type(bits, jnp.float32)

    def search(i, thr):
        cand = thr | jnp.left_shift(jnp.int32(1), 30 - i)
        return jnp.where(count(a >= as_float(cand)) >= cap, cand, thr)

    thr = lax.fori_loop(0, 31, search, jnp.zeros((E, 1, 1), I32))
    gt = a >= as_float(thr + 1)
    eq = jnp.logical_and(a >= as_float(thr), jnp.logical_not(gt))
    need = cap - count(gt)

    def prefix(mask):
        m2 = mask.astype(BF16).reshape(E * NC, LANES)
        within = _dot(m2, upper_ref[...])
        total = within[:, LANES - 1:LANES].astype(BF16)
        tot_b = jnp.broadcast_to(total, (E * NC, LANES))
        starts = jnp.concatenate(
            [_dot(strict_ref[...], tot_b[e * NC:(e + 1) * NC]) for e in range(E)], axis=0)
        return (within + starts).reshape(E, NC, LANES), starts.reshape(E, NC, LANES)

    eq_incl, _ = prefix(eq)
    sel = gt | (eq & (eq_incl - 1.0 < need))
    sel_incl, starts = prefix(sel)
    pos_ref[0] = jnp.where(sel, sel_incl - 1.0, -1.0).astype(I32)
    tile = jnp.minimum(jnp.floor(starts * (1.0 / TOKEN_CHUNK)), float((cap - SLOT_WINDOW) // TOKEN_CHUNK))
    tile_ref[0] = tile.astype(I32)


def route_topk(aff_t, batch, cap):
    E, T = aff_t.shape
    B, S = batch, T // batch
    NC = S // TOKEN_CHUNK
    t = np.arange(LANES)
    upper = jnp.asarray(t[:, None] <= t[None, :], BF16)
    c = np.arange(NC)
    strict = jnp.asarray(c[None, :] < c[:, None], BF16)
    blk = pl.BlockSpec((1, E, NC, LANES), lambda b: (b, 0, 0, 0))
    pos, tiles = pl.pallas_call(
        functools.partial(_topk_kernel, cap=cap),
        grid=(B,),
        in_specs=[pl.BlockSpec((E, NC, LANES), lambda b: (0, b, 0)),
                  pl.BlockSpec(upper.shape, lambda b: (0, 0)), pl.BlockSpec(strict.shape, lambda b: (0, 0))],
        out_specs=[blk, blk],
        out_shape=[jax.ShapeDtypeStruct((B, E, NC, LANES), I32)] * 2,
        compiler_params=_params("parallel"),
        name="route_topk",
    )(aff_t.reshape(E, B * NC, LANES), upper, strict)
    return pos, tiles[..., 0].reshape(-1)


def _index_kernel(tiles_ref, pos_ref, sel_ref, idx_ref, acc_ref):
    b, e = pl.program_id(0), pl.program_id(1)
    n_exp = pl.num_programs(1)
    NC = pos_ref.shape[2]
    acc_ref[...] = jnp.zeros_like(acc_ref)
    slot = lax.broadcasted_iota(I32, (SLOT_WINDOW, TOKEN_CHUNK), 0)

    def body(c, carry):
        j = tiles_ref[(b * n_exp + e) * NC + c]
        onehot = jnp.where(slot + j * TOKEN_CHUNK == pos_ref[0, 0, pl.ds(c, 1), :], 1.0, 0.0).astype(BF16)
        hit = _dot_nt(sel_ref[...], onehot)
        tok = hit[0:1] + hit[1:2] * lax.convert_element_type(c * TOKEN_CHUNK, F32)
        acc_ref[j, 0:1, :] += tok[:, 0:TOKEN_CHUNK]
        acc_ref[j + 1, 0:1, :] += tok[:, TOKEN_CHUNK:]
        return carry

    lax.fori_loop(0, NC, body, 0, unroll=32)
    idx_ref[0, 0] = acc_ref[...].astype(I32)


def moe_slot_tokens(tiles, pos, cap):
    B, E, NC, _ = pos.shape
    nt = cap // TOKEN_CHUNK
    sel = np.zeros((8, TOKEN_CHUNK), np.float32)
    sel[0] = np.arange(TOKEN_CHUNK)
    sel[1] = 1.0
    idx = pl.pallas_call(
        _index_kernel,
        grid_spec=pltpu.PrefetchScalarGridSpec(
            num_scalar_prefetch=1,
            grid=(B, E),
            in_specs=[pl.BlockSpec((1, 1, NC, LANES), lambda b, e, s: (b, e, 0, 0)),
                      pl.BlockSpec((8, TOKEN_CHUNK), lambda b, e, s: (0, 0))],
            out_specs=pl.BlockSpec((1, 1, nt, 8, LANES), lambda b, e, s: (b, e, 0, 0, 0)),
            scratch_shapes=[pltpu.VMEM((nt, 8, LANES), F32)]),
        out_shape=jax.ShapeDtypeStruct((B, E, nt, 8, LANES), I32),
        compiler_params=_params("arbitrary", "arbitrary"),
        name="moe_slot_tokens",
    )(tiles, pos, jnp.asarray(sel, BF16))
    return idx[:, :, :, 0, :].reshape(-1)


def _ffn_kernel(idx_ref, h_ref, wg_ref, wu_ref, wd_ref, y_ref, x_buf, sem, wgb_ref, wub_ref, wdb_ref, *, tr):
    e, b = pl.program_id(0), pl.program_id(1)
    n_exp, nb = pl.num_programs(0), pl.num_programs(1)
    cap = x_buf.shape[1]
    step = e * nb + b

    def row_copy(st, buf, i):
        eb = st // nb
        bb = st - eb * nb
        tok = idx_ref[(bb * n_exp + eb) * cap + i]
        return pltpu.make_async_copy(h_ref.at[bb, pl.ds(tok, 1), :], x_buf.at[buf, pl.ds(i, 1), :], sem.at[buf])

    cur = jnp.bitwise_and(step, 1)

    @pl.when(step == 0)
    def _():
        lax.fori_loop(0, cap, lambda i, c: (row_copy(step, 0, i).start(), c)[1], 0, unroll=8)

    @pl.when(b == 0)
    def _():
        wgb_ref[...] = wg_ref[0, 0].astype(BF16)
        wub_ref[...] = wu_ref[0, 0].astype(BF16)
        wdb_ref[...] = wd_ref[0, 0].astype(BF16)

    pltpu.make_async_copy(h_ref.at[0, pl.ds(0, cap), :], x_buf.at[cur], sem.at[cur]).wait()

    def compute(prefetch):
        for r in range(cap // tr):
            rows = pl.ds(r * tr, tr)
            x = x_buf[cur, rows, :].astype(BF16)
            if prefetch:
                for i in range(r * tr, (r + 1) * tr):
                    row_copy(step + 1, 1 - cur, i).start()
            g = _dot(x, wgb_ref[...])
            u = _dot(x, wub_ref[...])
            mid = (g * jax.nn.sigmoid(g) * u).astype(BF16)
            y_ref[0, 0, rows, :] = _dot(mid, wdb_ref[...]).astype(BF16)

    is_last = step + 1 == n_exp * nb
    pl.when(jnp.logical_not(is_last))(lambda: compute(True))
    pl.when(is_last)(lambda: compute(False))


def moe_ffn(idx, h, w_gate, w_up, w_down, layer, cap, tr=512):
    B, S, D = h.shape
    E, F = w_gate.shape[1], w_gate.shape[-1]
    wspec = lambda r, c: pl.BlockSpec((1, 1, r, c), lambda e, b, s: (layer, e, 0, 0))
    return pl.pallas_call(
        functools.partial(_ffn_kernel, tr=min(tr, cap)),
        grid_spec=pltpu.PrefetchScalarGridSpec(
            num_scalar_prefetch=1,
            grid=(E, B),
            in_specs=[pl.BlockSpec(memory_space=pl.ANY), wspec(D, F), wspec(D, F), wspec(F, D)],
            out_specs=pl.BlockSpec((1, 1, cap, D), lambda e, b, s: (b, e, 0, 0)),
            scratch_shapes=[pltpu.VMEM((2, cap, D), F32), pltpu.SemaphoreType.DMA((2,)),
                            pltpu.VMEM((D, F), BF16), pltpu.VMEM((D, F), BF16), pltpu.VMEM((F, D), BF16)]),
        out_shape=jax.ShapeDtypeStruct((B, E, cap, D), BF16),
        compiler_params=_params("arbitrary", "arbitrary"),
        name="moe_ffn",
    )(idx, h, w_gate, w_up, w_down)


def _combine_kernel(tiles_ref, post_ref, gate_ref, x_ref, y_ref, *rest):
    g_ref, o_ref = rest if len(rest) == 2 else (None, rest[0])
    b, i = pl.program_id(0), pl.program_id(1)
    n_exp = y_ref.shape[1]
    per_step = x_ref.shape[1] // TOKEN_CHUNK
    NC = pl.num_programs(1) * per_step
    slot = lax.broadcasted_iota(I32, (TOKEN_CHUNK, SLOT_WINDOW), 1)
    for k in range(per_step):
        rows = slice(k * TOKEN_CHUNK, (k + 1) * TOKEN_CHUNK)
        acc = x_ref[0, rows, :]
        for e in range(n_exp):
            tile = tiles_ref[(b * n_exp + e) * NC + i * per_step + k]
            base = pl.multiple_of(tile * TOKEN_CHUNK, TOKEN_CHUNK)
            select = jnp.where(slot + base == post_ref[0, rows, e:e + 1], gate_ref[0, rows, e:e + 1], 0.0)
            acc = acc + _dot(select.astype(BF16), y_ref[0, e, pl.ds(base, SLOT_WINDOW), :])
        o_ref[0, rows, :] = acc if g_ref is None else _rms(acc, g_ref[...])


def moe_combine(tiles, pos_t, gate, x, y, out_norm=None, tm=512):
    B, S, D = x.shape
    E, cap = y.shape[1], y.shape[2]
    tok = lambda w: pl.BlockSpec((1, tm, w), lambda b, c, s: (b, c, 0))
    in_specs = [tok(E), tok(E), tok(D),
                pl.BlockSpec((1, E, cap, D), lambda b, c, s: (b, 0, 0, 0), pipeline_mode=pl.Buffered(1))]
    args = [tiles, pos_t, gate, x, y]
    if out_norm is not None:
        in_specs.append(pl.BlockSpec((1, D), lambda b, c, s: (0, 0)))
        args.append(out_norm.reshape(1, D))
    return pl.pallas_call(
        _combine_kernel,
        grid_spec=pltpu.PrefetchScalarGridSpec(
            num_scalar_prefetch=1,
            grid=(B, S // tm),
            in_specs=in_specs,
            out_specs=tok(D)),
        out_shape=jax.ShapeDtypeStruct((B, S, D), F32),
        compiler_params=_params("arbitrary", "arbitrary"),
        name="moe_combine",
    )(*args)


def expert_choice_moe(x, h, aff_t, batch, w_gate, w_up, w_down, layer, out_norm=None):
    T, D = x.shape
    S = T // batch
    cap = max(1, CAPACITY_FACTOR * S // N_EXPERTS)
    aff = aff_t.T.reshape(batch, S, N_EXPERTS)
    pos, tiles = route_topk(aff_t, batch, cap)
    idx = moe_slot_tokens(tiles, pos, cap)
    y = moe_ffn(idx, h.reshape(batch, S, D), w_gate, w_up, w_down, layer, cap)
    pos_t = jnp.swapaxes(pos.reshape(batch, N_EXPERTS, S), 1, 2)
    return moe_combine(tiles, pos_t, aff, x.reshape(batch, S, D), y, out_norm).reshape(T, D)


ATTN_TQ = 128
ATTN_TK = ATTN_TQ + 2 * HALF_WINDOW
ATTN_SUB = 4


def _t5_bucket(rel):
    half_buckets = REL_BUCKETS // 2
    max_exact = half_buckets // 2
    n = jnp.abs(rel)
    scaled = (jnp.log(jnp.maximum(n, 1).astype(jnp.float32) / max_exact)
              / math.log(REL_MAX_DISTANCE / max_exact))
    large = jnp.minimum(max_exact + jnp.floor(scaled * (half_buckets - max_exact)).astype(jnp.int32),
                        half_buckets - 1)
    return jnp.where(rel > 0, half_buckets, 0) + jnp.where(n < max_exact, n, large)


def _bias_kernel(table_ref, bucket_ref, o_ref):
    h = pl.program_id(1)
    bucket = bucket_ref[0]
    acc = jnp.zeros(bucket.shape, F32)
    for bk in range(REL_BUCKETS):
        acc = jnp.where(bucket == bk, table_ref[bk * C_HEADS + h], acc)
    q = lax.broadcasted_iota(I32, bucket.shape, 0)
    kc = lax.broadcasted_iota(I32, bucket.shape, 1)
    middle = jnp.where(jnp.abs(kc - HALF_WINDOW - q) <= HALF_WINDOW, acc, NEG_INF)
    o_ref[0, 0, 0] = jnp.where(kc >= HALF_WINDOW, middle, NEG_INF)
    o_ref[0, 1, 0] = middle
    o_ref[0, 2, 0] = jnp.where(kc < HALF_WINDOW + ATTN_TQ, middle, NEG_INF)


def attention_bias(rel_bias):
    rel = np.arange(ATTN_TK)[None, :] - HALF_WINDOW - np.arange(ATTN_TQ)[:, None]
    buckets = jnp.stack([_t5_bucket(jnp.asarray(rel * d, I32)) for d in DILATIONS]).astype(I32)
    P = len(DILATIONS)
    return pl.pallas_call(
        _bias_kernel,
        grid_spec=pltpu.PrefetchScalarGridSpec(
            num_scalar_prefetch=1,
            grid=(P, C_HEADS),
            in_specs=[pl.BlockSpec((1, ATTN_TQ, ATTN_TK), lambda p, h, t: (p, 0, 0))],
            out_specs=pl.BlockSpec((1, 3, 1, ATTN_TQ, ATTN_TK), lambda p, h, t: (p, 0, h, 0, 0))),
        out_shape=jax.ShapeDtypeStruct((P, 3, C_HEADS, ATTN_TQ, ATTN_TK), F32),
        compiler_params=_params("arbitrary", "arbitrary"),
        name="attention_bias",
    )(rel_bias.reshape(-1), buckets)


def _attn_kernel(q_ref, kp_ref, km_ref, kn_ref, vp_ref, vm_ref, vn_ref, bias_ref, o_ref, lse_ref,
                 k_buf, v_buf, s_buf, p_buf):
    i = pl.program_id(2)
    last = pl.num_programs(2) - 1
    hw = HALF_WINDOW
    rows = q_ref.shape[1]
    k_buf[0:hw] = kp_ref[0]
    k_buf[hw:hw + rows] = km_ref[0]
    k_buf[hw + rows:] = kn_ref[0]
    v_buf[0:hw] = vp_ref[0]
    v_buf[hw:hw + rows] = vm_ref[0]
    v_buf[hw + rows:] = vn_ref[0]
    first_head = lax.broadcasted_iota(I32, (ATTN_TQ, LANES), 1) < C_HEAD_DIM
    scale = C_HEAD_DIM ** -0.5
    lse_ref[...] = jnp.zeros_like(lse_ref)
    for sub in range(rows // ATTN_TQ):
        qrows = slice(sub * ATTN_TQ, (sub + 1) * ATTN_TQ)
        krows = slice(sub * ATTN_TQ, sub * ATTN_TQ + ATTN_TK)
        if sub == 0:
            variant = jnp.where(i == 0, 0, 1)
        elif sub == rows // ATTN_TQ - 1:
            variant = jnp.where(i == last, 2, 1)
        else:
            variant = 1
        for pair in range(C_HEADS // 2):
            cols = slice(pair * LANES, (pair + 1) * LANES)
            q = q_ref[0, qrows, cols] * scale
            for half in range(2):
                mine = first_head if half == 0 else jnp.logical_not(first_head)
                s_buf[2 * pair + half] = (_dot_nt(jnp.where(mine, q, jnp.zeros_like(q)), k_buf[krows, cols])
                                          + bias_ref[variant, 2 * pair + half])
        for h in range(C_HEADS):
            s = s_buf[h]
            m = jnp.max(s, axis=-1, keepdims=True)
            p = jnp.exp(s - m)
            den = jnp.sum(p, axis=-1, keepdims=True)
            p_buf[h] = (p / den).astype(BF16)
            lse_ref[0, qrows, h:h + 1] = m + jnp.log(den)
        for pair in range(C_HEADS // 2):
            cols = slice(pair * LANES, (pair + 1) * LANES)
            o_ref[0, qrows, cols] = jnp.where(first_head, _dot(p_buf[2 * pair], v_buf[krows, cols]),
                                              _dot(p_buf[2 * pair + 1], v_buf[krows, cols])).astype(BF16)


def dilated_attention(view, bias, d):
    batch, n, D3 = view.shape
    D = D3 // d // 3
    rows = ATTN_TQ * ATTN_SUB
    nb = n // HALF_WINDOW
    r = rows // HALF_WINDOW
    assert n // rows >= 1 and ATTN_SUB >= 2
    main = lambda c: pl.BlockSpec((1, rows, D), lambda b, j, i: (b, i, 3 * j + c))
    prev = lambda c: pl.BlockSpec((1, HALF_WINDOW, D), lambda b, j, i: (b, jnp.maximum(i * r - 1, 0), 3 * j + c))
    nxt = lambda c: pl.BlockSpec((1, HALF_WINDOW, D),
                                 lambda b, j, i: (b, jnp.minimum(i * r + r, nb - 1), 3 * j + c))
    return pl.pallas_call(
        _attn_kernel,
        grid=(batch, d, n // rows),
        in_specs=[main(0), prev(1), main(1), nxt(1), prev(2), main(2), nxt(2),
                  pl.BlockSpec(bias.shape, lambda b, j, i: (0, 0, 0, 0))],
        out_specs=[pl.BlockSpec((1, rows, D), lambda b, j, i: (b, i, j)),
                   pl.BlockSpec((1, rows, LANES), lambda b, j, i: (b, i, j))],
        out_shape=[jax.ShapeDtypeStruct((batch, n, d * D), BF16),
                   jax.ShapeDtypeStruct((batch, n, d * LANES), F32)],
        scratch_shapes=[pltpu.VMEM((rows + 2 * HALF_WINDOW, D), BF16), pltpu.VMEM((rows + 2 * HALF_WINDOW, D), BF16),
                        pltpu.VMEM((C_HEADS, ATTN_TQ, ATTN_TK), F32), pltpu.VMEM((C_HEADS, ATTN_TQ, ATTN_TK), BF16)],
        compiler_params=_params("parallel", "parallel", "parallel"),
        name="dilated_attention",
    )(view, view, view, view, view, view, view, bias)


def dilated_mixture(views, rel_bias):
    bias = attention_bias(rel_bias)
    results = [dilated_attention(view, bias[p], d) for p, (view, d) in enumerate(zip(views, DILATIONS))]
    return [o for o, _ in results], [lse for _, lse in results]


def kernel(x, mix_norm, ffn_norm, final_norm, w_in_even, w_out_even, a_ln_g, a_ln_b, a_w_s, a_b_s, b_lb_table,
           b_norm_g, w_qkv_odd, w_o_odd, rel_bias, w_router, w_gate, w_up, w_down):
    B, S, D = x.shape
    depth = mix_norm.shape[0]
    xt = x.reshape(B * S, D)
    for layer in range(depth):
        j = layer // 2
        if layer % 2 == 0:
            proj = norm_matmul(xt, mix_norm[layer], w_in_even[j].astype(BF16), B)[0].reshape(B * S, -1)
            o_f, o_b = mixer_b(proj, b_lb_table, layer, B)
            xt, h, aff = outproj_even(proj, a_ln_g[j], a_ln_b[j], a_w_s[j], a_b_s[j], o_f, o_b, b_norm_g[j],
                                      w_out_even[j], xt, ffn_norm[layer], w_router[layer])
        else:
            views = norm_matmul(xt, mix_norm[layer], w_qkv_odd[j].astype(BF16), B, DILATIONS, tm=512)
            outs, lses = dilated_mixture(views, rel_bias)
            xt, h, aff = outproj_odd(outs, lses, w_o_odd[j], xt, ffn_norm[layer], w_router[layer])
        xt = expert_choice_moe(xt, h, aff, B, w_gate, w_up, w_down, layer,
                               out_norm=final_norm if layer == depth - 1 else None)
    return xt.reshape(B, S, D)
```

```python
import functools
import math

import numpy as np
import jax
import jax.numpy as jnp
from jax import lax
from jax.experimental import pallas as pl
from jax.experimental.pallas import tpu as pltpu

F32 = jnp.float32
BF16 = jnp.bfloat16
I32 = jnp.int32
EPS = 1e-6
NEG_INF = -1e30

LANES = 128
VMEM_LIMIT = 56 * 1024 * 1024

A_GROUPS = 4
A_CHUNK = 128
B_HEADS = 4
B_DIM = 128
GLA_CHUNK = 128
C_HEADS = 16
C_HEAD_DIM = 64
HALF_WINDOW = 64
DILATIONS = (1, 4, 16)
REL_BUCKETS = 32
REL_MAX_DISTANCE = 1024
N_EXPERTS = 16
CAPACITY_FACTOR = 2
TOKEN_CHUNK = 128
SLOT_WINDOW = 256


def _params(*sem):
    return pltpu.CompilerParams(dimension_semantics=sem, vmem_limit_bytes=VMEM_LIMIT)


def _dot(a, b, **kw):
    return jnp.dot(a, b, preferred_element_type=F32, **kw)


def _dot_nt(a, b):
    return lax.dot_general(a, b, (((1,), (1,)), ((), ())), preferred_element_type=F32)


def _dot_tn(a, b):
    return lax.dot_general(a, b, (((0,), (0,)), ((), ())), preferred_element_type=F32)


def _dot_split01(a2, x):
    hi = x.astype(BF16)
    lo = (x - hi.astype(F32)).astype(BF16)
    return _dot(a2, jnp.concatenate([hi, lo], axis=0))


def _sigmoid(x):
    return 0.5 * jnp.tanh(0.5 * x) + 0.5


def _rms(x, g):
    return x * lax.rsqrt(jnp.mean(x * x, axis=-1, keepdims=True) + EPS) * g


def _norm_matmul_kernel(x_ref, g_ref, w_ref, *rest, tn, dilations):
    o_refs, scratch = rest[:len(dilations)], rest[len(dilations):]
    tm, N = x_ref.shape[0], w_ref.shape[1]
    h = _rms(x_ref[...], g_ref[...]).astype(BF16)
    for j in range(N // tn):
        res = _dot(h, w_ref[:, j * tn:(j + 1) * tn])
        if scratch:
            for t in range(tn // LANES):
                scratch[0][t] = res[:, t * LANES:(t + 1) * LANES]
        for o_ref, d in zip(o_refs, dilations):
            if d == 1:
                o_ref[0, :, j * tn:(j + 1) * tn] = res.astype(BF16)
                continue
            for r in range(d):
                for t in range(tn // LANES):
                    lane0 = r * N + j * tn + t * LANES
                    o_ref[0, :, lane0:lane0 + LANES] = (
                        scratch[0][t, pl.ds(r, tm // d, stride=d), :].astype(BF16))


def norm_matmul(x, g, w, batch, dilations=(1,), tm=1024, tn=512):
    T, D = x.shape
    N = w.shape[1]
    S = T // batch
    tiles = S // tm
    strided = any(d > 1 for d in dilations)
    return pl.pallas_call(
        functools.partial(_norm_matmul_kernel, tn=tn, dilations=dilations),
        grid=(T // tm,),
        in_specs=[
            pl.BlockSpec((tm, D), lambda i: (i, 0)),
            pl.BlockSpec((1, D), lambda i: (0, 0)),
            pl.BlockSpec((D, N), lambda i: (0, 0)),
        ],
        out_specs=[pl.BlockSpec((1, tm // d, d * N), lambda i: (i // tiles, i % tiles, 0)) for d in dilations],
        out_shape=[jax.ShapeDtypeStruct((batch, S // d, d * N), BF16) for d in dilations],
        scratch_shapes=[pltpu.VMEM((tn // LANES, tm, LANES), F32)] if strided else [],
        compiler_params=_params("parallel"),
        name="norm_matmul",
    )(x, g.reshape(1, D), w)


def _spatial_gating(u_ref, v_ref, lg_ref, lb_ref, ws_ref, bs_ref):
    tm = u_ref.shape[0]
    u = jax.nn.gelu(u_ref[...].astype(F32))
    v = jax.nn.gelu(v_ref[...].astype(F32))
    mu = jnp.mean(v, axis=-1, keepdims=True)
    vc = v - mu
    vn = vc * lax.rsqrt(jnp.mean(vc * vc, axis=-1, keepdims=True) + EPS)
    vb = (vn * lg_ref[...] + lb_ref[...]).astype(BF16)
    rows_out = []
    for n in range(tm // A_CHUNK):
        rows = slice(n * A_CHUNK, (n + 1) * A_CHUNK)
        groups = []
        for g in range(A_GROUPS):
            cols = slice(g * LANES, (g + 1) * LANES)
            mixed = _dot(ws_ref[g], vb[rows, cols]) + bs_ref[:, g:g + 1]
            groups.append((u[rows, cols] * mixed).astype(BF16))
        rows_out.append(jnp.concatenate(groups, axis=-1))
    return jnp.concatenate(rows_out, axis=0)


def _gla_consts(C, reverse):
    t = np.arange(C)[:, None]
    r = np.arange(C)[None, :]
    L = int(round(math.log2(C)))
    spans = [(r >= t) if reverse else (r <= t)]
    level = np.where(np.eye(C, dtype=bool), L, -1).astype(np.int32)
    for l in range(L):
        bs = (t >> (l + 1)) << (l + 1)
        mid = bs + (1 << l)
        if reverse:
            act_q = t < mid
            span = np.where(act_q, (r >= t) & (r < mid), (r >= mid) & (r < t))
        else:
            act_q = t >= mid
            span = np.where(act_q, (r >= mid) & (r <= t), (r > t) & (r < mid))
        if l > 0:
            spans.append(span)
        level[(bs == bs.T) & act_q & ~act_q.T] = l
    return np.tile(np.concatenate(spans, axis=0).astype(np.float32), (1, 2)), level


def _gla_kernel(qf_ref, ff_ref, if_ref, qb_ref, fb_ref, ib_ref, tbl_ref, span_ref, level_ref,
                of_ref, ob_ref, state_ref, w_buf, q_buf, k_buf, v_buf, s_buf, p_buf, *, layer, C):
    @pl.when(pl.program_id(1) == 0)
    def _():
        state_ref[...] = jnp.zeros_like(state_ref)

    L = span_ref.shape[1] // C
    subs = qf_ref.shape[0] // C
    odd_row = jnp.bitwise_and(lax.broadcasted_iota(I32, (C, qf_ref.shape[1]), 0), 1) == 1

    def chunk_rows(sub, d):
        k = sub if d == 0 else subs - 1 - sub
        return slice(k * C, (k + 1) * C)

    groups = [(sub, d) for sub in range(subs) for d in range(2)]
    heads = [(sub * 2 + d, sub, d, h, slice(h * B_DIM, (h + 1) * B_DIM)) for sub, d in groups for h in range(B_HEADS)]
    tots = {}
    for sub, d in groups:
        g = sub * 2 + d
        q_ref, f_ref, i_ref = (qf_ref, ff_ref, if_ref) if d == 0 else (qb_ref, fb_ref, ib_ref)
        rows = chunk_rows(sub, d)
        tb = tbl_ref[d]
        e = jnp.exp(tb - jnp.max(tb, axis=0, keepdims=True))
        lb = jnp.sum(e[0:layer + 1], axis=0, keepdims=True) / jnp.sum(e, axis=0, keepdims=True)
        f = lb + (1.0 - lb) * _sigmoid(f_ref[rows, :].astype(F32))
        e2 = _dot_split01(span_ref[d], jnp.log2(f))
        tot2 = e2[0:1] if d == 1 else e2[C - 1:C]
        tots[g] = jnp.exp2(tot2)
        w_buf[g, 0:C] = jnp.exp2(e2[0:C]).astype(BF16)
        w_buf[g, C:2 * C] = jnp.exp2(tot2 - e2[0:C]).astype(BF16)
        w_buf[g, 2 * C:3 * C] = jnp.where(odd_row if d == 0 else jnp.logical_not(odd_row), f, 1.0).astype(BF16)
        w_buf[g, 3 * C:] = jnp.exp2(e2[C:]).astype(BF16)
        qr = q_ref[rows, :].astype(F32)
        q_buf[g] = (qr * _sigmoid(qr)).astype(BF16)
        k_buf[g] = (1.0 - f).astype(BF16)
        v_buf[g] = i_ref[rows, :]
    for g, sub, d, h, sl in heads:
        qh, kh = q_buf[g, :, sl], k_buf[g, :, sl]
        s_buf[g, h, L] = _dot_nt(qh, kh)
        for l in range(L):
            wl = w_buf[g, (l + 2) * C:(l + 3) * C, sl]
            s_buf[g, h, l] = _dot_nt(qh * wl, kh * wl)
    for g, sub, d, h, sl in heads:
        level = level_ref[d]
        p = jnp.where(level == L, s_buf[g, h, L], 0.0)
        for l in range(L):
            p = jnp.where(level == l, s_buf[g, h, l], p)
        p_buf[g, h] = p.astype(BF16)
    for g, sub, d, h, sl in heads:
        o_ref = of_ref if d == 0 else ob_ref
        st = state_ref[d, h]
        vh = v_buf[g, :, sl]
        o_ref[chunk_rows(sub, d), sl] = (_dot(p_buf[g, h], vh) + _dot_nt(
            q_buf[g, :, sl] * w_buf[g, 0:C, sl], st.astype(BF16))).astype(o_ref.dtype)
        state_ref[d, h] = st * tots[g][:, sl] + _dot_tn(vh, k_buf[g, :, sl] * w_buf[g, C:2 * C, sl])


def mixer_b(proj, b_lb_table, layer, batch, C=GLA_CHUNK, subs=4):
    T = proj.shape[0]
    W = B_HEADS * B_DIM
    R = C * subs
    n = T // batch // R
    consts = [_gla_consts(C, rev) for rev in (False, True)]
    span = jnp.asarray(np.stack([c[0] for c in consts]), BF16)
    level = jnp.asarray(np.stack([c[1] for c in consts]))
    fwd = lambda col: pl.BlockSpec((R, W), lambda b, c: (b * n + c, col))
    bwd = lambda col: pl.BlockSpec((R, W), lambda b, c: (b * n + n - 1 - c, col))
    full = lambda a: pl.BlockSpec(a.shape, lambda b, c: (0,) * a.ndim)
    G = 2 * subs
    return pl.pallas_call(
        functools.partial(_gla_kernel, layer=layer, C=C),
        grid=(batch, n),
        in_specs=[fwd(2), fwd(3), fwd(5), bwd(2), bwd(4), bwd(5),
                  full(b_lb_table), full(span), full(level)],
        out_specs=[pl.BlockSpec((R, W), lambda b, c: (b * n + c, 0)),
                   pl.BlockSpec((R, W), lambda b, c: (b * n + n - 1 - c, 0))],
        out_shape=[jax.ShapeDtypeStruct((T, W), BF16)] * 2,
        scratch_shapes=[pltpu.VMEM((2, B_HEADS, B_DIM, B_DIM), F32),
                        pltpu.VMEM((G, span.shape[1] + 2 * C, W), BF16),
                        pltpu.VMEM((G, C, W), BF16), pltpu.VMEM((G, C, W), BF16), pltpu.VMEM((G, C, W), BF16),
                        pltpu.VMEM((G, B_HEADS, span.shape[1] // C + 1, C, C), F32),
                        pltpu.VMEM((G, B_HEADS, C, C), BF16)],
        compiler_params=_params("arbitrary", "arbitrary"),
        name="mixer_b",
    )(proj, proj, proj, proj, proj, proj, b_lb_table, span, level)


def _router_epilogue(x_new, g_ref, wr_ref, x_ref, h_ref, aff_ref):
    x_ref[...] = x_new
    h = _rms(x_new, g_ref[...])
    h_ref[...] = h
    w = wr_ref[...]
    n_exp = w.shape[0]
    h_hi, w_hi = h.astype(BF16), w.astype(BF16)
    h_lo, w_lo = (h - h_hi.astype(F32)).astype(BF16), (w - w_hi.astype(F32)).astype(BF16)
    both = _dot_nt(jnp.concatenate([w_hi, w_lo], axis=0), h_hi)
    logits = both[0:n_exp] + both[n_exp:] + _dot_nt(w_hi, h_lo)
    e = jnp.exp(logits - jnp.max(logits, axis=0, keepdims=True))
    aff_ref[...] = e / jnp.sum(e, axis=0, keepdims=True)


def _outproj_even_kernel(u_ref, v_ref, lg_ref, lb_ref, ws_ref, bs_ref, of_ref, ob_ref, gate_ref, bng_ref, w_ref,
                         x_ref, g_ref, wr_ref, xo_ref, h_ref, aff_ref):
    a = _spatial_gating(u_ref, v_ref, lg_ref, lb_ref, ws_ref, bs_ref)
    o = of_ref[...].astype(F32) + ob_ref[...].astype(F32)
    parts = []
    for h in range(B_HEADS):
        oh = o[:, h * B_DIM:(h + 1) * B_DIM]
        parts.append(oh * lax.rsqrt(jnp.mean(oh * oh, axis=-1, keepdims=True) + EPS))
    on = jnp.concatenate(parts, axis=-1) * bng_ref[...] * jax.nn.sigmoid(gate_ref[...].astype(F32))
    wa = a.shape[1]
    mixed = _dot(a, w_ref[0:wa, :]) + _dot(on.astype(BF16), w_ref[wa:, :])
    _router_epilogue(x_ref[...] + mixed, g_ref, wr_ref, xo_ref, h_ref, aff_ref)


def _router_out(T, D, tm):
    specs = [pl.BlockSpec((tm, D), lambda i: (i, 0)), pl.BlockSpec((tm, D), lambda i: (i, 0)),
             pl.BlockSpec((N_EXPERTS, tm), lambda i: (0, i))]
    shapes = [jax.ShapeDtypeStruct((T, D), F32), jax.ShapeDtypeStruct((T, D), F32),
              jax.ShapeDtypeStruct((N_EXPERTS, T), F32)]
    return specs, shapes


def outproj_even(proj, a_ln_g, a_ln_b, a_w_s, a_b_s, o_f, o_b, b_norm_g, w_out, x, ffn_g, w_router, tm=512):
    T, D = x.shape
    W = o_f.shape[1]
    row = lambda w, col=0: pl.BlockSpec((tm, w), lambda i: (i, col))
    full = lambda s: pl.BlockSpec(s, lambda i: (0,) * len(s))
    out_specs, out_shapes = _router_out(T, D, tm)
    return pl.pallas_call(
        _outproj_even_kernel,
        grid=(T // tm,),
        in_specs=[row(W, 0), row(W, 1), full((1, W)), full((1, W)), full(a_w_s.shape), full(a_b_s.T.shape),
                  row(W), row(W), row(W, 6), full((1, W)), full(w_out.shape), row(D),
                  full((1, D)), full(w_router.T.shape)],
        out_specs=out_specs,
        out_shape=out_shapes,
        compiler_params=_params("parallel"),
        name="outproj_even",
    )(proj, proj, a_ln_g.reshape(1, W), a_ln_b.reshape(1, W), a_w_s.astype(BF16), a_b_s.T,
      o_f, o_b, proj, b_norm_g.reshape(1, W), w_out.astype(BF16), x, ffn_g.reshape(1, D), w_router.T)


def _outproj_odd_kernel(*refs):
    P = len(DILATIONS)
    o_refs, l_refs = refs[:P], refs[P:2 * P]
    ex_ref, w_ref, x_ref, g_ref, wr_ref, xo_ref, h_ref, aff_ref, o_buf, l_buf = refs[2 * P:]
    tm = x_ref.shape[0]

    def token_order(ref, buf, d):
        if d == 1:
            return ref[0].astype(F32)
        tiles = buf.shape[0]
        for r in range(d):
            for t in range(tiles):
                lane0 = (r * tiles + t) * LANES
                buf[t, pl.ds(r, tm // d, stride=d), :] = ref[0, :, lane0:lane0 + LANES].astype(F32)
        return jnp.concatenate([buf[t] for t in range(tiles)], axis=-1)

    ls = [token_order(l_ref, l_buf, d)[:, 0:C_HEADS] for l_ref, d in zip(l_refs, DILATIONS)]
    m = functools.reduce(jnp.maximum, ls)
    es = [jnp.exp(l - m) for l in ls]
    den = functools.reduce(jnp.add, es)
    attn = None
    for e, o_ref, d in zip(es, o_refs, DILATIONS):
        wfull = _dot((e / den).astype(BF16), ex_ref[...])
        term = wfull * token_order(o_ref, o_buf, d)
        attn = term if attn is None else attn + term
    mixed = _dot(attn.astype(BF16), w_ref[...])
    _router_epilogue(x_ref[...] + mixed, g_ref, wr_ref, xo_ref, h_ref, aff_ref)


def outproj_odd(outs, lses, w_o, x, ffn_g, w_router, tm=512):
    T, D = x.shape
    tiles = outs[0].shape[1] // tm
    expand = jnp.asarray(np.kron(np.eye(C_HEADS), np.ones((1, C_HEAD_DIM))), BF16)
    dil = lambda w: [pl.BlockSpec((1, tm // d, d * w), lambda i: (i // tiles, i % tiles, 0)) for d in DILATIONS]
    full = lambda s: pl.BlockSpec(s, lambda i: (0, 0))
    out_specs, out_shapes = _router_out(T, D, tm)
    return pl.pallas_call(
        _outproj_odd_kernel,
        grid=(T // tm,),
        in_specs=dil(D) + dil(LANES) + [full(expand.shape), full(w_o.shape),
                                        pl.BlockSpec((tm, D), lambda i: (i, 0)), full((1, D)),
                                        full(w_router.T.shape)],
        out_specs=out_specs,
        out_shape=out_shapes,
        scratch_shapes=[pltpu.VMEM((D // LANES, tm, LANES), F32), pltpu.VMEM((1, tm, LANES), F32)],
        compiler_params=_params("parallel"),
        name="outproj_odd",
    )(*outs, *lses, expand, w_o.astype(BF16), x, ffn_g.reshape(1, D), w_router.T)


def _topk_kernel(aff_ref, upper_ref, strict_ref, pos_ref, tile_ref, *, cap):
    a = aff_ref[...]
    E, NC, _ = a.shape

    def count(mask):
        return jnp.sum(jnp.sum(mask.astype(F32), axis=2, keepdims=True), axis=1, keepdims=True)

    def as_float(bits):
        return lax.bitcast_convert_type(bits, jnp.float32)

    def search(i, thr):
        cand = thr | jnp.left_shift(jnp.int32(1), 30 - i)
        return jnp.where(count(a >= as_float(cand)) >= cap, cand, thr)

    thr = lax.fori_loop(0, 31, search, jnp.zeros((E, 1, 1), I32))
    gt = a >= as_float(thr + 1)
    eq = jnp.logical_and(a >= as_float(thr), jnp.logical_not(gt))
    need = cap - count(gt)

    def prefix(mask):
        m2 = mask.astype(BF16).reshape(E * NC, LANES)
        within = _dot(m2, upper_ref[...])
        total = within[:, LANES - 1:LANES].astype(BF16)
        tot_b = jnp.broadcast_to(total, (E * NC, LANES))
        starts = jnp.concatenate(
            [_dot(strict_ref[...], tot_b[e * NC:(e + 1) * NC]) for e in range(E)], axis=0)
        return (within + starts).reshape(E, NC, LANES), starts.reshape(E, NC, LANES)

    eq_incl, _ = prefix(eq)
    sel = gt | (eq & (eq_incl - 1.0 < need))
    sel_incl, starts = prefix(sel)
    pos_ref[0] = jnp.where(sel, sel_incl - 1.0, -1.0).astype(I32)
    tile = jnp.minimum(jnp.floor(starts * (1.0 / TOKEN_CHUNK)), float((cap - SLOT_WINDOW) // TOKEN_CHUNK))
    tile_ref[0] = tile.astype(I32)


def route_topk(aff_t, batch, cap):
    E, T = aff_t.shape
    B, S = batch, T // batch
    NC = S // TOKEN_CHUNK
    t = np.arange(LANES)
    upper = jnp.asarray(t[:, None] <= t[None, :], BF16)
    c = np.arange(NC)
    strict = jnp.asarray(c[None, :] < c[:, None], BF16)
    blk = pl.BlockSpec((1, E, NC, LANES), lambda b: (b, 0, 0, 0))
    pos, tiles = pl.pallas_call(
        functools.partial(_topk_kernel, cap=cap),
        grid=(B,),
        in_specs=[pl.BlockSpec((E, NC, LANES), lambda b: (0, b, 0)),
                  pl.BlockSpec(upper.shape, lambda b: (0, 0)), pl.BlockSpec(strict.shape, lambda b: (0, 0))],
        out_specs=[blk, blk],
        out_shape=[jax.ShapeDtypeStruct((B, E, NC, LANES), I32)] * 2,
        compiler_params=_params("parallel"),
        name="route_topk",
    )(aff_t.reshape(E, B * NC, LANES), upper, strict)
    return pos, tiles[..., 0].reshape(-1)


def _index_kernel(tiles_ref, pos_ref, sel_ref, idx_ref, acc_ref):
    b, e = pl.program_id(0), pl.program_id(1)
    n_exp = pl.num_programs(1)
    NC = pos_ref.shape[2]
    acc_ref[...] = jnp.zeros_like(acc_ref)
    slot = lax.broadcasted_iota(I32, (SLOT_WINDOW, TOKEN_CHUNK), 0)

    def body(c, carry):
        j = tiles_ref[(b * n_exp + e) * NC + c]
        onehot = jnp.where(slot + j * TOKEN_CHUNK == pos_ref[0, 0, pl.ds(c, 1), :], 1.0, 0.0).astype(BF16)
        hit = _dot_nt(sel_ref[...], onehot)
        tok = hit[0:1] + hit[1:2] * lax.convert_element_type(c * TOKEN_CHUNK, F32)
        acc_ref[j, 0:1, :] += tok[:, 0:TOKEN_CHUNK]
        acc_ref[j + 1, 0:1, :] += tok[:, TOKEN_CHUNK:]
        return carry

    lax.fori_loop(0, NC, body, 0, unroll=True)
    idx_ref[0, 0] = acc_ref[...].astype(I32)


def moe_slot_tokens(tiles, pos, cap):
    B, E, NC, _ = pos.shape
    nt = cap // TOKEN_CHUNK
    sel = np.zeros((8, TOKEN_CHUNK), np.float32)
    sel[0] = np.arange(TOKEN_CHUNK)
    sel[1] = 1.0
    idx = pl.pallas_call(
        _index_kernel,
        grid_spec=pltpu.PrefetchScalarGridSpec(
            num_scalar_prefetch=1,
            grid=(B, E),
            in_specs=[pl.BlockSpec((1, 1, NC, LANES), lambda b, e, s: (b, e, 0, 0)),
                      pl.BlockSpec((8, TOKEN_CHUNK), lambda b, e, s: (0, 0))],
            out_specs=pl.BlockSpec((1, 1, nt, 8, LANES), lambda b, e, s: (b, e, 0, 0, 0)),
            scratch_shapes=[pltpu.VMEM((nt, 8, LANES), F32)]),
        out_shape=jax.ShapeDtypeStruct((B, E, nt, 8, LANES), I32),
        compiler_params=_params("arbitrary", "arbitrary"),
        name="moe_slot_tokens",
    )(tiles, pos, jnp.asarray(sel, BF16))
    return idx[:, :, :, 0, :].reshape(-1)


def _ffn_kernel(idx_ref, h_ref, wg_ref, wu_ref, wd_ref, y_ref, x_buf, sem, wgb_ref, wub_ref, wdb_ref, *, tr):
    e, b = pl.program_id(0), pl.program_id(1)
    n_exp, nb = pl.num_programs(0), pl.num_programs(1)
    cap = x_buf.shape[1]
    step = e * nb + b

    def row_copy(st, buf, i):
        eb = st // nb
        bb = st - eb * nb
        tok = idx_ref[(bb * n_exp + eb) * cap + i]
        return pltpu.make_async_copy(h_ref.at[bb, pl.ds(tok, 1), :], x_buf.at[buf, pl.ds(i, 1), :], sem.at[buf])

    cur = jnp.bitwise_and(step, 1)

    @pl.when(step == 0)
    def _():
        lax.fori_loop(0, cap, lambda i, c: (row_copy(step, 0, i).start(), c)[1], 0, unroll=8)

    @pl.when(b == 0)
    def _():
        wgb_ref[...] = wg_ref[0, 0].astype(BF16)
        wub_ref[...] = wu_ref[0, 0].astype(BF16)
        wdb_ref[...] = wd_ref[0, 0].astype(BF16)

    pltpu.make_async_copy(h_ref.at[0, pl.ds(0, cap), :], x_buf.at[cur], sem.at[cur]).wait()

    def compute(prefetch):
        for r in range(cap // tr):
            rows = pl.ds(r * tr, tr)
            x = x_buf[cur, rows, :].astype(BF16)
            if prefetch:
                for i in range(r * tr, (r + 1) * tr):
                    row_copy(step + 1, 1 - cur, i).start()
            g = _dot(x, wgb_ref[...])
            u = _dot(x, wub_ref[...])
            mid = (g * jax.nn.sigmoid(g) * u).astype(BF16)
            y_ref[0, 0, rows, :] = _dot(mid, wdb_ref[...]).astype(BF16)

    is_last = step + 1 == n_exp * nb
    pl.when(jnp.logical_not(is_last))(lambda: compute(True))
    pl.when(is_last)(lambda: compute(False))


def moe_ffn(idx, h, w_gate, w_up, w_down, layer, cap, tr=512):
    B, S, D = h.shape
    E, F = w_gate.shape[1], w_gate.shape[-1]
    wspec = lambda r, c: pl.BlockSpec((1, 1, r, c), lambda e, b, s: (layer, e, 0, 0))
    return pl.pallas_call(
        functools.partial(_ffn_kernel, tr=min(tr, cap)),
        grid_spec=pltpu.PrefetchScalarGridSpec(
            num_scalar_prefetch=1,
            grid=(E, B),
            in_specs=[pl.BlockSpec(memory_space=pl.ANY), wspec(D, F), wspec(D, F), wspec(F, D)],
            out_specs=pl.BlockSpec((1, 1, cap, D), lambda e, b, s: (b, e, 0, 0)),
            scratch_shapes=[pltpu.VMEM((2, cap, D), F32), pltpu.SemaphoreType.DMA((2,)),
                            pltpu.VMEM((D, F), BF16), pltpu.VMEM((D, F), BF16), pltpu.VMEM((F, D), BF16)]),
        out_shape=jax.ShapeDtypeStruct((B, E, cap, D), BF16),
        compiler_params=_params("arbitrary", "arbitrary"),
        name="moe_ffn",
    )(idx, h, w_gate, w_up, w_down)


def _combine_kernel(tiles_ref, post_ref, gate_ref, x_ref, y_ref, *rest):
    g_ref, o_ref = rest if len(rest) == 2 else (None, rest[0])
    b, i = pl.program_id(0), pl.program_id(1)
    n_exp = y_ref.shape[1]
    per_step = x_ref.shape[1] // TOKEN_CHUNK
    NC = pl.num_programs(1) * per_step
    slot = lax.broadcasted_iota(I32, (TOKEN_CHUNK, SLOT_WINDOW), 1)
    for k in range(per_step):
        rows = slice(k * TOKEN_CHUNK, (k + 1) * TOKEN_CHUNK)
        acc = x_ref[0, rows, :]
        for e in range(n_exp):
            tile = tiles_ref[(b * n_exp + e) * NC + i * per_step + k]
            base = pl.multiple_of(tile * TOKEN_CHUNK, TOKEN_CHUNK)
            select = jnp.where(slot + base == post_ref[0, rows, e:e + 1], gate_ref[0, rows, e:e + 1], 0.0)
            acc = acc + _dot(select.astype(BF16), y_ref[0, e, pl.ds(base, SLOT_WINDOW), :])
        o_ref[0, rows, :] = acc if g_ref is None else _rms(acc, g_ref[...])


def moe_combine(tiles, pos_t, gate, x, y, out_norm=None, tm=512):
    B, S, D = x.shape
    E, cap = y.shape[1], y.shape[2]
    tok = lambda w: pl.BlockSpec((1, tm, w), lambda b, c, s: (b, c, 0))
    in_specs = [tok(E), tok(E), tok(D),
                pl.BlockSpec((1, E, cap, D), lambda b, c, s: (b, 0, 0, 0), pipeline_mode=pl.Buffered(1))]
    args = [tiles, pos_t, gate, x, y]
    if out_norm is not None:
        in_specs.append(pl.BlockSpec((1, D), lambda b, c, s: (0, 0)))
        args.append(out_norm.reshape(1, D))
    return pl.pallas_call(
        _combine_kernel,
        grid_spec=pltpu.PrefetchScalarGridSpec(
            num_scalar_prefetch=1,
            grid=(B, S // tm),
            in_specs=in_specs,
            out_specs=tok(D)),
        out_shape=jax.ShapeDtypeStruct((B, S, D), F32),
        compiler_params=_params("arbitrary", "arbitrary"),
        name="moe_combine",
    )(*args)


def expert_choice_moe(x, h, aff_t, batch, w_gate, w_up, w_down, layer, out_norm=None):
    T, D = x.shape
    S = T // batch
    cap = max(1, CAPACITY_FACTOR * S // N_EXPERTS)
    aff = aff_t.T.reshape(batch, S, N_EXPERTS)
    pos, tiles = route_topk(aff_t, batch, cap)
    idx = moe_slot_tokens(tiles, pos, cap)
    y = moe_ffn(idx, h.reshape(batch, S, D), w_gate, w_up, w_down, layer, cap)
    pos_t = jnp.swapaxes(pos.reshape(batch, N_EXPERTS, S), 1, 2)
    return moe_combine(tiles, pos_t, aff, x.reshape(batch, S, D), y, out_norm).reshape(T, D)


ATTN_TQ = 128
ATTN_TK = ATTN_TQ + 2 * HALF_WINDOW
ATTN_SUB = 4


def _t5_bucket(rel):
    half_buckets = REL_BUCKETS // 2
    max_exact = half_buckets // 2
    n = jnp.abs(rel)
    scaled = (jnp.log(jnp.maximum(n, 1).astype(jnp.float32) / max_exact)
              / math.log(REL_MAX_DISTANCE / max_exact))
    large = jnp.minimum(max_exact + jnp.floor(scaled * (half_buckets - max_exact)).astype(jnp.int32),
                        half_buckets - 1)
    return jnp.where(rel > 0, half_buckets, 0) + jnp.where(n < max_exact, n, large)


def _bias_kernel(table_ref, bucket_ref, o_ref):
    bucket = bucket_ref[0]
    q = lax.broadcasted_iota(I32, bucket.shape, 0)
    kc = lax.broadcasted_iota(I32, bucket.shape, 1)
    in_band = jnp.abs(kc - HALF_WINDOW - q) <= HALF_WINDOW
    for h in range(C_HEADS):
        acc = jnp.zeros(bucket.shape, F32)
        for bk in range(REL_BUCKETS):
            acc = jnp.where(bucket == bk, table_ref[bk * C_HEADS + h], acc)
        middle = jnp.where(in_band, acc, NEG_INF)
        o_ref[0, 0, h] = jnp.where(kc >= HALF_WINDOW, middle, NEG_INF)
        o_ref[0, 1, h] = middle
        o_ref[0, 2, h] = jnp.where(kc < HALF_WINDOW + ATTN_TQ, middle, NEG_INF)


def attention_bias(rel_bias):
    rel = np.arange(ATTN_TK)[None, :] - HALF_WINDOW - np.arange(ATTN_TQ)[:, None]
    buckets = jnp.stack([_t5_bucket(jnp.asarray(rel * d, I32)) for d in DILATIONS]).astype(I32)
    P = len(DILATIONS)
    return pl.pallas_call(
        _bias_kernel,
        grid_spec=pltpu.PrefetchScalarGridSpec(
            num_scalar_prefetch=1,
            grid=(P,),
            in_specs=[pl.BlockSpec((1, ATTN_TQ, ATTN_TK), lambda p, t: (p, 0, 0))],
            out_specs=pl.BlockSpec((1, 3, C_HEADS, ATTN_TQ, ATTN_TK), lambda p, t: (p, 0, 0, 0, 0))),
        out_shape=jax.ShapeDtypeStruct((P, 3, C_HEADS, ATTN_TQ, ATTN_TK), F32),
        compiler_params=_params("arbitrary"),
        name="attention_bias",
    )(rel_bias.reshape(-1), buckets)


def _attn_kernel(q_ref, kp_ref, km_ref, kn_ref, vp_ref, vm_ref, vn_ref, bias_ref, o_ref, lse_ref,
                 k_buf, v_buf, s_buf, p_buf):
    i = pl.program_id(2)
    last = pl.num_programs(2) - 1
    hw = HALF_WINDOW
    rows = q_ref.shape[1]
    k_buf[0:hw] = kp_ref[0]
    k_buf[hw:hw + rows] = km_ref[0]
    k_buf[hw + rows:] = kn_ref[0]
    v_buf[0:hw] = vp_ref[0]
    v_buf[hw:hw + rows] = vm_ref[0]
    v_buf[hw + rows:] = vn_ref[0]
    first_head = lax.broadcasted_iota(I32, (ATTN_TQ, LANES), 1) < C_HEAD_DIM
    scale = C_HEAD_DIM ** -0.5
    lse_ref[...] = jnp.zeros_like(lse_ref)
    for sub in range(rows // ATTN_TQ):
        qrows = slice(sub * ATTN_TQ, (sub + 1) * ATTN_TQ)
        krows = slice(sub * ATTN_TQ, sub * ATTN_TQ + ATTN_TK)
        if sub == 0:
            variant = jnp.where(i == 0, 0, 1)
        elif sub == rows // ATTN_TQ - 1:
            variant = jnp.where(i == last, 2, 1)
        else:
            variant = 1
        for pair in range(C_HEADS // 2):
            cols = slice(pair * LANES, (pair + 1) * LANES)
            q = q_ref[0, qrows, cols] * scale
            for half in range(2):
                mine = first_head if half == 0 else jnp.logical_not(first_head)
                s_buf[2 * pair + half] = (_dot_nt(jnp.where(mine, q, jnp.zeros_like(q)), k_buf[krows, cols])
                                          + bias_ref[variant, 2 * pair + half])
        for h in range(C_HEADS):
            s = s_buf[h]
            m = jnp.max(s, axis=-1, keepdims=True)
            p = jnp.exp(s - m)
            den = jnp.sum(p, axis=-1, keepdims=True)
            p_buf[h] = (p / den).astype(BF16)
            lse_ref[0, qrows, h:h + 1] = m + jnp.log(den)
        for pair in range(C_HEADS // 2):
            cols = slice(pair * LANES, (pair + 1) * LANES)
            o_ref[0, qrows, cols] = jnp.where(first_head, _dot(p_buf[2 * pair], v_buf[krows, cols]),
                                              _dot(p_buf[2 * pair + 1], v_buf[krows, cols])).astype(BF16)


def dilated_attention(view, bias, d):
    batch, n, D3 = view.shape
    D = D3 // d // 3
    rows = ATTN_TQ * ATTN_SUB
    nb = n // HALF_WINDOW
    r = rows // HALF_WINDOW
    assert n // rows >= 1 and ATTN_SUB >= 2
    main = lambda c: pl.BlockSpec((1, rows, D), lambda b, j, i: (b, i, 3 * j + c))
    prev = lambda c: pl.BlockSpec((1, HALF_WINDOW, D), lambda b, j, i: (b, jnp.maximum(i * r - 1, 0), 3 * j + c))
    nxt = lambda c: pl.BlockSpec((1, HALF_WINDOW, D),
                                 lambda b, j, i: (b, jnp.minimum(i * r + r, nb - 1), 3 * j + c))
    return pl.pallas_call(
        _attn_kernel,
        grid=(batch, d, n // rows),
        in_specs=[main(0), prev(1), main(1), nxt(1), prev(2), main(2), nxt(2),
                  pl.BlockSpec(bias.shape, lambda b, j, i: (0, 0, 0, 0))],
        out_specs=[pl.BlockSpec((1, rows, D), lambda b, j, i: (b, i, j)),
                   pl.BlockSpec((1, rows, LANES), lambda b, j, i: (b, i, j))],
        out_shape=[jax.ShapeDtypeStruct((batch, n, d * D), BF16),
                   jax.ShapeDtypeStruct((batch, n, d * LANES), F32)],
        scratch_shapes=[pltpu.VMEM((rows + 2 * HALF_WINDOW, D), BF16), pltpu.VMEM((rows + 2 * HALF_WINDOW, D), BF16),
                        pltpu.VMEM((C_HEADS, ATTN_TQ, ATTN_TK), F32), pltpu.VMEM((C_HEADS, ATTN_TQ, ATTN_TK), BF16)],
        compiler_params=_params("parallel", "parallel", "parallel"),
        name="dilated_attention",
    )(view, view, view, view, view, view, view, bias)


def dilated_mixture(views, rel_bias):
    bias = attention_bias(rel_bias)
    results = [dilated_attention(view, bias[p], d) for p, (view, d) in enumerate(zip(views, DILATIONS))]
    return [o for o, _ in results], [lse for _, lse in results]


def kernel(x, mix_norm, ffn_norm, final_norm, w_in_even, w_out_even, a_ln_g, a_ln_b, a_w_s, a_b_s, b_lb_table,
           b_norm_g, w_qkv_odd, w_o_odd, rel_bias, w_router, w_gate, w_up, w_down):
    B, S, D = x.shape
    depth = mix_norm.shape[0]
    xt = x.reshape(B * S, D)
    for layer in range(depth):
        j = layer // 2
        if layer % 2 == 0:
            proj = norm_matmul(xt, mix_norm[layer], w_in_even[j].astype(BF16), B)[0].reshape(B * S, -1)
            o_f, o_b = mixer_b(proj, b_lb_table, layer, B)
            xt, h, aff = outproj_even(proj, a_ln_g[j], a_ln_b[j], a_w_s[j], a_b_s[j], o_f, o_b, b_norm_g[j],
                                      w_out_even[j], xt, ffn_norm[layer], w_router[layer])
        else:
            views = norm_matmul(xt, mix_norm[layer], w_qkv_odd[j].astype(BF16), B, DILATIONS, tm=512)
            outs, lses = dilated_mixture(views, rel_bias)
            xt, h, aff = outproj_odd(outs, lses, w_o_odd[j], xt, ffn_norm[layer], w_router[layer])
        xt = expert_choice_moe(xt, h, aff, B, w_gate, w_up, w_down, layer,
                               out_norm=final_norm if layer == depth - 1 else None)
    return xt.reshape(B, S, D)
```

```python
import functools
import math

import numpy as np
import jax
import jax.numpy as jnp
from jax import lax
from jax.experimental import pallas as pl
from jax.experimental.pallas import tpu as pltpu

F32 = jnp.float32
BF16 = jnp.bfloat16
I32 = jnp.int32
EPS = 1e-6
NEG_INF = -1e30

LANES = 128
VMEM_LIMIT = 56 * 1024 * 1024

A_GROUPS = 4
A_CHUNK = 128
B_HEADS = 4
B_DIM = 128
GLA_CHUNK = 128
C_HEADS = 16
C_HEAD_DIM = 64
HALF_WINDOW = 64
DILATIONS = (1, 4, 16)
REL_BUCKETS = 32
REL_MAX_DISTANCE = 1024
N_EXPERTS = 16
CAPACITY_FACTOR = 2
TOKEN_CHUNK = 128
SLOT_WINDOW = 256
NARROW_WINDOW = 64


def _params(*sem):
    return pltpu.CompilerParams(dimension_semantics=sem, vmem_limit_bytes=VMEM_LIMIT)


def _dot(a, b, **kw):
    return jnp.dot(a, b, preferred_element_type=F32, **kw)


def _dot_nt(a, b):
    return lax.dot_general(a, b, (((1,), (1,)), ((), ())), preferred_element_type=F32)


def _dot_tn(a, b):
    return lax.dot_general(a, b, (((0,), (0,)), ((), ())), preferred_element_type=F32)


def _dot_split01(a2, x):
    hi = x.astype(BF16)
    lo = (x - hi.astype(F32)).astype(BF16)
    return _dot(a2, jnp.concatenate([hi, lo], axis=0))


def _sigmoid(x):
    return 0.5 * jnp.tanh(0.5 * x) + 0.5


def _rms(x, g):
    return x * lax.rsqrt(jnp.mean(x * x, axis=-1, keepdims=True) + EPS) * g


def _norm_matmul_kernel(x_ref, g_ref, w_ref, *rest, tn, dilations):
    o_refs, scratch = rest[:len(dilations)], rest[len(dilations):]
    tm, N = x_ref.shape[0], w_ref.shape[1]
    h = _rms(x_ref[...], g_ref[...]).astype(BF16)
    for j in range(N // tn):
        res = _dot(h, w_ref[:, j * tn:(j + 1) * tn])
        if scratch:
            for t in range(tn // LANES):
                scratch[0][t] = res[:, t * LANES:(t + 1) * LANES]
        for o_ref, d in zip(o_refs, dilations):
            if d == 1:
                o_ref[0, :, j * tn:(j + 1) * tn] = res.astype(BF16)
                continue
            for r in range(d):
                for t in range(tn // LANES):
                    lane0 = r * N + j * tn + t * LANES
                    o_ref[0, :, lane0:lane0 + LANES] = (
                        scratch[0][t, pl.ds(r, tm // d, stride=d), :].astype(BF16))


def norm_matmul(x, g, w, batch, dilations=(1,), tm=1024, tn=512):
    T, D = x.shape
    N = w.shape[1]
    S = T // batch
    tiles = S // tm
    strided = any(d > 1 for d in dilations)
    return pl.pallas_call(
        functools.partial(_norm_matmul_kernel, tn=tn, dilations=dilations),
        grid=(T // tm,),
        in_specs=[
            pl.BlockSpec((tm, D), lambda i: (i, 0)),
            pl.BlockSpec((1, D), lambda i: (0, 0)),
            pl.BlockSpec((D, N), lambda i: (0, 0)),
        ],
        out_specs=[pl.BlockSpec((1, tm // d, d * N), lambda i: (i // tiles, i % tiles, 0)) for d in dilations],
        out_shape=[jax.ShapeDtypeStruct((batch, S // d, d * N), BF16) for d in dilations],
        scratch_shapes=[pltpu.VMEM((tn // LANES, tm, LANES), F32)] if strided else [],
        compiler_params=_params("parallel"),
        name="norm_matmul",
    )(x, g.reshape(1, D), w)


def _spatial_gating(u_ref, v_ref, lg_ref, lb_ref, ws_ref, bs_ref):
    tm = u_ref.shape[0]
    u = jax.nn.gelu(u_ref[...].astype(F32))
    v = jax.nn.gelu(v_ref[...].astype(F32))
    mu = jnp.mean(v, axis=-1, keepdims=True)
    vc = v - mu
    vn = vc * lax.rsqrt(jnp.mean(vc * vc, axis=-1, keepdims=True) + EPS)
    vb = (vn * lg_ref[...] + lb_ref[...]).astype(BF16)
    rows_out = []
    for n in range(tm // A_CHUNK):
        rows = slice(n * A_CHUNK, (n + 1) * A_CHUNK)
        groups = []
        for g in range(A_GROUPS):
            cols = slice(g * LANES, (g + 1) * LANES)
            mixed = _dot(ws_ref[g], vb[rows, cols]) + bs_ref[:, g:g + 1]
            groups.append((u[rows, cols] * mixed).astype(BF16))
        rows_out.append(jnp.concatenate(groups, axis=-1))
    return jnp.concatenate(rows_out, axis=0)


def _gla_consts(C, reverse):
    t = np.arange(C)[:, None]
    r = np.arange(C)[None, :]
    L = int(round(math.log2(C)))
    spans = [(r >= t) if reverse else (r <= t)]
    level = np.where(np.eye(C, dtype=bool), L, -1).astype(np.int32)
    for l in range(L):
        bs = (t >> (l + 1)) << (l + 1)
        mid = bs + (1 << l)
        if reverse:
            act_q = t < mid
            span = np.where(act_q, (r >= t) & (r < mid), (r >= mid) & (r < t))
        else:
            act_q = t >= mid
            span = np.where(act_q, (r >= mid) & (r <= t), (r > t) & (r < mid))
        if l > 0:
            spans.append(span)
        level[(bs == bs.T) & act_q & ~act_q.T] = l
    return np.tile(np.concatenate(spans, axis=0).astype(np.float32), (1, 2)), level


def _gla_kernel(qf_ref, ff_ref, if_ref, qb_ref, fb_ref, ib_ref, tbl_ref, span_ref, level_ref,
                of_ref, ob_ref, state_ref, w_buf, q_buf, k_buf, v_buf, s_buf, p_buf, *, layer, C):
    @pl.when(pl.program_id(1) == 0)
    def _():
        state_ref[...] = jnp.zeros_like(state_ref)

    L = span_ref.shape[1] // C
    subs = qf_ref.shape[0] // C
    odd_row = jnp.bitwise_and(lax.broadcasted_iota(I32, (C, qf_ref.shape[1]), 0), 1) == 1

    def chunk_rows(sub, d):
        k = sub if d == 0 else subs - 1 - sub
        return slice(k * C, (k + 1) * C)

    groups = [(sub, d) for sub in range(subs) for d in range(2)]
    heads = [(sub * 2 + d, sub, d, h, slice(h * B_DIM, (h + 1) * B_DIM)) for sub, d in groups for h in range(B_HEADS)]
    tots = {}
    for sub, d in groups:
        g = sub * 2 + d
        q_ref, f_ref, i_ref = (qf_ref, ff_ref, if_ref) if d == 0 else (qb_ref, fb_ref, ib_ref)
        rows = chunk_rows(sub, d)
        tb = tbl_ref[d]
        e = jnp.exp(tb - jnp.max(tb, axis=0, keepdims=True))
        lb = jnp.sum(e[0:layer + 1], axis=0, keepdims=True) / jnp.sum(e, axis=0, keepdims=True)
        f = lb + (1.0 - lb) * _sigmoid(f_ref[rows, :].astype(F32))
        e2 = _dot_split01(span_ref[d], jnp.log2(f))
        tot2 = e2[0:1] if d == 1 else e2[C - 1:C]
        tots[g] = jnp.exp2(tot2)
        w_buf[g, 0:C] = jnp.exp2(e2[0:C]).astype(BF16)
        w_buf[g, C:2 * C] = jnp.exp2(tot2 - e2[0:C]).astype(BF16)
        w_buf[g, 2 * C:3 * C] = jnp.where(odd_row if d == 0 else jnp.logical_not(odd_row), f, 1.0).astype(BF16)
        w_buf[g, 3 * C:] = jnp.exp2(e2[C:]).astype(BF16)
        qr = q_ref[rows, :].astype(F32)
        q_buf[g] = (qr * _sigmoid(qr)).astype(BF16)
        k_buf[g] = (1.0 - f).astype(BF16)
        v_buf[g] = i_ref[rows, :]
    for g, sub, d, h, sl in heads:
        qh, kh = q_buf[g, :, sl], k_buf[g, :, sl]
        s_buf[g, h, L] = _dot_nt(qh, kh)
        for l in range(L):
            wl = w_buf[g, (l + 2) * C:(l + 3) * C, sl]
            s_buf[g, h, l] = _dot_nt(qh * wl, kh * wl)
    for g, sub, d, h, sl in heads:
        level = level_ref[d]
        p = jnp.where(level == L, s_buf[g, h, L], 0.0)
        for l in range(L):
            p = jnp.where(level == l, s_buf[g, h, l], p)
        p_buf[g, h] = p.astype(BF16)
    for g, sub, d, h, sl in heads:
        o_ref = of_ref if d == 0 else ob_ref
        st = state_ref[d, h]
        vh = v_buf[g, :, sl]
        o_ref[chunk_rows(sub, d), sl] = (_dot(p_buf[g, h], vh) + _dot_nt(
            q_buf[g, :, sl] * w_buf[g, 0:C, sl], st.astype(BF16))).astype(o_ref.dtype)
        state_ref[d, h] = st * tots[g][:, sl] + _dot_tn(vh, k_buf[g, :, sl] * w_buf[g, C:2 * C, sl])


def mixer_b(proj, b_lb_table, layer, batch, C=GLA_CHUNK, subs=4):
    T = proj.shape[0]
    W = B_HEADS * B_DIM
    R = C * subs
    n = T // batch // R
    consts = [_gla_consts(C, rev) for rev in (False, True)]
    span = jnp.asarray(np.stack([c[0] for c in consts]), BF16)
    level = jnp.asarray(np.stack([c[1] for c in consts]))
    fwd = lambda col: pl.BlockSpec((R, W), lambda b, c: (b * n + c, col))
    bwd = lambda col: pl.BlockSpec((R, W), lambda b, c: (b * n + n - 1 - c, col))
    full = lambda a: pl.BlockSpec(a.shape, lambda b, c: (0,) * a.ndim)
    G = 2 * subs
    return pl.pallas_call(
        functools.partial(_gla_kernel, layer=layer, C=C),
        grid=(batch, n),
        in_specs=[fwd(2), fwd(3), fwd(5), bwd(2), bwd(4), bwd(5),
                  full(b_lb_table), full(span), full(level)],
        out_specs=[pl.BlockSpec((R, W), lambda b, c: (b * n + c, 0)),
                   pl.BlockSpec((R, W), lambda b, c: (b * n + n - 1 - c, 0))],
        out_shape=[jax.ShapeDtypeStruct((T, W), BF16)] * 2,
        scratch_shapes=[pltpu.VMEM((2, B_HEADS, B_DIM, B_DIM), F32),
                        pltpu.VMEM((G, span.shape[1] + 2 * C, W), BF16),
                        pltpu.VMEM((G, C, W), BF16), pltpu.VMEM((G, C, W), BF16), pltpu.VMEM((G, C, W), BF16),
                        pltpu.VMEM((G, B_HEADS, span.shape[1] // C + 1, C, C), F32),
                        pltpu.VMEM((G, B_HEADS, C, C), BF16)],
        compiler_params=_params("arbitrary", "arbitrary"),
        name="mixer_b",
    )(proj, proj, proj, proj, proj, proj, b_lb_table, span, level)


def _router_epilogue(x_new, g_ref, wr_ref, x_ref, h_ref, aff_ref):
    x_ref[...] = x_new
    h = _rms(x_new, g_ref[...])
    h_ref[...] = h
    w = wr_ref[...]
    n_exp = w.shape[0]
    h_hi, w_hi = h.astype(BF16), w.astype(BF16)
    h_lo, w_lo = (h - h_hi.astype(F32)).astype(BF16), (w - w_hi.astype(F32)).astype(BF16)
    both = _dot_nt(jnp.concatenate([w_hi, w_lo], axis=0), h_hi)
    logits = both[0:n_exp] + both[n_exp:] + _dot_nt(w_hi, h_lo)
    e = jnp.exp(logits - jnp.max(logits, axis=0, keepdims=True))
    aff_ref[...] = e / jnp.sum(e, axis=0, keepdims=True)


def _outproj_even_kernel(u_ref, v_ref, lg_ref, lb_ref, ws_ref, bs_ref, of_ref, ob_ref, gate_ref, bng_ref, w_ref,
                         x_ref, g_ref, wr_ref, xo_ref, h_ref, aff_ref):
    a = _spatial_gating(u_ref, v_ref, lg_ref, lb_ref, ws_ref, bs_ref)
    o = of_ref[...].astype(F32) + ob_ref[...].astype(F32)
    parts = []
    for h in range(B_HEADS):
        oh = o[:, h * B_DIM:(h + 1) * B_DIM]
        parts.append(oh * lax.rsqrt(jnp.mean(oh * oh, axis=-1, keepdims=True) + EPS))
    on = jnp.concatenate(parts, axis=-1) * bng_ref[...] * jax.nn.sigmoid(gate_ref[...].astype(F32))
    wa = a.shape[1]
    mixed = _dot(a, w_ref[0:wa, :]) + _dot(on.astype(BF16), w_ref[wa:, :])
    _router_epilogue(x_ref[...] + mixed, g_ref, wr_ref, xo_ref, h_ref, aff_ref)


def _router_out(T, D, tm):
    specs = [pl.BlockSpec((tm, D), lambda i: (i, 0)), pl.BlockSpec((tm, D), lambda i: (i, 0)),
             pl.BlockSpec((N_EXPERTS, tm), lambda i: (0, i))]
    shapes = [jax.ShapeDtypeStruct((T, D), F32), jax.ShapeDtypeStruct((T, D), F32),
              jax.ShapeDtypeStruct((N_EXPERTS, T), F32)]
    return specs, shapes


def outproj_even(proj, a_ln_g, a_ln_b, a_w_s, a_b_s, o_f, o_b, b_norm_g, w_out, x, ffn_g, w_router, tm=512):
    T, D = x.shape
    W = o_f.shape[1]
    row = lambda w, col=0: pl.BlockSpec((tm, w), lambda i: (i, col))
    full = lambda s: pl.BlockSpec(s, lambda i: (0,) * len(s))
    out_specs, out_shapes = _router_out(T, D, tm)
    return pl.pallas_call(
        _outproj_even_kernel,
        grid=(T // tm,),
        in_specs=[row(W, 0), row(W, 1), full((1, W)), full((1, W)), full(a_w_s.shape), full(a_b_s.T.shape),
                  row(W), row(W), row(W, 6), full((1, W)), full(w_out.shape), row(D),
                  full((1, D)), full(w_router.T.shape)],
        out_specs=out_specs,
        out_shape=out_shapes,
        compiler_params=_params("parallel"),
        name="outproj_even",
    )(proj, proj, a_ln_g.reshape(1, W), a_ln_b.reshape(1, W), a_w_s.astype(BF16), a_b_s.T,
      o_f, o_b, proj, b_norm_g.reshape(1, W), w_out.astype(BF16), x, ffn_g.reshape(1, D), w_router.T)


def _outproj_odd_kernel(*refs):
    P = len(DILATIONS)
    o_refs, l_refs = refs[:P], refs[P:2 * P]
    ex_ref, w_ref, x_ref, g_ref, wr_ref, xo_ref, h_ref, aff_ref, o_buf, l_buf = refs[2 * P:]
    tm = x_ref.shape[0]

    def token_order(ref, buf, d):
        if d == 1:
            return ref[0].astype(F32)
        tiles = buf.shape[0]
        for r in range(d):
            for t in range(tiles):
                lane0 = (r * tiles + t) * LANES
                buf[t, pl.ds(r, tm // d, stride=d), :] = ref[0, :, lane0:lane0 + LANES].astype(F32)
        return jnp.concatenate([buf[t] for t in range(tiles)], axis=-1)

    ls = [token_order(l_ref, l_buf, d)[:, 0:C_HEADS] for l_ref, d in zip(l_refs, DILATIONS)]
    m = functools.reduce(jnp.maximum, ls)
    es = [jnp.exp(l - m) for l in ls]
    den = functools.reduce(jnp.add, es)
    attn = None
    for e, o_ref, d in zip(es, o_refs, DILATIONS):
        wfull = _dot((e / den).astype(BF16), ex_ref[...])
        term = wfull * token_order(o_ref, o_buf, d)
        attn = term if attn is None else attn + term
    mixed = _dot(attn.astype(BF16), w_ref[...])
    _router_epilogue(x_ref[...] + mixed, g_ref, wr_ref, xo_ref, h_ref, aff_ref)


def outproj_odd(outs, lses, w_o, x, ffn_g, w_router, tm=512):
    T, D = x.shape
    tiles = outs[0].shape[1] // tm
    expand = jnp.asarray(np.kron(np.eye(C_HEADS), np.ones((1, C_HEAD_DIM))), BF16)
    dil = lambda w: [pl.BlockSpec((1, tm // d, d * w), lambda i: (i // tiles, i % tiles, 0)) for d in DILATIONS]
    full = lambda s: pl.BlockSpec(s, lambda i: (0, 0))
    out_specs, out_shapes = _router_out(T, D, tm)
    return pl.pallas_call(
        _outproj_odd_kernel,
        grid=(T // tm,),
        in_specs=dil(D) + dil(LANES) + [full(expand.shape), full(w_o.shape),
                                        pl.BlockSpec((tm, D), lambda i: (i, 0)), full((1, D)),
                                        full(w_router.T.shape)],
        out_specs=out_specs,
        out_shape=out_shapes,
        scratch_shapes=[pltpu.VMEM((D // LANES, tm, LANES), F32), pltpu.VMEM((1, tm, LANES), F32)],
        compiler_params=_params("parallel"),
        name="outproj_odd",
    )(*outs, *lses, expand, w_o.astype(BF16), x, ffn_g.reshape(1, D), w_router.T)


def _topk_kernel(aff_ref, upper_ref, strict_ref, pos_ref, tile_ref, win_ref, fit_ref, *, cap):
    a = aff_ref[...]
    E, NC, _ = a.shape

    def count(mask):
        return jnp.sum(jnp.sum(mask.astype(F32), axis=2, keepdims=True), axis=1, keepdims=True)

    def as_float(bits):
        return lax.bitcast_convert_type(bits, jnp.float32)

    def search(i, thr):
        cand = thr | jnp.left_shift(jnp.int32(1), 30 - i)
        return jnp.where(count(a >= as_float(cand)) >= cap, cand, thr)

    thr = lax.fori_loop(0, 31, search, jnp.zeros((E, 1, 1), I32))
    gt = a >= as_float(thr + 1)
    eq = jnp.logical_and(a >= as_float(thr), jnp.logical_not(gt))
    need = cap - count(gt)

    def prefix(mask):
        m2 = mask.astype(BF16).reshape(E * NC, LANES)
        within = _dot(m2, upper_ref[...])
        total = jnp.broadcast_to(within[:, LANES - 1:LANES], (E * NC, LANES))
        tot_b = total.astype(BF16)
        starts = jnp.concatenate(
            [_dot(strict_ref[...], tot_b[e * NC:(e + 1) * NC]) for e in range(E)], axis=0)
        return [t.reshape(E, NC, LANES) for t in (within + starts, starts, total)]

    eq_incl, _, _ = prefix(eq)
    sel = gt | (eq & (eq_incl - 1.0 < need))
    sel_incl, starts, count = prefix(sel)
    pos_ref[0] = jnp.where(sel, sel_incl - 1.0, -1.0).astype(I32)
    tile = jnp.minimum(jnp.floor(starts * (1.0 / TOKEN_CHUNK)), float((cap - SLOT_WINDOW) // TOKEN_CHUNK))
    tile_ref[0] = tile.astype(I32)
    win = jnp.minimum(jnp.floor(starts * (1.0 / 16)) * 16, float(cap - NARROW_WINDOW))
    win_ref[0] = win.astype(I32)
    fits = jnp.where(starts + count - win <= NARROW_WINDOW, 1.0, 0.0)
    fit_ref[0] = jnp.min(fits, axis=0, keepdims=True).astype(I32)


def route_topk(aff_t, batch, cap):
    E, T = aff_t.shape
    B, S = batch, T // batch
    NC = S // TOKEN_CHUNK
    t = np.arange(LANES)
    upper = jnp.asarray(t[:, None] <= t[None, :], BF16)
    c = np.arange(NC)
    strict = jnp.asarray(c[None, :] < c[:, None], BF16)
    blk = pl.BlockSpec((1, E, NC, LANES), lambda b: (b, 0, 0, 0))
    one = pl.BlockSpec((1, 1, NC, LANES), lambda b: (b, 0, 0, 0))
    pos, tiles, wins, fits = pl.pallas_call(
        functools.partial(_topk_kernel, cap=cap),
        grid=(B,),
        in_specs=[pl.BlockSpec((E, NC, LANES), lambda b: (0, b, 0)),
                  pl.BlockSpec(upper.shape, lambda b: (0, 0)), pl.BlockSpec(strict.shape, lambda b: (0, 0))],
        out_specs=[blk, blk, blk, one],
        out_shape=[jax.ShapeDtypeStruct((B, E, NC, LANES), I32)] * 3 + [jax.ShapeDtypeStruct((B, 1, NC, LANES), I32)],
        compiler_params=_params("parallel"),
        name="route_topk",
    )(aff_t.reshape(E, B * NC, LANES), upper, strict)
    return pos, tiles[..., 0].reshape(-1), wins[..., 0].reshape(-1), fits[..., 0].reshape(-1)


def _index_kernel(tiles_ref, pos_ref, sel_ref, idx_ref, acc_ref):
    b, e = pl.program_id(0), pl.program_id(1)
    n_exp = pl.num_programs(1)
    NC = pos_ref.shape[2]
    acc_ref[...] = jnp.zeros_like(acc_ref)
    slot = lax.broadcasted_iota(I32, (SLOT_WINDOW, TOKEN_CHUNK), 0)

    def body(c, carry):
        j = tiles_ref[(b * n_exp + e) * NC + c]
        onehot = jnp.where(slot + j * TOKEN_CHUNK == pos_ref[0, 0, pl.ds(c, 1), :], 1.0, 0.0).astype(BF16)
        hit = _dot_nt(sel_ref[...], onehot)
        tok = hit[0:1] + hit[1:2] * lax.convert_element_type(c * TOKEN_CHUNK, F32)
        acc_ref[j, 0:1, :] += tok[:, 0:TOKEN_CHUNK]
        acc_ref[j + 1, 0:1, :] += tok[:, TOKEN_CHUNK:]
        return carry

    lax.fori_loop(0, NC, body, 0, unroll=True)
    idx_ref[0, 0] = acc_ref[...].astype(I32)


def moe_slot_tokens(tiles, pos, cap):
    B, E, NC, _ = pos.shape
    nt = cap // TOKEN_CHUNK
    sel = np.zeros((8, TOKEN_CHUNK), np.float32)
    sel[0] = np.arange(TOKEN_CHUNK)
    sel[1] = 1.0
    idx = pl.pallas_call(
        _index_kernel,
        grid_spec=pltpu.PrefetchScalarGridSpec(
            num_scalar_prefetch=1,
            grid=(B, E),
            in_specs=[pl.BlockSpec((1, 1, NC, LANES), lambda b, e, s: (b, e, 0, 0)),
                      pl.BlockSpec((8, TOKEN_CHUNK), lambda b, e, s: (0, 0))],
            out_specs=pl.BlockSpec((1, 1, nt, 8, LANES), lambda b, e, s: (b, e, 0, 0, 0)),
            scratch_shapes=[pltpu.VMEM((nt, 8, LANES), F32)]),
        out_shape=jax.ShapeDtypeStruct((B, E, nt, 8, LANES), I32),
        compiler_params=_params("arbitrary", "arbitrary"),
        name="moe_slot_tokens",
    )(tiles, pos, jnp.asarray(sel, BF16))
    return idx[:, :, :, 0, :].reshape(-1)


def _ffn_kernel(idx_ref, h_ref, wg_ref, wu_ref, wd_ref, y_ref, x_buf, sem, wgb_ref, wub_ref, wdb_ref, *, tr):
    e, b = pl.program_id(0), pl.program_id(1)
    n_exp, nb = pl.num_programs(0), pl.num_programs(1)
    cap = x_buf.shape[1]
    step = e * nb + b

    def row_copy(st, buf, i):
        eb = st // nb
        bb = st - eb * nb
        tok = idx_ref[(bb * n_exp + eb) * cap + i]
        return pltpu.make_async_copy(h_ref.at[bb, pl.ds(tok, 1), :], x_buf.at[buf, pl.ds(i, 1), :], sem.at[buf])

    cur = jnp.bitwise_and(step, 1)

    @pl.when(step == 0)
    def _():
        lax.fori_loop(0, cap, lambda i, c: (row_copy(step, 0, i).start(), c)[1], 0, unroll=8)

    @pl.when(b == 0)
    def _():
        wgb_ref[...] = wg_ref[0, 0].astype(BF16)
        wub_ref[...] = wu_ref[0, 0].astype(BF16)
        wdb_ref[...] = wd_ref[0, 0].astype(BF16)

    pltpu.make_async_copy(h_ref.at[0, pl.ds(0, cap), :], x_buf.at[cur], sem.at[cur]).wait()

    def compute(prefetch):
        for r in range(cap // tr):
            rows = pl.ds(r * tr, tr)
            x = x_buf[cur, rows, :].astype(BF16)
            if prefetch:
                for i in range(r * tr, (r + 1) * tr):
                    row_copy(step + 1, 1 - cur, i).start()
            g = _dot(x, wgb_ref[...])
            u = _dot(x, wub_ref[...])
            mid = (g * jax.nn.sigmoid(g) * u).astype(BF16)
            y_ref[0, 0, rows, :] = _dot(mid, wdb_ref[...]).astype(BF16)

    is_last = step + 1 == n_exp * nb
    pl.when(jnp.logical_not(is_last))(lambda: compute(True))
    pl.when(is_last)(lambda: compute(False))


def moe_ffn(idx, h, w_gate, w_up, w_down, layer, cap, tr=512):
    B, S, D = h.shape
    E, F = w_gate.shape[1], w_gate.shape[-1]
    wspec = lambda r, c: pl.BlockSpec((1, 1, r, c), lambda e, b, s: (layer, e, 0, 0))
    return pl.pallas_call(
        functools.partial(_ffn_kernel, tr=min(tr, cap)),
        grid_spec=pltpu.PrefetchScalarGridSpec(
            num_scalar_prefetch=1,
            grid=(E, B),
            in_specs=[pl.BlockSpec(memory_space=pl.ANY), wspec(D, F), wspec(D, F), wspec(F, D)],
            out_specs=pl.BlockSpec((1, 1, cap, D), lambda e, b, s: (b, e, 0, 0)),
            scratch_shapes=[pltpu.VMEM((2, cap, D), F32), pltpu.SemaphoreType.DMA((2,)),
                            pltpu.VMEM((D, F), BF16), pltpu.VMEM((D, F), BF16), pltpu.VMEM((F, D), BF16)]),
        out_shape=jax.ShapeDtypeStruct((B, E, cap, D), BF16),
        compiler_params=_params("arbitrary", "arbitrary"),
        name="moe_ffn",
    )(idx, h, w_gate, w_up, w_down)


def _combine_kernel(tiles_ref, wins_ref, fits_ref, post_ref, gate_ref, x_ref, y_ref, spread_ref, *rest):
    (g_ref, o_ref, y_buf) = rest if len(rest) == 3 else (None,) + tuple(rest)
    b, i = pl.program_id(0), pl.program_id(1)
    n_exp = y_ref.shape[1]
    per_step = x_ref.shape[1] // TOKEN_CHUNK
    NC = pl.num_programs(1) * per_step
    slot = lax.broadcasted_iota(I32, (TOKEN_CHUNK, SLOT_WINDOW), 1)
    expert_lane = lax.broadcasted_iota(I32, (1, n_exp), 1)
    slot_in_window = jnp.bitwise_and(lax.broadcasted_iota(I32, (TOKEN_CHUNK, n_exp * NARROW_WINDOW), 1),
                                     NARROW_WINDOW - 1).astype(F32)
    for k in range(per_step):
        rows = slice(k * TOKEN_CHUNK, (k + 1) * TOKEN_CHUNK)
        c = i * per_step + k
        x = x_ref[0, rows, :]

        def narrow():
            wins = [wins_ref[(b * n_exp + e) * NC + c] for e in range(n_exp)]
            for e in range(n_exp):
                y_buf[e * NARROW_WINDOW:(e + 1) * NARROW_WINDOW, :] = (
                    y_ref[0, e, pl.ds(pl.multiple_of(wins[e], 16), NARROW_WINDOW), :])
            start = jnp.zeros((1, n_exp), I32)
            for e in range(n_exp):
                start = jnp.where(expert_lane == e, wins[e], start)
            rel = post_ref[0, rows, :] - start
            rel = jnp.where(jnp.logical_and(rel >= 0, rel < NARROW_WINDOW), rel, 2 * NARROW_WINDOW - 1)
            rel_wide = _dot(rel.astype(F32).astype(BF16), spread_ref[...])
            gate_wide = _dot(gate_ref[0, rows, :].astype(BF16), spread_ref[...])
            select = jnp.where(rel_wide == slot_in_window, gate_wide, 0.0).astype(BF16)
            return x + _dot(select, y_buf[...])

        def wide():
            acc = x
            for e in range(n_exp):
                base = pl.multiple_of(tiles_ref[(b * n_exp + e) * NC + c] * TOKEN_CHUNK, TOKEN_CHUNK)
                select = jnp.where(slot + base == post_ref[0, rows, e:e + 1], gate_ref[0, rows, e:e + 1], 0.0)
                acc = acc + _dot(select.astype(BF16), y_ref[0, e, pl.ds(base, SLOT_WINDOW), :])
            return acc

        acc = lax.cond(fits_ref[b * NC + c] == 1, narrow, wide)
        o_ref[0, rows, :] = acc if g_ref is None else _rms(acc, g_ref[...])


def moe_combine(tiles, wins, fits, pos_t, gate, x, y, out_norm=None, tm=512):
    B, S, D = x.shape
    E, cap = y.shape[1], y.shape[2]
    tok = lambda w: pl.BlockSpec((1, tm, w), lambda b, c, *_: (b, c, 0))
    spread = jnp.asarray(np.kron(np.eye(E), np.ones((1, NARROW_WINDOW))), BF16)
    in_specs = [tok(E), tok(E), tok(D),
                pl.BlockSpec((1, E, cap, D), lambda b, c, *_: (b, 0, 0, 0), pipeline_mode=pl.Buffered(1)),
                pl.BlockSpec(spread.shape, lambda b, c, *_: (0, 0))]
    args = [tiles, wins, fits, pos_t, gate, x, y, spread]
    if out_norm is not None:
        in_specs.append(pl.BlockSpec((1, D), lambda b, c, *_: (0, 0)))
        args.append(out_norm.reshape(1, D))
    return pl.pallas_call(
        _combine_kernel,
        grid_spec=pltpu.PrefetchScalarGridSpec(
            num_scalar_prefetch=3,
            grid=(B, S // tm),
            in_specs=in_specs,
            out_specs=tok(D),
            scratch_shapes=[pltpu.VMEM((E * NARROW_WINDOW, D), BF16)]),
        out_shape=jax.ShapeDtypeStruct((B, S, D), F32),
        compiler_params=_params("arbitrary", "arbitrary"),
        name="moe_combine",
    )(*args)


def expert_choice_moe(x, h, aff_t, batch, w_gate, w_up, w_down, layer, out_norm=None):
    T, D = x.shape
    S = T // batch
    cap = max(1, CAPACITY_FACTOR * S // N_EXPERTS)
    aff = aff_t.T.reshape(batch, S, N_EXPERTS)
    pos, tiles, wins, fits = route_topk(aff_t, batch, cap)
    idx = moe_slot_tokens(tiles, pos, cap)
    y = moe_ffn(idx, h.reshape(batch, S, D), w_gate, w_up, w_down, layer, cap)
    pos_t = jnp.swapaxes(pos.reshape(batch, N_EXPERTS, S), 1, 2)
    return moe_combine(tiles, wins, fits, pos_t, aff, x.reshape(batch, S, D), y, out_norm).reshape(T, D)


ATTN_TQ = 128
ATTN_TK = ATTN_TQ + 2 * HALF_WINDOW
ATTN_SUB = 4


def _t5_bucket(rel):
    half_buckets = REL_BUCKETS // 2
    max_exact = half_buckets // 2
    n = jnp.abs(rel)
    scaled = (jnp.log(jnp.maximum(n, 1).astype(jnp.float32) / max_exact)
              / math.log(REL_MAX_DISTANCE / max_exact))
    large = jnp.minimum(max_exact + jnp.floor(scaled * (half_buckets - max_exact)).astype(jnp.int32),
                        half_buckets - 1)
    return jnp.where(rel > 0, half_buckets, 0) + jnp.where(n < max_exact, n, large)


def _bias_kernel(table_ref, bucket_ref, o_ref):
    bucket = bucket_ref[0]
    q = lax.broadcasted_iota(I32, bucket.shape, 0)
    kc = lax.broadcasted_iota(I32, bucket.shape, 1)
    in_band = jnp.abs(kc - HALF_WINDOW - q) <= HALF_WINDOW
    for h in range(C_HEADS):
        acc = jnp.zeros(bucket.shape, F32)
        for bk in range(REL_BUCKETS):
            acc = jnp.where(bucket == bk, table_ref[bk * C_HEADS + h], acc)
        middle = jnp.where(in_band, acc, NEG_INF)
        o_ref[0, 0, h] = jnp.where(kc >= HALF_WINDOW, middle, NEG_INF)
        o_ref[0, 1, h] = middle
        o_ref[0, 2, h] = jnp.where(kc < HALF_WINDOW + ATTN_TQ, middle, NEG_INF)


def attention_bias(rel_bias):
    rel = np.arange(ATTN_TK)[None, :] - HALF_WINDOW - np.arange(ATTN_TQ)[:, None]
    buckets = jnp.stack([_t5_bucket(jnp.asarray(rel * d, I32)) for d in DILATIONS]).astype(I32)
    P = len(DILATIONS)
    return pl.pallas_call(
        _bias_kernel,
        grid_spec=pltpu.PrefetchScalarGridSpec(
            num_scalar_prefetch=1,
            grid=(P,),
            in_specs=[pl.BlockSpec((1, ATTN_TQ, ATTN_TK), lambda p, t: (p, 0, 0))],
            out_specs=pl.BlockSpec((1, 3, C_HEADS, ATTN_TQ, ATTN_TK), lambda p, t: (p, 0, 0, 0, 0))),
        out_shape=jax.ShapeDtypeStruct((P, 3, C_HEADS, ATTN_TQ, ATTN_TK), F32),
        compiler_params=_params("arbitrary"),
        name="attention_bias",
    )(rel_bias.reshape(-1), buckets)


def _attn_kernel(q_ref, kp_ref, km_ref, kn_ref, vp_ref, vm_ref, vn_ref, bias_ref, o_ref, lse_ref,
                 k_buf, v_buf, s_buf, p_buf):
    i = pl.program_id(2)
    last = pl.num_programs(2) - 1
    hw = HALF_WINDOW
    rows = q_ref.shape[1]
    k_buf[0:hw] = kp_ref[0]
    k_buf[hw:hw + rows] = km_ref[0]
    k_buf[hw + rows:] = kn_ref[0]
    v_buf[0:hw] = vp_ref[0]
    v_buf[hw:hw + rows] = vm_ref[0]
    v_buf[hw + rows:] = vn_ref[0]
    first_head = lax.broadcasted_iota(I32, (ATTN_TQ, LANES), 1) < C_HEAD_DIM
    scale = C_HEAD_DIM ** -0.5
    lse_ref[...] = jnp.zeros_like(lse_ref)
    for sub in range(rows // ATTN_TQ):
        qrows = slice(sub * ATTN_TQ, (sub + 1) * ATTN_TQ)
        krows = slice(sub * ATTN_TQ, sub * ATTN_TQ + ATTN_TK)
        if sub == 0:
            variant = jnp.where(i == 0, 0, 1)
        elif sub == rows // ATTN_TQ - 1:
            variant = jnp.where(i == last, 2, 1)
        else:
            variant = 1
        for pair in range(C_HEADS // 2):
            cols = slice(pair * LANES, (pair + 1) * LANES)
            q = q_ref[0, qrows, cols] * scale
            for half in range(2):
                mine = first_head if half == 0 else jnp.logical_not(first_head)
                s_buf[2 * pair + half] = (_dot_nt(jnp.where(mine, q, jnp.zeros_like(q)), k_buf[krows, cols])
                                          + bias_ref[variant, 2 * pair + half])
        for h in range(C_HEADS):
            s = s_buf[h]
            m = jnp.max(s, axis=-1, keepdims=True)
            p = jnp.exp(s - m)
            den = jnp.sum(p, axis=-1, keepdims=True)
            p_buf[h] = (p / den).astype(BF16)
            lse_ref[0, qrows, h:h + 1] = m + jnp.log(den)
        for pair in range(C_HEADS // 2):
            cols = slice(pair * LANES, (pair + 1) * LANES)
            o_ref[0, qrows, cols] = jnp.where(first_head, _dot(p_buf[2 * pair], v_buf[krows, cols]),
                                              _dot(p_buf[2 * pair + 1], v_buf[krows, cols])).astype(BF16)


def dilated_attention(view, bias, d):
    batch, n, D3 = view.shape
    D = D3 // d // 3
    rows = ATTN_TQ * ATTN_SUB
    nb = n // HALF_WINDOW
    r = rows // HALF_WINDOW
    assert n // rows >= 1 and ATTN_SUB >= 2
    main = lambda c: pl.BlockSpec((1, rows, D), lambda b, j, i: (b, i, 3 * j + c))
    prev = lambda c: pl.BlockSpec((1, HALF_WINDOW, D), lambda b, j, i: (b, jnp.maximum(i * r - 1, 0), 3 * j + c))
    nxt = lambda c: pl.BlockSpec((1, HALF_WINDOW, D),
                                 lambda b, j, i: (b, jnp.minimum(i * r + r, nb - 1), 3 * j + c))
    return pl.pallas_call(
        _attn_kernel,
        grid=(batch, d, n // rows),
        in_specs=[main(0), prev(1), main(1), nxt(1), prev(2), main(2), nxt(2),
                  pl.BlockSpec(bias.shape, lambda b, j, i: (0, 0, 0, 0))],
        out_specs=[pl.BlockSpec((1, rows, D), lambda b, j, i: (b, i, j)),
                   pl.BlockSpec((1, rows, LANES), lambda b, j, i: (b, i, j))],
        out_shape=[jax.ShapeDtypeStruct((batch, n, d * D), BF16),
                   jax.ShapeDtypeStruct((batch, n, d * LANES), F32)],
        scratch_shapes=[pltpu.VMEM((rows + 2 * HALF_WINDOW, D), BF16), pltpu.VMEM((rows + 2 * HALF_WINDOW, D), BF16),
                        pltpu.VMEM((C_HEADS, ATTN_TQ, ATTN_TK), F32), pltpu.VMEM((C_HEADS, ATTN_TQ, ATTN_TK), BF16)],
        compiler_params=_params("parallel", "parallel", "parallel"),
        name="dilated_attention",
    )(view, view, view, view, view, view, view, bias)


def dilated_mixture(views, rel_bias):
    bias = attention_bias(rel_bias)
    results = [dilated_attention(view, bias[p], d) for p, (view, d) in enumerate(zip(views, DILATIONS))]
    return [o for o, _ in results], [lse for _, lse in results]


def kernel(x, mix_norm, ffn_norm, final_norm, w_in_even, w_out_even, a_ln_g, a_ln_b, a_w_s, a_b_s, b_lb_table,
           b_norm_g, w_qkv_odd, w_o_odd, rel_bias, w_router, w_gate, w_up, w_down):
    B, S, D = x.shape
    depth = mix_norm.shape[0]
    xt = x.reshape(B * S, D)
    for layer in range(depth):
        j = layer // 2
        if layer % 2 == 0:
            proj = norm_matmul(xt, mix_norm[layer], w_in_even[j].astype(BF16), B)[0].reshape(B * S, -1)
            o_f, o_b = mixer_b(proj, b_lb_table, layer, B)
            xt, h, aff = outproj_even(proj, a_ln_g[j], a_ln_b[j], a_w_s[j], a_b_s[j], o_f, o_b, b_norm_g[j],
                                      w_out_even[j], xt, ffn_norm[layer], w_router[layer])
        else:
            views = norm_matmul(xt, mix_norm[layer], w_qkv_odd[j].astype(BF16), B, DILATIONS, tm=512)
            outs, lses = dilated_mixture(views, rel_bias)
            xt, h, aff = outproj_odd(outs, lses, w_o_odd[j], xt, ffn_norm[layer], w_router[layer])
        xt = expert_choice_moe(xt, h, aff, B, w_gate, w_up, w_down, layer,
                               out_norm=final_norm if layer == depth - 1 else None)
    return xt.reshape(B, S, D)
```

```python
import functools
import math

import numpy as np
import jax
import jax.numpy as jnp
from jax import lax
from jax.experimental import pallas as pl
from jax.experimental.pallas import tpu as pltpu

F32 = jnp.float32
BF16 = jnp.bfloat16
I32 = jnp.int32
EPS = 1e-6
NEG_INF = -1e30

LANES = 128
VMEM_LIMIT = 56 * 1024 * 1024

A_GROUPS = 4
A_CHUNK = 128
B_HEADS = 4
B_DIM = 128
GLA_CHUNK = 128
C_HEADS = 16
C_HEAD_DIM = 64
HALF_WINDOW = 64
DILATIONS = (1, 4, 16)
REL_BUCKETS = 32
REL_MAX_DISTANCE = 1024
N_EXPERTS = 16
CAPACITY_FACTOR = 2
TOKEN_CHUNK = 128
SLOT_WINDOW = 256
NARROW_WINDOW = 64


def _params(*sem):
    return pltpu.CompilerParams(dimension_semantics=sem, vmem_limit_bytes=VMEM_LIMIT)


def _dot(a, b, **kw):
    return jnp.dot(a, b, preferred_element_type=F32, **kw)


def _dot_nt(a, b):
    return lax.dot_general(a, b, (((1,), (1,)), ((), ())), preferred_element_type=F32)


def _dot_tn(a, b):
    return lax.dot_general(a, b, (((0,), (0,)), ((), ())), preferred_element_type=F32)


def _dot_split01(a2, x):
    hi = x.astype(BF16)
    lo = (x - hi.astype(F32)).astype(BF16)
    return _dot(a2, jnp.concatenate([hi, lo], axis=0))


def _sigmoid(x):
    return 0.5 * jnp.tanh(0.5 * x) + 0.5


def _rms(x, g):
    return x * lax.rsqrt(jnp.mean(x * x, axis=-1, keepdims=True) + EPS) * g


def _norm_matmul_kernel(x_ref, g_ref, w_ref, *rest, tn, dilations):
    o_refs, scratch = rest[:len(dilations)], rest[len(dilations):]
    tm, N = x_ref.shape[0], w_ref.shape[1]
    h = _rms(x_ref[...], g_ref[...]).astype(BF16)
    for j in range(N // tn):
        res = _dot(h, w_ref[:, j * tn:(j + 1) * tn])
        if scratch:
            for t in range(tn // LANES):
                scratch[0][t] = res[:, t * LANES:(t + 1) * LANES]
        for o_ref, d in zip(o_refs, dilations):
            if d == 1:
                o_ref[0, :, j * tn:(j + 1) * tn] = res.astype(BF16)
                continue
            for r in range(d):
                for t in range(tn // LANES):
                    lane0 = r * N + j * tn + t * LANES
                    o_ref[0, :, lane0:lane0 + LANES] = (
                        scratch[0][t, pl.ds(r, tm // d, stride=d), :].astype(BF16))


def norm_matmul(x, g, w, batch, dilations=(1,), tm=1024, tn=512):
    T, D = x.shape
    N = w.shape[1]
    S = T // batch
    tiles = S // tm
    strided = any(d > 1 for d in dilations)
    return pl.pallas_call(
        functools.partial(_norm_matmul_kernel, tn=tn, dilations=dilations),
        grid=(T // tm,),
        in_specs=[
            pl.BlockSpec((tm, D), lambda i: (i, 0)),
            pl.BlockSpec((1, D), lambda i: (0, 0)),
            pl.BlockSpec((D, N), lambda i: (0, 0)),
        ],
        out_specs=[pl.BlockSpec((1, tm // d, d * N), lambda i: (i // tiles, i % tiles, 0)) for d in dilations],
        out_shape=[jax.ShapeDtypeStruct((batch, S // d, d * N), BF16) for d in dilations],
        scratch_shapes=[pltpu.VMEM((tn // LANES, tm, LANES), F32)] if strided else [],
        compiler_params=_params("parallel"),
        name="norm_matmul",
    )(x, g.reshape(1, D), w)


def _spatial_gating(u_ref, v_ref, lg_ref, lb_ref, ws_ref, bs_ref):
    tm = u_ref.shape[0]
    u = jax.nn.gelu(u_ref[...].astype(F32))
    v = jax.nn.gelu(v_ref[...].astype(F32))
    mu = jnp.mean(v, axis=-1, keepdims=True)
    vc = v - mu
    vn = vc * lax.rsqrt(jnp.mean(vc * vc, axis=-1, keepdims=True) + EPS)
    vb = (vn * lg_ref[...] + lb_ref[...]).astype(BF16)
    rows_out = []
    for n in range(tm // A_CHUNK):
        rows = slice(n * A_CHUNK, (n + 1) * A_CHUNK)
        groups = []
        for g in range(A_GROUPS):
            cols = slice(g * LANES, (g + 1) * LANES)
            mixed = _dot(ws_ref[g], vb[rows, cols]) + bs_ref[:, g:g + 1]
            groups.append((u[rows, cols] * mixed).astype(BF16))
        rows_out.append(jnp.concatenate(groups, axis=-1))
    return jnp.concatenate(rows_out, axis=0)


def _gla_consts(C, reverse):
    t = np.arange(C)[:, None]
    r = np.arange(C)[None, :]
    L = int(round(math.log2(C)))
    spans = [(r >= t) if reverse else (r <= t)]
    level = np.where(np.eye(C, dtype=bool), L, -1).astype(np.int32)
    for l in range(L):
        bs = (t >> (l + 1)) << (l + 1)
        mid = bs + (1 << l)
        if reverse:
            act_q = t < mid
            span = np.where(act_q, (r >= t) & (r < mid), (r >= mid) & (r < t))
        else:
            act_q = t >= mid
            span = np.where(act_q, (r >= mid) & (r <= t), (r > t) & (r < mid))
        if l > 0:
            spans.append(span)
        level[(bs == bs.T) & act_q & ~act_q.T] = l
    return np.tile(np.concatenate(spans, axis=0).astype(np.float32), (1, 2)), level


def _gla_kernel(qf_ref, ff_ref, if_ref, qb_ref, fb_ref, ib_ref, tbl_ref, span_ref, level_ref,
                of_ref, ob_ref, state_ref, w_buf, q_buf, k_buf, v_buf, s_buf, p_buf, *, layer, C):
    @pl.when(pl.program_id(1) == 0)
    def _():
        state_ref[...] = jnp.zeros_like(state_ref)

    L = span_ref.shape[1] // C
    subs = qf_ref.shape[0] // C
    odd_row = jnp.bitwise_and(lax.broadcasted_iota(I32, (C, qf_ref.shape[1]), 0), 1) == 1

    def chunk_rows(sub, d):
        k = sub if d == 0 else subs - 1 - sub
        return slice(k * C, (k + 1) * C)

    groups = [(sub, d) for sub in range(subs) for d in range(2)]
    heads = [(sub * 2 + d, sub, d, h, slice(h * B_DIM, (h + 1) * B_DIM)) for sub, d in groups for h in range(B_HEADS)]
    tots = {}
    for sub, d in groups:
        g = sub * 2 + d
        q_ref, f_ref, i_ref = (qf_ref, ff_ref, if_ref) if d == 0 else (qb_ref, fb_ref, ib_ref)
        rows = chunk_rows(sub, d)
        tb = tbl_ref[d]
        e = jnp.exp(tb - jnp.max(tb, axis=0, keepdims=True))
        lb = jnp.sum(e[0:layer + 1], axis=0, keepdims=True) / jnp.sum(e, axis=0, keepdims=True)
        f = lb + (1.0 - lb) * _sigmoid(f_ref[rows, :].astype(F32))
        e2 = _dot_split01(span_ref[d], jnp.log2(f))
        tot2 = e2[0:1] if d == 1 else e2[C - 1:C]
        tots[g] = jnp.exp2(tot2)
        w_buf[g, 0:C] = jnp.exp2(e2[0:C]).astype(BF16)
        w_buf[g, C:2 * C] = jnp.exp2(tot2 - e2[0:C]).astype(BF16)
        w_buf[g, 2 * C:3 * C] = jnp.where(odd_row if d == 0 else jnp.logical_not(odd_row), f, 1.0).astype(BF16)
        w_buf[g, 3 * C:] = jnp.exp2(e2[C:]).astype(BF16)
        qr = q_ref[rows, :].astype(F32)
        q_buf[g] = (qr * _sigmoid(qr)).astype(BF16)
        k_buf[g] = (1.0 - f).astype(BF16)
        v_buf[g] = i_ref[rows, :]
    for g, sub, d, h, sl in heads:
        qh, kh = q_buf[g, :, sl], k_buf[g, :, sl]
        s_buf[g, h, L] = _dot_nt(qh, kh)
        for l in range(L):
            wl = w_buf[g, (l + 2) * C:(l + 3) * C, sl]
            s_buf[g, h, l] = _dot_nt(qh * wl, kh * wl)
    for g, sub, d, h, sl in heads:
        level = level_ref[d]
        p = jnp.where(level == L, s_buf[g, h, L], 0.0)
        for l in range(L):
            p = jnp.where(level == l, s_buf[g, h, l], p)
        p_buf[g, h] = p.astype(BF16)
    for g, sub, d, h, sl in heads:
        o_ref = of_ref if d == 0 else ob_ref
        st = state_ref[d, h]
        vh = v_buf[g, :, sl]
        o_ref[chunk_rows(sub, d), sl] = (_dot(p_buf[g, h], vh) + _dot_nt(
            q_buf[g, :, sl] * w_buf[g, 0:C, sl], st.astype(BF16))).astype(o_ref.dtype)
        state_ref[d, h] = st * tots[g][:, sl] + _dot_tn(vh, k_buf[g, :, sl] * w_buf[g, C:2 * C, sl])


def mixer_b(proj, b_lb_table, layer, batch, C=GLA_CHUNK, subs=4):
    T = proj.shape[0]
    W = B_HEADS * B_DIM
    R = C * subs
    n = T // batch // R
    consts = [_gla_consts(C, rev) for rev in (False, True)]
    span = jnp.asarray(np.stack([c[0] for c in consts]), BF16)
    level = jnp.asarray(np.stack([c[1] for c in consts]))
    fwd = lambda col: pl.BlockSpec((R, W), lambda b, c: (b * n + c, col))
    bwd = lambda col: pl.BlockSpec((R, W), lambda b, c: (b * n + n - 1 - c, col))
    full = lambda a: pl.BlockSpec(a.shape, lambda b, c: (0,) * a.ndim)
    G = 2 * subs
    return pl.pallas_call(
        functools.partial(_gla_kernel, layer=layer, C=C),
        grid=(batch, n),
        in_specs=[fwd(2), fwd(3), fwd(5), bwd(2), bwd(4), bwd(5),
                  full(b_lb_table), full(span), full(level)],
        out_specs=[pl.BlockSpec((R, W), lambda b, c: (b * n + c, 0)),
                   pl.BlockSpec((R, W), lambda b, c: (b * n + n - 1 - c, 0))],
        out_shape=[jax.ShapeDtypeStruct((T, W), BF16)] * 2,
        scratch_shapes=[pltpu.VMEM((2, B_HEADS, B_DIM, B_DIM), F32),
                        pltpu.VMEM((G, span.shape[1] + 2 * C, W), BF16),
                        pltpu.VMEM((G, C, W), BF16), pltpu.VMEM((G, C, W), BF16), pltpu.VMEM((G, C, W), BF16),
                        pltpu.VMEM((G, B_HEADS, span.shape[1] // C + 1, C, C), F32),
                        pltpu.VMEM((G, B_HEADS, C, C), BF16)],
        compiler_params=_params("arbitrary", "arbitrary"),
        name="mixer_b",
    )(proj, proj, proj, proj, proj, proj, b_lb_table, span, level)


def _router_epilogue(x_new, g_ref, wr_ref, x_ref, h_ref, aff_ref):
    x_ref[...] = x_new
    h = _rms(x_new, g_ref[...])
    h_ref[...] = h
    w = wr_ref[...]
    n_exp = w.shape[0]
    h_hi, w_hi = h.astype(BF16), w.astype(BF16)
    h_lo, w_lo = (h - h_hi.astype(F32)).astype(BF16), (w - w_hi.astype(F32)).astype(BF16)
    both = _dot_nt(jnp.concatenate([w_hi, w_lo], axis=0), h_hi)
    logits = both[0:n_exp] + both[n_exp:] + _dot_nt(w_hi, h_lo)
    e = jnp.exp(logits - jnp.max(logits, axis=0, keepdims=True))
    aff_ref[...] = e / jnp.sum(e, axis=0, keepdims=True)


def _outproj_even_kernel(u_ref, v_ref, lg_ref, lb_ref, ws_ref, bs_ref, of_ref, ob_ref, gate_ref, bng_ref, w_ref,
                         x_ref, g_ref, wr_ref, xo_ref, h_ref, aff_ref):
    a = _spatial_gating(u_ref, v_ref, lg_ref, lb_ref, ws_ref, bs_ref)
    o = of_ref[...].astype(F32) + ob_ref[...].astype(F32)
    parts = []
    for h in range(B_HEADS):
        oh = o[:, h * B_DIM:(h + 1) * B_DIM]
        parts.append(oh * lax.rsqrt(jnp.mean(oh * oh, axis=-1, keepdims=True) + EPS))
    on = jnp.concatenate(parts, axis=-1) * bng_ref[...] * jax.nn.sigmoid(gate_ref[...].astype(F32))
    wa = a.shape[1]
    mixed = _dot(a, w_ref[0:wa, :]) + _dot(on.astype(BF16), w_ref[wa:, :])
    _router_epilogue(x_ref[...] + mixed, g_ref, wr_ref, xo_ref, h_ref, aff_ref)


def _router_out(T, D, tm):
    specs = [pl.BlockSpec((tm, D), lambda i: (i, 0)), pl.BlockSpec((tm, D), lambda i: (i, 0)),
             pl.BlockSpec((N_EXPERTS, tm), lambda i: (0, i))]
    shapes = [jax.ShapeDtypeStruct((T, D), F32), jax.ShapeDtypeStruct((T, D), F32),
              jax.ShapeDtypeStruct((N_EXPERTS, T), F32)]
    return specs, shapes


def outproj_even(proj, a_ln_g, a_ln_b, a_w_s, a_b_s, o_f, o_b, b_norm_g, w_out, x, ffn_g, w_router, tm=512):
    T, D = x.shape
    W = o_f.shape[1]
    row = lambda w, col=0: pl.BlockSpec((tm, w), lambda i: (i, col))
    full = lambda s: pl.BlockSpec(s, lambda i: (0,) * len(s))
    out_specs, out_shapes = _router_out(T, D, tm)
    return pl.pallas_call(
        _outproj_even_kernel,
        grid=(T // tm,),
        in_specs=[row(W, 0), row(W, 1), full((1, W)), full((1, W)), full(a_w_s.shape), full(a_b_s.T.shape),
                  row(W), row(W), row(W, 6), full((1, W)), full(w_out.shape), row(D),
                  full((1, D)), full(w_router.T.shape)],
        out_specs=out_specs,
        out_shape=out_shapes,
        compiler_params=_params("parallel"),
        name="outproj_even",
    )(proj, proj, a_ln_g.reshape(1, W), a_ln_b.reshape(1, W), a_w_s.astype(BF16), a_b_s.T,
      o_f, o_b, proj, b_norm_g.reshape(1, W), w_out.astype(BF16), x, ffn_g.reshape(1, D), w_router.T)


def _outproj_odd_kernel(*refs):
    P = len(DILATIONS)
    o_refs, l_refs = refs[:P], refs[P:2 * P]
    ex_ref, w_ref, x_ref, g_ref, wr_ref, xo_ref, h_ref, aff_ref, o_buf, l_buf = refs[2 * P:]
    tm = x_ref.shape[0]

    def token_order(ref, buf, d):
        if d == 1:
            return ref[0].astype(F32)
        tiles = buf.shape[0]
        for r in range(d):
            for t in range(tiles):
                lane0 = (r * tiles + t) * LANES
                buf[t, pl.ds(r, tm // d, stride=d), :] = ref[0, :, lane0:lane0 + LANES].astype(F32)
        return jnp.concatenate([buf[t] for t in range(tiles)], axis=-1)

    ls = [token_order(l_ref, l_buf, d)[:, 0:C_HEADS] for l_ref, d in zip(l_refs, DILATIONS)]
    m = functools.reduce(jnp.maximum, ls)
    es = [jnp.exp(l - m) for l in ls]
    den = functools.reduce(jnp.add, es)
    attn = None
    for e, o_ref, d in zip(es, o_refs, DILATIONS):
        wfull = _dot((e / den).astype(BF16), ex_ref[...])
        term = wfull * token_order(o_ref, o_buf, d)
        attn = term if attn is None else attn + term
    mixed = _dot(attn.astype(BF16), w_ref[...])
    _router_epilogue(x_ref[...] + mixed, g_ref, wr_ref, xo_ref, h_ref, aff_ref)


def outproj_odd(outs, lses, w_o, x, ffn_g, w_router, tm=512):
    T, D = x.shape
    tiles = outs[0].shape[1] // tm
    expand = jnp.asarray(np.kron(np.eye(C_HEADS), np.ones((1, C_HEAD_DIM))), BF16)
    dil = lambda w: [pl.BlockSpec((1, tm // d, d * w), lambda i: (i // tiles, i % tiles, 0)) for d in DILATIONS]
    full = lambda s: pl.BlockSpec(s, lambda i: (0, 0))
    out_specs, out_shapes = _router_out(T, D, tm)
    return pl.pallas_call(
        _outproj_odd_kernel,
        grid=(T // tm,),
        in_specs=dil(D) + dil(LANES) + [full(expand.shape), full(w_o.shape),
                                        pl.BlockSpec((tm, D), lambda i: (i, 0)), full((1, D)),
                                        full(w_router.T.shape)],
        out_specs=out_specs,
        out_shape=out_shapes,
        scratch_shapes=[pltpu.VMEM((D // LANES, tm, LANES), F32), pltpu.VMEM((1, tm, LANES), F32)],
        compiler_params=_params("parallel"),
        name="outproj_odd",
    )(*outs, *lses, expand, w_o.astype(BF16), x, ffn_g.reshape(1, D), w_router.T)


def _topk_kernel(aff_ref, upper_ref, strict_ref, pos_ref, tile_ref, win_ref, fit_ref, *, cap):
    a = aff_ref[...]
    E, NC, _ = a.shape

    def count(mask):
        return jnp.sum(jnp.sum(mask.astype(F32), axis=2, keepdims=True), axis=1, keepdims=True)

    def as_float(bits):
        return lax.bitcast_convert_type(bits, jnp.float32)

    def search(i, thr):
        cand = thr | jnp.left_shift(jnp.int32(1), 30 - i)
        return jnp.where(count(a >= as_float(cand)) >= cap, cand, thr)

    thr = lax.fori_loop(0, 31, search, jnp.zeros((E, 1, 1), I32))
    gt = a >= as_float(thr + 1)
    eq = jnp.logical_and(a >= as_float(thr), jnp.logical_not(gt))
    need = cap - count(gt)

    def prefix(mask):
        m2 = mask.astype(BF16).reshape(E * NC, LANES)
        within = _dot(m2, upper_ref[...])
        total = jnp.broadcast_to(within[:, LANES - 1:LANES], (E * NC, LANES))
        tot_b = total.astype(BF16)
        starts = jnp.concatenate(
            [_dot(strict_ref[...], tot_b[e * NC:(e + 1) * NC]) for e in range(E)], axis=0)
        return [t.reshape(E, NC, LANES) for t in (within + starts, starts, total)]

    eq_incl, _, _ = prefix(eq)
    sel = gt | (eq & (eq_incl - 1.0 < need))
    sel_incl, starts, count = prefix(sel)
    pos_ref[0] = jnp.where(sel, sel_incl - 1.0, -1.0).astype(I32)
    tile = jnp.minimum(jnp.floor(starts * (1.0 / TOKEN_CHUNK)), float((cap - SLOT_WINDOW) // TOKEN_CHUNK))
    tile_ref[0] = tile.astype(I32)
    win = jnp.minimum(jnp.floor(starts * (1.0 / 16)) * 16, float(cap - NARROW_WINDOW))
    win_ref[0] = win.astype(I32)
    fits = jnp.where(starts + count - win <= NARROW_WINDOW, 1.0, 0.0)
    fit_ref[0] = jnp.min(fits, axis=0, keepdims=True).astype(I32)


def route_topk(aff_t, batch, cap):
    E, T = aff_t.shape
    B, S = batch, T // batch
    NC = S // TOKEN_CHUNK
    t = np.arange(LANES)
    upper = jnp.asarray(t[:, None] <= t[None, :], BF16)
    c = np.arange(NC)
    strict = jnp.asarray(c[None, :] < c[:, None], BF16)
    blk = pl.BlockSpec((1, E, NC, LANES), lambda b: (b, 0, 0, 0))
    one = pl.BlockSpec((1, 1, NC, LANES), lambda b: (b, 0, 0, 0))
    pos, tiles, wins, fits = pl.pallas_call(
        functools.partial(_topk_kernel, cap=cap),
        grid=(B,),
        in_specs=[pl.BlockSpec((E, NC, LANES), lambda b: (0, b, 0)),
                  pl.BlockSpec(upper.shape, lambda b: (0, 0)), pl.BlockSpec(strict.shape, lambda b: (0, 0))],
        out_specs=[blk, blk, blk, one],
        out_shape=[jax.ShapeDtypeStruct((B, E, NC, LANES), I32)] * 3 + [jax.ShapeDtypeStruct((B, 1, NC, LANES), I32)],
        compiler_params=_params("parallel"),
        name="route_topk",
    )(aff_t.reshape(E, B * NC, LANES), upper, strict)
    return pos, tiles[..., 0].reshape(-1), wins[..., 0].reshape(-1), fits[..., 0].reshape(-1)


def _index_kernel(tiles_ref, pos_ref, sel_ref, idx_ref, acc_ref):
    b, e = pl.program_id(0), pl.program_id(1)
    n_exp = pl.num_programs(1)
    NC = pos_ref.shape[2]
    acc_ref[...] = jnp.zeros_like(acc_ref)
    slot = lax.broadcasted_iota(I32, (SLOT_WINDOW, TOKEN_CHUNK), 0)

    def body(c, carry):
        j = tiles_ref[(b * n_exp + e) * NC + c]
        onehot = jnp.where(slot + j * TOKEN_CHUNK == pos_ref[0, 0, pl.ds(c, 1), :], 1.0, 0.0).astype(BF16)
        hit = _dot_nt(sel_ref[...], onehot)
        tok = hit[0:1] + hit[1:2] * lax.convert_element_type(c * TOKEN_CHUNK, F32)
        acc_ref[j, 0:1, :] += tok[:, 0:TOKEN_CHUNK]
        acc_ref[j + 1, 0:1, :] += tok[:, TOKEN_CHUNK:]
        return carry

    lax.fori_loop(0, NC, body, 0, unroll=True)
    idx_ref[0, 0] = acc_ref[...].astype(I32)


def moe_slot_tokens(tiles, pos, cap):
    B, E, NC, _ = pos.shape
    nt = cap // TOKEN_CHUNK
    sel = np.zeros((8, TOKEN_CHUNK), np.float32)
    sel[0] = np.arange(TOKEN_CHUNK)
    sel[1] = 1.0
    idx = pl.pallas_call(
        _index_kernel,
        grid_spec=pltpu.PrefetchScalarGridSpec(
            num_scalar_prefetch=1,
            grid=(B, E),
            in_specs=[pl.BlockSpec((1, 1, NC, LANES), lambda b, e, s: (b, e, 0, 0)),
                      pl.BlockSpec((8, TOKEN_CHUNK), lambda b, e, s: (0, 0))],
            out_specs=pl.BlockSpec((1, 1, nt, 8, LANES), lambda b, e, s: (b, e, 0, 0, 0)),
            scratch_shapes=[pltpu.VMEM((nt, 8, LANES), F32)]),
        out_shape=jax.ShapeDtypeStruct((B, E, nt, 8, LANES), I32),
        compiler_params=_params("arbitrary", "arbitrary"),
        name="moe_slot_tokens",
    )(tiles, pos, jnp.asarray(sel, BF16))
    return idx[:, :, :, 0, :].reshape(-1)


def _ffn_kernel(idx_ref, h_ref, wg_ref, wu_ref, wd_ref, y_ref, x_buf, sem, wgb_ref, wub_ref, wdb_ref, *, tr):
    e, b = pl.program_id(0), pl.program_id(1)
    n_exp, nb = pl.num_programs(0), pl.num_programs(1)
    cap = x_buf.shape[1]
    step = e * nb + b

    def row_copy(st, buf, i):
        eb = st // nb
        bb = st - eb * nb
        tok = idx_ref[(bb * n_exp + eb) * cap + i]
        return pltpu.make_async_copy(h_ref.at[bb, pl.ds(tok, 1), :], x_buf.at[buf, pl.ds(i, 1), :], sem.at[buf])

    cur = jnp.bitwise_and(step, 1)

    @pl.when(step == 0)
    def _():
        lax.fori_loop(0, cap, lambda i, c: (row_copy(step, 0, i).start(), c)[1], 0, unroll=8)

    @pl.when(b == 0)
    def _():
        wgb_ref[...] = wg_ref[0, 0].astype(BF16)
        wub_ref[...] = wu_ref[0, 0].astype(BF16)
        wdb_ref[...] = wd_ref[0, 0].astype(BF16)

    pltpu.make_async_copy(h_ref.at[0, pl.ds(0, cap), :], x_buf.at[cur], sem.at[cur]).wait()

    def compute(prefetch):
        for r in range(cap // tr):
            rows = pl.ds(r * tr, tr)
            x = x_buf[cur, rows, :].astype(BF16)
            if prefetch:
                for i in range(r * tr, (r + 1) * tr):
                    row_copy(step + 1, 1 - cur, i).start()
            g = _dot(x, wgb_ref[...])
            u = _dot(x, wub_ref[...])
            mid = (g * jax.nn.sigmoid(g) * u).astype(BF16)
            y_ref[0, 0, rows, :] = _dot(mid, wdb_ref[...]).astype(BF16)

    is_last = step + 1 == n_exp * nb
    pl.when(jnp.logical_not(is_last))(lambda: compute(True))
    pl.when(is_last)(lambda: compute(False))


def moe_ffn(idx, h, w_gate, w_up, w_down, layer, cap, tr=512):
    B, S, D = h.shape
    E, F = w_gate.shape[1], w_gate.shape[-1]
    wspec = lambda r, c: pl.BlockSpec((1, 1, r, c), lambda e, b, s: (layer, e, 0, 0))
    return pl.pallas_call(
        functools.partial(_ffn_kernel, tr=min(tr, cap)),
        grid_spec=pltpu.PrefetchScalarGridSpec(
            num_scalar_prefetch=1,
            grid=(E, B),
            in_specs=[pl.BlockSpec(memory_space=pl.ANY), wspec(D, F), wspec(D, F), wspec(F, D)],
            out_specs=pl.BlockSpec((1, 1, cap, D), lambda e, b, s: (b, e, 0, 0)),
            scratch_shapes=[pltpu.VMEM((2, cap, D), F32), pltpu.SemaphoreType.DMA((2,)),
                            pltpu.VMEM((D, F), BF16), pltpu.VMEM((D, F), BF16), pltpu.VMEM((F, D), BF16)]),
        out_shape=jax.ShapeDtypeStruct((B, E, cap, D), BF16),
        compiler_params=_params("arbitrary", "arbitrary"),
        name="moe_ffn",
    )(idx, h, w_gate, w_up, w_down)


def _combine_kernel(tiles_ref, wins_ref, fits_ref, post_ref, gate_ref, x_ref, y_ref, spread_ref, *rest):
    (g_ref, o_ref, y_buf) = rest if len(rest) == 3 else (None,) + tuple(rest)
    b, i = pl.program_id(0), pl.program_id(1)
    n_exp = y_ref.shape[1]
    per_step = x_ref.shape[1] // TOKEN_CHUNK
    NC = pl.num_programs(1) * per_step
    slot = lax.broadcasted_iota(I32, (TOKEN_CHUNK, SLOT_WINDOW), 1)
    expert_lane = lax.broadcasted_iota(I32, (1, n_exp), 1)
    slot_in_window = jnp.bitwise_and(lax.broadcasted_iota(I32, (TOKEN_CHUNK, n_exp * NARROW_WINDOW), 1),
                                     NARROW_WINDOW - 1).astype(F32)
    for k in range(per_step):
        rows = slice(k * TOKEN_CHUNK, (k + 1) * TOKEN_CHUNK)
        c = i * per_step + k
        x = x_ref[0, rows, :]

        def narrow():
            wins = [wins_ref[(b * n_exp + e) * NC + c] for e in range(n_exp)]
            for e in range(n_exp):
                y_buf[e * NARROW_WINDOW:(e + 1) * NARROW_WINDOW, :] = (
                    y_ref[0, e, pl.ds(pl.multiple_of(wins[e], 16), NARROW_WINDOW), :])
            start = jnp.zeros((1, n_exp), I32)
            for e in range(n_exp):
                start = jnp.where(expert_lane == e, wins[e], start)
            rel = post_ref[0, rows, :] - start
            rel = jnp.where(jnp.logical_and(rel >= 0, rel < NARROW_WINDOW), rel, 2 * NARROW_WINDOW - 1)
            rel_wide = _dot(rel.astype(F32).astype(BF16), spread_ref[...])
            gate_wide = _dot(gate_ref[0, rows, :].astype(BF16), spread_ref[...])
            select = jnp.where(rel_wide == slot_in_window, gate_wide, 0.0).astype(BF16)
            return x + _dot(select, y_buf[...])

        def wide():
            acc = x
            for e in range(n_exp):
                base = pl.multiple_of(tiles_ref[(b * n_exp + e) * NC + c] * TOKEN_CHUNK, TOKEN_CHUNK)
                select = jnp.where(slot + base == post_ref[0, rows, e:e + 1], gate_ref[0, rows, e:e + 1], 0.0)
                acc = acc + _dot(select.astype(BF16), y_ref[0, e, pl.ds(base, SLOT_WINDOW), :])
            return acc

        acc = lax.cond(fits_ref[b * NC + c] == 1, narrow, wide)
        o_ref[0, rows, :] = acc if g_ref is None else _rms(acc, g_ref[...])


def moe_combine(tiles, wins, fits, pos_t, gate, x, y, out_norm=None, tm=1024):
    B, S, D = x.shape
    E, cap = y.shape[1], y.shape[2]
    tok = lambda w: pl.BlockSpec((1, tm, w), lambda b, c, *_: (b, c, 0))
    spread = jnp.asarray(np.kron(np.eye(E), np.ones((1, NARROW_WINDOW))), BF16)
    in_specs = [tok(E), tok(E), tok(D),
                pl.BlockSpec((1, E, cap, D), lambda b, c, *_: (b, 0, 0, 0), pipeline_mode=pl.Buffered(1)),
                pl.BlockSpec(spread.shape, lambda b, c, *_: (0, 0))]
    args = [tiles, wins, fits, pos_t, gate, x, y, spread]
    if out_norm is not None:
        in_specs.append(pl.BlockSpec((1, D), lambda b, c, *_: (0, 0)))
        args.append(out_norm.reshape(1, D))
    return pl.pallas_call(
        _combine_kernel,
        grid_spec=pltpu.PrefetchScalarGridSpec(
            num_scalar_prefetch=3,
            grid=(B, S // tm),
            in_specs=in_specs,
            out_specs=tok(D),
            scratch_shapes=[pltpu.VMEM((E * NARROW_WINDOW, D), BF16)]),
        out_shape=jax.ShapeDtypeStruct((B, S, D), F32),
        compiler_params=_params("arbitrary", "arbitrary"),
        name="moe_combine",
    )(*args)


def expert_choice_moe(x, h, aff_t, batch, w_gate, w_up, w_down, layer, out_norm=None):
    T, D = x.shape
    S = T // batch
    cap = max(1, CAPACITY_FACTOR * S // N_EXPERTS)
    aff = aff_t.T.reshape(batch, S, N_EXPERTS)
    pos, tiles, wins, fits = route_topk(aff_t, batch, cap)
    idx = moe_slot_tokens(tiles, pos, cap)
    y = moe_ffn(idx, h.reshape(batch, S, D), w_gate, w_up, w_down, layer, cap)
    pos_t = jnp.swapaxes(pos.reshape(batch, N_EXPERTS, S), 1, 2)
    return moe_combine(tiles, wins, fits, pos_t, aff, x.reshape(batch, S, D), y, out_norm).reshape(T, D)


ATTN_TQ = 128
ATTN_TK = ATTN_TQ + 2 * HALF_WINDOW
ATTN_SUB = 8


def _t5_bucket(rel):
    half_buckets = REL_BUCKETS // 2
    max_exact = half_buckets // 2
    n = jnp.abs(rel)
    scaled = (jnp.log(jnp.maximum(n, 1).astype(jnp.float32) / max_exact)
              / math.log(REL_MAX_DISTANCE / max_exact))
    large = jnp.minimum(max_exact + jnp.floor(scaled * (half_buckets - max_exact)).astype(jnp.int32),
                        half_buckets - 1)
    return jnp.where(rel > 0, half_buckets, 0) + jnp.where(n < max_exact, n, large)


def _bias_kernel(table_ref, bucket_ref, o_ref):
    bucket = bucket_ref[0]
    q = lax.broadcasted_iota(I32, bucket.shape, 0)
    kc = lax.broadcasted_iota(I32, bucket.shape, 1)
    in_band = jnp.abs(kc - HALF_WINDOW - q) <= HALF_WINDOW
    for h in range(C_HEADS):
        acc = jnp.zeros(bucket.shape, F32)
        for bk in range(REL_BUCKETS):
            acc = jnp.where(bucket == bk, table_ref[bk * C_HEADS + h], acc)
        middle = jnp.where(in_band, acc, NEG_INF)
        o_ref[0, 0, h] = jnp.where(kc >= HALF_WINDOW, middle, NEG_INF)
        o_ref[0, 1, h] = middle
        o_ref[0, 2, h] = jnp.where(kc < HALF_WINDOW + ATTN_TQ, middle, NEG_INF)


def attention_bias(rel_bias):
    rel = np.arange(ATTN_TK)[None, :] - HALF_WINDOW - np.arange(ATTN_TQ)[:, None]
    buckets = jnp.stack([_t5_bucket(jnp.asarray(rel * d, I32)) for d in DILATIONS]).astype(I32)
    P = len(DILATIONS)
    return pl.pallas_call(
        _bias_kernel,
        grid_spec=pltpu.PrefetchScalarGridSpec(
            num_scalar_prefetch=1,
            grid=(P,),
            in_specs=[pl.BlockSpec((1, ATTN_TQ, ATTN_TK), lambda p, t: (p, 0, 0))],
            out_specs=pl.BlockSpec((1, 3, C_HEADS, ATTN_TQ, ATTN_TK), lambda p, t: (p, 0, 0, 0, 0))),
        out_shape=jax.ShapeDtypeStruct((P, 3, C_HEADS, ATTN_TQ, ATTN_TK), F32),
        compiler_params=_params("arbitrary"),
        name="attention_bias",
    )(rel_bias.reshape(-1), buckets)


def _attn_kernel(q_ref, kp_ref, km_ref, kn_ref, vp_ref, vm_ref, vn_ref, bias_ref, o_ref, lse_ref,
                 k_buf, v_buf, s_buf, p_buf):
    i = pl.program_id(2)
    last = pl.num_programs(2) - 1
    hw = HALF_WINDOW
    rows = q_ref.shape[1]
    k_buf[0:hw] = kp_ref[0]
    k_buf[hw:hw + rows] = km_ref[0]
    k_buf[hw + rows:] = kn_ref[0]
    v_buf[0:hw] = vp_ref[0]
    v_buf[hw:hw + rows] = vm_ref[0]
    v_buf[hw + rows:] = vn_ref[0]
    first_head = lax.broadcasted_iota(I32, (ATTN_TQ, LANES), 1) < C_HEAD_DIM
    scale = C_HEAD_DIM ** -0.5
    lse_ref[...] = jnp.zeros_like(lse_ref)
    for sub in range(rows // ATTN_TQ):
        qrows = slice(sub * ATTN_TQ, (sub + 1) * ATTN_TQ)
        krows = slice(sub * ATTN_TQ, sub * ATTN_TQ + ATTN_TK)
        if sub == 0:
            variant = jnp.where(i == 0, 0, 1)
        elif sub == rows // ATTN_TQ - 1:
            variant = jnp.where(i == last, 2, 1)
        else:
            variant = 1
        for pair in range(C_HEADS // 2):
            cols = slice(pair * LANES, (pair + 1) * LANES)
            q = q_ref[0, qrows, cols] * scale
            for half in range(2):
                mine = first_head if half == 0 else jnp.logical_not(first_head)
                s_buf[2 * pair + half] = (_dot_nt(jnp.where(mine, q, jnp.zeros_like(q)), k_buf[krows, cols])
                                          + bias_ref[variant, 2 * pair + half])
        for h in range(C_HEADS):
            s = s_buf[h]
            m = jnp.max(s, axis=-1, keepdims=True)
            p = jnp.exp(s - m)
            den = jnp.sum(p, axis=-1, keepdims=True)
            p_buf[h] = (p / den).astype(BF16)
            lse_ref[0, qrows, h:h + 1] = m + jnp.log(den)
        for pair in range(C_HEADS // 2):
            cols = slice(pair * LANES, (pair + 1) * LANES)
            o_ref[0, qrows, cols] = jnp.where(first_head, _dot(p_buf[2 * pair], v_buf[krows, cols]),
                                              _dot(p_buf[2 * pair + 1], v_buf[krows, cols])).astype(BF16)


def dilated_attention(view, bias, d):
    batch, n, D3 = view.shape
    D = D3 // d // 3
    sub = min(ATTN_SUB, n // ATTN_TQ)
    rows = ATTN_TQ * sub
    nb = n // HALF_WINDOW
    r = rows // HALF_WINDOW
    assert n % rows == 0 and sub >= 2
    main = lambda c: pl.BlockSpec((1, rows, D), lambda b, j, i: (b, i, 3 * j + c))
    prev = lambda c: pl.BlockSpec((1, HALF_WINDOW, D), lambda b, j, i: (b, jnp.maximum(i * r - 1, 0), 3 * j + c))
    nxt = lambda c: pl.BlockSpec((1, HALF_WINDOW, D),
                                 lambda b, j, i: (b, jnp.minimum(i * r + r, nb - 1), 3 * j + c))
    return pl.pallas_call(
        _attn_kernel,
        grid=(batch, d, n // rows),
        in_specs=[main(0), prev(1), main(1), nxt(1), prev(2), main(2), nxt(2),
                  pl.BlockSpec(bias.shape, lambda b, j, i: (0, 0, 0, 0))],
        out_specs=[pl.BlockSpec((1, rows, D), lambda b, j, i: (b, i, j)),
                   pl.BlockSpec((1, rows, LANES), lambda b, j, i: (b, i, j))],
        out_shape=[jax.ShapeDtypeStruct((batch, n, d * D), BF16),
                   jax.ShapeDtypeStruct((batch, n, d * LANES), F32)],
        scratch_shapes=[pltpu.VMEM((rows + 2 * HALF_WINDOW, D), BF16), pltpu.VMEM((rows + 2 * HALF_WINDOW, D), BF16),
                        pltpu.VMEM((C_HEADS, ATTN_TQ, ATTN_TK), F32), pltpu.VMEM((C_HEADS, ATTN_TQ, ATTN_TK), BF16)],
        compiler_params=_params("parallel", "parallel", "parallel"),
        name="dilated_attention",
    )(view, view, view, view, view, view, view, bias)


def dilated_mixture(views, rel_bias):
    bias = attention_bias(rel_bias)
    results = [dilated_attention(view, bias[p], d) for p, (view, d) in enumerate(zip(views, DILATIONS))]
    return [o for o, _ in results], [lse for _, lse in results]


def kernel(x, mix_norm, ffn_norm, final_norm, w_in_even, w_out_even, a_ln_g, a_ln_b, a_w_s, a_b_s, b_lb_table,
           b_norm_g, w_qkv_odd, w_o_odd, rel_bias, w_router, w_gate, w_up, w_down):
    B, S, D = x.shape
    depth = mix_norm.shape[0]
    xt = x.reshape(B * S, D)
    for layer in range(depth):
        j = layer // 2
        if layer % 2 == 0:
            proj = norm_matmul(xt, mix_norm[layer], w_in_even[j].astype(BF16), B)[0].reshape(B * S, -1)
            o_f, o_b = mixer_b(proj, b_lb_table, layer, B)
            xt, h, aff = outproj_even(proj, a_ln_g[j], a_ln_b[j], a_w_s[j], a_b_s[j], o_f, o_b, b_norm_g[j],
                                      w_out_even[j], xt, ffn_norm[layer], w_router[layer])
        else:
            views = norm_matmul(xt, mix_norm[layer], w_qkv_odd[j].astype(BF16), B, DILATIONS, tm=512)
            outs, lses = dilated_mixture(views, rel_bias)
            xt, h, aff = outproj_odd(outs, lses, w_o_odd[j], xt, ffn_norm[layer], w_router[layer])
        xt = expert_choice_moe(xt, h, aff, B, w_gate, w_up, w_down, layer,
                               out_norm=final_norm if layer == depth - 1 else None)
    return xt.reshape(B, S, D)
```

```python
import functools
import math

import numpy as np
import jax
import jax.numpy as jnp
from jax import lax
from jax.experimental import pallas as pl
from jax.experimental.pallas import tpu as pltpu

F32 = jnp.float32
BF16 = jnp.bfloat16
I32 = jnp.int32
EPS = 1e-6
NEG_INF = -1e30

LANES = 128
VMEM_LIMIT = 56 * 1024 * 1024

A_GROUPS = 4
A_CHUNK = 128
B_HEADS = 4
B_DIM = 128
GLA_CHUNK = 128
C_HEADS = 16
C_HEAD_DIM = 64
HALF_WINDOW = 64
DILATIONS = (1, 4, 16)
REL_BUCKETS = 32
REL_MAX_DISTANCE = 1024
N_EXPERTS = 16
CAPACITY_FACTOR = 2
TOKEN_CHUNK = 128
SLOT_WINDOW = 256
NARROW_WINDOW = 64


def _params(*sem):
    return pltpu.CompilerParams(dimension_semantics=sem, vmem_limit_bytes=VMEM_LIMIT)


def _dot(a, b, **kw):
    return jnp.dot(a, b, preferred_element_type=F32, **kw)


def _dot_nt(a, b):
    return lax.dot_general(a, b, (((1,), (1,)), ((), ())), preferred_element_type=F32)


def _dot_tn(a, b):
    return lax.dot_general(a, b, (((0,), (0,)), ((), ())), preferred_element_type=F32)


def _dot_split01(a2, x):
    hi = x.astype(BF16)
    lo = (x - hi.astype(F32)).astype(BF16)
    return _dot(a2, jnp.concatenate([hi, lo], axis=0))


def _sigmoid(x):
    return 0.5 * jnp.tanh(0.5 * x) + 0.5


def _rms(x, g):
    return x * lax.rsqrt(jnp.mean(x * x, axis=-1, keepdims=True) + EPS) * g


def _norm_matmul_kernel(x_ref, g_ref, w_ref, *rest, tn, dilations):
    o_refs, scratch = rest[:len(dilations)], rest[len(dilations):]
    tm, N = x_ref.shape[0], w_ref.shape[1]
    h = _rms(x_ref[...], g_ref[...]).astype(BF16)
    for j in range(N // tn):
        res = _dot(h, w_ref[:, j * tn:(j + 1) * tn])
        if scratch:
            for t in range(tn // LANES):
                scratch[0][t] = res[:, t * LANES:(t + 1) * LANES]
        for o_ref, d in zip(o_refs, dilations):
            if d == 1:
                o_ref[0, :, j * tn:(j + 1) * tn] = res.astype(BF16)
                continue
            for r in range(d):
                for t in range(tn // LANES):
                    lane0 = r * N + j * tn + t * LANES
                    o_ref[0, :, lane0:lane0 + LANES] = (
                        scratch[0][t, pl.ds(r, tm // d, stride=d), :].astype(BF16))


def norm_matmul(x, g, w, batch, dilations=(1,), tm=1024, tn=512):
    T, D = x.shape
    N = w.shape[1]
    S = T // batch
    tiles = S // tm
    strided = any(d > 1 for d in dilations)
    return pl.pallas_call(
        functools.partial(_norm_matmul_kernel, tn=tn, dilations=dilations),
        grid=(T // tm,),
        in_specs=[
            pl.BlockSpec((tm, D), lambda i: (i, 0)),
            pl.BlockSpec((1, D), lambda i: (0, 0)),
            pl.BlockSpec((D, N), lambda i: (0, 0)),
        ],
        out_specs=[pl.BlockSpec((1, tm // d, d * N), lambda i: (i // tiles, i % tiles, 0)) for d in dilations],
        out_shape=[jax.ShapeDtypeStruct((batch, S // d, d * N), BF16) for d in dilations],
        scratch_shapes=[pltpu.VMEM((tn // LANES, tm, LANES), F32)] if strided else [],
        compiler_params=_params("parallel"),
        name="norm_matmul",
    )(x, g.reshape(1, D), w)


def _spatial_gating(u_ref, v_ref, lg_ref, lb_ref, ws_ref, bs_ref):
    tm = u_ref.shape[0]
    u = jax.nn.gelu(u_ref[...].astype(F32))
    v = jax.nn.gelu(v_ref[...].astype(F32))
    mu = jnp.mean(v, axis=-1, keepdims=True)
    vc = v - mu
    vn = vc * lax.rsqrt(jnp.mean(vc * vc, axis=-1, keepdims=True) + EPS)
    vb = (vn * lg_ref[...] + lb_ref[...]).astype(BF16)
    rows_out = []
    for n in range(tm // A_CHUNK):
        rows = slice(n * A_CHUNK, (n + 1) * A_CHUNK)
        groups = []
        for g in range(A_GROUPS):
            cols = slice(g * LANES, (g + 1) * LANES)
            mixed = _dot(ws_ref[g], vb[rows, cols]) + bs_ref[:, g:g + 1]
            groups.append((u[rows, cols] * mixed).astype(BF16))
        rows_out.append(jnp.concatenate(groups, axis=-1))
    return jnp.concatenate(rows_out, axis=0)


def _gla_consts(C, reverse):
    t = np.arange(C)[:, None]
    r = np.arange(C)[None, :]
    L = int(round(math.log2(C)))
    spans = [(r >= t) if reverse else (r <= t)]
    level = np.where(np.eye(C, dtype=bool), L, -1).astype(np.int32)
    for l in range(L):
        bs = (t >> (l + 1)) << (l + 1)
        mid = bs + (1 << l)
        if reverse:
            act_q = t < mid
            span = np.where(act_q, (r >= t) & (r < mid), (r >= mid) & (r < t))
        else:
            act_q = t >= mid
            span = np.where(act_q, (r >= mid) & (r <= t), (r > t) & (r < mid))
        if l > 0:
            spans.append(span)
        level[(bs == bs.T) & act_q & ~act_q.T] = l
    return np.tile(np.concatenate(spans, axis=0).astype(np.float32), (1, 2)), level


def _gla_kernel(qf_ref, ff_ref, if_ref, qb_ref, fb_ref, ib_ref, tbl_ref, span_ref, level_ref,
                of_ref, ob_ref, state_ref, w_buf, q_buf, k_buf, v_buf, s_buf, p_buf, *, layer, C):
    @pl.when(pl.program_id(1) == 0)
    def _():
        state_ref[...] = jnp.zeros_like(state_ref)

    L = span_ref.shape[1] // C
    subs = qf_ref.shape[0] // C
    odd_row = jnp.bitwise_and(lax.broadcasted_iota(I32, (C, qf_ref.shape[1]), 0), 1) == 1

    def chunk_rows(sub, d):
        k = sub if d == 0 else subs - 1 - sub
        return slice(k * C, (k + 1) * C)

    groups = [(sub, d) for sub in range(subs) for d in range(2)]
    heads = [(sub * 2 + d, sub, d, h, slice(h * B_DIM, (h + 1) * B_DIM)) for sub, d in groups for h in range(B_HEADS)]
    tots = {}
    for sub, d in groups:
        g = sub * 2 + d
        q_ref, f_ref, i_ref = (qf_ref, ff_ref, if_ref) if d == 0 else (qb_ref, fb_ref, ib_ref)
        rows = chunk_rows(sub, d)
        tb = tbl_ref[d]
        e = jnp.exp(tb - jnp.max(tb, axis=0, keepdims=True))
        lb = jnp.sum(e[0:layer + 1], axis=0, keepdims=True) / jnp.sum(e, axis=0, keepdims=True)
        f = lb + (1.0 - lb) * _sigmoid(f_ref[rows, :].astype(F32))
        e2 = _dot_split01(span_ref[d], jnp.log2(f))
        tot2 = e2[0:1] if d == 1 else e2[C - 1:C]
        tots[g] = jnp.exp2(tot2)
        w_buf[g, 0:C] = jnp.exp2(e2[0:C]).astype(BF16)
        w_buf[g, C:2 * C] = jnp.exp2(tot2 - e2[0:C]).astype(BF16)
        w_buf[g, 2 * C:3 * C] = jnp.where(odd_row if d == 0 else jnp.logical_not(odd_row), f, 1.0).astype(BF16)
        w_buf[g, 3 * C:] = jnp.exp2(e2[C:]).astype(BF16)
        qr = q_ref[rows, :].astype(F32)
        q_buf[g] = (qr * _sigmoid(qr)).astype(BF16)
        k_buf[g] = (1.0 - f).astype(BF16)
        v_buf[g] = i_ref[rows, :]
    for g, sub, d, h, sl in heads:
        qh, kh = q_buf[g, :, sl], k_buf[g, :, sl]
        s_buf[g, h, L] = _dot_nt(qh, kh)
        for l in range(L):
            wl = w_buf[g, (l + 2) * C:(l + 3) * C, sl]
            s_buf[g, h, l] = _dot_nt(qh * wl, kh * wl)
    for g, sub, d, h, sl in heads:
        level = level_ref[d]
        p = jnp.where(level == L, s_buf[g, h, L], 0.0)
        for l in range(L):
            p = jnp.where(level == l, s_buf[g, h, l], p)
        p_buf[g, h] = p.astype(BF16)
    for g, sub, d, h, sl in heads:
        o_ref = of_ref if d == 0 else ob_ref
        st = state_ref[d, h]
        vh = v_buf[g, :, sl]
        o_ref[chunk_rows(sub, d), sl] = (_dot(p_buf[g, h], vh) + _dot_nt(
            q_buf[g, :, sl] * w_buf[g, 0:C, sl], st.astype(BF16))).astype(o_ref.dtype)
        state_ref[d, h] = st * tots[g][:, sl] + _dot_tn(vh, k_buf[g, :, sl] * w_buf[g, C:2 * C, sl])


def mixer_b(proj, b_lb_table, layer, batch, C=GLA_CHUNK, subs=4):
    T = proj.shape[0]
    W = B_HEADS * B_DIM
    R = C * subs
    n = T // batch // R
    consts = [_gla_consts(C, rev) for rev in (False, True)]
    span = jnp.asarray(np.stack([c[0] for c in consts]), BF16)
    level = jnp.asarray(np.stack([c[1] for c in consts]))
    fwd = lambda col: pl.BlockSpec((R, W), lambda b, c: (b * n + c, col))
    bwd = lambda col: pl.BlockSpec((R, W), lambda b, c: (b * n + n - 1 - c, col))
    full = lambda a: pl.BlockSpec(a.shape, lambda b, c: (0,) * a.ndim)
    G = 2 * subs
    return pl.pallas_call(
        functools.partial(_gla_kernel, layer=layer, C=C),
        grid=(batch, n),
        in_specs=[fwd(2), fwd(3), fwd(5), bwd(2), bwd(4), bwd(5),
                  full(b_lb_table), full(span), full(level)],
        out_specs=[pl.BlockSpec((R, W), lambda b, c: (b * n + c, 0)),
                   pl.BlockSpec((R, W), lambda b, c: (b * n + n - 1 - c, 0))],
        out_shape=[jax.ShapeDtypeStruct((T, W), BF16)] * 2,
        scratch_shapes=[pltpu.VMEM((2, B_HEADS, B_DIM, B_DIM), F32),
                        pltpu.VMEM((G, span.shape[1] + 2 * C, W), BF16),
                        pltpu.VMEM((G, C, W), BF16), pltpu.VMEM((G, C, W), BF16), pltpu.VMEM((G, C, W), BF16),
                        pltpu.VMEM((G, B_HEADS, span.shape[1] // C + 1, C, C), F32),
                        pltpu.VMEM((G, B_HEADS, C, C), BF16)],
        compiler_params=_params("arbitrary", "arbitrary"),
        name="mixer_b",
    )(proj, proj, proj, proj, proj, proj, b_lb_table, span, level)


def _router_epilogue(x_new, g_ref, wr_ref, x_ref, h_ref, aff_ref):
    x_ref[...] = x_new
    h = _rms(x_new, g_ref[...])
    h_ref[...] = h
    w = wr_ref[...]
    n_exp = w.shape[0]
    h_hi, w_hi = h.astype(BF16), w.astype(BF16)
    h_lo, w_lo = (h - h_hi.astype(F32)).astype(BF16), (w - w_hi.astype(F32)).astype(BF16)
    both = _dot_nt(jnp.concatenate([w_hi, w_lo], axis=0), h_hi)
    logits = both[0:n_exp] + both[n_exp:] + _dot_nt(w_hi, h_lo)
    e = jnp.exp(logits - jnp.max(logits, axis=0, keepdims=True))
    aff_ref[...] = e / jnp.sum(e, axis=0, keepdims=True)


def _outproj_even_kernel(u_ref, v_ref, lg_ref, lb_ref, ws_ref, bs_ref, of_ref, ob_ref, gate_ref, bng_ref, w_ref,
                         x_ref, g_ref, wr_ref, xo_ref, h_ref, aff_ref):
    a = _spatial_gating(u_ref, v_ref, lg_ref, lb_ref, ws_ref, bs_ref)
    o = of_ref[...].astype(F32) + ob_ref[...].astype(F32)
    parts = []
    for h in range(B_HEADS):
        oh = o[:, h * B_DIM:(h + 1) * B_DIM]
        parts.append(oh * lax.rsqrt(jnp.mean(oh * oh, axis=-1, keepdims=True) + EPS))
    on = jnp.concatenate(parts, axis=-1) * bng_ref[...] * jax.nn.sigmoid(gate_ref[...].astype(F32))
    wa = a.shape[1]
    mixed = _dot(a, w_ref[0:wa, :]) + _dot(on.astype(BF16), w_ref[wa:, :])
    _router_epilogue(x_ref[...] + mixed, g_ref, wr_ref, xo_ref, h_ref, aff_ref)


def _router_out(T, D, tm):
    specs = [pl.BlockSpec((tm, D), lambda i: (i, 0)), pl.BlockSpec((tm, D), lambda i: (i, 0)),
             pl.BlockSpec((N_EXPERTS, tm), lambda i: (0, i))]
    shapes = [jax.ShapeDtypeStruct((T, D), F32), jax.ShapeDtypeStruct((T, D), F32),
              jax.ShapeDtypeStruct((N_EXPERTS, T), F32)]
    return specs, shapes


def outproj_even(proj, a_ln_g, a_ln_b, a_w_s, a_b_s, o_f, o_b, b_norm_g, w_out, x, ffn_g, w_router, tm=512):
    T, D = x.shape
    W = o_f.shape[1]
    row = lambda w, col=0: pl.BlockSpec((tm, w), lambda i: (i, col))
    full = lambda s: pl.BlockSpec(s, lambda i: (0,) * len(s))
    out_specs, out_shapes = _router_out(T, D, tm)
    return pl.pallas_call(
        _outproj_even_kernel,
        grid=(T // tm,),
        in_specs=[row(W, 0), row(W, 1), full((1, W)), full((1, W)), full(a_w_s.shape), full(a_b_s.T.shape),
                  row(W), row(W), row(W, 6), full((1, W)), full(w_out.shape), row(D),
                  full((1, D)), full(w_router.T.shape)],
        out_specs=out_specs,
        out_shape=out_shapes,
        compiler_params=_params("parallel"),
        name="outproj_even",
    )(proj, proj, a_ln_g.reshape(1, W), a_ln_b.reshape(1, W), a_w_s.astype(BF16), a_b_s.T,
      o_f, o_b, proj, b_norm_g.reshape(1, W), w_out.astype(BF16), x, ffn_g.reshape(1, D), w_router.T)


def _outproj_odd_kernel(*refs):
    P = len(DILATIONS)
    o_refs, l_refs = refs[:P], refs[P:2 * P]
    ex_ref, w_ref, x_ref, g_ref, wr_ref, xo_ref, h_ref, aff_ref, o_buf, l_buf = refs[2 * P:]
    tm = x_ref.shape[0]

    def token_order(ref, buf, d):
        if d == 1:
            return ref[0].astype(F32)
        tiles = buf.shape[0]
        for r in range(d):
            for t in range(tiles):
                lane0 = (r * tiles + t) * LANES
                buf[t, pl.ds(r, tm // d, stride=d), :] = ref[0, :, lane0:lane0 + LANES].astype(F32)
        return jnp.concatenate([buf[t] for t in range(tiles)], axis=-1)

    ls = [token_order(l_ref, l_buf, d)[:, 0:C_HEADS] for l_ref, d in zip(l_refs, DILATIONS)]
    m = functools.reduce(jnp.maximum, ls)
    es = [jnp.exp(l - m) for l in ls]
    den = functools.reduce(jnp.add, es)
    attn = None
    for e, o_ref, d in zip(es, o_refs, DILATIONS):
        wfull = _dot((e / den).astype(BF16), ex_ref[...])
        term = wfull * token_order(o_ref, o_buf, d)
        attn = term if attn is None else attn + term
    mixed = _dot(attn.astype(BF16), w_ref[...])
    _router_epilogue(x_ref[...] + mixed, g_ref, wr_ref, xo_ref, h_ref, aff_ref)


def outproj_odd(outs, lses, w_o, x, ffn_g, w_router, tm=512):
    T, D = x.shape
    tiles = outs[0].shape[1] // tm
    expand = jnp.asarray(np.kron(np.eye(C_HEADS), np.ones((1, C_HEAD_DIM))), BF16)
    dil = lambda w: [pl.BlockSpec((1, tm // d, d * w), lambda i: (i // tiles, i % tiles, 0)) for d in DILATIONS]
    full = lambda s: pl.BlockSpec(s, lambda i: (0, 0))
    out_specs, out_shapes = _router_out(T, D, tm)
    return pl.pallas_call(
        _outproj_odd_kernel,
        grid=(T // tm,),
        in_specs=dil(D) + dil(LANES) + [full(expand.shape), full(w_o.shape),
                                        pl.BlockSpec((tm, D), lambda i: (i, 0)), full((1, D)),
                                        full(w_router.T.shape)],
        out_specs=out_specs,
        out_shape=out_shapes,
        scratch_shapes=[pltpu.VMEM((D // LANES, tm, LANES), F32), pltpu.VMEM((1, tm, LANES), F32)],
        compiler_params=_params("parallel"),
        name="outproj_odd",
    )(*outs, *lses, expand, w_o.astype(BF16), x, ffn_g.reshape(1, D), w_router.T)


def _topk_kernel(aff_ref, upper_ref, strict_ref, pos_ref, tile_ref, win_ref, fit_ref, *, cap):
    a = aff_ref[...]
    E, NC, _ = a.shape

    def count(mask):
        return jnp.sum(jnp.sum(mask.astype(F32), axis=2, keepdims=True), axis=1, keepdims=True)

    def as_float(bits):
        return lax.bitcast_convert_type(bits, jnp.float32)

    def search(i, thr):
        cand = thr | jnp.left_shift(jnp.int32(1), 30 - i)
        return jnp.where(count(a >= as_float(cand)) >= cap, cand, thr)

    thr = lax.fori_loop(0, 31, search, jnp.zeros((E, 1, 1), I32))
    gt = a >= as_float(thr + 1)
    eq = jnp.logical_and(a >= as_float(thr), jnp.logical_not(gt))
    need = cap - count(gt)

    def prefix(mask):
        m2 = mask.astype(BF16).reshape(E * NC, LANES)
        within = _dot(m2, upper_ref[...])
        total = jnp.broadcast_to(within[:, LANES - 1:LANES], (E * NC, LANES))
        tot_b = total.astype(BF16)
        starts = jnp.concatenate(
            [_dot(strict_ref[...], tot_b[e * NC:(e + 1) * NC]) for e in range(E)], axis=0)
        return [t.reshape(E, NC, LANES) for t in (within + starts, starts, total)]

    eq_incl, _, _ = prefix(eq)
    sel = gt | (eq & (eq_incl - 1.0 < need))
    sel_incl, starts, count = prefix(sel)
    pos_ref[0] = jnp.where(sel, sel_incl - 1.0, -1.0).astype(I32)
    tile = jnp.minimum(jnp.floor(starts * (1.0 / TOKEN_CHUNK)), float((cap - SLOT_WINDOW) // TOKEN_CHUNK))
    tile_ref[0] = tile.astype(I32)
    win = jnp.minimum(jnp.floor(starts * (1.0 / 16)) * 16, float(cap - NARROW_WINDOW))
    win_ref[0] = win.astype(I32)
    fits = jnp.where(starts + count - win <= NARROW_WINDOW, 1.0, 0.0)
    fit_ref[0] = jnp.min(fits, axis=0, keepdims=True).astype(I32)


def route_topk(aff_t, batch, cap):
    E, T = aff_t.shape
    B, S = batch, T // batch
    NC = S // TOKEN_CHUNK
    t = np.arange(LANES)
    upper = jnp.asarray(t[:, None] <= t[None, :], BF16)
    c = np.arange(NC)
    strict = jnp.asarray(c[None, :] < c[:, None], BF16)
    blk = pl.BlockSpec((1, E, NC, LANES), lambda b: (b, 0, 0, 0))
    one = pl.BlockSpec((1, 1, NC, LANES), lambda b: (b, 0, 0, 0))
    pos, tiles, wins, fits = pl.pallas_call(
        functools.partial(_topk_kernel, cap=cap),
        grid=(B,),
        in_specs=[pl.BlockSpec((E, NC, LANES), lambda b: (0, b, 0)),
                  pl.BlockSpec(upper.shape, lambda b: (0, 0)), pl.BlockSpec(strict.shape, lambda b: (0, 0))],
        out_specs=[blk, blk, blk, one],
        out_shape=[jax.ShapeDtypeStruct((B, E, NC, LANES), I32)] * 3 + [jax.ShapeDtypeStruct((B, 1, NC, LANES), I32)],
        compiler_params=_params("parallel"),
        name="route_topk",
    )(aff_t.reshape(E, B * NC, LANES), upper, strict)
    return pos, tiles[..., 0].reshape(-1), wins[..., 0].reshape(-1), fits[..., 0].reshape(-1)


def _index_kernel(tiles_ref, pos_ref, sel_ref, idx_ref, acc_ref):
    b, e = pl.program_id(0), pl.program_id(1)
    n_exp = pl.num_programs(1)
    NC = pos_ref.shape[2]
    acc_ref[...] = jnp.zeros_like(acc_ref)
    slot = lax.broadcasted_iota(I32, (SLOT_WINDOW, TOKEN_CHUNK), 0)

    def body(c, carry):
        j = tiles_ref[(b * n_exp + e) * NC + c]
        onehot = jnp.where(slot + j * TOKEN_CHUNK == pos_ref[0, 0, pl.ds(c, 1), :], 1.0, 0.0).astype(BF16)
        hit = _dot_nt(sel_ref[...], onehot)
        tok = hit[0:1] + hit[1:2] * lax.convert_element_type(c * TOKEN_CHUNK, F32)
        acc_ref[j, 0:1, :] += tok[:, 0:TOKEN_CHUNK]
        acc_ref[j + 1, 0:1, :] += tok[:, TOKEN_CHUNK:]
        return carry

    lax.fori_loop(0, NC, body, 0, unroll=True)
    idx_ref[0, 0] = acc_ref[...].astype(I32)


def moe_slot_tokens(tiles, pos, cap):
    B, E, NC, _ = pos.shape
    nt = cap // TOKEN_CHUNK
    sel = np.zeros((8, TOKEN_CHUNK), np.float32)
    sel[0] = np.arange(TOKEN_CHUNK)
    sel[1] = 1.0
    idx = pl.pallas_call(
        _index_kernel,
        grid_spec=pltpu.PrefetchScalarGridSpec(
            num_scalar_prefetch=1,
            grid=(B, E),
            in_specs=[pl.BlockSpec((1, 1, NC, LANES), lambda b, e, s: (b, e, 0, 0)),
                      pl.BlockSpec((8, TOKEN_CHUNK), lambda b, e, s: (0, 0))],
            out_specs=pl.BlockSpec((1, 1, nt, 8, LANES), lambda b, e, s: (b, e, 0, 0, 0)),
            scratch_shapes=[pltpu.VMEM((nt, 8, LANES), F32)]),
        out_shape=jax.ShapeDtypeStruct((B, E, nt, 8, LANES), I32),
        compiler_params=_params("arbitrary", "arbitrary"),
        name="moe_slot_tokens",
    )(tiles, pos, jnp.asarray(sel, BF16))
    return idx[:, :, :, 0, :].reshape(-1)


def _ffn_kernel(idx_ref, h_ref, wg_ref, wu_ref, wd_ref, y_ref, x_buf, sem, wgb_ref, wub_ref, wdb_ref, *, tr):
    e, b = pl.program_id(0), pl.program_id(1)
    n_exp, nb = pl.num_programs(0), pl.num_programs(1)
    cap = x_buf.shape[1]
    step = e * nb + b

    def row_copy(st, buf, i):
        eb = st // nb
        bb = st - eb * nb
        tok = idx_ref[(bb * n_exp + eb) * cap + i]
        return pltpu.make_async_copy(h_ref.at[bb, pl.ds(tok, 1), :], x_buf.at[buf, pl.ds(i, 1), :], sem.at[buf])

    cur = jnp.bitwise_and(step, 1)

    @pl.when(step == 0)
    def _():
        lax.fori_loop(0, cap, lambda i, c: (row_copy(step, 0, i).start(), c)[1], 0, unroll=8)

    @pl.when(b == 0)
    def _():
        wgb_ref[...] = wg_ref[0, 0].astype(BF16)
        wub_ref[...] = wu_ref[0, 0].astype(BF16)
        wdb_ref[...] = wd_ref[0, 0].astype(BF16)

    pltpu.make_async_copy(h_ref.at[0, pl.ds(0, cap), :], x_buf.at[cur], sem.at[cur]).wait()

    def compute(prefetch):
        for r in range(cap // tr):
            rows = pl.ds(r * tr, tr)
            x = x_buf[cur, rows, :].astype(BF16)
            if prefetch:
                for i in range(r * tr, (r + 1) * tr):
                    row_copy(step + 1, 1 - cur, i).start()
            g = _dot(x, wgb_ref[...])
            u = _dot(x, wub_ref[...])
            mid = (g * jax.nn.sigmoid(g) * u).astype(BF16)
            y_ref[0, 0, rows, :] = _dot(mid, wdb_ref[...]).astype(BF16)

    is_last = step + 1 == n_exp * nb
    pl.when(jnp.logical_not(is_last))(lambda: compute(True))
    pl.when(is_last)(lambda: compute(False))


def moe_ffn(idx, h, w_gate, w_up, w_down, layer, cap, tr=1024):
    B, S, D = h.shape
    E, F = w_gate.shape[1], w_gate.shape[-1]
    wspec = lambda r, c: pl.BlockSpec((1, 1, r, c), lambda e, b, s: (layer, e, 0, 0))
    return pl.pallas_call(
        functools.partial(_ffn_kernel, tr=min(tr, cap)),
        grid_spec=pltpu.PrefetchScalarGridSpec(
            num_scalar_prefetch=1,
            grid=(E, B),
            in_specs=[pl.BlockSpec(memory_space=pl.ANY), wspec(D, F), wspec(D, F), wspec(F, D)],
            out_specs=pl.BlockSpec((1, 1, cap, D), lambda e, b, s: (b, e, 0, 0)),
            scratch_shapes=[pltpu.VMEM((2, cap, D), F32), pltpu.SemaphoreType.DMA((2,)),
                            pltpu.VMEM((D, F), BF16), pltpu.VMEM((D, F), BF16), pltpu.VMEM((F, D), BF16)]),
        out_shape=jax.ShapeDtypeStruct((B, E, cap, D), BF16),
        compiler_params=_params("arbitrary", "arbitrary"),
        name="moe_ffn",
    )(idx, h, w_gate, w_up, w_down)


def _combine_kernel(tiles_ref, wins_ref, fits_ref, post_ref, gate_ref, x_ref, y_ref, spread_ref, *rest):
    (g_ref, o_ref, y_buf) = rest if len(rest) == 3 else (None,) + tuple(rest)
    b, i = pl.program_id(0), pl.program_id(1)
    n_exp = y_ref.shape[1]
    per_step = x_ref.shape[1] // TOKEN_CHUNK
    NC = pl.num_programs(1) * per_step
    slot = lax.broadcasted_iota(I32, (TOKEN_CHUNK, SLOT_WINDOW), 1)
    expert_lane = lax.broadcasted_iota(I32, (1, n_exp), 1)
    slot_in_window = jnp.bitwise_and(lax.broadcasted_iota(I32, (TOKEN_CHUNK, n_exp * NARROW_WINDOW), 1),
                                     NARROW_WINDOW - 1).astype(F32)
    for k in range(per_step):
        rows = slice(k * TOKEN_CHUNK, (k + 1) * TOKEN_CHUNK)
        c = i * per_step + k
        x = x_ref[0, rows, :]

        def narrow():
            wins = [wins_ref[(b * n_exp + e) * NC + c] for e in range(n_exp)]
            for e in range(n_exp):
                y_buf[e * NARROW_WINDOW:(e + 1) * NARROW_WINDOW, :] = (
                    y_ref[0, e, pl.ds(pl.multiple_of(wins[e], 16), NARROW_WINDOW), :])
            start = jnp.zeros((1, n_exp), I32)
            for e in range(n_exp):
                start = jnp.where(expert_lane == e, wins[e], start)
            rel = post_ref[0, rows, :] - start
            rel = jnp.where(jnp.logical_and(rel >= 0, rel < NARROW_WINDOW), rel, 2 * NARROW_WINDOW - 1)
            rel_wide = _dot(rel.astype(F32).astype(BF16), spread_ref[...])
            gate_wide = _dot(gate_ref[0, rows, :].astype(BF16), spread_ref[...])
            select = jnp.where(rel_wide == slot_in_window, gate_wide, 0.0).astype(BF16)
            return x + _dot(select, y_buf[...])

        def wide():
            acc = x
            for e in range(n_exp):
                base = pl.multiple_of(tiles_ref[(b * n_exp + e) * NC + c] * TOKEN_CHUNK, TOKEN_CHUNK)
                select = jnp.where(slot + base == post_ref[0, rows, e:e + 1], gate_ref[0, rows, e:e + 1], 0.0)
                acc = acc + _dot(select.astype(BF16), y_ref[0, e, pl.ds(base, SLOT_WINDOW), :])
            return acc

        acc = lax.cond(fits_ref[b * NC + c] == 1, narrow, wide)
        o_ref[0, rows, :] = acc if g_ref is None else _rms(acc, g_ref[...])


def moe_combine(tiles, wins, fits, pos_t, gate, x, y, out_norm=None, tm=1024):
    B, S, D = x.shape
    E, cap = y.shape[1], y.shape[2]
    tok = lambda w: pl.BlockSpec((1, tm, w), lambda b, c, *_: (b, c, 0))
    spread = jnp.asarray(np.kron(np.eye(E), np.ones((1, NARROW_WINDOW))), BF16)
    in_specs = [tok(E), tok(E), tok(D),
                pl.BlockSpec((1, E, cap, D), lambda b, c, *_: (b, 0, 0, 0), pipeline_mode=pl.Buffered(1)),
                pl.BlockSpec(spread.shape, lambda b, c, *_: (0, 0))]
    args = [tiles, wins, fits, pos_t, gate, x, y, spread]
    if out_norm is not None:
        in_specs.append(pl.BlockSpec((1, D), lambda b, c, *_: (0, 0)))
        args.append(out_norm.reshape(1, D))
    return pl.pallas_call(
        _combine_kernel,
        grid_spec=pltpu.PrefetchScalarGridSpec(
            num_scalar_prefetch=3,
            grid=(B, S // tm),
            in_specs=in_specs,
            out_specs=tok(D),
            scratch_shapes=[pltpu.VMEM((E * NARROW_WINDOW, D), BF16)]),
        out_shape=jax.ShapeDtypeStruct((B, S, D), F32),
        compiler_params=_params("arbitrary", "arbitrary"),
        name="moe_combine",
    )(*args)


def expert_choice_moe(x, h, aff_t, batch, w_gate, w_up, w_down, layer, out_norm=None):
    T, D = x.shape
    S = T // batch
    cap = max(1, CAPACITY_FACTOR * S // N_EXPERTS)
    aff = aff_t.T.reshape(batch, S, N_EXPERTS)
    pos, tiles, wins, fits = route_topk(aff_t, batch, cap)
    idx = moe_slot_tokens(tiles, pos, cap)
    y = moe_ffn(idx, h.reshape(batch, S, D), w_gate, w_up, w_down, layer, cap)
    pos_t = jnp.swapaxes(pos.reshape(batch, N_EXPERTS, S), 1, 2)
    return moe_combine(tiles, wins, fits, pos_t, aff, x.reshape(batch, S, D), y, out_norm).reshape(T, D)


ATTN_TQ = 128
ATTN_TK = ATTN_TQ + 2 * HALF_WINDOW
ATTN_SUB = 8


def _t5_bucket(rel):
    half_buckets = REL_BUCKETS // 2
    max_exact = half_buckets // 2
    n = jnp.abs(rel)
    scaled = (jnp.log(jnp.maximum(n, 1).astype(jnp.float32) / max_exact)
              / math.log(REL_MAX_DISTANCE / max_exact))
    large = jnp.minimum(max_exact + jnp.floor(scaled * (half_buckets - max_exact)).astype(jnp.int32),
                        half_buckets - 1)
    return jnp.where(rel > 0, half_buckets, 0) + jnp.where(n < max_exact, n, large)


def _bias_kernel(table_ref, bucket_ref, o_ref):
    bucket = bucket_ref[0]
    q = lax.broadcasted_iota(I32, bucket.shape, 0)
    kc = lax.broadcasted_iota(I32, bucket.shape, 1)
    in_band = jnp.abs(kc - HALF_WINDOW - q) <= HALF_WINDOW
    for h in range(C_HEADS):
        acc = jnp.zeros(bucket.shape, F32)
        for bk in range(REL_BUCKETS):
            acc = jnp.where(bucket == bk, table_ref[bk * C_HEADS + h], acc)
        middle = jnp.where(in_band, acc, NEG_INF)
        o_ref[0, 0, h] = jnp.where(kc >= HALF_WINDOW, middle, NEG_INF)
        o_ref[0, 1, h] = middle
        o_ref[0, 2, h] = jnp.where(kc < HALF_WINDOW + ATTN_TQ, middle, NEG_INF)


def attention_bias(rel_bias):
    rel = np.arange(ATTN_TK)[None, :] - HALF_WINDOW - np.arange(ATTN_TQ)[:, None]
    buckets = jnp.stack([_t5_bucket(jnp.asarray(rel * d, I32)) for d in DILATIONS]).astype(I32)
    P = len(DILATIONS)
    return pl.pallas_call(
        _bias_kernel,
        grid_spec=pltpu.PrefetchScalarGridSpec(
            num_scalar_prefetch=1,
            grid=(P,),
            in_specs=[pl.BlockSpec((1, ATTN_TQ, ATTN_TK), lambda p, t: (p, 0, 0))],
            out_specs=pl.BlockSpec((1, 3, C_HEADS, ATTN_TQ, ATTN_TK), lambda p, t: (p, 0, 0, 0, 0))),
        out_shape=jax.ShapeDtypeStruct((P, 3, C_HEADS, ATTN_TQ, ATTN_TK), F32),
        compiler_params=_params("arbitrary"),
        name="attention_bias",
    )(rel_bias.reshape(-1), buckets)


def _attn_kernel(q_ref, kp_ref, km_ref, kn_ref, vp_ref, vm_ref, vn_ref, bias_ref, o_ref, lse_ref,
                 k_buf, v_buf, s_buf, p_buf):
    i = pl.program_id(2)
    last = pl.num_programs(2) - 1
    hw = HALF_WINDOW
    rows = q_ref.shape[1]
    k_buf[0:hw] = kp_ref[0]
    k_buf[hw:hw + rows] = km_ref[0]
    k_buf[hw + rows:] = kn_ref[0]
    v_buf[0:hw] = vp_ref[0]
    v_buf[hw:hw + rows] = vm_ref[0]
    v_buf[hw + rows:] = vn_ref[0]
    first_head = lax.broadcasted_iota(I32, (ATTN_TQ, LANES), 1) < C_HEAD_DIM
    scale = C_HEAD_DIM ** -0.5
    lse_ref[...] = jnp.zeros_like(lse_ref)
    for sub in range(rows // ATTN_TQ):
        qrows = slice(sub * ATTN_TQ, (sub + 1) * ATTN_TQ)
        krows = slice(sub * ATTN_TQ, sub * ATTN_TQ + ATTN_TK)
        if sub == 0:
            variant = jnp.where(i == 0, 0, 1)
        elif sub == rows // ATTN_TQ - 1:
            variant = jnp.where(i == last, 2, 1)
        else:
            variant = 1
        for pair in range(C_HEADS // 2):
            cols = slice(pair * LANES, (pair + 1) * LANES)
            q = q_ref[0, qrows, cols] * scale
            for half in range(2):
                mine = first_head if half == 0 else jnp.logical_not(first_head)
                s_buf[2 * pair + half] = (_dot_nt(jnp.where(mine, q, jnp.zeros_like(q)), k_buf[krows, cols])
                                          + bias_ref[variant, 2 * pair + half])
        for h in range(C_HEADS):
            s = s_buf[h]
            m = jnp.max(s, axis=-1, keepdims=True)
            p = jnp.exp(s - m)
            den = jnp.sum(p, axis=-1, keepdims=True)
            p_buf[h] = (p / den).astype(BF16)
            lse_ref[0, qrows, h:h + 1] = m + jnp.log(den)
        for pair in range(C_HEADS // 2):
            cols = slice(pair * LANES, (pair + 1) * LANES)
            o_ref[0, qrows, cols] = jnp.where(first_head, _dot(p_buf[2 * pair], v_buf[krows, cols]),
                                              _dot(p_buf[2 * pair + 1], v_buf[krows, cols])).astype(BF16)


def dilated_attention(view, bias, d):
    batch, n, D3 = view.shape
    D = D3 // d // 3
    sub = min(ATTN_SUB, n // ATTN_TQ)
    rows = ATTN_TQ * sub
    nb = n // HALF_WINDOW
    r = rows // HALF_WINDOW
    assert n % rows == 0 and sub >= 2
    main = lambda c: pl.BlockSpec((1, rows, D), lambda b, j, i: (b, i, 3 * j + c))
    prev = lambda c: pl.BlockSpec((1, HALF_WINDOW, D), lambda b, j, i: (b, jnp.maximum(i * r - 1, 0), 3 * j + c))
    nxt = lambda c: pl.BlockSpec((1, HALF_WINDOW, D),
                                 lambda b, j, i: (b, jnp.minimum(i * r + r, nb - 1), 3 * j + c))
    return pl.pallas_call(
        _attn_kernel,
        grid=(batch, d, n // rows),
        in_specs=[main(0), prev(1), main(1), nxt(1), prev(2), main(2), nxt(2),
                  pl.BlockSpec(bias.shape, lambda b, j, i: (0, 0, 0, 0))],
        out_specs=[pl.BlockSpec((1, rows, D), lambda b, j, i: (b, i, j)),
                   pl.BlockSpec((1, rows, LANES), lambda b, j, i: (b, i, j))],
        out_shape=[jax.ShapeDtypeStruct((batch, n, d * D), BF16),
                   jax.ShapeDtypeStruct((batch, n, d * LANES), F32)],
        scratch_shapes=[pltpu.VMEM((rows + 2 * HALF_WINDOW, D), BF16), pltpu.VMEM((rows + 2 * HALF_WINDOW, D), BF16),
                        pltpu.VMEM((C_HEADS, ATTN_TQ, ATTN_TK), F32), pltpu.VMEM((C_HEADS, ATTN_TQ, ATTN_TK), BF16)],
        compiler_params=_params("parallel", "parallel", "parallel"),
        name="dilated_attention",
    )(view, view, view, view, view, view, view, bias)


def dilated_mixture(views, rel_bias):
    bias = attention_bias(rel_bias)
    results = [dilated_attention(view, bias[p], d) for p, (view, d) in enumerate(zip(views, DILATIONS))]
    return [o for o, _ in results], [lse for _, lse in results]


def kernel(x, mix_norm, ffn_norm, final_norm, w_in_even, w_out_even, a_ln_g, a_ln_b, a_w_s, a_b_s, b_lb_table,
           b_norm_g, w_qkv_odd, w_o_odd, rel_bias, w_router, w_gate, w_up, w_down):
    B, S, D = x.shape
    depth = mix_norm.shape[0]
    xt = x.reshape(B * S, D)
    for layer in range(depth):
        j = layer // 2
        if layer % 2 == 0:
            proj = norm_matmul(xt, mix_norm[layer], w_in_even[j].astype(BF16), B)[0].reshape(B * S, -1)
            o_f, o_b = mixer_b(proj, b_lb_table, layer, B)
            xt, h, aff = outproj_even(proj, a_ln_g[j], a_ln_b[j], a_w_s[j], a_b_s[j], o_f, o_b, b_norm_g[j],
                                      w_out_even[j], xt, ffn_norm[layer], w_router[layer])
        else:
            views = norm_matmul(xt, mix_norm[layer], w_qkv_odd[j].astype(BF16), B, DILATIONS, tm=512)
            outs, lses = dilated_mixture(views, rel_bias)
            xt, h, aff = outproj_odd(outs, lses, w_o_odd[j], xt, ffn_norm[layer], w_router[layer])
        xt = expert_choice_moe(xt, h, aff, B, w_gate, w_up, w_down, layer,
                               out_norm=final_norm if layer == depth - 1 else None)
    return xt.reshape(B, S, D)
```

```python
import functools
import math

import numpy as np
import jax
import jax.numpy as jnp
from jax import lax
from jax.experimental import pallas as pl
from jax.experimental.pallas import tpu as pltpu

F32 = jnp.float32
BF16 = jnp.bfloat16
I32 = jnp.int32
EPS = 1e-6
NEG_INF = -1e30

LANES = 128
VMEM_LIMIT = 56 * 1024 * 1024

A_GROUPS = 4
A_CHUNK = 128
B_HEADS = 4
B_DIM = 128
GLA_CHUNK = 128
C_HEADS = 16
C_HEAD_DIM = 64
HALF_WINDOW = 64
DILATIONS = (1, 4, 16)
REL_BUCKETS = 32
REL_MAX_DISTANCE = 1024
N_EXPERTS = 16
CAPACITY_FACTOR = 2
TOKEN_CHUNK = 128
SLOT_WINDOW = 256
NARROW_WINDOW = 64


def _params(*sem):
    return pltpu.CompilerParams(dimension_semantics=sem, vmem_limit_bytes=VMEM_LIMIT)


def _dot(a, b, **kw):
    return jnp.dot(a, b, preferred_element_type=F32, **kw)


def _dot_nt(a, b):
    return lax.dot_general(a, b, (((1,), (1,)), ((), ())), preferred_element_type=F32)


def _dot_tn(a, b):
    return lax.dot_general(a, b, (((0,), (0,)), ((), ())), preferred_element_type=F32)


def _dot_split01(a2, x):
    hi = x.astype(BF16)
    lo = (x - hi.astype(F32)).astype(BF16)
    return _dot(a2, jnp.concatenate([hi, lo], axis=0))


def _sigmoid(x):
    return 0.5 * jnp.tanh(0.5 * x) + 0.5


def _rms(x, g):
    return x * lax.rsqrt(jnp.mean(x * x, axis=-1, keepdims=True) + EPS) * g


def _norm_matmul_kernel(x_ref, g_ref, w_ref, *rest, tn, dilations):
    o_refs, scratch = rest[:len(dilations)], rest[len(dilations):]
    tm, N = x_ref.shape[0], w_ref.shape[1]
    h = _rms(x_ref[...], g_ref[...]).astype(BF16)
    for j in range(N // tn):
        res = _dot(h, w_ref[:, j * tn:(j + 1) * tn])
        if scratch:
            for t in range(tn // LANES):
                scratch[0][t] = res[:, t * LANES:(t + 1) * LANES]
        for o_ref, d in zip(o_refs, dilations):
            if d == 1:
                o_ref[0, :, j * tn:(j + 1) * tn] = res.astype(BF16)
                continue
            for r in range(d):
                for t in range(tn // LANES):
                    lane0 = r * N + j * tn + t * LANES
                    o_ref[0, :, lane0:lane0 + LANES] = (
                        scratch[0][t, pl.ds(r, tm // d, stride=d), :].astype(BF16))


def norm_matmul(x, g, w, batch, dilations=(1,), tm=1024, tn=512):
    T, D = x.shape
    N = w.shape[1]
    S = T // batch
    tiles = S // tm
    strided = any(d > 1 for d in dilations)
    return pl.pallas_call(
        functools.partial(_norm_matmul_kernel, tn=tn, dilations=dilations),
        grid=(T // tm,),
        in_specs=[
            pl.BlockSpec((tm, D), lambda i: (i, 0)),
            pl.BlockSpec((1, D), lambda i: (0, 0)),
            pl.BlockSpec((D, N), lambda i: (0, 0)),
        ],
        out_specs=[pl.BlockSpec((1, tm // d, d * N), lambda i: (i // tiles, i % tiles, 0)) for d in dilations],
        out_shape=[jax.ShapeDtypeStruct((batch, S // d, d * N), BF16) for d in dilations],
        scratch_shapes=[pltpu.VMEM((tn // LANES, tm, LANES), F32)] if strided else [],
        compiler_params=_params("parallel"),
        name="norm_matmul",
    )(x, g.reshape(1, D), w)


def _spatial_gating(u_ref, v_ref, lg_ref, lb_ref, ws_ref, bs_ref):
    tm = u_ref.shape[0]
    u = jax.nn.gelu(u_ref[...].astype(F32))
    v = jax.nn.gelu(v_ref[...].astype(F32))
    mu = jnp.mean(v, axis=-1, keepdims=True)
    vc = v - mu
    vn = vc * lax.rsqrt(jnp.mean(vc * vc, axis=-1, keepdims=True) + EPS)
    vb = (vn * lg_ref[...] + lb_ref[...]).astype(BF16)
    rows_out = []
    for n in range(tm // A_CHUNK):
        rows = slice(n * A_CHUNK, (n + 1) * A_CHUNK)
        groups = []
        for g in range(A_GROUPS):
            cols = slice(g * LANES, (g + 1) * LANES)
            mixed = _dot(ws_ref[g], vb[rows, cols]) + bs_ref[:, g:g + 1]
            groups.append((u[rows, cols] * mixed).astype(BF16))
        rows_out.append(jnp.concatenate(groups, axis=-1))
    return jnp.concatenate(rows_out, axis=0)


def _gla_consts(C, reverse):
    t = np.arange(C)[:, None]
    r = np.arange(C)[None, :]
    L = int(round(math.log2(C)))
    spans = [(r >= t) if reverse else (r <= t)]
    level = np.where(np.eye(C, dtype=bool), L, -1).astype(np.int32)
    for l in range(L):
        bs = (t >> (l + 1)) << (l + 1)
        mid = bs + (1 << l)
        if reverse:
            act_q = t < mid
            span = np.where(act_q, (r >= t) & (r < mid), (r >= mid) & (r < t))
        else:
            act_q = t >= mid
            span = np.where(act_q, (r >= mid) & (r <= t), (r > t) & (r < mid))
        if l > 0:
            spans.append(span)
        level[(bs == bs.T) & act_q & ~act_q.T] = l
    return np.tile(np.concatenate(spans, axis=0).astype(np.float32), (1, 2)), level


def _gla_kernel(qf_ref, ff_ref, if_ref, qb_ref, fb_ref, ib_ref, tbl_ref, span_ref, level_ref,
                of_ref, ob_ref, state_ref, w_buf, q_buf, k_buf, v_buf, s_buf, p_buf, *, layer, C):
    @pl.when(pl.program_id(1) == 0)
    def _():
        state_ref[...] = jnp.zeros_like(state_ref)

    L = span_ref.shape[1] // C
    subs = qf_ref.shape[0] // C
    odd_row = jnp.bitwise_and(lax.broadcasted_iota(I32, (C, qf_ref.shape[1]), 0), 1) == 1

    def chunk_rows(sub, d):
        k = sub if d == 0 else subs - 1 - sub
        return slice(k * C, (k + 1) * C)

    groups = [(sub, d) for sub in range(subs) for d in range(2)]
    heads = [(sub * 2 + d, sub, d, h, slice(h * B_DIM, (h + 1) * B_DIM)) for sub, d in groups for h in range(B_HEADS)]
    tots = {}
    for sub, d in groups:
        g = sub * 2 + d
        q_ref, f_ref, i_ref = (qf_ref, ff_ref, if_ref) if d == 0 else (qb_ref, fb_ref, ib_ref)
        rows = chunk_rows(sub, d)
        tb = tbl_ref[d]
        e = jnp.exp(tb - jnp.max(tb, axis=0, keepdims=True))
        lb = jnp.sum(e[0:layer + 1], axis=0, keepdims=True) / jnp.sum(e, axis=0, keepdims=True)
        f = lb + (1.0 - lb) * _sigmoid(f_ref[rows, :].astype(F32))
        e2 = _dot_split01(span_ref[d], jnp.log2(f))
        tot2 = e2[0:1] if d == 1 else e2[C - 1:C]
        tots[g] = jnp.exp2(tot2)
        w_buf[g, 0:C] = jnp.exp2(e2[0:C]).astype(BF16)
        w_buf[g, C:2 * C] = jnp.exp2(tot2 - e2[0:C]).astype(BF16)
        w_buf[g, 2 * C:3 * C] = jnp.where(odd_row if d == 0 else jnp.logical_not(odd_row), f, 1.0).astype(BF16)
        w_buf[g, 3 * C:] = jnp.exp2(e2[C:]).astype(BF16)
        qr = q_ref[rows, :].astype(F32)
        q_buf[g] = (qr * _sigmoid(qr)).astype(BF16)
        k_buf[g] = (1.0 - f).astype(BF16)
        v_buf[g] = i_ref[rows, :]
    for g, sub, d, h, sl in heads:
        qh, kh = q_buf[g, :, sl], k_buf[g, :, sl]
        s_buf[g, h, L] = _dot_nt(qh, kh)
        for l in range(L):
            wl = w_buf[g, (l + 2) * C:(l + 3) * C, sl]
            s_buf[g, h, l] = _dot_nt(qh * wl, kh * wl)
    for g, sub, d, h, sl in heads:
        level = level_ref[d]
        p = jnp.where(level == L, s_buf[g, h, L], 0.0)
        for l in range(L):
            p = jnp.where(level == l, s_buf[g, h, l], p)
        p_buf[g, h] = p.astype(BF16)
    for g, sub, d, h, sl in heads:
        o_ref = of_ref if d == 0 else ob_ref
        st = state_ref[d, h]
        vh = v_buf[g, :, sl]
        o_ref[chunk_rows(sub, d), sl] = (_dot(p_buf[g, h], vh) + _dot_nt(
            q_buf[g, :, sl] * w_buf[g, 0:C, sl], st.astype(BF16))).astype(o_ref.dtype)
        state_ref[d, h] = st * tots[g][:, sl] + _dot_tn(vh, k_buf[g, :, sl] * w_buf[g, C:2 * C, sl])


def mixer_b(proj, b_lb_table, layer, batch, C=GLA_CHUNK, subs=4):
    T = proj.shape[0]
    W = B_HEADS * B_DIM
    R = C * subs
    n = T // batch // R
    consts = [_gla_consts(C, rev) for rev in (False, True)]
    span = jnp.asarray(np.stack([c[0] for c in consts]), BF16)
    level = jnp.asarray(np.stack([c[1] for c in consts]))
    fwd = lambda col: pl.BlockSpec((R, W), lambda b, c: (b * n + c, col))
    bwd = lambda col: pl.BlockSpec((R, W), lambda b, c: (b * n + n - 1 - c, col))
    full = lambda a: pl.BlockSpec(a.shape, lambda b, c: (0,) * a.ndim)
    G = 2 * subs
    return pl.pallas_call(
        functools.partial(_gla_kernel, layer=layer, C=C),
        grid=(batch, n),
        in_specs=[fwd(2), fwd(3), fwd(5), bwd(2), bwd(4), bwd(5),
                  full(b_lb_table), full(span), full(level)],
        out_specs=[pl.BlockSpec((R, W), lambda b, c: (b * n + c, 0)),
                   pl.BlockSpec((R, W), lambda b, c: (b * n + n - 1 - c, 0))],
        out_shape=[jax.ShapeDtypeStruct((T, W), BF16)] * 2,
        scratch_shapes=[pltpu.VMEM((2, B_HEADS, B_DIM, B_DIM), F32),
                        pltpu.VMEM((G, span.shape[1] + 2 * C, W), BF16),
                        pltpu.VMEM((G, C, W), BF16), pltpu.VMEM((G, C, W), BF16), pltpu.VMEM((G, C, W), BF16),
                        pltpu.VMEM((G, B_HEADS, span.shape[1] // C + 1, C, C), F32),
                        pltpu.VMEM((G, B_HEADS, C, C), BF16)],
        compiler_params=_params("arbitrary", "arbitrary"),
        name="mixer_b",
    )(proj, proj, proj, proj, proj, proj, b_lb_table, span, level)


def _router_epilogue(x_new, g_ref, wr_ref, x_ref, h_ref, aff_ref):
    x_ref[...] = x_new
    h = _rms(x_new, g_ref[...])
    h_ref[...] = h
    w = wr_ref[...]
    n_exp = w.shape[0]
    h_hi, w_hi = h.astype(BF16), w.astype(BF16)
    h_lo, w_lo = (h - h_hi.astype(F32)).astype(BF16), (w - w_hi.astype(F32)).astype(BF16)
    both = _dot_nt(jnp.concatenate([w_hi, w_lo], axis=0), h_hi)
    logits = both[0:n_exp] + both[n_exp:] + _dot_nt(w_hi, h_lo)
    e = jnp.exp(logits - jnp.max(logits, axis=0, keepdims=True))
    aff_ref[...] = e / jnp.sum(e, axis=0, keepdims=True)


def _outproj_even_kernel(u_ref, v_ref, lg_ref, lb_ref, ws_ref, bs_ref, of_ref, ob_ref, gate_ref, bng_ref, w_ref,
                         x_ref, g_ref, wr_ref, xo_ref, h_ref, aff_ref):
    a = _spatial_gating(u_ref, v_ref, lg_ref, lb_ref, ws_ref, bs_ref)
    o = of_ref[...].astype(F32) + ob_ref[...].astype(F32)
    parts = []
    for h in range(B_HEADS):
        oh = o[:, h * B_DIM:(h + 1) * B_DIM]
        parts.append(oh * lax.rsqrt(jnp.mean(oh * oh, axis=-1, keepdims=True) + EPS))
    on = jnp.concatenate(parts, axis=-1) * bng_ref[...] * jax.nn.sigmoid(gate_ref[...].astype(F32))
    wa = a.shape[1]
    mixed = _dot(a, w_ref[0:wa, :]) + _dot(on.astype(BF16), w_ref[wa:, :])
    _router_epilogue(x_ref[...] + mixed, g_ref, wr_ref, xo_ref, h_ref, aff_ref)


def _router_out(T, D, tm):
    specs = [pl.BlockSpec((tm, D), lambda i: (i, 0)), pl.BlockSpec((tm, D), lambda i: (i, 0)),
             pl.BlockSpec((N_EXPERTS, tm), lambda i: (0, i))]
    shapes = [jax.ShapeDtypeStruct((T, D), F32), jax.ShapeDtypeStruct((T, D), F32),
              jax.ShapeDtypeStruct((N_EXPERTS, T), F32)]
    return specs, shapes


def outproj_even(proj, a_ln_g, a_ln_b, a_w_s, a_b_s, o_f, o_b, b_norm_g, w_out, x, ffn_g, w_router, tm=1024):
    T, D = x.shape
    W = o_f.shape[1]
    row = lambda w, col=0: pl.BlockSpec((tm, w), lambda i: (i, col))
    full = lambda s: pl.BlockSpec(s, lambda i: (0,) * len(s))
    out_specs, out_shapes = _router_out(T, D, tm)
    return pl.pallas_call(
        _outproj_even_kernel,
        grid=(T // tm,),
        in_specs=[row(W, 0), row(W, 1), full((1, W)), full((1, W)), full(a_w_s.shape), full(a_b_s.T.shape),
                  row(W), row(W), row(W, 6), full((1, W)), full(w_out.shape), row(D),
                  full((1, D)), full(w_router.T.shape)],
        out_specs=out_specs,
        out_shape=out_shapes,
        compiler_params=_params("parallel"),
        name="outproj_even",
    )(proj, proj, a_ln_g.reshape(1, W), a_ln_b.reshape(1, W), a_w_s.astype(BF16), a_b_s.T,
      o_f, o_b, proj, b_norm_g.reshape(1, W), w_out.astype(BF16), x, ffn_g.reshape(1, D), w_router.T)


def _outproj_odd_kernel(*refs):
    P = len(DILATIONS)
    o_refs, l_refs = refs[:P], refs[P:2 * P]
    ex_ref, w_ref, x_ref, g_ref, wr_ref, xo_ref, h_ref, aff_ref, o_buf, l_buf = refs[2 * P:]
    tm = x_ref.shape[0]

    def token_order(ref, buf, d):
        if d == 1:
            return ref[0].astype(F32)
        tiles = buf.shape[0]
        for r in range(d):
            for t in range(tiles):
                lane0 = (r * tiles + t) * LANES
                buf[t, pl.ds(r, tm // d, stride=d), :] = ref[0, :, lane0:lane0 + LANES].astype(F32)
        return jnp.concatenate([buf[t] for t in range(tiles)], axis=-1)

    ls = [token_order(l_ref, l_buf, d)[:, 0:C_HEADS] for l_ref, d in zip(l_refs, DILATIONS)]
    m = functools.reduce(jnp.maximum, ls)
    es = [jnp.exp(l - m) for l in ls]
    den = functools.reduce(jnp.add, es)
    attn = None
    for e, o_ref, d in zip(es, o_refs, DILATIONS):
        wfull = _dot((e / den).astype(BF16), ex_ref[...])
        term = wfull * token_order(o_ref, o_buf, d)
        attn = term if attn is None else attn + term
    mixed = _dot(attn.astype(BF16), w_ref[...])
    _router_epilogue(x_ref[...] + mixed, g_ref, wr_ref, xo_ref, h_ref, aff_ref)


def outproj_odd(outs, lses, w_o, x, ffn_g, w_router, tm=512):
    T, D = x.shape
    tiles = outs[0].shape[1] // tm
    expand = jnp.asarray(np.kron(np.eye(C_HEADS), np.ones((1, C_HEAD_DIM))), BF16)
    dil = lambda w: [pl.BlockSpec((1, tm // d, d * w), lambda i: (i // tiles, i % tiles, 0)) for d in DILATIONS]
    full = lambda s: pl.BlockSpec(s, lambda i: (0, 0))
    out_specs, out_shapes = _router_out(T, D, tm)
    return pl.pallas_call(
        _outproj_odd_kernel,
        grid=(T // tm,),
        in_specs=dil(D) + dil(LANES) + [full(expand.shape), full(w_o.shape),
                                        pl.BlockSpec((tm, D), lambda i: (i, 0)), full((1, D)),
                                        full(w_router.T.shape)],
        out_specs=out_specs,
        out_shape=out_shapes,
        scratch_shapes=[pltpu.VMEM((D // LANES, tm, LANES), F32), pltpu.VMEM((1, tm, LANES), F32)],
        compiler_params=_params("parallel"),
        name="outproj_odd",
    )(*outs, *lses, expand, w_o.astype(BF16), x, ffn_g.reshape(1, D), w_router.T)


def _topk_kernel(aff_ref, upper_ref, strict_ref, pos_ref, tile_ref, win_ref, fit_ref, *, cap):
    a = aff_ref[...]
    E, NC, _ = a.shape

    def count(mask):
        return jnp.sum(jnp.sum(mask.astype(F32), axis=2, keepdims=True), axis=1, keepdims=True)

    def as_float(bits):
        return lax.bitcast_convert_type(bits, jnp.float32)

    def search(i, thr):
        cand = thr | jnp.left_shift(jnp.int32(1), 30 - i)
        return jnp.where(count(a >= as_float(cand)) >= cap, cand, thr)

    thr = lax.fori_loop(0, 31, search, jnp.zeros((E, 1, 1), I32))
    gt = a >= as_float(thr + 1)
    eq = jnp.logical_and(a >= as_float(thr), jnp.logical_not(gt))
    need = cap - count(gt)

    def prefix(mask):
        m2 = mask.astype(BF16).reshape(E * NC, LANES)
        within = _dot(m2, upper_ref[...])
        total = jnp.broadcast_to(within[:, LANES - 1:LANES], (E * NC, LANES))
        tot_b = total.astype(BF16)
        starts = jnp.concatenate(
            [_dot(strict_ref[...], tot_b[e * NC:(e + 1) * NC]) for e in range(E)], axis=0)
        return [t.reshape(E, NC, LANES) for t in (within + starts, starts, total)]

    eq_incl, _, _ = prefix(eq)
    sel = gt | (eq & (eq_incl - 1.0 < need))
    sel_incl, starts, count = prefix(sel)
    pos_ref[0] = jnp.where(sel, sel_incl - 1.0, -1.0).astype(I32)
    tile = jnp.minimum(jnp.floor(starts * (1.0 / TOKEN_CHUNK)), float((cap - SLOT_WINDOW) // TOKEN_CHUNK))
    tile_ref[0] = tile.astype(I32)
    win = jnp.minimum(jnp.floor(starts * (1.0 / 16)) * 16, float(cap - NARROW_WINDOW))
    win_ref[0] = win.astype(I32)
    fits = jnp.where(starts + count - win <= NARROW_WINDOW, 1.0, 0.0)
    fit_ref[0] = jnp.min(fits, axis=0, keepdims=True).astype(I32)


def route_topk(aff_t, batch, cap):
    E, T = aff_t.shape
    B, S = batch, T // batch
    NC = S // TOKEN_CHUNK
    t = np.arange(LANES)
    upper = jnp.asarray(t[:, None] <= t[None, :], BF16)
    c = np.arange(NC)
    strict = jnp.asarray(c[None, :] < c[:, None], BF16)
    blk = pl.BlockSpec((1, E, NC, LANES), lambda b: (b, 0, 0, 0))
    one = pl.BlockSpec((1, 1, NC, LANES), lambda b: (b, 0, 0, 0))
    pos, tiles, wins, fits = pl.pallas_call(
        functools.partial(_topk_kernel, cap=cap),
        grid=(B,),
        in_specs=[pl.BlockSpec((E, NC, LANES), lambda b: (0, b, 0)),
                  pl.BlockSpec(upper.shape, lambda b: (0, 0)), pl.BlockSpec(strict.shape, lambda b: (0, 0))],
        out_specs=[blk, blk, blk, one],
        out_shape=[jax.ShapeDtypeStruct((B, E, NC, LANES), I32)] * 3 + [jax.ShapeDtypeStruct((B, 1, NC, LANES), I32)],
        compiler_params=_params("parallel"),
        name="route_topk",
    )(aff_t.reshape(E, B * NC, LANES), upper, strict)
    return pos, tiles[..., 0].reshape(-1), wins[..., 0].reshape(-1), fits[..., 0].reshape(-1)


def _index_kernel(tiles_ref, pos_ref, sel_ref, idx_ref, acc_ref):
    b, e = pl.program_id(0), pl.program_id(1)
    n_exp = pl.num_programs(1)
    NC = pos_ref.shape[2]
    acc_ref[...] = jnp.zeros_like(acc_ref)
    slot = lax.broadcasted_iota(I32, (SLOT_WINDOW, TOKEN_CHUNK), 0)

    def body(c, carry):
        j = tiles_ref[(b * n_exp + e) * NC + c]
        onehot = jnp.where(slot + j * TOKEN_CHUNK == pos_ref[0, 0, pl.ds(c, 1), :], 1.0, 0.0).astype(BF16)
        hit = _dot_nt(sel_ref[...], onehot)
        tok = hit[0:1] + hit[1:2] * lax.convert_element_type(c * TOKEN_CHUNK, F32)
        acc_ref[j, 0:1, :] += tok[:, 0:TOKEN_CHUNK]
        acc_ref[j + 1, 0:1, :] += tok[:, TOKEN_CHUNK:]
        return carry

    lax.fori_loop(0, NC, body, 0, unroll=True)
    idx_ref[0, 0] = acc_ref[...].astype(I32)


def moe_slot_tokens(tiles, pos, cap):
    B, E, NC, _ = pos.shape
    nt = cap // TOKEN_CHUNK
    sel = np.zeros((8, TOKEN_CHUNK), np.float32)
    sel[0] = np.arange(TOKEN_CHUNK)
    sel[1] = 1.0
    idx = pl.pallas_call(
        _index_kernel,
        grid_spec=pltpu.PrefetchScalarGridSpec(
            num_scalar_prefetch=1,
            grid=(B, E),
            in_specs=[pl.BlockSpec((1, 1, NC, LANES), lambda b, e, s: (b, e, 0, 0)),
                      pl.BlockSpec((8, TOKEN_CHUNK), lambda b, e, s: (0, 0))],
            out_specs=pl.BlockSpec((1, 1, nt, 8, LANES), lambda b, e, s: (b, e, 0, 0, 0)),
            scratch_shapes=[pltpu.VMEM((nt, 8, LANES), F32)]),
        out_shape=jax.ShapeDtypeStruct((B, E, nt, 8, LANES), I32),
        compiler_params=_params("arbitrary", "arbitrary"),
        name="moe_slot_tokens",
    )(tiles, pos, jnp.asarray(sel, BF16))
    return idx[:, :, :, 0, :].reshape(-1)


def _ffn_kernel(idx_ref, h_ref, wg_ref, wu_ref, wd_ref, y_ref, x_buf, sem, wgb_ref, wub_ref, wdb_ref, *, tr):
    e, b = pl.program_id(0), pl.program_id(1)
    n_exp, nb = pl.num_programs(0), pl.num_programs(1)
    cap = x_buf.shape[1]
    step = e * nb + b

    def row_copy(st, buf, i):
        eb = st // nb
        bb = st - eb * nb
        tok = idx_ref[(bb * n_exp + eb) * cap + i]
        return pltpu.make_async_copy(h_ref.at[bb, pl.ds(tok, 1), :], x_buf.at[buf, pl.ds(i, 1), :], sem.at[buf])

    cur = jnp.bitwise_and(step, 1)

    @pl.when(step == 0)
    def _():
        lax.fori_loop(0, cap, lambda i, c: (row_copy(step, 0, i).start(), c)[1], 0, unroll=8)

    @pl.when(b == 0)
    def _():
        wgb_ref[...] = wg_ref[0, 0].astype(BF16)
        wub_ref[...] = wu_ref[0, 0].astype(BF16)
        wdb_ref[...] = wd_ref[0, 0].astype(BF16)

    pltpu.make_async_copy(h_ref.at[0, pl.ds(0, cap), :], x_buf.at[cur], sem.at[cur]).wait()

    def compute(prefetch):
        for r in range(cap // tr):
            rows = pl.ds(r * tr, tr)
            x = x_buf[cur, rows, :].astype(BF16)
            if prefetch:
                for i in range(r * tr, (r + 1) * tr):
                    row_copy(step + 1, 1 - cur, i).start()
            g = _dot(x, wgb_ref[...])
            u = _dot(x, wub_ref[...])
            mid = (g * jax.nn.sigmoid(g) * u).astype(BF16)
            y_ref[0, 0, rows, :] = _dot(mid, wdb_ref[...]).astype(BF16)

    is_last = step + 1 == n_exp * nb
    pl.when(jnp.logical_not(is_last))(lambda: compute(True))
    pl.when(is_last)(lambda: compute(False))


def moe_ffn(idx, h, w_gate, w_up, w_down, layer, cap, tr=1024):
    B, S, D = h.shape
    E, F = w_gate.shape[1], w_gate.shape[-1]
    wspec = lambda r, c: pl.BlockSpec((1, 1, r, c), lambda e, b, s: (layer, e, 0, 0))
    return pl.pallas_call(
        functools.partial(_ffn_kernel, tr=min(tr, cap)),
        grid_spec=pltpu.PrefetchScalarGridSpec(
            num_scalar_prefetch=1,
            grid=(E, B),
            in_specs=[pl.BlockSpec(memory_space=pl.ANY), wspec(D, F), wspec(D, F), wspec(F, D)],
            out_specs=pl.BlockSpec((1, 1, cap, D), lambda e, b, s: (b, e, 0, 0)),
            scratch_shapes=[pltpu.VMEM((2, cap, D), F32), pltpu.SemaphoreType.DMA((2,)),
                            pltpu.VMEM((D, F), BF16), pltpu.VMEM((D, F), BF16), pltpu.VMEM((F, D), BF16)]),
        out_shape=jax.ShapeDtypeStruct((B, E, cap, D), BF16),
        compiler_params=_params("arbitrary", "arbitrary"),
        name="moe_ffn",
    )(idx, h, w_gate, w_up, w_down)


def _combine_kernel(tiles_ref, wins_ref, fits_ref, post_ref, gate_ref, x_ref, y_ref, spread_ref, *rest):
    (g_ref, o_ref, y_buf) = rest if len(rest) == 3 else (None,) + tuple(rest)
    b, i = pl.program_id(0), pl.program_id(1)
    n_exp = y_ref.shape[1]
    per_step = x_ref.shape[1] // TOKEN_CHUNK
    NC = pl.num_programs(1) * per_step
    slot = lax.broadcasted_iota(I32, (TOKEN_CHUNK, SLOT_WINDOW), 1)
    expert_lane = lax.broadcasted_iota(I32, (1, n_exp), 1)
    slot_in_window = jnp.bitwise_and(lax.broadcasted_iota(I32, (TOKEN_CHUNK, n_exp * NARROW_WINDOW), 1),
                                     NARROW_WINDOW - 1).astype(F32)
    for k in range(per_step):
        rows = slice(k * TOKEN_CHUNK, (k + 1) * TOKEN_CHUNK)
        c = i * per_step + k
        x = x_ref[0, rows, :]

        def narrow():
            wins = [wins_ref[(b * n_exp + e) * NC + c] for e in range(n_exp)]
            for e in range(n_exp):
                y_buf[e * NARROW_WINDOW:(e + 1) * NARROW_WINDOW, :] = (
                    y_ref[0, e, pl.ds(pl.multiple_of(wins[e], 16), NARROW_WINDOW), :])
            start = jnp.zeros((1, n_exp), I32)
            for e in range(n_exp):
                start = jnp.where(expert_lane == e, wins[e], start)
            rel = post_ref[0, rows, :] - start
            rel = jnp.where(jnp.logical_and(rel >= 0, rel < NARROW_WINDOW), rel, 2 * NARROW_WINDOW - 1)
            rel_wide = _dot(rel.astype(F32).astype(BF16), spread_ref[...])
            gate_wide = _dot(gate_ref[0, rows, :].astype(BF16), spread_ref[...])
            select = jnp.where(rel_wide == slot_in_window, gate_wide, 0.0).astype(BF16)
            return x + _dot(select, y_buf[...])

        def wide():
            acc = x
            for e in range(n_exp):
                base = pl.multiple_of(tiles_ref[(b * n_exp + e) * NC + c] * TOKEN_CHUNK, TOKEN_CHUNK)
                select = jnp.where(slot + base == post_ref[0, rows, e:e + 1], gate_ref[0, rows, e:e + 1], 0.0)
                acc = acc + _dot(select.astype(BF16), y_ref[0, e, pl.ds(base, SLOT_WINDOW), :])
            return acc

        acc = lax.cond(fits_ref[b * NC + c] == 1, narrow, wide)
        o_ref[0, rows, :] = acc if g_ref is None else _rms(acc, g_ref[...])


def moe_combine(tiles, wins, fits, pos_t, gate, x, y, out_norm=None, tm=1024):
    B, S, D = x.shape
    E, cap = y.shape[1], y.shape[2]
    tok = lambda w: pl.BlockSpec((1, tm, w), lambda b, c, *_: (b, c, 0))
    spread = jnp.asarray(np.kron(np.eye(E), np.ones((1, NARROW_WINDOW))), BF16)
    in_specs = [tok(E), tok(E), tok(D),
                pl.BlockSpec((1, E, cap, D), lambda b, c, *_: (b, 0, 0, 0), pipeline_mode=pl.Buffered(1)),
                pl.BlockSpec(spread.shape, lambda b, c, *_: (0, 0))]
    args = [tiles, wins, fits, pos_t, gate, x, y, spread]
    if out_norm is not None:
        in_specs.append(pl.BlockSpec((1, D), lambda b, c, *_: (0, 0)))
        args.append(out_norm.reshape(1, D))
    return pl.pallas_call(
        _combine_kernel,
        grid_spec=pltpu.PrefetchScalarGridSpec(
            num_scalar_prefetch=3,
            grid=(B, S // tm),
            in_specs=in_specs,
            out_specs=tok(D),
            scratch_shapes=[pltpu.VMEM((E * NARROW_WINDOW, D), BF16)]),
        out_shape=jax.ShapeDtypeStruct((B, S, D), F32),
        compiler_params=_params("arbitrary", "arbitrary"),
        name="moe_combine",
    )(*args)


def expert_choice_moe(x, h, aff_t, batch, w_gate, w_up, w_down, layer, out_norm=None):
    T, D = x.shape
    S = T // batch
    cap = max(1, CAPACITY_FACTOR * S // N_EXPERTS)
    aff = aff_t.T.reshape(batch, S, N_EXPERTS)
    pos, tiles, wins, fits = route_topk(aff_t, batch, cap)
    idx = moe_slot_tokens(tiles, pos, cap)
    y = moe_ffn(idx, h.reshape(batch, S, D), w_gate, w_up, w_down, layer, cap)
    pos_t = jnp.swapaxes(pos.reshape(batch, N_EXPERTS, S), 1, 2)
    return moe_combine(tiles, wins, fits, pos_t, aff, x.reshape(batch, S, D), y, out_norm).reshape(T, D)


ATTN_TQ = 128
ATTN_TK = ATTN_TQ + 2 * HALF_WINDOW
ATTN_SUB = 8


def _t5_bucket(rel):
    half_buckets = REL_BUCKETS // 2
    max_exact = half_buckets // 2
    n = jnp.abs(rel)
    scaled = (jnp.log(jnp.maximum(n, 1).astype(jnp.float32) / max_exact)
              / math.log(REL_MAX_DISTANCE / max_exact))
    large = jnp.minimum(max_exact + jnp.floor(scaled * (half_buckets - max_exact)).astype(jnp.int32),
                        half_buckets - 1)
    return jnp.where(rel > 0, half_buckets, 0) + jnp.where(n < max_exact, n, large)


def _bias_kernel(table_ref, bucket_ref, o_ref):
    bucket = bucket_ref[0]
    q = lax.broadcasted_iota(I32, bucket.shape, 0)
    kc = lax.broadcasted_iota(I32, bucket.shape, 1)
    in_band = jnp.abs(kc - HALF_WINDOW - q) <= HALF_WINDOW
    for h in range(C_HEADS):
        acc = jnp.zeros(bucket.shape, F32)
        for bk in range(REL_BUCKETS):
            acc = jnp.where(bucket == bk, table_ref[bk * C_HEADS + h], acc)
        middle = jnp.where(in_band, acc, NEG_INF)
        o_ref[0, 0, h] = jnp.where(kc >= HALF_WINDOW, middle, NEG_INF)
        o_ref[0, 1, h] = middle
        o_ref[0, 2, h] = jnp.where(kc < HALF_WINDOW + ATTN_TQ, middle, NEG_INF)


def attention_bias(rel_bias):
    rel = np.arange(ATTN_TK)[None, :] - HALF_WINDOW - np.arange(ATTN_TQ)[:, None]
    buckets = jnp.stack([_t5_bucket(jnp.asarray(rel * d, I32)) for d in DILATIONS]).astype(I32)
    P = len(DILATIONS)
    return pl.pallas_call(
        _bias_kernel,
        grid_spec=pltpu.PrefetchScalarGridSpec(
            num_scalar_prefetch=1,
            grid=(P,),
            in_specs=[pl.BlockSpec((1, ATTN_TQ, ATTN_TK), lambda p, t: (p, 0, 0))],
            out_specs=pl.BlockSpec((1, 3, C_HEADS, ATTN_TQ, ATTN_TK), lambda p, t: (p, 0, 0, 0, 0))),
        out_shape=jax.ShapeDtypeStruct((P, 3, C_HEADS, ATTN_TQ, ATTN_TK), F32),
        compiler_params=_params("arbitrary"),
        name="attention_bias",
    )(rel_bias.reshape(-1), buckets)


def _attn_kernel(q_ref, kp_ref, km_ref, kn_ref, vp_ref, vm_ref, vn_ref, bias_ref, o_ref, lse_ref,
                 k_buf, v_buf, s_buf, p_buf):
    i = pl.program_id(2)
    last = pl.num_programs(2) - 1
    hw = HALF_WINDOW
    rows = q_ref.shape[1]
    k_buf[0:hw] = kp_ref[0]
    k_buf[hw:hw + rows] = km_ref[0]
    k_buf[hw + rows:] = kn_ref[0]
    v_buf[0:hw] = vp_ref[0]
    v_buf[hw:hw + rows] = vm_ref[0]
    v_buf[hw + rows:] = vn_ref[0]
    first_head = lax.broadcasted_iota(I32, (ATTN_TQ, LANES), 1) < C_HEAD_DIM
    scale = C_HEAD_DIM ** -0.5
    lse_ref[...] = jnp.zeros_like(lse_ref)
    for sub in range(rows // ATTN_TQ):
        qrows = slice(sub * ATTN_TQ, (sub + 1) * ATTN_TQ)
        krows = slice(sub * ATTN_TQ, sub * ATTN_TQ + ATTN_TK)
        if sub == 0:
            variant = jnp.where(i == 0, 0, 1)
        elif sub == rows // ATTN_TQ - 1:
            variant = jnp.where(i == last, 2, 1)
        else:
            variant = 1
        for pair in range(C_HEADS // 2):
            cols = slice(pair * LANES, (pair + 1) * LANES)
            q = q_ref[0, qrows, cols] * scale
            for half in range(2):
                mine = first_head if half == 0 else jnp.logical_not(first_head)
                s_buf[2 * pair + half] = (_dot_nt(jnp.where(mine, q, jnp.zeros_like(q)), k_buf[krows, cols])
                                          + bias_ref[variant, 2 * pair + half])
        for h in range(C_HEADS):
            s = s_buf[h]
            m = jnp.max(s, axis=-1, keepdims=True)
            p = jnp.exp(s - m)
            den = jnp.sum(p, axis=-1, keepdims=True)
            p_buf[h] = (p / den).astype(BF16)
            lse_ref[0, qrows, h:h + 1] = m + jnp.log(den)
        for pair in range(C_HEADS // 2):
            cols = slice(pair * LANES, (pair + 1) * LANES)
            o_ref[0, qrows, cols] = jnp.where(first_head, _dot(p_buf[2 * pair], v_buf[krows, cols]),
                                              _dot(p_buf[2 * pair + 1], v_buf[krows, cols])).astype(BF16)


def dilated_attention(view, bias, d):
    batch, n, D3 = view.shape
    D = D3 // d // 3
    sub = min(ATTN_SUB, n // ATTN_TQ)
    rows = ATTN_TQ * sub
    nb = n // HALF_WINDOW
    r = rows // HALF_WINDOW
    assert n % rows == 0 and sub >= 2
    main = lambda c: pl.BlockSpec((1, rows, D), lambda b, j, i: (b, i, 3 * j + c))
    prev = lambda c: pl.BlockSpec((1, HALF_WINDOW, D), lambda b, j, i: (b, jnp.maximum(i * r - 1, 0), 3 * j + c))
    nxt = lambda c: pl.BlockSpec((1, HALF_WINDOW, D),
                                 lambda b, j, i: (b, jnp.minimum(i * r + r, nb - 1), 3 * j + c))
    return pl.pallas_call(
        _attn_kernel,
        grid=(batch, d, n // rows),
        in_specs=[main(0), prev(1), main(1), nxt(1), prev(2), main(2), nxt(2),
                  pl.BlockSpec(bias.shape, lambda b, j, i: (0, 0, 0, 0))],
        out_specs=[pl.BlockSpec((1, rows, D), lambda b, j, i: (b, i, j)),
                   pl.BlockSpec((1, rows, LANES), lambda b, j, i: (b, i, j))],
        out_shape=[jax.ShapeDtypeStruct((batch, n, d * D), BF16),
                   jax.ShapeDtypeStruct((batch, n, d * LANES), F32)],
        scratch_shapes=[pltpu.VMEM((rows + 2 * HALF_WINDOW, D), BF16), pltpu.VMEM((rows + 2 * HALF_WINDOW, D), BF16),
                        pltpu.VMEM((C_HEADS, ATTN_TQ, ATTN_TK), F32), pltpu.VMEM((C_HEADS, ATTN_TQ, ATTN_TK), BF16)],
        compiler_params=_params("parallel", "parallel", "parallel"),
        name="dilated_attention",
    )(view, view, view, view, view, view, view, bias)


def dilated_mixture(views, rel_bias):
    bias = attention_bias(rel_bias)
    results = [dilated_attention(view, bias[p], d) for p, (view, d) in enumerate(zip(views, DILATIONS))]
    return [o for o, _ in results], [lse for _, lse in results]


def kernel(x, mix_norm, ffn_norm, final_norm, w_in_even, w_out_even, a_ln_g, a_ln_b, a_w_s, a_b_s, b_lb_table,
           b_norm_g, w_qkv_odd, w_o_odd, rel_bias, w_router, w_gate, w_up, w_down):
    B, S, D = x.shape
    depth = mix_norm.shape[0]
    xt = x.reshape(B * S, D)
    for layer in range(depth):
        j = layer // 2
        if layer % 2 == 0:
            proj = norm_matmul(xt, mix_norm[layer], w_in_even[j].astype(BF16), B)[0].reshape(B * S, -1)
            o_f, o_b = mixer_b(proj, b_lb_table, layer, B)
            xt, h, aff = outproj_even(proj, a_ln_g[j], a_ln_b[j], a_w_s[j], a_b_s[j], o_f, o_b, b_norm_g[j],
                                      w_out_even[j], xt, ffn_norm[layer], w_router[layer])
        else:
            views = norm_matmul(xt, mix_norm[layer], w_qkv_odd[j].astype(BF16), B, DILATIONS, tm=512)
            outs, lses = dilated_mixture(views, rel_bias)
            xt, h, aff = outproj_odd(outs, lses, w_o_odd[j], xt, ffn_norm[layer], w_router[layer])
        xt = expert_choice_moe(xt, h, aff, B, w_gate, w_up, w_down, layer,
                               out_norm=final_norm if layer == depth - 1 else None)
    return xt.reshape(B, S, D)
```

```python
import functools
import math

import numpy as np
import jax
import jax.numpy as jnp
from jax import lax
from jax.experimental import pallas as pl
from jax.experimental.pallas import tpu as pltpu

F32 = jnp.float32
BF16 = jnp.bfloat16
I32 = jnp.int32
EPS = 1e-6
NEG_INF = -1e30

LANES = 128
VMEM_LIMIT = 56 * 1024 * 1024

A_GROUPS = 4
A_CHUNK = 128
B_HEADS = 4
B_DIM = 128
GLA_CHUNK = 128
C_HEADS = 16
C_HEAD_DIM = 64
HALF_WINDOW = 64
DILATIONS = (1, 4, 16)
REL_BUCKETS = 32
REL_MAX_DISTANCE = 1024
N_EXPERTS = 16
CAPACITY_FACTOR = 2
TOKEN_CHUNK = 128
SLOT_WINDOW = 256
NARROW_WINDOW = 64


def _params(*sem):
    return pltpu.CompilerParams(dimension_semantics=sem, vmem_limit_bytes=VMEM_LIMIT)


def _dot(a, b, **kw):
    return jnp.dot(a, b, preferred_element_type=F32, **kw)


def _dot_nt(a, b):
    return lax.dot_general(a, b, (((1,), (1,)), ((), ())), preferred_element_type=F32)


def _dot_tn(a, b):
    return lax.dot_general(a, b, (((0,), (0,)), ((), ())), preferred_element_type=F32)


def _dot_split01(a2, x):
    hi = x.astype(BF16)
    lo = (x - hi.astype(F32)).astype(BF16)
    return _dot(a2, jnp.concatenate([hi, lo], axis=0))


def _sigmoid(x):
    return 0.5 * jnp.tanh(0.5 * x) + 0.5


def _rms(x, g):
    return x * lax.rsqrt(jnp.mean(x * x, axis=-1, keepdims=True) + EPS) * g


def _norm_matmul_kernel(x_ref, g_ref, w_ref, *rest, tn, dilations):
    o_refs, scratch = rest[:len(dilations)], rest[len(dilations):]
    tm, N = x_ref.shape[0], w_ref.shape[1]
    h = _rms(x_ref[...], g_ref[...]).astype(BF16)
    for j in range(N // tn):
        res = _dot(h, w_ref[:, j * tn:(j + 1) * tn])
        if scratch:
            for t in range(tn // LANES):
                scratch[0][t] = res[:, t * LANES:(t + 1) * LANES]
        for o_ref, d in zip(o_refs, dilations):
            if d == 1:
                o_ref[0, :, j * tn:(j + 1) * tn] = res.astype(BF16)
                continue
            for r in range(d):
                for t in range(tn // LANES):
                    lane0 = r * N + j * tn + t * LANES
                    o_ref[0, :, lane0:lane0 + LANES] = (
                        scratch[0][t, pl.ds(r, tm // d, stride=d), :].astype(BF16))


def norm_matmul(x, g, w, batch, dilations=(1,), tm=1024, tn=512):
    T, D = x.shape
    N = w.shape[1]
    S = T // batch
    tiles = S // tm
    strided = any(d > 1 for d in dilations)
    return pl.pallas_call(
        functools.partial(_norm_matmul_kernel, tn=tn, dilations=dilations),
        grid=(T // tm,),
        in_specs=[
            pl.BlockSpec((tm, D), lambda i: (i, 0)),
            pl.BlockSpec((1, D), lambda i: (0, 0)),
            pl.BlockSpec((D, N), lambda i: (0, 0)),
        ],
        out_specs=[pl.BlockSpec((1, tm // d, d * N), lambda i: (i // tiles, i % tiles, 0)) for d in dilations],
        out_shape=[jax.ShapeDtypeStruct((batch, S // d, d * N), BF16) for d in dilations],
        scratch_shapes=[pltpu.VMEM((tn // LANES, tm, LANES), F32)] if strided else [],
        compiler_params=_params("parallel"),
        name="norm_matmul",
    )(x, g.reshape(1, D), w)


def _spatial_gating(u_ref, v_ref, lg_ref, lb_ref, ws_ref, bs_ref):
    tm = u_ref.shape[0]
    u = jax.nn.gelu(u_ref[...].astype(F32))
    v = jax.nn.gelu(v_ref[...].astype(F32))
    mu = jnp.mean(v, axis=-1, keepdims=True)
    vc = v - mu
    vn = vc * lax.rsqrt(jnp.mean(vc * vc, axis=-1, keepdims=True) + EPS)
    vb = (vn * lg_ref[...] + lb_ref[...]).astype(BF16)
    rows_out = []
    for n in range(tm // A_CHUNK):
        rows = slice(n * A_CHUNK, (n + 1) * A_CHUNK)
        groups = []
        for g in range(A_GROUPS):
            cols = slice(g * LANES, (g + 1) * LANES)
            mixed = _dot(ws_ref[g], vb[rows, cols]) + bs_ref[:, g:g + 1]
            groups.append((u[rows, cols] * mixed).astype(BF16))
        rows_out.append(jnp.concatenate(groups, axis=-1))
    return jnp.concatenate(rows_out, axis=0)


def _gla_consts(C, reverse):
    t = np.arange(C)[:, None]
    r = np.arange(C)[None, :]
    L = int(round(math.log2(C)))
    spans = [(r >= t) if reverse else (r <= t)]
    level = np.where(np.eye(C, dtype=bool), L, -1).astype(np.int32)
    for l in range(L):
        bs = (t >> (l + 1)) << (l + 1)
        mid = bs + (1 << l)
        if reverse:
            act_q = t < mid
            span = np.where(act_q, (r >= t) & (r < mid), (r >= mid) & (r < t))
        else:
            act_q = t >= mid
            span = np.where(act_q, (r >= mid) & (r <= t), (r > t) & (r < mid))
        if l > 0:
            spans.append(span)
        level[(bs == bs.T) & act_q & ~act_q.T] = l
    return np.tile(np.concatenate(spans, axis=0).astype(np.float32), (1, 2)), level


def _gla_kernel(qf_ref, ff_ref, if_ref, qb_ref, fb_ref, ib_ref, tbl_ref, span_ref, level_ref,
                of_ref, ob_ref, state_ref, w_buf, q_buf, k_buf, v_buf, s_buf, p_buf, *, layer, C):
    @pl.when(pl.program_id(1) == 0)
    def _():
        state_ref[...] = jnp.zeros_like(state_ref)

    L = span_ref.shape[1] // C
    subs = qf_ref.shape[0] // C
    odd_row = jnp.bitwise_and(lax.broadcasted_iota(I32, (C, qf_ref.shape[1]), 0), 1) == 1

    def chunk_rows(sub, d):
        k = sub if d == 0 else subs - 1 - sub
        return slice(k * C, (k + 1) * C)

    groups = [(sub, d) for sub in range(subs) for d in range(2)]
    heads = [(sub * 2 + d, sub, d, h, slice(h * B_DIM, (h + 1) * B_DIM)) for sub, d in groups for h in range(B_HEADS)]
    tots = {}
    for sub, d in groups:
        g = sub * 2 + d
        q_ref, f_ref, i_ref = (qf_ref, ff_ref, if_ref) if d == 0 else (qb_ref, fb_ref, ib_ref)
        rows = chunk_rows(sub, d)
        tb = tbl_ref[d]
        e = jnp.exp(tb - jnp.max(tb, axis=0, keepdims=True))
        lb = jnp.sum(e[0:layer + 1], axis=0, keepdims=True) / jnp.sum(e, axis=0, keepdims=True)
        f = lb + (1.0 - lb) * _sigmoid(f_ref[rows, :].astype(F32))
        e2 = _dot_split01(span_ref[d], jnp.log2(f))
        tot2 = e2[0:1] if d == 1 else e2[C - 1:C]
        tots[g] = jnp.exp2(tot2)
        w_buf[g, 0:C] = jnp.exp2(e2[0:C]).astype(BF16)
        w_buf[g, C:2 * C] = jnp.exp2(tot2 - e2[0:C]).astype(BF16)
        w_buf[g, 2 * C:3 * C] = jnp.where(odd_row if d == 0 else jnp.logical_not(odd_row), f, 1.0).astype(BF16)
        w_buf[g, 3 * C:] = jnp.exp2(e2[C:]).astype(BF16)
        qr = q_ref[rows, :].astype(F32)
        q_buf[g] = (qr * _sigmoid(qr)).astype(BF16)
        k_buf[g] = (1.0 - f).astype(BF16)
        v_buf[g] = i_ref[rows, :]
    for g, sub, d, h, sl in heads:
        qh, kh = q_buf[g, :, sl], k_buf[g, :, sl]
        s_buf[g, h, L] = _dot_nt(qh, kh)
        for l in range(L):
            wl = w_buf[g, (l + 2) * C:(l + 3) * C, sl]
            s_buf[g, h, l] = _dot_nt(qh * wl, kh * wl)
    for g, sub, d, h, sl in heads:
        level = level_ref[d]
        p = jnp.where(level == L, s_buf[g, h, L], 0.0)
        for l in range(L):
            p = jnp.where(level == l, s_buf[g, h, l], p)
        p_buf[g, h] = p.astype(BF16)
    for g, sub, d, h, sl in heads:
        o_ref = of_ref if d == 0 else ob_ref
        st = state_ref[d, h]
        vh = v_buf[g, :, sl]
        o_ref[chunk_rows(sub, d), sl] = (_dot(p_buf[g, h], vh) + _dot_nt(
            q_buf[g, :, sl] * w_buf[g, 0:C, sl], st.astype(BF16))).astype(o_ref.dtype)
        state_ref[d, h] = st * tots[g][:, sl] + _dot_tn(vh, k_buf[g, :, sl] * w_buf[g, C:2 * C, sl])


def mixer_b(proj, b_lb_table, layer, batch, C=GLA_CHUNK, subs=4):
    T = proj.shape[0]
    W = B_HEADS * B_DIM
    R = C * subs
    n = T // batch // R
    consts = [_gla_consts(C, rev) for rev in (False, True)]
    span = jnp.asarray(np.stack([c[0] for c in consts]), BF16)
    level = jnp.asarray(np.stack([c[1] for c in consts]))
    fwd = lambda col: pl.BlockSpec((R, W), lambda b, c: (b * n + c, col))
    bwd = lambda col: pl.BlockSpec((R, W), lambda b, c: (b * n + n - 1 - c, col))
    full = lambda a: pl.BlockSpec(a.shape, lambda b, c: (0,) * a.ndim)
    G = 2 * subs
    return pl.pallas_call(
        functools.partial(_gla_kernel, layer=layer, C=C),
        grid=(batch, n),
        in_specs=[fwd(2), fwd(3), fwd(5), bwd(2), bwd(4), bwd(5),
                  full(b_lb_table), full(span), full(level)],
        out_specs=[pl.BlockSpec((R, W), lambda b, c: (b * n + c, 0)),
                   pl.BlockSpec((R, W), lambda b, c: (b * n + n - 1 - c, 0))],
        out_shape=[jax.ShapeDtypeStruct((T, W), BF16)] * 2,
        scratch_shapes=[pltpu.VMEM((2, B_HEADS, B_DIM, B_DIM), F32),
                        pltpu.VMEM((G, span.shape[1] + 2 * C, W), BF16),
                        pltpu.VMEM((G, C, W), BF16), pltpu.VMEM((G, C, W), BF16), pltpu.VMEM((G, C, W), BF16),
                        pltpu.VMEM((G, B_HEADS, span.shape[1] // C + 1, C, C), F32),
                        pltpu.VMEM((G, B_HEADS, C, C), BF16)],
        compiler_params=_params("arbitrary", "arbitrary"),
        name="mixer_b",
    )(proj, proj, proj, proj, proj, proj, b_lb_table, span, level)


def _router_epilogue(x_new, g_ref, wr_ref, x_ref, h_ref, aff_ref):
    x_ref[...] = x_new
    h = _rms(x_new, g_ref[...])
    h_ref[...] = h
    w = wr_ref[...]
    n_exp = w.shape[0]
    h_hi, w_hi = h.astype(BF16), w.astype(BF16)
    h_lo, w_lo = (h - h_hi.astype(F32)).astype(BF16), (w - w_hi.astype(F32)).astype(BF16)
    both = _dot_nt(jnp.concatenate([w_hi, w_lo], axis=0), h_hi)
    logits = both[0:n_exp] + both[n_exp:] + _dot_nt(w_hi, h_lo)
    e = jnp.exp(logits - jnp.max(logits, axis=0, keepdims=True))
    aff_ref[...] = e / jnp.sum(e, axis=0, keepdims=True)


def _outproj_even_kernel(u_ref, v_ref, lg_ref, lb_ref, ws_ref, bs_ref, of_ref, ob_ref, gate_ref, bng_ref, w_ref,
                         x_ref, g_ref, wr_ref, xo_ref, h_ref, aff_ref):
    a = _spatial_gating(u_ref, v_ref, lg_ref, lb_ref, ws_ref, bs_ref)
    o = of_ref[...].astype(F32) + ob_ref[...].astype(F32)
    parts = []
    for h in range(B_HEADS):
        oh = o[:, h * B_DIM:(h + 1) * B_DIM]
        parts.append(oh * lax.rsqrt(jnp.mean(oh * oh, axis=-1, keepdims=True) + EPS))
    on = jnp.concatenate(parts, axis=-1) * bng_ref[...] * jax.nn.sigmoid(gate_ref[...].astype(F32))
    wa = a.shape[1]
    mixed = _dot(a, w_ref[0:wa, :]) + _dot(on.astype(BF16), w_ref[wa:, :])
    _router_epilogue(x_ref[...] + mixed, g_ref, wr_ref, xo_ref, h_ref, aff_ref)


def _router_out(T, D, tm):
    specs = [pl.BlockSpec((tm, D), lambda i: (i, 0)), pl.BlockSpec((tm, D), lambda i: (i, 0)),
             pl.BlockSpec((N_EXPERTS, tm), lambda i: (0, i))]
    shapes = [jax.ShapeDtypeStruct((T, D), F32), jax.ShapeDtypeStruct((T, D), F32),
              jax.ShapeDtypeStruct((N_EXPERTS, T), F32)]
    return specs, shapes


def outproj_even(proj, a_ln_g, a_ln_b, a_w_s, a_b_s, o_f, o_b, b_norm_g, w_out, x, ffn_g, w_router, tm=1024):
    T, D = x.shape
    W = o_f.shape[1]
    row = lambda w, col=0: pl.BlockSpec((tm, w), lambda i: (i, col))
    full = lambda s: pl.BlockSpec(s, lambda i: (0,) * len(s))
    out_specs, out_shapes = _router_out(T, D, tm)
    return pl.pallas_call(
        _outproj_even_kernel,
        grid=(T // tm,),
        in_specs=[row(W, 0), row(W, 1), full((1, W)), full((1, W)), full(a_w_s.shape), full(a_b_s.T.shape),
                  row(W), row(W), row(W, 6), full((1, W)), full(w_out.shape), row(D),
                  full((1, D)), full(w_router.T.shape)],
        out_specs=out_specs,
        out_shape=out_shapes,
        compiler_params=_params("parallel"),
        name="outproj_even",
    )(proj, proj, a_ln_g.reshape(1, W), a_ln_b.reshape(1, W), a_w_s.astype(BF16), a_b_s.T,
      o_f, o_b, proj, b_norm_g.reshape(1, W), w_out.astype(BF16), x, ffn_g.reshape(1, D), w_router.T)


def _outproj_odd_kernel(*refs):
    P = len(DILATIONS)
    o_refs, l_refs = refs[:P], refs[P:2 * P]
    ex_ref, w_ref, x_ref, g_ref, wr_ref, xo_ref, h_ref, aff_ref, o_buf, l_buf = refs[2 * P:]
    tm = x_ref.shape[0]

    def token_order(ref, buf, d):
        if d == 1:
            return ref[0].astype(F32)
        tiles = buf.shape[0]
        for r in range(d):
            for t in range(tiles):
                lane0 = (r * tiles + t) * LANES
                buf[t, pl.ds(r, tm // d, stride=d), :] = ref[0, :, lane0:lane0 + LANES].astype(F32)
        return jnp.concatenate([buf[t] for t in range(tiles)], axis=-1)

    ls = [token_order(l_ref, l_buf, d)[:, 0:C_HEADS] for l_ref, d in zip(l_refs, DILATIONS)]
    m = functools.reduce(jnp.maximum, ls)
    es = [jnp.exp(l - m) for l in ls]
    den = functools.reduce(jnp.add, es)
    attn = None
    for e, o_ref, d in zip(es, o_refs, DILATIONS):
        wfull = _dot((e / den).astype(BF16), ex_ref[...])
        term = wfull * token_order(o_ref, o_buf, d)
        attn = term if attn is None else attn + term
    mixed = _dot(attn.astype(BF16), w_ref[...])
    _router_epilogue(x_ref[...] + mixed, g_ref, wr_ref, xo_ref, h_ref, aff_ref)


def outproj_odd(outs, lses, w_o, x, ffn_g, w_router, tm=1024):
    T, D = x.shape
    tiles = outs[0].shape[1] // tm
    expand = jnp.asarray(np.kron(np.eye(C_HEADS), np.ones((1, C_HEAD_DIM))), BF16)
    dil = lambda w: [pl.BlockSpec((1, tm // d, d * w), lambda i: (i // tiles, i % tiles, 0)) for d in DILATIONS]
    full = lambda s: pl.BlockSpec(s, lambda i: (0, 0))
    out_specs, out_shapes = _router_out(T, D, tm)
    return pl.pallas_call(
        _outproj_odd_kernel,
        grid=(T // tm,),
        in_specs=dil(D) + dil(LANES) + [full(expand.shape), full(w_o.shape),
                                        pl.BlockSpec((tm, D), lambda i: (i, 0)), full((1, D)),
                                        full(w_router.T.shape)],
        out_specs=out_specs,
        out_shape=out_shapes,
        scratch_shapes=[pltpu.VMEM((D // LANES, tm, LANES), F32), pltpu.VMEM((1, tm, LANES), F32)],
        compiler_params=_params("parallel"),
        name="outproj_odd",
    )(*outs, *lses, expand, w_o.astype(BF16), x, ffn_g.reshape(1, D), w_router.T)


def _topk_kernel(aff_ref, upper_ref, strict_ref, pos_ref, tile_ref, win_ref, fit_ref, *, cap):
    a = aff_ref[...]
    E, NC, _ = a.shape

    def count(mask):
        return jnp.sum(jnp.sum(mask.astype(F32), axis=2, keepdims=True), axis=1, keepdims=True)

    def as_float(bits):
        return lax.bitcast_convert_type(bits, jnp.float32)

    def search(i, thr):
        cand = thr | jnp.left_shift(jnp.int32(1), 30 - i)
        return jnp.where(count(a >= as_float(cand)) >= cap, cand, thr)

    thr = lax.fori_loop(0, 31, search, jnp.zeros((E, 1, 1), I32))
    gt = a >= as_float(thr + 1)
    eq = jnp.logical_and(a >= as_float(thr), jnp.logical_not(gt))
    need = cap - count(gt)

    def prefix(mask):
        m2 = mask.astype(BF16).reshape(E * NC, LANES)
        within = _dot(m2, upper_ref[...])
        total = jnp.broadcast_to(within[:, LANES - 1:LANES], (E * NC, LANES))
        tot_b = total.astype(BF16)
        starts = jnp.concatenate(
            [_dot(strict_ref[...], tot_b[e * NC:(e + 1) * NC]) for e in range(E)], axis=0)
        return [t.reshape(E, NC, LANES) for t in (within + starts, starts, total)]

    eq_incl, _, _ = prefix(eq)
    sel = gt | (eq & (eq_incl - 1.0 < need))
    sel_incl, starts, count = prefix(sel)
    pos_ref[0] = jnp.where(sel, sel_incl - 1.0, -1.0).astype(I32)
    tile = jnp.minimum(jnp.floor(starts * (1.0 / TOKEN_CHUNK)), float((cap - SLOT_WINDOW) // TOKEN_CHUNK))
    tile_ref[0] = tile.astype(I32)
    win = jnp.minimum(jnp.floor(starts * (1.0 / 16)) * 16, float(cap - NARROW_WINDOW))
    win_ref[0] = win.astype(I32)
    fits = jnp.where(starts + count - win <= NARROW_WINDOW, 1.0, 0.0)
    fit_ref[0] = jnp.min(fits, axis=0, keepdims=True).astype(I32)


def route_topk(aff_t, batch, cap):
    E, T = aff_t.shape
    B, S = batch, T // batch
    NC = S // TOKEN_CHUNK
    t = np.arange(LANES)
    upper = jnp.asarray(t[:, None] <= t[None, :], BF16)
    c = np.arange(NC)
    strict = jnp.asarray(c[None, :] < c[:, None], BF16)
    blk = pl.BlockSpec((1, E, NC, LANES), lambda b: (b, 0, 0, 0))
    one = pl.BlockSpec((1, 1, NC, LANES), lambda b: (b, 0, 0, 0))
    pos, tiles, wins, fits = pl.pallas_call(
        functools.partial(_topk_kernel, cap=cap),
        grid=(B,),
        in_specs=[pl.BlockSpec((E, NC, LANES), lambda b: (0, b, 0)),
                  pl.BlockSpec(upper.shape, lambda b: (0, 0)), pl.BlockSpec(strict.shape, lambda b: (0, 0))],
        out_specs=[blk, blk, blk, one],
        out_shape=[jax.ShapeDtypeStruct((B, E, NC, LANES), I32)] * 3 + [jax.ShapeDtypeStruct((B, 1, NC, LANES), I32)],
        compiler_params=_params("parallel"),
        name="route_topk",
    )(aff_t.reshape(E, B * NC, LANES), upper, strict)
    return pos, tiles[..., 0].reshape(-1), wins[..., 0].reshape(-1), fits[..., 0].reshape(-1)


def _index_kernel(tiles_ref, pos_ref, sel_ref, idx_ref, acc_ref):
    b, e = pl.program_id(0), pl.program_id(1)
    n_exp = pl.num_programs(1)
    NC = pos_ref.shape[2]
    acc_ref[...] = jnp.zeros_like(acc_ref)
    slot = lax.broadcasted_iota(I32, (SLOT_WINDOW, TOKEN_CHUNK), 0)

    def body(c, carry):
        j = tiles_ref[(b * n_exp + e) * NC + c]
        onehot = jnp.where(slot + j * TOKEN_CHUNK == pos_ref[0, 0, pl.ds(c, 1), :], 1.0, 0.0).astype(BF16)
        hit = _dot_nt(sel_ref[...], onehot)
        tok = hit[0:1] + hit[1:2] * lax.convert_element_type(c * TOKEN_CHUNK, F32)
        acc_ref[j, 0:1, :] += tok[:, 0:TOKEN_CHUNK]
        acc_ref[j + 1, 0:1, :] += tok[:, TOKEN_CHUNK:]
        return carry

    lax.fori_loop(0, NC, body, 0, unroll=True)
    idx_ref[0, 0] = acc_ref[...].astype(I32)


def moe_slot_tokens(tiles, pos, cap):
    B, E, NC, _ = pos.shape
    nt = cap // TOKEN_CHUNK
    sel = np.zeros((8, TOKEN_CHUNK), np.float32)
    sel[0] = np.arange(TOKEN_CHUNK)
    sel[1] = 1.0
    idx = pl.pallas_call(
        _index_kernel,
        grid_spec=pltpu.PrefetchScalarGridSpec(
            num_scalar_prefetch=1,
            grid=(B, E),
            in_specs=[pl.BlockSpec((1, 1, NC, LANES), lambda b, e, s: (b, e, 0, 0)),
                      pl.BlockSpec((8, TOKEN_CHUNK), lambda b, e, s: (0, 0))],
            out_specs=pl.BlockSpec((1, 1, nt, 8, LANES), lambda b, e, s: (b, e, 0, 0, 0)),
            scratch_shapes=[pltpu.VMEM((nt, 8, LANES), F32)]),
        out_shape=jax.ShapeDtypeStruct((B, E, nt, 8, LANES), I32),
        compiler_params=_params("arbitrary", "arbitrary"),
        name="moe_slot_tokens",
    )(tiles, pos, jnp.asarray(sel, BF16))
    return idx[:, :, :, 0, :].reshape(-1)


def _ffn_kernel(idx_ref, h_ref, wg_ref, wu_ref, wd_ref, y_ref, x_buf, sem, wgb_ref, wub_ref, wdb_ref, *, tr):
    e, b = pl.program_id(0), pl.program_id(1)
    n_exp, nb = pl.num_programs(0), pl.num_programs(1)
    cap = x_buf.shape[1]
    step = e * nb + b

    def row_copy(st, buf, i):
        eb = st // nb
        bb = st - eb * nb
        tok = idx_ref[(bb * n_exp + eb) * cap + i]
        return pltpu.make_async_copy(h_ref.at[bb, pl.ds(tok, 1), :], x_buf.at[buf, pl.ds(i, 1), :], sem.at[buf])

    cur = jnp.bitwise_and(step, 1)

    @pl.when(step == 0)
    def _():
        lax.fori_loop(0, cap, lambda i, c: (row_copy(step, 0, i).start(), c)[1], 0, unroll=8)

    @pl.when(b == 0)
    def _():
        wgb_ref[...] = wg_ref[0, 0].astype(BF16)
        wub_ref[...] = wu_ref[0, 0].astype(BF16)
        wdb_ref[...] = wd_ref[0, 0].astype(BF16)

    pltpu.make_async_copy(h_ref.at[0, pl.ds(0, cap), :], x_buf.at[cur], sem.at[cur]).wait()

    def compute(prefetch):
        for r in range(cap // tr):
            rows = pl.ds(r * tr, tr)
            x = x_buf[cur, rows, :].astype(BF16)
            if prefetch:
                for i in range(r * tr, (r + 1) * tr):
                    row_copy(step + 1, 1 - cur, i).start()
            g = _dot(x, wgb_ref[...])
            u = _dot(x, wub_ref[...])
            mid = (g * jax.nn.sigmoid(g) * u).astype(BF16)
            y_ref[0, 0, rows, :] = _dot(mid, wdb_ref[...]).astype(BF16)

    is_last = step + 1 == n_exp * nb
    pl.when(jnp.logical_not(is_last))(lambda: compute(True))
    pl.when(is_last)(lambda: compute(False))


def moe_ffn(idx, h, w_gate, w_up, w_down, layer, cap, tr=1024):
    B, S, D = h.shape
    E, F = w_gate.shape[1], w_gate.shape[-1]
    wspec = lambda r, c: pl.BlockSpec((1, 1, r, c), lambda e, b, s: (layer, e, 0, 0))
    return pl.pallas_call(
        functools.partial(_ffn_kernel, tr=min(tr, cap)),
        grid_spec=pltpu.PrefetchScalarGridSpec(
            num_scalar_prefetch=1,
            grid=(E, B),
            in_specs=[pl.BlockSpec(memory_space=pl.ANY), wspec(D, F), wspec(D, F), wspec(F, D)],
            out_specs=pl.BlockSpec((1, 1, cap, D), lambda e, b, s: (b, e, 0, 0)),
            scratch_shapes=[pltpu.VMEM((2, cap, D), F32), pltpu.SemaphoreType.DMA((2,)),
                            pltpu.VMEM((D, F), BF16), pltpu.VMEM((D, F), BF16), pltpu.VMEM((F, D), BF16)]),
        out_shape=jax.ShapeDtypeStruct((B, E, cap, D), BF16),
        compiler_params=_params("arbitrary", "arbitrary"),
        name="moe_ffn",
    )(idx, h, w_gate, w_up, w_down)


def _combine_kernel(tiles_ref, wins_ref, fits_ref, post_ref, gate_ref, x_ref, y_ref, spread_ref, *rest):
    (g_ref, o_ref, y_buf) = rest if len(rest) == 3 else (None,) + tuple(rest)
    b, i = pl.program_id(0), pl.program_id(1)
    n_exp = y_ref.shape[1]
    per_step = x_ref.shape[1] // TOKEN_CHUNK
    NC = pl.num_programs(1) * per_step
    slot = lax.broadcasted_iota(I32, (TOKEN_CHUNK, SLOT_WINDOW), 1)
    expert_lane = lax.broadcasted_iota(I32, (1, n_exp), 1)
    slot_in_window = jnp.bitwise_and(lax.broadcasted_iota(I32, (TOKEN_CHUNK, n_exp * NARROW_WINDOW), 1),
                                     NARROW_WINDOW - 1).astype(F32)
    for k in range(per_step):
        rows = slice(k * TOKEN_CHUNK, (k + 1) * TOKEN_CHUNK)
        c = i * per_step + k
        x = x_ref[0, rows, :]

        def narrow():
            wins = [wins_ref[(b * n_exp + e) * NC + c] for e in range(n_exp)]
            for e in range(n_exp):
                y_buf[e * NARROW_WINDOW:(e + 1) * NARROW_WINDOW, :] = (
                    y_ref[0, e, pl.ds(pl.multiple_of(wins[e], 16), NARROW_WINDOW), :])
            start = jnp.zeros((1, n_exp), I32)
            for e in range(n_exp):
                start = jnp.where(expert_lane == e, wins[e], start)
            rel = post_ref[0, rows, :] - start
            rel = jnp.where(jnp.logical_and(rel >= 0, rel < NARROW_WINDOW), rel, 2 * NARROW_WINDOW - 1)
            rel_wide = _dot(rel.astype(F32).astype(BF16), spread_ref[...])
            gate_wide = _dot(gate_ref[0, rows, :].astype(BF16), spread_ref[...])
            select = jnp.where(rel_wide == slot_in_window, gate_wide, 0.0).astype(BF16)
            return x + _dot(select, y_buf[...])

        def wide():
            acc = x
            for e in range(n_exp):
                base = pl.multiple_of(tiles_ref[(b * n_exp + e) * NC + c] * TOKEN_CHUNK, TOKEN_CHUNK)
                select = jnp.where(slot + base == post_ref[0, rows, e:e + 1], gate_ref[0, rows, e:e + 1], 0.0)
                acc = acc + _dot(select.astype(BF16), y_ref[0, e, pl.ds(base, SLOT_WINDOW), :])
            return acc

        acc = lax.cond(fits_ref[b * NC + c] == 1, narrow, wide)
        o_ref[0, rows, :] = acc if g_ref is None else _rms(acc, g_ref[...])


def moe_combine(tiles, wins, fits, pos_t, gate, x, y, out_norm=None, tm=1024):
    B, S, D = x.shape
    E, cap = y.shape[1], y.shape[2]
    tok = lambda w: pl.BlockSpec((1, tm, w), lambda b, c, *_: (b, c, 0))
    spread = jnp.asarray(np.kron(np.eye(E), np.ones((1, NARROW_WINDOW))), BF16)
    in_specs = [tok(E), tok(E), tok(D),
                pl.BlockSpec((1, E, cap, D), lambda b, c, *_: (b, 0, 0, 0), pipeline_mode=pl.Buffered(1)),
                pl.BlockSpec(spread.shape, lambda b, c, *_: (0, 0))]
    args = [tiles, wins, fits, pos_t, gate, x, y, spread]
    if out_norm is not None:
        in_specs.append(pl.BlockSpec((1, D), lambda b, c, *_: (0, 0)))
        args.append(out_norm.reshape(1, D))
    return pl.pallas_call(
        _combine_kernel,
        grid_spec=pltpu.PrefetchScalarGridSpec(
            num_scalar_prefetch=3,
            grid=(B, S // tm),
            in_specs=in_specs,
            out_specs=tok(D),
            scratch_shapes=[pltpu.VMEM((E * NARROW_WINDOW, D), BF16)]),
        out_shape=jax.ShapeDtypeStruct((B, S, D), F32),
        compiler_params=_params("arbitrary", "arbitrary"),
        name="moe_combine",
    )(*args)


def expert_choice_moe(x, h, aff_t, batch, w_gate, w_up, w_down, layer, out_norm=None):
    T, D = x.shape
    S = T // batch
    cap = max(1, CAPACITY_FACTOR * S // N_EXPERTS)
    aff = aff_t.T.reshape(batch, S, N_EXPERTS)
    pos, tiles, wins, fits = route_topk(aff_t, batch, cap)
    idx = moe_slot_tokens(tiles, pos, cap)
    y = moe_ffn(idx, h.reshape(batch, S, D), w_gate, w_up, w_down, layer, cap)
    pos_t = jnp.swapaxes(pos.reshape(batch, N_EXPERTS, S), 1, 2)
    return moe_combine(tiles, wins, fits, pos_t, aff, x.reshape(batch, S, D), y, out_norm).reshape(T, D)


ATTN_TQ = 128
ATTN_TK = ATTN_TQ + 2 * HALF_WINDOW
ATTN_SUB = 8


def _t5_bucket(rel):
    half_buckets = REL_BUCKETS // 2
    max_exact = half_buckets // 2
    n = jnp.abs(rel)
    scaled = (jnp.log(jnp.maximum(n, 1).astype(jnp.float32) / max_exact)
              / math.log(REL_MAX_DISTANCE / max_exact))
    large = jnp.minimum(max_exact + jnp.floor(scaled * (half_buckets - max_exact)).astype(jnp.int32),
                        half_buckets - 1)
    return jnp.where(rel > 0, half_buckets, 0) + jnp.where(n < max_exact, n, large)


def _bias_kernel(table_ref, bucket_ref, o_ref):
    bucket = bucket_ref[0]
    q = lax.broadcasted_iota(I32, bucket.shape, 0)
    kc = lax.broadcasted_iota(I32, bucket.shape, 1)
    in_band = jnp.abs(kc - HALF_WINDOW - q) <= HALF_WINDOW
    for h in range(C_HEADS):
        acc = jnp.zeros(bucket.shape, F32)
        for bk in range(REL_BUCKETS):
            acc = jnp.where(bucket == bk, table_ref[bk * C_HEADS + h], acc)
        middle = jnp.where(in_band, acc, NEG_INF)
        o_ref[0, 0, h] = jnp.where(kc >= HALF_WINDOW, middle, NEG_INF)
        o_ref[0, 1, h] = middle
        o_ref[0, 2, h] = jnp.where(kc < HALF_WINDOW + ATTN_TQ, middle, NEG_INF)


def attention_bias(rel_bias):
    rel = np.arange(ATTN_TK)[None, :] - HALF_WINDOW - np.arange(ATTN_TQ)[:, None]
    buckets = jnp.stack([_t5_bucket(jnp.asarray(rel * d, I32)) for d in DILATIONS]).astype(I32)
    P = len(DILATIONS)
    return pl.pallas_call(
        _bias_kernel,
        grid_spec=pltpu.PrefetchScalarGridSpec(
            num_scalar_prefetch=1,
            grid=(P,),
            in_specs=[pl.BlockSpec((1, ATTN_TQ, ATTN_TK), lambda p, t: (p, 0, 0))],
            out_specs=pl.BlockSpec((1, 3, C_HEADS, ATTN_TQ, ATTN_TK), lambda p, t: (p, 0, 0, 0, 0))),
        out_shape=jax.ShapeDtypeStruct((P, 3, C_HEADS, ATTN_TQ, ATTN_TK), F32),
        compiler_params=_params("arbitrary"),
        name="attention_bias",
    )(rel_bias.reshape(-1), buckets)


def _attn_kernel(q_ref, kp_ref, km_ref, kn_ref, vp_ref, vm_ref, vn_ref, bias_ref, o_ref, lse_ref,
                 k_buf, v_buf, s_buf, p_buf):
    i = pl.program_id(2)
    last = pl.num_programs(2) - 1
    hw = HALF_WINDOW
    rows = q_ref.shape[1]
    k_buf[0:hw] = kp_ref[0]
    k_buf[hw:hw + rows] = km_ref[0]
    k_buf[hw + rows:] = kn_ref[0]
    v_buf[0:hw] = vp_ref[0]
    v_buf[hw:hw + rows] = vm_ref[0]
    v_buf[hw + rows:] = vn_ref[0]
    first_head = lax.broadcasted_iota(I32, (ATTN_TQ, LANES), 1) < C_HEAD_DIM
    scale = C_HEAD_DIM ** -0.5
    lse_ref[...] = jnp.zeros_like(lse_ref)
    for sub in range(rows // ATTN_TQ):
        qrows = slice(sub * ATTN_TQ, (sub + 1) * ATTN_TQ)
        krows = slice(sub * ATTN_TQ, sub * ATTN_TQ + ATTN_TK)
        if sub == 0:
            variant = jnp.where(i == 0, 0, 1)
        elif sub == rows // ATTN_TQ - 1:
            variant = jnp.where(i == last, 2, 1)
        else:
            variant = 1
        for pair in range(C_HEADS // 2):
            cols = slice(pair * LANES, (pair + 1) * LANES)
            q = q_ref[0, qrows, cols] * scale
            for half in range(2):
                mine = first_head if half == 0 else jnp.logical_not(first_head)
                s_buf[2 * pair + half] = (_dot_nt(jnp.where(mine, q, jnp.zeros_like(q)), k_buf[krows, cols])
                                          + bias_ref[variant, 2 * pair + half])
        for h in range(C_HEADS):
            s = s_buf[h]
            m = jnp.max(s, axis=-1, keepdims=True)
            p = jnp.exp(s - m)
            den = jnp.sum(p, axis=-1, keepdims=True)
            p_buf[h] = (p / den).astype(BF16)
            lse_ref[0, qrows, h:h + 1] = m + jnp.log(den)
        for pair in range(C_HEADS // 2):
            cols = slice(pair * LANES, (pair + 1) * LANES)
            o_ref[0, qrows, cols] = jnp.where(first_head, _dot(p_buf[2 * pair], v_buf[krows, cols]),
                                              _dot(p_buf[2 * pair + 1], v_buf[krows, cols])).astype(BF16)


def dilated_attention(view, bias, d):
    batch, n, D3 = view.shape
    D = D3 // d // 3
    sub = min(ATTN_SUB, n // ATTN_TQ)
    rows = ATTN_TQ * sub
    nb = n // HALF_WINDOW
    r = rows // HALF_WINDOW
    assert n % rows == 0 and sub >= 2
    main = lambda c: pl.BlockSpec((1, rows, D), lambda b, j, i: (b, i, 3 * j + c))
    prev = lambda c: pl.BlockSpec((1, HALF_WINDOW, D), lambda b, j, i: (b, jnp.maximum(i * r - 1, 0), 3 * j + c))
    nxt = lambda c: pl.BlockSpec((1, HALF_WINDOW, D),
                                 lambda b, j, i: (b, jnp.minimum(i * r + r, nb - 1), 3 * j + c))
    return pl.pallas_call(
        _attn_kernel,
        grid=(batch, d, n // rows),
        in_specs=[main(0), prev(1), main(1), nxt(1), prev(2), main(2), nxt(2),
                  pl.BlockSpec(bias.shape, lambda b, j, i: (0, 0, 0, 0))],
        out_specs=[pl.BlockSpec((1, rows, D), lambda b, j, i: (b, i, j)),
                   pl.BlockSpec((1, rows, LANES), lambda b, j, i: (b, i, j))],
        out_shape=[jax.ShapeDtypeStruct((batch, n, d * D), BF16),
                   jax.ShapeDtypeStruct((batch, n, d * LANES), F32)],
        scratch_shapes=[pltpu.VMEM((rows + 2 * HALF_WINDOW, D), BF16), pltpu.VMEM((rows + 2 * HALF_WINDOW, D), BF16),
                        pltpu.VMEM((C_HEADS, ATTN_TQ, ATTN_TK), F32), pltpu.VMEM((C_HEADS, ATTN_TQ, ATTN_TK), BF16)],
        compiler_params=_params("parallel", "parallel", "parallel"),
        name="dilated_attention",
    )(view, view, view, view, view, view, view, bias)


def dilated_mixture(views, rel_bias):
    bias = attention_bias(rel_bias)
    results = [dilated_attention(view, bias[p], d) for p, (view, d) in enumerate(zip(views, DILATIONS))]
    return [o for o, _ in results], [lse for _, lse in results]


def kernel(x, mix_norm, ffn_norm, final_norm, w_in_even, w_out_even, a_ln_g, a_ln_b, a_w_s, a_b_s, b_lb_table,
           b_norm_g, w_qkv_odd, w_o_odd, rel_bias, w_router, w_gate, w_up, w_down):
    B, S, D = x.shape
    depth = mix_norm.shape[0]
    xt = x.reshape(B * S, D)
    for layer in range(depth):
        j = layer // 2
        if layer % 2 == 0:
            proj = norm_matmul(xt, mix_norm[layer], w_in_even[j].astype(BF16), B)[0].reshape(B * S, -1)
            o_f, o_b = mixer_b(proj, b_lb_table, layer, B)
            xt, h, aff = outproj_even(proj, a_ln_g[j], a_ln_b[j], a_w_s[j], a_b_s[j], o_f, o_b, b_norm_g[j],
                                      w_out_even[j], xt, ffn_norm[layer], w_router[layer])
        else:
            views = norm_matmul(xt, mix_norm[layer], w_qkv_odd[j].astype(BF16), B, DILATIONS, tm=512)
            outs, lses = dilated_mixture(views, rel_bias)
            xt, h, aff = outproj_odd(outs, lses, w_o_odd[j], xt, ffn_norm[layer], w_router[layer])
        xt = expert_choice_moe(xt, h, aff, B, w_gate, w_up, w_down, layer,
                               out_norm=final_norm if layer == depth - 1 else None)
    return xt.reshape(B, S, D)
```
